```python
import math
import jax, jax.numpy as jnp
from jax import lax
import numpy as np

D_MODEL = 1024
BATCH = 8
SEQ = 4096
DEPTH = 4

PLE_DIM = 256
DIFF_HEADS = 8
DIFF_HEAD_DIM = 64
DIFF_V_DIM = 2 * DIFF_HEAD_DIM
MLA_HEADS = 8
MLA_Q_RANK = 384
MLA_KV_RANK = 256
MLA_NOPE_DIM = 64
MLA_ROPE_DIM = 32
MLA_V_DIM = 64
ROPE_THETA = 10000.0
REL_BUCKETS = 32
REL_MAX_DIST = 128
DENSE_FF = 2816
N_EXPERTS = 8
TOP_K = 2
EXPERT_FF = 3584
EXPERT_BLOCK = 128
Q_BLOCK = 128
EPS = 1e-5
DEEPNORM_ALPHA = (2.0 * DEPTH) ** 0.25
DEEPNORM_BETA = (8.0 * DEPTH) ** -0.25
N_DENSE = (DEPTH + 1) // 2
N_MOE = DEPTH // 2

DIFF_Q_COLS = DIFF_HEADS * 2 * DIFF_HEAD_DIM
DIFF_K_COLS = DIFF_HEADS * 2 * DIFF_HEAD_DIM
DIFF_V_COLS = DIFF_HEADS * DIFF_V_DIM
GATE_COLS = 2 * D_MODEL
IN_SIZES = (DIFF_Q_COLS, DIFF_K_COLS, DIFF_V_COLS, MLA_Q_RANK, MLA_KV_RANK, MLA_ROPE_DIM, GATE_COLS)
IN_SPLIT_IDX = [sum(IN_SIZES[:j + 1]) for j in range(len(IN_SIZES) - 1)]
IN_COLS = sum(IN_SIZES)

kernel_name = "hybrid_diffattn_mla_gated_moe_deepnorm"


def _layernorm(x, g, b):
    xf = x.astype(jnp.float32)
    mu = jnp.mean(xf, axis=-1, keepdims=True)
    var = jnp.mean(jnp.square(xf - mu), axis=-1, keepdims=True)
    y = (xf - mu) * lax.rsqrt(var + EPS) * g.astype(jnp.float32) + b.astype(jnp.float32)
    return y.astype(x.dtype)


def _rmsnorm(x, g):
    xf = x.astype(jnp.float32)
    y = xf * lax.rsqrt(jnp.mean(jnp.square(xf), axis=-1, keepdims=True) + EPS) * g.astype(jnp.float32)
    return y.astype(x.dtype)


def _rope(x, pos):
    half = x.shape[-1] // 2
    inv_freq = ROPE_THETA ** (-jnp.arange(half, dtype=jnp.float32) / half)
    ang = pos.astype(jnp.float32)[:, :, None] * inv_freq
    c = jnp.cos(ang)[:, :, None, :]
    s = jnp.sin(ang)[:, :, None, :]
    xf = x.astype(jnp.float32)
    x1, x2 = xf[..., :half], xf[..., half:]
    return jnp.concatenate([x1 * c - x2 * s, x1 * s + x2 * c], axis=-1).astype(x.dtype)


def _t5_bucket(dist):
    n = jnp.maximum(dist, 0)
    max_exact = REL_BUCKETS // 2
    large = max_exact + (jnp.log(jnp.maximum(n, 1).astype(jnp.float32) / max_exact)
                         / math.log(REL_MAX_DIST / max_exact) * (REL_BUCKETS - max_exact)).astype(jnp.int32)
    large = jnp.minimum(large, REL_BUCKETS - 1)
    return jnp.where(n < max_exact, n, large)


def _t5_bias(table, q_pos, k_pos):
    bucket = _t5_bucket(q_pos[:, :, None] - k_pos[:, None, :])
    return jnp.transpose(table[bucket].astype(jnp.float32), (0, 3, 1, 2))


def _causal_mask(q0, kend):
    return jnp.arange(kend)[None, :] <= (q0 + jnp.arange(Q_BLOCK))[:, None]


def _diff_attention(q, k, v, pos, rel_table, lam, lam_init, subln_g):
    B, S, H = q.shape[0], q.shape[1], q.shape[2]
    scale = DIFF_HEAD_DIM ** -0.5
    outs = []
    for q0 in range(0, S, Q_BLOCK):
        kend = q0 + Q_BLOCK
        s = jnp.einsum('bqhmd,bkhmd->bhmqk', q[:, q0:kend], k[:, :kend]).astype(jnp.float32) * scale
        s = s + _t5_bias(rel_table, pos[:, q0:kend], pos[:, :kend])[:, :, None]
        s = jnp.where(_causal_mask(q0, kend), s, -jnp.inf)
        a = jax.nn.softmax(s, axis=-1)
        w = a[:, :, 0] - lam * a[:, :, 1]
        outs.append(jnp.einsum('bhqk,bkhe->bqhe', w.astype(v.dtype), v[:, :kend]))
    o = jnp.concatenate(outs, axis=1)
    o = _rmsnorm(o, subln_g) * (1.0 - lam_init)
    return o.reshape(B, S, H * DIFF_V_DIM)


def _mla(c_q, c_kv, k_r, pos, q_norm_g, w_uq, kv_norm_g, w_ukv):
    B, S = c_q.shape[0], c_q.shape[1]
    q = (_rmsnorm(c_q, q_norm_g) @ w_uq).reshape(B, S, MLA_HEADS, MLA_NOPE_DIM + MLA_ROPE_DIM)
    q_nope, q_rope = q[..., :MLA_NOPE_DIM], _rope(q[..., MLA_NOPE_DIM:], pos)
    kv = (_rmsnorm(c_kv, kv_norm_g) @ w_ukv).reshape(B, S, MLA_HEADS, MLA_NOPE_DIM + MLA_V_DIM)
    k_nope, v = kv[..., :MLA_NOPE_DIM], kv[..., MLA_NOPE_DIM:]
    k_rope = _rope(k_r[:, :, None, :], pos)[:, :, 0]
    scale = (MLA_NOPE_DIM + MLA_ROPE_DIM) ** -0.5
    outs = []
    for q0 in range(0, S, Q_BLOCK):
        kend = q0 + Q_BLOCK
        s = (jnp.einsum('bqhd,bkhd->bhqk', q_nope[:, q0:kend], k_nope[:, :kend])
             + jnp.einsum('bqhr,bkr->bhqk', q_rope[:, q0:kend], k_rope[:, :kend])).astype(jnp.float32) * scale
        s = jnp.where(_causal_mask(q0, kend), s, -jnp.inf)
        a = jax.nn.softmax(s, axis=-1)
        outs.append(jnp.einsum('bhqk,bkhd->bqhd', a.astype(v.dtype), v[:, :kend]))
    return jnp.concatenate(outs, axis=1).reshape(B, S, MLA_HEADS * MLA_V_DIM)


def _mixer(h, pos, rel_table, w_in, b_gate, lq1, lk1, lq2, lk2, lam_init, diff_subln_g,
           mla_q_norm_g, w_uq, mla_kv_norm_g, w_ukv, w_branch_diff, w_branch_mla, w_out):
    B, S, _ = h.shape
    z = h @ w_in
    dq, dk, dv, cq, ckv, kr, gl = jnp.split(z, IN_SPLIT_IDX, axis=-1)
    lam = (jnp.exp(jnp.sum(lq1.astype(jnp.float32) * lk1.astype(jnp.float32)))
           - jnp.exp(jnp.sum(lq2.astype(jnp.float32) * lk2.astype(jnp.float32))) + lam_init)
    o_a = _diff_attention(dq.reshape(B, S, DIFF_HEADS, 2, DIFF_HEAD_DIM),
                          dk.reshape(B, S, DIFF_HEADS, 2, DIFF_HEAD_DIM),
                          dv.reshape(B, S, DIFF_HEADS, DIFF_V_DIM),
                          pos, rel_table, lam, lam_init, diff_subln_g)
    o_b = _mla(cq, ckv, kr, pos, mla_q_norm_g, w_uq, mla_kv_norm_g, w_ukv)
    g = jax.nn.sigmoid(gl + b_gate)
    g_a, g_b = g[..., :D_MODEL], g[..., D_MODEL:]
    m = g_a * (o_a @ w_branch_diff) + g_b * (o_b @ w_branch_mla)
    return m @ w_out


def _swiglu(x, w1, w3, w2):
    return (jax.nn.silu(x @ w1) * (x @ w3)) @ w2


def _moe(x, w_router, w1, w3, w2):
    B, S, D = x.shape
    xt = x.reshape(-1, D)
    T = xt.shape[0]
    M = T * TOP_K
    logits = (xt @ w_router).astype(jnp.float32)
    top_v, top_e = lax.top_k(logits, TOP_K)
    gates = jax.nn.softmax(top_v, axis=-1)
    flat_e = top_e.reshape(-1)
    flat_tok = jnp.repeat(jnp.arange(T, dtype=jnp.int32), TOP_K)
    order = jnp.argsort(flat_e, stable=True)
    se, stok, sg = flat_e[order], flat_tok[order], gates.reshape(-1)[order]
    counts = jnp.bincount(flat_e, length=N_EXPERTS)
    padded = ((counts + EXPERT_BLOCK - 1) // EXPERT_BLOCK) * EXPERT_BLOCK
    start = jnp.cumsum(counts) - counts
    pstart = jnp.cumsum(padded) - padded
    dest = pstart[se] + jnp.arange(M, dtype=jnp.int32) - start[se]
    n_blocks = -(-M // EXPERT_BLOCK) + N_EXPERTS
    P = n_blocks * EXPERT_BLOCK
    buf = jnp.zeros((P, D), x.dtype).at[dest].set(xt[stok])
    block_e = jnp.minimum(jnp.searchsorted(jnp.cumsum(padded), jnp.arange(n_blocks) * EXPERT_BLOCK,
                                           side='right'), N_EXPERTS - 1)

    def run(args):
        xb, e = args
        return _swiglu(xb, w1[e], w3[e], w2[e])

    yb = lax.map(run, (buf.reshape(n_blocks, EXPERT_BLOCK, D), block_e)).reshape(P, D)
    y = jnp.zeros((T, D), x.dtype).at[stok].add(yb[dest] * sg[:, None].astype(x.dtype))
    return y.reshape(B, S, D)


def setup_inputs(seed: int = 0) -> dict:
    key = jax.random.key(seed)
    ks = jax.random.split(key, 32)
    f32 = jnp.float32

    def nrm(k, shape, scale):
        return jax.random.normal(k, shape, f32) * scale

    beta = DEEPNORM_BETA
    return {
        "x": nrm(ks[0], (BATCH, SEQ, D_MODEL), 1.0),
        "p": nrm(ks[1], (DEPTH, BATCH, SEQ, PLE_DIM), 1.0),
        "positions": jnp.broadcast_to(jnp.arange(SEQ, dtype=jnp.int32), (BATCH, SEQ)),
        "rel_bias_table": nrm(ks[2], (REL_BUCKETS, DIFF_HEADS), 0.5),
        "w_in": nrm(ks[3], (DEPTH, D_MODEL, IN_COLS), D_MODEL ** -0.5),
        "b_gate": nrm(ks[4], (DEPTH, GATE_COLS), 0.02),
        "lambda_q1": nrm(ks[5], (DEPTH, DIFF_HEAD_DIM), 0.1),
        "lambda_k1": nrm(ks[6], (DEPTH, DIFF_HEAD_DIM), 0.1),
        "lambda_q2": nrm(ks[7], (DEPTH, DIFF_HEAD_DIM), 0.1),
        "lambda_k2": nrm(ks[8], (DEPTH, DIFF_HEAD_DIM), 0.1),
        "diff_subln_g": 1.0 + nrm(ks[9], (DEPTH, DIFF_V_DIM), 0.02),
        "mla_q_norm_g": 1.0 + nrm(ks[10], (DEPTH, MLA_Q_RANK), 0.02),
        "w_uq": nrm(ks[11], (DEPTH, MLA_Q_RANK, MLA_HEADS * (MLA_NOPE_DIM + MLA_ROPE_DIM)), MLA_Q_RANK ** -0.5),
        "mla_kv_norm_g": 1.0 + nrm(ks[12], (DEPTH, MLA_KV_RANK), 0.02),
        "w_ukv": nrm(ks[13], (DEPTH, MLA_KV_RANK, MLA_HEADS * (MLA_NOPE_DIM + MLA_V_DIM)), MLA_KV_RANK ** -0.5),
        "w_branch_diff": nrm(ks[14], (DEPTH, DIFF_HEADS * DIFF_V_DIM, D_MODEL), (DIFF_HEADS * DIFF_V_DIM) ** -0.5),
        "w_branch_mla": nrm(ks[15], (DEPTH, MLA_HEADS * MLA_V_DIM, D_MODEL), (MLA_HEADS * MLA_V_DIM) ** -0.5),
        "w_out": nrm(ks[16], (DEPTH, D_MODEL, D_MODEL), beta * D_MODEL ** -0.5),
        "ln_mix_g": 1.0 + nrm(ks[17], (DEPTH, D_MODEL), 0.02),
        "ln_mix_b": nrm(ks[18], (DEPTH, D_MODEL), 0.02),
        "dense_w1": nrm(ks[19], (N_DENSE, D_MODEL, DENSE_FF), D_MODEL ** -0.5),
        "dense_w3": nrm(ks[20], (N_DENSE, D_MODEL, DENSE_FF), D_MODEL ** -0.5),
        "dense_w2": nrm(ks[21], (N_DENSE, DENSE_FF, D_MODEL), beta * DENSE_FF ** -0.5),
        "router_w": nrm(ks[22], (N_MOE, D_MODEL, N_EXPERTS), D_MODEL ** -0.5),
        "expert_w1": nrm(ks[23], (N_MOE, N_EXPERTS, D_MODEL, EXPERT_FF), D_MODEL ** -0.5),
        "expert_w3": nrm(ks[24], (N_MOE, N_EXPERTS, D_MODEL, EXPERT_FF), D_MODEL ** -0.5),
        "expert_w2": nrm(ks[25], (N_MOE, N_EXPERTS, EXPERT_FF, D_MODEL), beta * EXPERT_FF ** -0.5),
        "w_ple_gate": nrm(ks[26], (DEPTH, D_MODEL, D_MODEL), D_MODEL ** -0.5),
        "w_ple_proj": nrm(ks[27], (DEPTH, PLE_DIM, D_MODEL), beta * PLE_DIM ** -0.5),
        "ln_ffn_g": 1.0 + nrm(ks[28], (DEPTH, D_MODEL), 0.02),
        "ln_ffn_b": nrm(ks[29], (DEPTH, D_MODEL), 0.02),
    }


def reference(x, p, positions, rel_bias_table, w_in, b_gate, lambda_q1, lambda_k1, lambda_q2, lambda_k2,
              diff_subln_g, mla_q_norm_g, w_uq, mla_kv_norm_g, w_ukv, w_branch_diff, w_branch_mla, w_out,
              ln_mix_g, ln_mix_b, dense_w1, dense_w3, dense_w2, router_w, expert_w1, expert_w3, expert_w2,
              w_ple_gate, w_ple_proj, ln_ffn_g, ln_ffn_b):
    for i in range(DEPTH):
        lam_init = 0.8 - 0.6 * math.exp(-0.3 * i)
        y = _mixer(x, positions, rel_bias_table, w_in[i], b_gate[i], lambda_q1[i], lambda_k1[i],
                   lambda_q2[i], lambda_k2[i], lam_init, diff_subln_g[i], mla_q_norm_g[i], w_uq[i],
                   mla_kv_norm_g[i], w_ukv[i], w_branch_diff[i], w_branch_mla[i], w_out[i])
        x = _layernorm(DEEPNORM_ALPHA * x + y, ln_mix_g[i], ln_mix_b[i])
        j = i // 2
        if i % 2 == 0:
            f = _swiglu(x, dense_w1[j], dense_w3[j], dense_w2[j])
        else:
            f = _moe(x, router_w[j], expert_w1[j], expert_w3[j], expert_w2[j])
        e = jax.nn.sigmoid(x @ w_ple_gate[i]) * (p[i] @ w_ple_proj[i])
        x = _layernorm(DEEPNORM_ALPHA * x + f + e, ln_ffn_g[i], ln_ffn_b[i])
    return x
```

```python
import functools
import math

import jax
import jax.numpy as jnp
from jax import lax
from jax.experimental import pallas as pl
from jax.experimental.pallas import tpu as pltpu

F32 = jnp.float32
BF16 = jnp.bfloat16

D_MODEL = 1024
PLE_DIM = 256
DIFF_HEADS = 8
DIFF_HEAD_DIM = 64
MLA_HEADS = 8
MLA_Q_RANK = 384
MLA_KV_RANK = 256
MLA_NOPE_DIM = 64
MLA_ROPE_DIM = 32
MLA_V_DIM = 64
ROPE_THETA = 10000.0
REL_BUCKETS = 32
REL_MAX_DIST = 128
DENSE_FF = 2816
N_EXPERTS = 8
TOP_K = 2
EXPERT_FF = 3584
EPS = 1e-5

HEAD_LANES = 128
NEG = -1e30
ATT_TILE = 256
ROW_TILE = 512
GROUP_ROWS = 512
VMEM_LIMIT = 52 * 1024 * 1024


def _cp(sem):
    return pltpu.CompilerParams(dimension_semantics=sem, vmem_limit_bytes=VMEM_LIMIT)


def _dot(a, b):
    return jnp.dot(a, b, preferred_element_type=F32)


def _sigmoid(x):
    return 1.0 / (1.0 + jnp.exp(-x))


def _layernorm(r, g, b):
    mu = jnp.mean(r, axis=-1, keepdims=True)
    d = r - mu
    var = jnp.mean(d * d, axis=-1, keepdims=True)
    return d * lax.rsqrt(var + EPS) * g + b


def _rmsnorm(x, g):
    return x * lax.rsqrt(jnp.mean(x * x, axis=-1, keepdims=True) + EPS) * g


def _mm_kernel(a_ref, w_ref, o_ref):
    o_ref[...] = _dot(a_ref[...], w_ref[...]).astype(o_ref.dtype)


def _matmul(a, w, out_dtype, tm, tn):
    m, k = a.shape
    n = w.shape[1]
    return pl.pallas_call(
        _mm_kernel,
        grid=(m // tm, n // tn),
        in_specs=[pl.BlockSpec((tm, k), lambda i, j: (i, 0)),
                  pl.BlockSpec((k, tn), lambda i, j: (0, j))],
        out_specs=pl.BlockSpec((tm, tn), lambda i, j: (i, j)),
        out_shape=jax.ShapeDtypeStruct((m, n), out_dtype),
        compiler_params=_cp(("parallel", "arbitrary")),
    )(a, w)


C_COLS = MLA_Q_RANK + MLA_KV_RANK + 2 * HEAD_LANES


def _latent_kernel(a_ref, w_ref, gq_ref, gkv_ref, cos_ref, sin_ref, cq_ref, ckv_ref, kr_ref):
    z = _dot(a_ref[...], w_ref[...])
    cq = z[:, :MLA_Q_RANK]
    ckv = z[:, MLA_Q_RANK:MLA_Q_RANK + MLA_KV_RANK]
    kr = z[:, MLA_Q_RANK + MLA_KV_RANK:MLA_Q_RANK + MLA_KV_RANK + HEAD_LANES]
    kr_rot = z[:, MLA_Q_RANK + MLA_KV_RANK + HEAD_LANES:]
    cq_ref[...] = _rmsnorm(cq, gq_ref[...]).astype(BF16)
    ckv_ref[...] = _rmsnorm(ckv, gkv_ref[...]).astype(BF16)
    kr_ref[...] = (kr * cos_ref[...] + kr_rot * sin_ref[...]).astype(BF16)


def _latent(xb, w_c, gq, gkv, cos_t, sin_t, tm):
    t = xb.shape[0]
    row = lambda n: pl.BlockSpec((tm, n), lambda i: (i, 0))
    full = lambda a: pl.BlockSpec(a.shape, lambda i: (0, 0))
    return pl.pallas_call(
        _latent_kernel,
        grid=(t // tm,),
        in_specs=[row(D_MODEL), full(w_c), full(gq), full(gkv), row(HEAD_LANES), row(HEAD_LANES)],
        out_specs=[row(MLA_Q_RANK), row(MLA_KV_RANK), row(HEAD_LANES)],
        out_shape=[jax.ShapeDtypeStruct((t, MLA_Q_RANK), BF16),
                   jax.ShapeDtypeStruct((t, MLA_KV_RANK), BF16),
                   jax.ShapeDtypeStruct((t, HEAD_LANES), BF16)],
        compiler_params=_cp(("parallel",)),
    )(xb, w_c, gq, gkv, cos_t, sin_t)


def _qup_kernel(a_ref, w_ref, cos_ref, sin_ref, o_ref, *, scale):
    z = _dot(a_ref[...], w_ref[...])
    c = cos_ref[...] * scale
    s = sin_ref[...] * scale
    hl = HEAD_LANES
    for h in range(MLA_HEADS):
        o_ref[:, h * hl:(h + 1) * hl] = (
            z[:, h * hl:(h + 1) * hl] * c
            + z[:, (MLA_HEADS + h) * hl:(MLA_HEADS + h + 1) * hl] * s).astype(BF16)


def _qup(cq, w_q2, cos_t, sin_t, tm):
    t = cq.shape[0]
    n = MLA_HEADS * HEAD_LANES
    scale = (MLA_NOPE_DIM + MLA_ROPE_DIM) ** -0.5
    return pl.pallas_call(
        functools.partial(_qup_kernel, scale=scale),
        grid=(t // tm,),
        in_specs=[pl.BlockSpec((tm, MLA_Q_RANK), lambda i: (i, 0)),
                  pl.BlockSpec(w_q2.shape, lambda i: (0, 0)),
                  pl.BlockSpec((tm, HEAD_LANES), lambda i: (i, 0)),
                  pl.BlockSpec((tm, HEAD_LANES), lambda i: (i, 0))],
        out_specs=pl.BlockSpec((tm, n), lambda i: (i, 0)),
        out_shape=jax.ShapeDtypeStruct((t, n), BF16),
        compiler_params=_cp(("parallel",)),
    )(cq, w_q2, cos_t, sin_t)


def _online_softmax_step(s, v, m_ref, l_ref, acc_ref):
    m_old = m_ref[...]
    m_new = jnp.maximum(m_old, jnp.max(s, axis=-1, keepdims=True))
    p = jnp.exp(s - m_new)
    alpha = jnp.exp(m_old - m_new)
    l_ref[...] = alpha * l_ref[...] + jnp.sum(p, axis=-1, keepdims=True)
    acc_ref[...] = alpha * acc_ref[...] + _dot(p.astype(BF16), v)
    m_ref[...] = m_new


def _qk(q, k):
    return lax.dot_general(q, k, (((1,), (1,)), ((), ())), preferred_element_type=F32)


def _diff_attn_kernel(q_ref, k_ref, v_ref, bias_ref, lq1_ref, lk1_ref, lq2_ref, lk2_ref, g_ref, o_ref,
                      m_scr, l_scr, acc_scr, *, lam_init, tile):
    qi = pl.program_id(2)
    q = q_ref[0]
    lane = lax.broadcasted_iota(jnp.int32, q.shape, 1)
    zero = jnp.zeros_like(q)
    q_maps = (jnp.where(lane < DIFF_HEAD_DIM, q, zero), jnp.where(lane >= DIFF_HEAD_DIM, q, zero))
    m_scr[...] = jnp.full(m_scr.shape, NEG, F32)
    l_scr[...] = jnp.zeros(l_scr.shape, F32)
    acc_scr[...] = jnp.zeros(acc_scr.shape, F32)

    def key_tile(kj, bias):
        rows = pl.ds(pl.multiple_of(kj * tile, tile), tile)
        k = k_ref[0, rows, :]
        v = v_ref[0, rows, :]
        for mi in range(2):
            s = _qk(q_maps[mi], k)
            if bias is not None:
                s = s + bias
            _online_softmax_step(s, v, m_scr.at[mi], l_scr.at[mi], acc_scr.at[mi])

    def far_body(kj, carry):
        key_tile(kj, None)
        return carry

    lax.fori_loop(0, jnp.maximum(qi - 1, 0), far_body, 0)

    @pl.when(qi >= 1)
    def _():
        key_tile(qi - 1, bias_ref[0, 1])

    key_tile(qi, bias_ref[0, 0])

    lam = (jnp.exp(jnp.sum(lq1_ref[...] * lk1_ref[...], axis=-1, keepdims=True))
           - jnp.exp(jnp.sum(lq2_ref[...] * lk2_ref[...], axis=-1, keepdims=True)) + lam_init)
    o = acc_scr[0] / l_scr[0] - lam * (acc_scr[1] / l_scr[1])
    o_ref[0] = (_rmsnorm(o, g_ref[...]) * (1.0 - lam_init)).astype(BF16)


def _diff_attention(qkv, bias, lq1, lk1, lq2, lk2, g, lam_init, batch, seq):
    tile = ATT_TILE
    hl = HEAD_LANES
    nh = DIFF_HEADS
    vec = lambda a: pl.BlockSpec(a.shape, lambda b, h, i: (0, 0))
    return pl.pallas_call(
        functools.partial(_diff_attn_kernel, lam_init=lam_init, tile=tile),
        grid=(batch, nh, seq // tile),
        in_specs=[pl.BlockSpec((1, tile, hl), lambda b, h, i: (b, i, h)),
                  pl.BlockSpec((1, seq, hl), lambda b, h, i: (b, 0, nh + h)),
                  pl.BlockSpec((1, seq, hl), lambda b, h, i: (b, 0, 2 * nh + h)),
                  pl.BlockSpec((1, 2, tile, tile), lambda b, h, i: (h, 0, 0, 0)),
                  vec(lq1), vec(lk1), vec(lq2), vec(lk2), vec(g)],
        out_specs=pl.BlockSpec((1, tile, hl), lambda b, h, i: (b, i, h)),
        out_shape=jax.ShapeDtypeStruct((batch, seq, nh * hl), BF16),
        scratch_shapes=[pltpu.VMEM((2, tile, 1), F32), pltpu.VMEM((2, tile, 1), F32),
                        pltpu.VMEM((2, tile, hl), F32)],
        compiler_params=_cp(("parallel", "parallel", "arbitrary")),
    )(qkv, qkv, qkv, bias, lq1, lk1, lq2, lk2, g)


def _mla_attn_kernel(q_ref, k_ref, kr_ref, v_ref, o_ref, kf_scr, m_scr, l_scr, acc_scr, *, tile):
    qi = pl.program_id(2)

    @pl.when(qi == 0)
    def _():
        kf_scr[...] = k_ref[0] + kr_ref[0]

    q = q_ref[0]
    m_scr[...] = jnp.full(m_scr.shape, NEG, F32)
    l_scr[...] = jnp.zeros(l_scr.shape, F32)
    acc_scr[...] = jnp.zeros(acc_scr.shape, F32)

    def key_tile(kj, causal):
        rows = pl.ds(pl.multiple_of(kj * tile, tile), tile)
        s = _qk(q, kf_scr[rows, :])
        if causal:
            r = lax.broadcasted_iota(jnp.int32, s.shape, 0)
            c = lax.broadcasted_iota(jnp.int32, s.shape, 1)
            s = jnp.where(c <= r, s, NEG)
        _online_softmax_step(s, v_ref[0, rows, :], m_scr, l_scr, acc_scr)

    def body(kj, carry):
        key_tile(kj, False)
        return carry

    lax.fori_loop(0, qi, body, 0)
    key_tile(qi, True)
    o_ref[0] = (acc_scr[...] / l_scr[...]).astype(BF16)


def _mla_attention(q, kv, kr, batch, seq):
    tile = ATT_TILE
    hl = HEAD_LANES
    nh = MLA_HEADS
    return pl.pallas_call(
        functools.partial(_mla_attn_kernel, tile=tile),
        grid=(batch, nh, seq // tile),
        in_specs=[pl.BlockSpec((1, tile, hl), lambda b, h, i: (b, i, h)),
                  pl.BlockSpec((1, seq, hl), lambda b, h, i: (b, 0, h)),
                  pl.BlockSpec((1, seq, hl), lambda b, h, i: (b, 0, 0)),
                  pl.BlockSpec((1, seq, hl), lambda b, h, i: (b, 0, nh + h))],
        out_specs=pl.BlockSpec((1, tile, hl), lambda b, h, i: (b, i, h)),
        out_shape=jax.ShapeDtypeStruct((batch, seq, nh * hl), BF16),
        scratch_shapes=[pltpu.VMEM((seq, hl), BF16), pltpu.VMEM((tile, 1), F32), pltpu.VMEM((tile, 1), F32),
                        pltpu.VMEM((tile, hl), F32)],
        compiler_params=_cp(("parallel", "parallel", "arbitrary")),
    )(q, kv, kr, kv)


def _merge_kernel(xb_ref, oa_ref, ob_ref, wga_ref, wgb_ref, bga_ref, bgb_ref, wbd_ref, wbm_ref, o_ref):
    xb = xb_ref[...]
    g_a = _sigmoid(_dot(xb, wga_ref[...]) + bga_ref[...])
    g_b = _sigmoid(_dot(xb, wgb_ref[...]) + bgb_ref[...])
    m = g_a * _dot(oa_ref[...], wbd_ref[...]) + g_b * _dot(ob_ref[...], wbm_ref[...])
    o_ref[...] = m.astype(BF16)


def _merge(xb, o_a, o_b, w_g, b_g, w_bd, w_bm, tm, tn):
    t = xb.shape[0]
    nj = D_MODEL // tn
    row = pl.BlockSpec((tm, D_MODEL), lambda i, j: (i, 0))
    col = lambda off: pl.BlockSpec((D_MODEL, tn), lambda i, j: (0, j + off))
    vec = lambda off: pl.BlockSpec((1, tn), lambda i, j: (0, j + off))
    return pl.pallas_call(
        _merge_kernel,
        grid=(t // tm, nj),
        in_specs=[row, row, row, col(0), col(nj), vec(0), vec(nj), col(0), col(0)],
        out_specs=pl.BlockSpec((tm, tn), lambda i, j: (i, j)),
        out_shape=jax.ShapeDtypeStruct((t, D_MODEL), BF16),
        compiler_params=_cp(("parallel", "arbitrary")),
    )(xb, o_a, o_b, w_g, w_g, b_g, b_g, w_bd, w_bm)


def _out_ln_kernel(m_ref, w_ref, x_ref, g_ref, b_ref, xo_ref, xbo_ref, *, alpha):
    r = alpha * x_ref[...] + _dot(m_ref[...], w_ref[...])
    y = _layernorm(r, g_ref[...], b_ref[...])
    xo_ref[...] = y
    xbo_ref[...] = y.astype(BF16)


def _out_ln(m, w_out, x, g, b, alpha, tm):
    t = x.shape[0]
    row = pl.BlockSpec((tm, D_MODEL), lambda i: (i, 0))
    full = lambda a: pl.BlockSpec(a.shape, lambda i: (0, 0))
    return pl.pallas_call(
        functools.partial(_out_ln_kernel, alpha=alpha),
        grid=(t // tm,),
        in_specs=[row, full(w_out), row, full(g), full(b)],
        out_specs=[row, row],
        out_shape=[jax.ShapeDtypeStruct((t, D_MODEL), F32), jax.ShapeDtypeStruct((t, D_MODEL), BF16)],
        compiler_params=_cp(("parallel",)),
    )(m, w_out, x, g, b)


def _swiglu_up_kernel(a_ref, w1_ref, w3_ref, o_ref):
    a = a_ref[...]
    u = _dot(a, w1_ref[...])
    o_ref[...] = (u * _sigmoid(u) * _dot(a, w3_ref[...])).astype(BF16)


def _swiglu_up(xb, w1, w3, tm, tn):
    t = xb.shape[0]
    ff = w1.shape[1]
    return pl.pallas_call(
        _swiglu_up_kernel,
        grid=(t // tm, ff // tn),
        in_specs=[pl.BlockSpec((tm, D_MODEL), lambda i, j: (i, 0)),
                  pl.BlockSpec((D_MODEL, tn), lambda i, j: (0, j)),
                  pl.BlockSpec((D_MODEL, tn), lambda i, j: (0, j))],
        out_specs=pl.BlockSpec((tm, tn), lambda i, j: (i, j)),
        out_shape=jax.ShapeDtypeStruct((t, ff), BF16),
        compiler_params=_cp(("parallel", "arbitrary")),
    )(xb, w1, w3)


def _ffn_ln_kernel(f_ref, w2_ref, x_ref, xb_ref, p_ref, wpg_ref, wpp_ref, g_ref, b_ref, xo_ref, xbo_ref, *,
                   alpha, project):
    f = _dot(f_ref[...], w2_ref[...]) if project else f_ref[...]
    e = _sigmoid(_dot(xb_ref[...], wpg_ref[...])) * _dot(p_ref[...].astype(BF16), wpp_ref[...])
    y = _layernorm(alpha * x_ref[...] + f + e, g_ref[...], b_ref[...])
    xo_ref[...] = y
    xbo_ref[...] = y.astype(BF16)


def _ffn_ln(f, w2, x, xb, p, w_pg, w_pp, g, b, alpha, tm, project):
    t = x.shape[0]
    row = lambda n: pl.BlockSpec((tm, n), lambda i: (i, 0))
    full = lambda a: pl.BlockSpec(a.shape, lambda i: (0, 0))
    return pl.pallas_call(
        functools.partial(_ffn_ln_kernel, alpha=alpha, project=project),
        grid=(t // tm,),
        in_specs=[row(f.shape[1]), full(w2), row(D_MODEL), row(D_MODEL), row(PLE_DIM), full(w_pg), full(w_pp),
                  full(g), full(b)],
        out_specs=[row(D_MODEL), row(D_MODEL)],
        out_shape=[jax.ShapeDtypeStruct((t, D_MODEL), F32), jax.ShapeDtypeStruct((t, D_MODEL), BF16)],
        compiler_params=_cp(("parallel",)),
    )(f, w2, x, xb, p, w_pg, w_pp, g, b)


def _router_kernel(x_ref, w_ref, o_ref):
    logits = jnp.dot(x_ref[...], w_ref[...], preferred_element_type=F32, precision=lax.Precision.HIGHEST)
    lane = lax.broadcasted_iota(jnp.int32, logits.shape, 1)
    lg = jnp.where(lane < N_EXPERTS, logits, -jnp.inf)
    v1 = jnp.max(lg, axis=-1, keepdims=True)
    i1 = jnp.min(jnp.where(lg == v1, lane, HEAD_LANES), axis=-1, keepdims=True)
    lg2 = jnp.where(lane == i1, -jnp.inf, lg)
    v2 = jnp.max(lg2, axis=-1, keepdims=True)
    i2 = jnp.min(jnp.where(lg2 == v2, lane, HEAD_LANES), axis=-1, keepdims=True)
    e2 = jnp.exp(v2 - v1)
    g1 = 1.0 / (1.0 + e2)
    g2 = e2 / (1.0 + e2)
    o_ref[...] = jnp.where(lane == 0, i1.astype(F32),
                           jnp.where(lane == 1, i2.astype(F32),
                                     jnp.where(lane == 2, g1, jnp.where(lane == 3, g2, 0.0))))


def _router(x, w_r, tm):
    t = x.shape[0]
    return pl.pallas_call(
        _router_kernel,
        grid=(t // tm,),
        in_specs=[pl.BlockSpec((tm, D_MODEL), lambda i: (i, 0)),
                  pl.BlockSpec(w_r.shape, lambda i: (0, 0))],
        out_specs=pl.BlockSpec((tm, HEAD_LANES), lambda i: (i, 0)),
        out_shape=jax.ShapeDtypeStruct((t, HEAD_LANES), F32),
        compiler_params=_cp(("parallel",)),
    )(x, w_r)


def _expert_up_kernel(be_ref, nb_ref, a_ref, w1_ref, w3_ref, o_ref):
    @pl.when(pl.program_id(0) < nb_ref[0])
    def _():
        a = a_ref[...]
        u = _dot(a, w1_ref[0])
        o_ref[...] = (u * _sigmoid(u) * _dot(a, w3_ref[0])).astype(BF16)

    @pl.when(pl.program_id(0) >= nb_ref[0])
    def _():
        o_ref[...] = jnp.zeros(o_ref.shape, BF16)


def _expert_up(a, w1, w3, blk_e, n_used, tm, tn):
    rows = a.shape[0]
    ff = w1.shape[2]
    grid_spec = pltpu.PrefetchScalarGridSpec(
        num_scalar_prefetch=2,
        grid=(rows // tm, ff // tn),
        in_specs=[pl.BlockSpec((tm, D_MODEL), lambda i, j, be, nb: (i, 0)),
                  pl.BlockSpec((1, D_MODEL, tn), lambda i, j, be, nb: (be[i], 0, j)),
                  pl.BlockSpec((1, D_MODEL, tn), lambda i, j, be, nb: (be[i], 0, j))],
        out_specs=pl.BlockSpec((tm, tn), lambda i, j, be, nb: (i, j)),
    )
    return pl.pallas_call(
        _expert_up_kernel,
        grid_spec=grid_spec,
        out_shape=jax.ShapeDtypeStruct((rows, ff), BF16),
        compiler_params=_cp(("parallel", "arbitrary")),
    )(blk_e, n_used, a, w1, w3)


def _expert_down_kernel(be_ref, nb_ref, h_ref, w2_ref, o_ref):
    @pl.when(pl.program_id(0) < nb_ref[0])
    def _():
        o_ref[...] = _dot(h_ref[...], w2_ref[0])

    @pl.when(pl.program_id(0) >= nb_ref[0])
    def _():
        o_ref[...] = jnp.zeros(o_ref.shape, F32)


def _expert_down(h, w2, blk_e, n_used, tm):
    rows, ff = h.shape
    grid_spec = pltpu.PrefetchScalarGridSpec(
        num_scalar_prefetch=2,
        grid=(rows // tm,),
        in_specs=[pl.BlockSpec((tm, ff), lambda i, be, nb: (i, 0)),
                  pl.BlockSpec((1, ff, D_MODEL), lambda i, be, nb: (be[i], 0, 0))],
        out_specs=pl.BlockSpec((tm, D_MODEL), lambda i, be, nb: (i, 0)),
    )
    return pl.pallas_call(
        _expert_down_kernel,
        grid_spec=grid_spec,
        out_shape=jax.ShapeDtypeStruct((rows, D_MODEL), F32),
        compiler_params=_cp(("parallel",)),
    )(blk_e, n_used, h, w2)


def _moe(x, xb, w_r, w1, w3, w2, tm):
    t = x.shape[0]
    m = t * TOP_K
    gb = GROUP_ROWS
    routed = _router(x, w_r, tm)
    flat_e = routed[:, :TOP_K].astype(jnp.int32).reshape(-1)
    gates = routed[:, TOP_K:2 * TOP_K]
    onehot = (flat_e[:, None] == jnp.arange(N_EXPERTS, dtype=jnp.int32)[None, :]).astype(jnp.int32)
    csum = jnp.cumsum(onehot, axis=0)
    counts = csum[-1]
    rank = jnp.take_along_axis(csum, flat_e[:, None], axis=1)[:, 0] - 1
    padded = ((counts + gb - 1) // gb) * gb
    pend = jnp.cumsum(padded)
    dest = (pend - padded)[flat_e] + rank
    n_blocks = m // gb + N_EXPERTS
    src_tok = jnp.zeros((n_blocks * gb,), jnp.int32).at[dest].set(jnp.arange(m, dtype=jnp.int32) // TOP_K)
    blk_e = jnp.minimum(jnp.searchsorted(pend, jnp.arange(n_blocks, dtype=jnp.int32) * gb, side='right'),
                        N_EXPERTS - 1).astype(jnp.int32)
    n_used = (pend[-1:] // gb).astype(jnp.int32)
    a = jnp.take(xb, src_tok, axis=0)
    h = _expert_up(a, w1, w3, blk_e, n_used, gb, EXPERT_FF // 2)
    yb = _expert_down(h, w2, blk_e, n_used, gb)
    dest2 = dest.reshape(t, TOP_K)
    return (jnp.take(yb, dest2[:, 0], axis=0) * gates[:, 0:1]
            + jnp.take(yb, dest2[:, 1], axis=0) * gates[:, 1:2])


def _t5_bucket(dist):
    n = jnp.maximum(dist, 0)
    max_exact = REL_BUCKETS // 2
    large = max_exact + (jnp.log(jnp.maximum(n, 1).astype(F32) / max_exact)
                         / math.log(REL_MAX_DIST / max_exact) * (REL_BUCKETS - max_exact)).astype(jnp.int32)
    large = jnp.minimum(large, REL_BUCKETS - 1)
    return jnp.where(n < max_exact, n, large)


def _bias_tiles(table, tile):
    a = jnp.arange(tile, dtype=jnp.int32)
    d0 = a[:, None] - a[None, :]
    far = table[REL_BUCKETS - 1].astype(F32)
    f0 = jnp.transpose(table[_t5_bucket(d0)].astype(F32) - far, (2, 0, 1))
    f1 = jnp.transpose(table[_t5_bucket(d0 + tile)].astype(F32) - far, (2, 0, 1))
    f0 = jnp.where((d0 >= 0)[None], f0, NEG)
    return jnp.stack([f0, f1], axis=1)


def _rope_tables(positions):
    half = MLA_ROPE_DIM // 2
    inv_freq = ROPE_THETA ** (-jnp.arange(half, dtype=F32) / half)
    ang = positions.astype(F32)[:, :, None] * inv_freq
    c, s = jnp.cos(ang), jnp.sin(ang)
    b, sq = positions.shape
    ones = jnp.ones((b, sq, MLA_NOPE_DIM), F32)
    z_nope = jnp.zeros((b, sq, MLA_NOPE_DIM), F32)
    z_pad = jnp.zeros((b, sq, HEAD_LANES - MLA_NOPE_DIM - MLA_ROPE_DIM), F32)
    cos_t = jnp.concatenate([ones, c, c, z_pad], axis=-1).reshape(b * sq, HEAD_LANES)
    sin_t = jnp.concatenate([z_nope, s, s, z_pad], axis=-1).reshape(b * sq, HEAD_LANES)
    return cos_t, sin_t


def _rot_cols(w):
    half = MLA_ROPE_DIM // 2
    return jnp.concatenate([-w[..., half:], w[..., :half]], axis=-1)


def _mixer_weights(w_in, w_uq, w_ukv, w_branch_mla):
    d = D_MODEL
    hq = DIFF_HEADS * 2 * DIFF_HEAD_DIM
    o = 0
    w_dq = w_in[:, o:o + hq] * (DIFF_HEAD_DIM ** -0.5); o += hq
    w_dk = w_in[:, o:o + hq]; o += hq
    w_dv = w_in[:, o:o + hq]; o += hq
    w_cq = w_in[:, o:o + MLA_Q_RANK]; o += MLA_Q_RANK
    w_ckv = w_in[:, o:o + MLA_KV_RANK]; o += MLA_KV_RANK
    w_kr = w_in[:, o:o + MLA_ROPE_DIM]; o += MLA_ROPE_DIM
    w_g = w_in[:, o:]
    w_qkv = jnp.concatenate([w_dq, w_dk, w_dv], axis=1).astype(BF16)
    pad = HEAD_LANES - MLA_NOPE_DIM - MLA_ROPE_DIM
    z = lambda n: jnp.zeros((d, n), F32)
    w_c = jnp.concatenate([w_cq, w_ckv, z(MLA_NOPE_DIM), w_kr, z(pad), z(MLA_NOPE_DIM), _rot_cols(w_kr), z(pad)],
                          axis=1).astype(BF16)
    wq = w_uq.reshape(MLA_Q_RANK, MLA_HEADS, MLA_NOPE_DIM + MLA_ROPE_DIM)
    nope, rope = wq[..., :MLA_NOPE_DIM], wq[..., MLA_NOPE_DIM:]
    zq = lambda n: jnp.zeros((MLA_Q_RANK, MLA_HEADS, n), F32)
    plain = jnp.concatenate([nope, rope, zq(pad)], axis=-1).reshape(MLA_Q_RANK, -1)
    rot = jnp.concatenate([zq(MLA_NOPE_DIM), _rot_cols(rope), zq(pad)], axis=-1).reshape(MLA_Q_RANK, -1)
    w_q2 = jnp.concatenate([plain, rot], axis=1).astype(BF16)
    wkv = w_ukv.reshape(MLA_KV_RANK, MLA_HEADS, MLA_NOPE_DIM + MLA_V_DIM)
    zk = jnp.zeros((MLA_KV_RANK, MLA_HEADS, HEAD_LANES - MLA_NOPE_DIM), F32)
    zv = jnp.zeros((MLA_KV_RANK, MLA_HEADS, HEAD_LANES - MLA_V_DIM), F32)
    w_kv2 = jnp.concatenate([
        jnp.concatenate([wkv[..., :MLA_NOPE_DIM], zk], axis=-1).reshape(MLA_KV_RANK, -1),
        jnp.concatenate([wkv[..., MLA_NOPE_DIM:], zv], axis=-1).reshape(MLA_KV_RANK, -1)], axis=1).astype(BF16)
    wbm = w_branch_mla.reshape(MLA_HEADS, MLA_V_DIM, d)
    w_bm = jnp.concatenate([wbm, jnp.zeros((MLA_HEADS, HEAD_LANES - MLA_V_DIM, d), F32)],
                           axis=1).reshape(MLA_HEADS * HEAD_LANES, d).astype(BF16)
    return w_qkv, w_c, w_g.astype(BF16), w_q2, w_kv2, w_bm


def kernel(x, p, positions, rel_bias_table, w_in, b_gate, lambda_q1, lambda_k1, lambda_q2, lambda_k2,
           diff_subln_g, mla_q_norm_g, w_uq, mla_kv_norm_g, w_ukv, w_branch_diff, w_branch_mla, w_out,
           ln_mix_g, ln_mix_b, dense_w1, dense_w3, dense_w2, router_w, expert_w1, expert_w3, expert_w2,
           w_ple_gate, w_ple_proj, ln_ffn_g, ln_ffn_b):
    batch, seq, d = x.shape
    depth = w_in.shape[0]
    t = batch * seq
    tm = min(ROW_TILE, t)
    alpha = (2.0 * depth) ** 0.25
    row = lambda v: v.reshape(1, -1).astype(F32)

    cos_t, sin_t = _rope_tables(positions)
    bias = _bias_tiles(rel_bias_table, ATT_TILE)
    xf = x.reshape(t, d)
    xb = xf.astype(BF16)

    for i in range(depth):
        lam_init = 0.8 - 0.6 * math.exp(-0.3 * i)
        w_qkv, w_c, w_g, w_q2, w_kv2, w_bm = _mixer_weights(w_in[i], w_uq[i], w_ukv[i], w_branch_mla[i])
        qkv = _matmul(xb, w_qkv, BF16, tm, 1024)
        cq, ckv, kr = _latent(xb, w_c, row(mla_q_norm_g[i]), row(mla_kv_norm_g[i]), cos_t, sin_t, tm)
        q_mla = _qup(cq, w_q2, cos_t, sin_t, tm)
        kv_mla = _matmul(ckv, w_kv2, BF16, tm, 1024)
        o_a = _diff_attention(qkv.reshape(batch, seq, -1), bias, row(lambda_q1[i]), row(lambda_k1[i]),
                              row(lambda_q2[i]), row(lambda_k2[i]), row(diff_subln_g[i]), lam_init, batch, seq)
        o_b = _mla_attention(q_mla.reshape(batch, seq, -1), kv_mla.reshape(batch, seq, -1),
                             kr.reshape(batch, seq, -1), batch, seq)
        mix = _merge(xb, o_a.reshape(t, -1), o_b.reshape(t, -1), w_g, row(b_gate[i]),
                     w_branch_diff[i].astype(BF16), w_bm, tm, 512)
        xf, xb = _out_ln(mix, w_out[i].astype(BF16), xf, row(ln_mix_g[i]), row(ln_mix_b[i]), alpha, tm)
        j = i // 2
        p_i = p[i].reshape(t, PLE_DIM)
        w_pg = w_ple_gate[i].astype(BF16)
        w_pp = w_ple_proj[i].astype(BF16)
        if i % 2 == 0:
            hmid = _swiglu_up(xb, dense_w1[j].astype(BF16), dense_w3[j].astype(BF16), tm, DENSE_FF // 2)
            xf, xb = _ffn_ln(hmid, dense_w2[j].astype(BF16), xf, xb, p_i, w_pg, w_pp,
                             row(ln_ffn_g[i]), row(ln_ffn_b[i]), alpha, tm, True)
        else:
            w_r = jnp.concatenate([router_w[j], jnp.zeros((d, HEAD_LANES - N_EXPERTS), F32)], axis=1)
            f = _moe(xf, xb, w_r, expert_w1[j].astype(BF16), expert_w3[j].astype(BF16),
                     expert_w2[j].astype(BF16), tm)
            xf, xb = _ffn_ln(f, jnp.zeros((8, HEAD_LANES), BF16), xf, xb, p_i, w_pg, w_pp,
                             row(ln_ffn_g[i]), row(ln_ffn_b[i]), alpha, tm, False)
    return xf.reshape(batch, seq, d)
```

```python
import functools
import math

import jax
import jax.numpy as jnp
from jax import lax
from jax.experimental import pallas as pl
from jax.experimental.pallas import tpu as pltpu

F32 = jnp.float32
BF16 = jnp.bfloat16

D_MODEL = 1024
PLE_DIM = 256
DIFF_HEADS = 8
DIFF_HEAD_DIM = 64
MLA_HEADS = 8
MLA_Q_RANK = 384
MLA_KV_RANK = 256
MLA_NOPE_DIM = 64
MLA_ROPE_DIM = 32
MLA_V_DIM = 64
ROPE_THETA = 10000.0
REL_BUCKETS = 32
REL_MAX_DIST = 128
DENSE_FF = 2816
N_EXPERTS = 8
TOP_K = 2
EXPERT_FF = 3584
EPS = 1e-5

HEAD_LANES = 128
NEG = -1e30
LOG2E = math.log2(math.e)
ATT_TILE = 256
ROW_TILE = 512
GROUP_ROWS = 512
VMEM_LIMIT = 52 * 1024 * 1024


def _cp(sem):
    return pltpu.CompilerParams(dimension_semantics=sem, vmem_limit_bytes=VMEM_LIMIT)


def _dot(a, b):
    return jnp.dot(a, b, preferred_element_type=F32)


def _sigmoid(x):
    return 1.0 / (1.0 + jnp.exp(-x))


def _layernorm(r, g, b):
    mu = jnp.mean(r, axis=-1, keepdims=True)
    d = r - mu
    var = jnp.mean(d * d, axis=-1, keepdims=True)
    return d * lax.rsqrt(var + EPS) * g + b


def _rmsnorm(x, g):
    return x * lax.rsqrt(jnp.mean(x * x, axis=-1, keepdims=True) + EPS) * g


def _mm_kernel(a_ref, w_ref, o_ref, *, scaled_tiles, scale):
    acc = _dot(a_ref[...], w_ref[...])
    if scaled_tiles:
        acc = acc * jnp.where(pl.program_id(1) < scaled_tiles, scale, 1.0)
    o_ref[...] = acc.astype(o_ref.dtype)


def _matmul(a, w, out_dtype, tm, tn, scaled_tiles=0, scale=1.0):
    m, k = a.shape
    n = w.shape[1]
    return pl.pallas_call(
        functools.partial(_mm_kernel, scaled_tiles=scaled_tiles, scale=scale),
        grid=(m // tm, n // tn),
        in_specs=[pl.BlockSpec((tm, k), lambda i, j: (i, 0)),
                  pl.BlockSpec((k, tn), lambda i, j: (0, j))],
        out_specs=pl.BlockSpec((tm, tn), lambda i, j: (i, j)),
        out_shape=jax.ShapeDtypeStruct((m, n), out_dtype),
        compiler_params=_cp(("parallel", "arbitrary")),
    )(a, w)


C_COLS = MLA_Q_RANK + MLA_KV_RANK + 2 * HEAD_LANES


def _latent_kernel(a_ref, w_ref, gq_ref, gkv_ref, cos_ref, sin_ref, cq_ref, ckv_ref, kr_ref):
    z = _dot(a_ref[...], w_ref[...])
    cq = z[:, :MLA_Q_RANK]
    ckv = z[:, MLA_Q_RANK:MLA_Q_RANK + MLA_KV_RANK]
    kr = z[:, MLA_Q_RANK + MLA_KV_RANK:MLA_Q_RANK + MLA_KV_RANK + HEAD_LANES]
    kr_rot = z[:, MLA_Q_RANK + MLA_KV_RANK + HEAD_LANES:]
    cq_ref[...] = _rmsnorm(cq, gq_ref[...]).astype(BF16)
    ckv_ref[...] = _rmsnorm(ckv, gkv_ref[...]).astype(BF16)
    kr_ref[...] = (kr * cos_ref[...] + kr_rot * sin_ref[...]).astype(BF16)


def _latent(xb, w_c, gq, gkv, cos_t, sin_t, tm):
    t = xb.shape[0]
    row = lambda n: pl.BlockSpec((tm, n), lambda i: (i, 0))
    full = lambda a: pl.BlockSpec(a.shape, lambda i: (0, 0))
    return pl.pallas_call(
        _latent_kernel,
        grid=(t // tm,),
        in_specs=[row(D_MODEL), full(w_c), full(gq), full(gkv), row(HEAD_LANES), row(HEAD_LANES)],
        out_specs=[row(MLA_Q_RANK), row(MLA_KV_RANK), row(HEAD_LANES)],
        out_shape=[jax.ShapeDtypeStruct((t, MLA_Q_RANK), BF16),
                   jax.ShapeDtypeStruct((t, MLA_KV_RANK), BF16),
                   jax.ShapeDtypeStruct((t, HEAD_LANES), BF16)],
        compiler_params=_cp(("parallel",)),
    )(xb, w_c, gq, gkv, cos_t, sin_t)


def _qup_kernel(a_ref, w_ref, cos_ref, sin_ref, o_ref, *, scale):
    z = _dot(a_ref[...], w_ref[...])
    c = cos_ref[...] * scale
    s = sin_ref[...] * scale
    hl = HEAD_LANES
    for h in range(MLA_HEADS):
        o_ref[:, h * hl:(h + 1) * hl] = (
            z[:, h * hl:(h + 1) * hl] * c
            + z[:, (MLA_HEADS + h) * hl:(MLA_HEADS + h + 1) * hl] * s).astype(BF16)


def _qup(cq, w_q2, cos_t, sin_t, tm):
    t = cq.shape[0]
    n = MLA_HEADS * HEAD_LANES
    scale = (MLA_NOPE_DIM + MLA_ROPE_DIM) ** -0.5 * LOG2E
    return pl.pallas_call(
        functools.partial(_qup_kernel, scale=scale),
        grid=(t // tm,),
        in_specs=[pl.BlockSpec((tm, MLA_Q_RANK), lambda i: (i, 0)),
                  pl.BlockSpec(w_q2.shape, lambda i: (0, 0)),
                  pl.BlockSpec((tm, HEAD_LANES), lambda i: (i, 0)),
                  pl.BlockSpec((tm, HEAD_LANES), lambda i: (i, 0))],
        out_specs=pl.BlockSpec((tm, n), lambda i: (i, 0)),
        out_shape=jax.ShapeDtypeStruct((t, n), BF16),
        compiler_params=_cp(("parallel",)),
    )(cq, w_q2, cos_t, sin_t)


ONES_ROWS = 16


def _scores_t(k, q):
    return lax.dot_general(k, q, (((1,), (1,)), ((), ())), preferred_element_type=F32)


def _softmax_step_t(s, vt, m_ref, acc_ref):
    m_old = m_ref[...]
    m_new = jnp.maximum(m_old, jnp.max(s, axis=0, keepdims=True))
    p = jnp.exp2(s - m_new).astype(BF16)
    alpha = jnp.exp2(m_old - m_new)
    acc_ref[...] = alpha * acc_ref[...] + _dot(vt, p)
    m_ref[...] = m_new


def _fill_vt(v_ref, vt_scr, v_rows, tile):
    for j in range(vt_scr.shape[0]):
        vt = v_ref[0, j * tile:(j + 1) * tile, :].astype(F32).T
        vt_scr[j, 0:v_rows, :] = vt[0:v_rows].astype(BF16)
        vt_scr[j, v_rows:v_rows + ONES_ROWS, :] = jnp.ones((ONES_ROWS, tile), BF16)


def _normalised(acc, v_rows):
    return acc[0:v_rows] * (1.0 / acc[v_rows:v_rows + 1])


def _diff_attn_kernel(q_ref, k_ref, v_ref, bias_ref, lq1_ref, lk1_ref, lq2_ref, lk2_ref, g_ref, o_ref,
                      vt_scr, m_scr, acc_scr, *, lam_init, tile):
    qi = pl.program_id(2)
    hd = 2 * DIFF_HEAD_DIM

    @pl.when(qi == 0)
    def _():
        _fill_vt(v_ref, vt_scr, hd, tile)

    q = q_ref[0]
    lane = lax.broadcasted_iota(jnp.int32, q.shape, 1)
    zero = jnp.zeros_like(q)
    q_cat = jnp.concatenate([jnp.where(lane < DIFF_HEAD_DIM, q, zero),
                             jnp.where(lane >= DIFF_HEAD_DIM, q, zero)], axis=0)
    m_scr[...] = jnp.full(m_scr.shape, NEG, F32)
    acc_scr[...] = jnp.zeros(acc_scr.shape, F32)

    def scores(kj):
        return _scores_t(k_ref[0, pl.ds(pl.multiple_of(kj * tile, tile), tile), :], q_cat)

    def update(s, kj):
        _softmax_step_t(s, vt_scr[kj], m_scr, acc_scr)

    def far_body(kj, s):
        s_next = scores(kj + 1)
        update(s, kj)
        return s_next

    s = lax.fori_loop(0, jnp.maximum(qi - 1, 0), far_body, scores(0))

    @pl.when(qi >= 1)
    def _():
        s_diag = scores(qi)
        update(s + bias_ref[0, 1], qi - 1)
        update(s_diag + bias_ref[0, 0], qi)

    @pl.when(qi == 0)
    def _():
        update(s + bias_ref[0, 0], 0)

    lam = (jnp.exp(jnp.sum(lq1_ref[...] * lk1_ref[...], axis=-1, keepdims=True))
           - jnp.exp(jnp.sum(lq2_ref[...] * lk2_ref[...], axis=-1, keepdims=True)) + lam_init)
    acc = acc_scr[...]
    o = _normalised(acc[:, :tile], hd) - lam * _normalised(acc[:, tile:], hd)
    o = o * lax.rsqrt(jnp.mean(o * o, axis=0, keepdims=True) + EPS)
    o_ref[0] = (o.T * (g_ref[...] * (1.0 - lam_init))).astype(BF16)


def _diff_attention(qkv, bias, lq1, lk1, lq2, lk2, g, lam_init, batch, seq):
    tile = ATT_TILE
    hl = HEAD_LANES
    nh = DIFF_HEADS
    vec = lambda a: pl.BlockSpec(a.shape, lambda b, h, i: (0, 0))
    return pl.pallas_call(
        functools.partial(_diff_attn_kernel, lam_init=lam_init, tile=tile),
        grid=(batch, nh, seq // tile),
        in_specs=[pl.BlockSpec((1, tile, hl), lambda b, h, i: (b, i, h)),
                  pl.BlockSpec((1, seq, hl), lambda b, h, i: (b, 0, nh + h)),
                  pl.BlockSpec((1, seq, hl), lambda b, h, i: (b, 0, 2 * nh + h)),
                  pl.BlockSpec((1, 2, tile, 2 * tile), lambda b, h, i: (h, 0, 0, 0)),
                  vec(lq1), vec(lk1), vec(lq2), vec(lk2), vec(g)],
        out_specs=pl.BlockSpec((1, tile, hl), lambda b, h, i: (b, i, h)),
        out_shape=jax.ShapeDtypeStruct((batch, seq, nh * hl), BF16),
        scratch_shapes=[pltpu.VMEM((seq // tile, hl + ONES_ROWS, tile), BF16),
                        pltpu.VMEM((1, 2 * tile), F32),
                        pltpu.VMEM((hl + ONES_ROWS, 2 * tile), F32)],
        compiler_params=_cp(("parallel", "parallel", "arbitrary")),
    )(qkv, qkv, qkv, bias, lq1, lk1, lq2, lk2, g)


def _mla_attn_kernel(q_ref, k_ref, kr_ref, v_ref, o_ref, kf_scr, vt_scr, m_scr, acc_scr, *, tile):
    qi = pl.program_id(2)

    @pl.when(qi == 0)
    def _():
        kf_scr[...] = k_ref[0] + kr_ref[0]
        _fill_vt(v_ref, vt_scr, MLA_V_DIM, tile)

    q = q_ref[0]
    m_scr[...] = jnp.full(m_scr.shape, NEG, F32)
    acc_scr[...] = jnp.zeros(acc_scr.shape, F32)

    def scores(kj):
        return _scores_t(kf_scr[pl.ds(pl.multiple_of(kj * tile, tile), tile), :], q)

    def body(kj, s):
        s_next = scores(kj + 1)
        _softmax_step_t(s, vt_scr[kj], m_scr, acc_scr)
        return s_next

    s = lax.fori_loop(0, 2 * qi, body, scores(0))
    s_last = scores(2 * qi + 1)
    key = lax.broadcasted_iota(jnp.int32, s.shape, 0)
    qry = lax.broadcasted_iota(jnp.int32, s.shape, 1)
    _softmax_step_t(jnp.where(key <= qry, s, NEG), vt_scr[2 * qi], m_scr, acc_scr)
    _softmax_step_t(jnp.where(key + tile <= qry, s_last, NEG), vt_scr[2 * qi + 1], m_scr, acc_scr)
    o = _normalised(acc_scr[...], MLA_V_DIM)
    o_ref[0] = jnp.concatenate([o, jnp.zeros_like(o)], axis=0).T.astype(BF16)


def _mla_attention(q, kv, kr, batch, seq):
    tile = ATT_TILE
    tq = 2 * tile
    hl = HEAD_LANES
    nh = MLA_HEADS
    return pl.pallas_call(
        functools.partial(_mla_attn_kernel, tile=tile),
        grid=(batch, nh, seq // tq),
        in_specs=[pl.BlockSpec((1, tq, hl), lambda b, h, i: (b, i, h)),
                  pl.BlockSpec((1, seq, hl), lambda b, h, i: (b, 0, h)),
                  pl.BlockSpec((1, seq, hl), lambda b, h, i: (b, 0, 0)),
                  pl.BlockSpec((1, seq, hl), lambda b, h, i: (b, 0, nh + h))],
        out_specs=pl.BlockSpec((1, tq, hl), lambda b, h, i: (b, i, h)),
        out_shape=jax.ShapeDtypeStruct((batch, seq, nh * hl), BF16),
        scratch_shapes=[pltpu.VMEM((seq, hl), BF16),
                        pltpu.VMEM((seq // tile, MLA_V_DIM + ONES_ROWS, tile), BF16),
                        pltpu.VMEM((1, tq), F32),
                        pltpu.VMEM((MLA_V_DIM + ONES_ROWS, tq), F32)],
        compiler_params=_cp(("parallel", "parallel", "arbitrary")),
    )(q, kv, kr, kv)


def _merge_kernel(xb_ref, oa_ref, ob_ref, wga_ref, wgb_ref, bga_ref, bgb_ref, wbd_ref, wbm_ref, o_ref):
    xb = xb_ref[...]
    g_a = _sigmoid(_dot(xb, wga_ref[...]) + bga_ref[...])
    g_b = _sigmoid(_dot(xb, wgb_ref[...]) + bgb_ref[...])
    m = g_a * _dot(oa_ref[...], wbd_ref[...]) + g_b * _dot(ob_ref[...], wbm_ref[...])
    o_ref[...] = m.astype(BF16)


def _merge(xb, o_a, o_b, w_g, b_g, w_bd, w_bm, tm, tn):
    t = xb.shape[0]
    nj = D_MODEL // tn
    row = pl.BlockSpec((tm, D_MODEL), lambda i, j: (i, 0))
    col = lambda off: pl.BlockSpec((D_MODEL, tn), lambda i, j: (0, j + off))
    vec = lambda off: pl.BlockSpec((1, tn), lambda i, j: (0, j + off))
    return pl.pallas_call(
        _merge_kernel,
        grid=(t // tm, nj),
        in_specs=[row, row, row, col(0), col(nj), vec(0), vec(nj), col(0), col(0)],
        out_specs=pl.BlockSpec((tm, tn), lambda i, j: (i, j)),
        out_shape=jax.ShapeDtypeStruct((t, D_MODEL), BF16),
        compiler_params=_cp(("parallel", "arbitrary")),
    )(xb, o_a, o_b, w_g, w_g, b_g, b_g, w_bd, w_bm)


def _out_ln_kernel(m_ref, w_ref, x_ref, g_ref, b_ref, xo_ref, xbo_ref, *, alpha):
    r = alpha * x_ref[...] + _dot(m_ref[...], w_ref[...])
    y = _layernorm(r, g_ref[...], b_ref[...])
    xo_ref[...] = y
    xbo_ref[...] = y.astype(BF16)


def _out_ln(m, w_out, x, g, b, alpha, tm):
    t = x.shape[0]
    row = pl.BlockSpec((tm, D_MODEL), lambda i: (i, 0))
    full = lambda a: pl.BlockSpec(a.shape, lambda i: (0, 0))
    return pl.pallas_call(
        functools.partial(_out_ln_kernel, alpha=alpha),
        grid=(t // tm,),
        in_specs=[row, full(w_out), row, full(g), full(b)],
        out_specs=[row, row],
        out_shape=[jax.ShapeDtypeStruct((t, D_MODEL), F32), jax.ShapeDtypeStruct((t, D_MODEL), BF16)],
        compiler_params=_cp(("parallel",)),
    )(m, w_out, x, g, b)


def _swiglu_up_kernel(a_ref, w1_ref, w3_ref, o_ref):
    a = a_ref[...]
    u = _dot(a, w1_ref[...])
    o_ref[...] = (u * _sigmoid(u) * _dot(a, w3_ref[...])).astype(BF16)


def _swiglu_up(xb, w1, w3, tm, tn):
    t = xb.shape[0]
    ff = w1.shape[1]
    return pl.pallas_call(
        _swiglu_up_kernel,
        grid=(t // tm, ff // tn),
        in_specs=[pl.BlockSpec((tm, D_MODEL), lambda i, j: (i, 0)),
                  pl.BlockSpec((D_MODEL, tn), lambda i, j: (0, j)),
                  pl.BlockSpec((D_MODEL, tn), lambda i, j: (0, j))],
        out_specs=pl.BlockSpec((tm, tn), lambda i, j: (i, j)),
        out_shape=jax.ShapeDtypeStruct((t, ff), BF16),
        compiler_params=_cp(("parallel", "arbitrary")),
    )(xb, w1, w3)


def _ffn_ln_kernel(f_ref, w2_ref, x_ref, xb_ref, p_ref, wpg_ref, wpp_ref, g_ref, b_ref, xo_ref, xbo_ref, *,
                   alpha, project):
    f = _dot(f_ref[...], w2_ref[...]) if project else f_ref[...]
    e = _sigmoid(_dot(xb_ref[...], wpg_ref[...])) * _dot(p_ref[...].astype(BF16), wpp_ref[...])
    y = _layernorm(alpha * x_ref[...] + f + e, g_ref[...], b_ref[...])
    xo_ref[...] = y
    xbo_ref[...] = y.astype(BF16)


def _ffn_ln(f, w2, x, xb, p, w_pg, w_pp, g, b, alpha, tm, project):
    t = x.shape[0]
    row = lambda n: pl.BlockSpec((tm, n), lambda i: (i, 0))
    full = lambda a: pl.BlockSpec(a.shape, lambda i: (0, 0))
    return pl.pallas_call(
        functools.partial(_ffn_ln_kernel, alpha=alpha, project=project),
        grid=(t // tm,),
        in_specs=[row(f.shape[1]), full(w2), row(D_MODEL), row(D_MODEL), row(PLE_DIM), full(w_pg), full(w_pp),
                  full(g), full(b)],
        out_specs=[row(D_MODEL), row(D_MODEL)],
        out_shape=[jax.ShapeDtypeStruct((t, D_MODEL), F32), jax.ShapeDtypeStruct((t, D_MODEL), BF16)],
        compiler_params=_cp(("parallel",)),
    )(f, w2, x, xb, p, w_pg, w_pp, g, b)


def _router_kernel(x_ref, w_ref, o_ref):
    logits = jnp.dot(x_ref[...], w_ref[...], preferred_element_type=F32, precision=lax.Precision.HIGHEST)
    lane = lax.broadcasted_iota(jnp.int32, logits.shape, 1)
    lg = jnp.where(lane < N_EXPERTS, logits, -jnp.inf)
    v1 = jnp.max(lg, axis=-1, keepdims=True)
    i1 = jnp.min(jnp.where(lg == v1, lane, HEAD_LANES), axis=-1, keepdims=True)
    lg2 = jnp.where(lane == i1, -jnp.inf, lg)
    v2 = jnp.max(lg2, axis=-1, keepdims=True)
    i2 = jnp.min(jnp.where(lg2 == v2, lane, HEAD_LANES), axis=-1, keepdims=True)
    e2 = jnp.exp(v2 - v1)
    g1 = 1.0 / (1.0 + e2)
    g2 = e2 / (1.0 + e2)
    o_ref[...] = jnp.where(lane == 0, i1.astype(F32),
                           jnp.where(lane == 1, i2.astype(F32),
                                     jnp.where(lane == 2, g1, jnp.where(lane == 3, g2, 0.0))))


def _router(x, w_r, tm):
    t = x.shape[0]
    return pl.pallas_call(
        _router_kernel,
        grid=(t // tm,),
        in_specs=[pl.BlockSpec((tm, D_MODEL), lambda i: (i, 0)),
                  pl.BlockSpec(w_r.shape, lambda i: (0, 0))],
        out_specs=pl.BlockSpec((tm, HEAD_LANES), lambda i: (i, 0)),
        out_shape=jax.ShapeDtypeStruct((t, HEAD_LANES), F32),
        compiler_params=_cp(("parallel",)),
    )(x, w_r)


def _expert_up_kernel(be_ref, nb_ref, a_ref, w1_ref, w3_ref, o_ref):
    @pl.when(pl.program_id(0) < nb_ref[0])
    def _():
        a = a_ref[...]
        u = _dot(a, w1_ref[0])
        o_ref[...] = (u * _sigmoid(u) * _dot(a, w3_ref[0])).astype(BF16)

    @pl.when(pl.program_id(0) >= nb_ref[0])
    def _():
        o_ref[...] = jnp.zeros(o_ref.shape, BF16)


def _expert_up(a, w1, w3, blk_e, n_used, tm, tn):
    rows = a.shape[0]
    ff = w1.shape[2]
    grid_spec = pltpu.PrefetchScalarGridSpec(
        num_scalar_prefetch=2,
        grid=(rows // tm, ff // tn),
        in_specs=[pl.BlockSpec((tm, D_MODEL), lambda i, j, be, nb: (i, 0)),
                  pl.BlockSpec((1, D_MODEL, tn), lambda i, j, be, nb: (be[i], 0, j)),
                  pl.BlockSpec((1, D_MODEL, tn), lambda i, j, be, nb: (be[i], 0, j))],
        out_specs=pl.BlockSpec((tm, tn), lambda i, j, be, nb: (i, j)),
    )
    return pl.pallas_call(
        _expert_up_kernel,
        grid_spec=grid_spec,
        out_shape=jax.ShapeDtypeStruct((rows, ff), BF16),
        compiler_params=_cp(("parallel", "arbitrary")),
    )(blk_e, n_used, a, w1, w3)


def _expert_down_kernel(be_ref, nb_ref, h_ref, w2_ref, o_ref):
    @pl.when(pl.program_id(0) < nb_ref[0])
    def _():
        o_ref[...] = _dot(h_ref[...], w2_ref[0])

    @pl.when(pl.program_id(0) >= nb_ref[0])
    def _():
        o_ref[...] = jnp.zeros(o_ref.shape, F32)


def _expert_down(h, w2, blk_e, n_used, tm):
    rows, ff = h.shape
    grid_spec = pltpu.PrefetchScalarGridSpec(
        num_scalar_prefetch=2,
        grid=(rows // tm,),
        in_specs=[pl.BlockSpec((tm, ff), lambda i, be, nb: (i, 0)),
                  pl.BlockSpec((1, ff, D_MODEL), lambda i, be, nb: (be[i], 0, 0))],
        out_specs=pl.BlockSpec((tm, D_MODEL), lambda i, be, nb: (i, 0)),
    )
    return pl.pallas_call(
        _expert_down_kernel,
        grid_spec=grid_spec,
        out_shape=jax.ShapeDtypeStruct((rows, D_MODEL), F32),
        compiler_params=_cp(("parallel",)),
    )(blk_e, n_used, h, w2)


def _moe(x, xb, w_r, w1, w3, w2, tm):
    t = x.shape[0]
    m = t * TOP_K
    gb = GROUP_ROWS
    routed = _router(x, w_r, tm)
    flat_e = routed[:, :TOP_K].astype(jnp.int32).reshape(-1)
    gates = routed[:, TOP_K:2 * TOP_K]
    onehot = (flat_e[:, None] == jnp.arange(N_EXPERTS, dtype=jnp.int32)[None, :]).astype(jnp.int32)
    csum = jnp.cumsum(onehot, axis=0)
    counts = csum[-1]
    rank = jnp.take_along_axis(csum, flat_e[:, None], axis=1)[:, 0] - 1
    padded = ((counts + gb - 1) // gb) * gb
    pend = jnp.cumsum(padded)
    dest = (pend - padded)[flat_e] + rank
    n_blocks = m // gb + N_EXPERTS
    src_tok = jnp.zeros((n_blocks * gb,), jnp.int32).at[dest].set(jnp.arange(m, dtype=jnp.int32) // TOP_K)
    blk_e = jnp.minimum(jnp.searchsorted(pend, jnp.arange(n_blocks, dtype=jnp.int32) * gb, side='right'),
                        N_EXPERTS - 1).astype(jnp.int32)
    n_used = (pend[-1:] // gb).astype(jnp.int32)
    a = jnp.take(xb, src_tok, axis=0)
    h = _expert_up(a, w1, w3, blk_e, n_used, gb, EXPERT_FF // 2)
    yb = _expert_down(h, w2, blk_e, n_used, gb)
    dest2 = dest.reshape(t, TOP_K)
    return (jnp.take(yb, dest2[:, 0], axis=0) * gates[:, 0:1]
            + jnp.take(yb, dest2[:, 1], axis=0) * gates[:, 1:2])


def _t5_bucket(dist):
    n = jnp.maximum(dist, 0)
    max_exact = REL_BUCKETS // 2
    large = max_exact + (jnp.log(jnp.maximum(n, 1).astype(F32) / max_exact)
                         / math.log(REL_MAX_DIST / max_exact) * (REL_BUCKETS - max_exact)).astype(jnp.int32)
    large = jnp.minimum(large, REL_BUCKETS - 1)
    return jnp.where(n < max_exact, n, large)


def _bias_tiles(table, tile):
    a = jnp.arange(tile, dtype=jnp.int32)
    d0 = a[None, :] - a[:, None]
    far = table[REL_BUCKETS - 1].astype(F32)
    f0 = jnp.transpose(table[_t5_bucket(d0)].astype(F32) - far, (2, 0, 1))
    f1 = jnp.transpose(table[_t5_bucket(d0 + tile)].astype(F32) - far, (2, 0, 1))
    f0 = jnp.where((d0 >= 0)[None], f0, NEG)
    tiles = jnp.stack([f0, f1], axis=1)
    return jnp.concatenate([tiles, tiles], axis=-1)


def _rope_tables(positions):
    half = MLA_ROPE_DIM // 2
    inv_freq = ROPE_THETA ** (-jnp.arange(half, dtype=F32) / half)
    ang = positions.astype(F32)[:, :, None] * inv_freq
    c, s = jnp.cos(ang), jnp.sin(ang)
    b, sq = positions.shape
    ones = jnp.ones((b, sq, MLA_NOPE_DIM), F32)
    z_nope = jnp.zeros((b, sq, MLA_NOPE_DIM), F32)
    z_pad = jnp.zeros((b, sq, HEAD_LANES - MLA_NOPE_DIM - MLA_ROPE_DIM), F32)
    cos_t = jnp.concatenate([ones, c, c, z_pad], axis=-1).reshape(b * sq, HEAD_LANES)
    sin_t = jnp.concatenate([z_nope, s, s, z_pad], axis=-1).reshape(b * sq, HEAD_LANES)
    return cos_t, sin_t


def _rot_cols(w):
    half = MLA_ROPE_DIM // 2
    return jnp.concatenate([-w[..., half:], w[..., :half]], axis=-1)


def _mixer_weights(w_in, w_uq, w_ukv, w_branch_mla):
    d = D_MODEL
    hq = DIFF_HEADS * 2 * DIFF_HEAD_DIM
    o = 0
    w_dq = w_in[:, o:o + hq]; o += hq
    w_dk = w_in[:, o:o + hq]; o += hq
    w_dv = w_in[:, o:o + hq]; o += hq
    w_cq = w_in[:, o:o + MLA_Q_RANK]; o += MLA_Q_RANK
    w_ckv = w_in[:, o:o + MLA_KV_RANK]; o += MLA_KV_RANK
    w_kr = w_in[:, o:o + MLA_ROPE_DIM]; o += MLA_ROPE_DIM
    w_g = w_in[:, o:]
    w_qkv = jnp.concatenate([w_dq, w_dk, w_dv], axis=1).astype(BF16)
    pad = HEAD_LANES - MLA_NOPE_DIM - MLA_ROPE_DIM
    z = lambda n: jnp.zeros((d, n), F32)
    w_c = jnp.concatenate([w_cq, w_ckv, z(MLA_NOPE_DIM), w_kr, z(pad), z(MLA_NOPE_DIM), _rot_cols(w_kr), z(pad)],
                          axis=1).astype(BF16)
    wq = w_uq.reshape(MLA_Q_RANK, MLA_HEADS, MLA_NOPE_DIM + MLA_ROPE_DIM)
    nope, rope = wq[..., :MLA_NOPE_DIM], wq[..., MLA_NOPE_DIM:]
    zq = lambda n: jnp.zeros((MLA_Q_RANK, MLA_HEADS, n), F32)
    plain = jnp.concatenate([nope, rope, zq(pad)], axis=-1).reshape(MLA_Q_RANK, -1)
    rot = jnp.concatenate([zq(MLA_NOPE_DIM), _rot_cols(rope), zq(pad)], axis=-1).reshape(MLA_Q_RANK, -1)
    w_q2 = jnp.concatenate([plain, rot], axis=1).astype(BF16)
    wkv = w_ukv.reshape(MLA_KV_RANK, MLA_HEADS, MLA_NOPE_DIM + MLA_V_DIM)
    zk = jnp.zeros((MLA_KV_RANK, MLA_HEADS, HEAD_LANES - MLA_NOPE_DIM), F32)
    zv = jnp.zeros((MLA_KV_RANK, MLA_HEADS, HEAD_LANES - MLA_V_DIM), F32)
    w_kv2 = jnp.concatenate([
        jnp.concatenate([wkv[..., :MLA_NOPE_DIM], zk], axis=-1).reshape(MLA_KV_RANK, -1),
        jnp.concatenate([wkv[..., MLA_NOPE_DIM:], zv], axis=-1).reshape(MLA_KV_RANK, -1)], axis=1).astype(BF16)
    wbm = w_branch_mla.reshape(MLA_HEADS, MLA_V_DIM, d)
    w_bm = jnp.concatenate([wbm, jnp.zeros((MLA_HEADS, HEAD_LANES - MLA_V_DIM, d), F32)],
                           axis=1).reshape(MLA_HEADS * HEAD_LANES, d).astype(BF16)
    return w_qkv, w_c, w_g.astype(BF16), w_q2, w_kv2, w_bm


def kernel(x, p, positions, rel_bias_table, w_in, b_gate, lambda_q1, lambda_k1, lambda_q2, lambda_k2,
           diff_subln_g, mla_q_norm_g, w_uq, mla_kv_norm_g, w_ukv, w_branch_diff, w_branch_mla, w_out,
           ln_mix_g, ln_mix_b, dense_w1, dense_w3, dense_w2, router_w, expert_w1, expert_w3, expert_w2,
           w_ple_gate, w_ple_proj, ln_ffn_g, ln_ffn_b):
    batch, seq, d = x.shape
    depth = w_in.shape[0]
    t = batch * seq
    tm = min(ROW_TILE, t)
    alpha = (2.0 * depth) ** 0.25
    row = lambda v: v.reshape(1, -1).astype(F32)

    cos_t, sin_t = _rope_tables(positions)
    bias = _bias_tiles(rel_bias_table * LOG2E, ATT_TILE)
    xf = x.reshape(t, d)
    xb = xf.astype(BF16)

    for i in range(depth):
        lam_init = 0.8 - 0.6 * math.exp(-0.3 * i)
        w_qkv, w_c, w_g, w_q2, w_kv2, w_bm = _mixer_weights(w_in[i], w_uq[i], w_ukv[i], w_branch_mla[i])
        qkv = _matmul(xb, w_qkv, BF16, tm, 1024, scaled_tiles=1, scale=DIFF_HEAD_DIM ** -0.5 * LOG2E)
        cq, ckv, kr = _latent(xb, w_c, row(mla_q_norm_g[i]), row(mla_kv_norm_g[i]), cos_t, sin_t, tm)
        q_mla = _qup(cq, w_q2, cos_t, sin_t, tm)
        kv_mla = _matmul(ckv, w_kv2, BF16, tm, 1024)
        o_a = _diff_attention(qkv.reshape(batch, seq, -1), bias, row(lambda_q1[i]), row(lambda_k1[i]),
                              row(lambda_q2[i]), row(lambda_k2[i]), row(diff_subln_g[i]), lam_init, batch, seq)
        o_b = _mla_attention(q_mla.reshape(batch, seq, -1), kv_mla.reshape(batch, seq, -1),
                             kr.reshape(batch, seq, -1), batch, seq)
        mix = _merge(xb, o_a.reshape(t, -1), o_b.reshape(t, -1), w_g, row(b_gate[i]),
                     w_branch_diff[i].astype(BF16), w_bm, tm, 512)
        xf, xb = _out_ln(mix, w_out[i].astype(BF16), xf, row(ln_mix_g[i]), row(ln_mix_b[i]), alpha, tm)
        j = i // 2
        p_i = p[i].reshape(t, PLE_DIM)
        w_pg = w_ple_gate[i].astype(BF16)
        w_pp = w_ple_proj[i].astype(BF16)
        if i % 2 == 0:
            hmid = _swiglu_up(xb, dense_w1[j].astype(BF16), dense_w3[j].astype(BF16), tm, DENSE_FF // 2)
            xf, xb = _ffn_ln(hmid, dense_w2[j].astype(BF16), xf, xb, p_i, w_pg, w_pp,
                             row(ln_ffn_g[i]), row(ln_ffn_b[i]), alpha, tm, True)
        else:
            w_r = jnp.concatenate([router_w[j], jnp.zeros((d, HEAD_LANES - N_EXPERTS), F32)], axis=1)
            f = _moe(xf, xb, w_r, expert_w1[j].astype(BF16), expert_w3[j].astype(BF16),
                     expert_w2[j].astype(BF16), tm)
            xf, xb = _ffn_ln(f, jnp.zeros((8, HEAD_LANES), BF16), xf, xb, p_i, w_pg, w_pp,
                             row(ln_ffn_g[i]), row(ln_ffn_b[i]), alpha, tm, False)
    return xf.reshape(batch, seq, d)
```

```python
import functools
import math

import jax
import jax.numpy as jnp
from jax import lax
from jax.experimental import pallas as pl
from jax.experimental.pallas import tpu as pltpu

F32 = jnp.float32
BF16 = jnp.bfloat16

D_MODEL = 1024
PLE_DIM = 256
DIFF_HEADS = 8
DIFF_HEAD_DIM = 64
MLA_HEADS = 8
MLA_Q_RANK = 384
MLA_KV_RANK = 256
MLA_NOPE_DIM = 64
MLA_ROPE_DIM = 32
MLA_V_DIM = 64
ROPE_THETA = 10000.0
REL_BUCKETS = 32
REL_MAX_DIST = 128
DENSE_FF = 2816
N_EXPERTS = 8
TOP_K = 2
EXPERT_FF = 3584
EPS = 1e-5

HEAD_LANES = 128
NEG = -1e30
LOG2E = math.log2(math.e)
ATT_TILE = 256
ATT_HEADS_PER_STEP = 2
ROW_TILE = 512
GROUP_ROWS = 512
VMEM_LIMIT = 52 * 1024 * 1024


def _cp(sem):
    return pltpu.CompilerParams(dimension_semantics=sem, vmem_limit_bytes=VMEM_LIMIT)


def _dot(a, b):
    return jnp.dot(a, b, preferred_element_type=F32)


def _sigmoid(x):
    return 1.0 / (1.0 + jnp.exp(-x))


def _layernorm(r, g, b):
    mu = jnp.mean(r, axis=-1, keepdims=True)
    d = r - mu
    var = jnp.mean(d * d, axis=-1, keepdims=True)
    return d * lax.rsqrt(var + EPS) * g + b


def _rmsnorm(x, g):
    return x * lax.rsqrt(jnp.mean(x * x, axis=-1, keepdims=True) + EPS) * g


def _mm_kernel(a_ref, w_ref, o_ref, *, scaled_tiles, scale):
    acc = _dot(a_ref[...], w_ref[...])
    if scaled_tiles:
        acc = acc * jnp.where(pl.program_id(1) < scaled_tiles, scale, 1.0)
    o_ref[...] = acc.astype(o_ref.dtype)


def _matmul(a, w, out_dtype, tm, tn, scaled_tiles=0, scale=1.0):
    m, k = a.shape
    n = w.shape[1]
    return pl.pallas_call(
        functools.partial(_mm_kernel, scaled_tiles=scaled_tiles, scale=scale),
        grid=(m // tm, n // tn),
        in_specs=[pl.BlockSpec((tm, k), lambda i, j: (i, 0)),
                  pl.BlockSpec((k, tn), lambda i, j: (0, j))],
        out_specs=pl.BlockSpec((tm, tn), lambda i, j: (i, j)),
        out_shape=jax.ShapeDtypeStruct((m, n), out_dtype),
        compiler_params=_cp(("parallel", "arbitrary")),
    )(a, w)


C_COLS = MLA_Q_RANK + MLA_KV_RANK + 2 * HEAD_LANES


def _latent_kernel(a_ref, w_ref, gq_ref, gkv_ref, cos_ref, sin_ref, cq_ref, ckv_ref, kr_ref):
    z = _dot(a_ref[...], w_ref[...])
    cq = z[:, :MLA_Q_RANK]
    ckv = z[:, MLA_Q_RANK:MLA_Q_RANK + MLA_KV_RANK]
    kr = z[:, MLA_Q_RANK + MLA_KV_RANK:MLA_Q_RANK + MLA_KV_RANK + HEAD_LANES]
    kr_rot = z[:, MLA_Q_RANK + MLA_KV_RANK + HEAD_LANES:]
    cq_ref[...] = _rmsnorm(cq, gq_ref[...]).astype(BF16)
    ckv_ref[...] = _rmsnorm(ckv, gkv_ref[...]).astype(BF16)
    kr_ref[...] = (kr * cos_ref[...] + kr_rot * sin_ref[...]).astype(BF16)


def _latent(xb, w_c, gq, gkv, cos_t, sin_t, tm):
    t = xb.shape[0]
    row = lambda n: pl.BlockSpec((tm, n), lambda i: (i, 0))
    full = lambda a: pl.BlockSpec(a.shape, lambda i: (0, 0))
    return pl.pallas_call(
        _latent_kernel,
        grid=(t // tm,),
        in_specs=[row(D_MODEL), full(w_c), full(gq), full(gkv), row(HEAD_LANES), row(HEAD_LANES)],
        out_specs=[row(MLA_Q_RANK), row(MLA_KV_RANK), row(HEAD_LANES)],
        out_shape=[jax.ShapeDtypeStruct((t, MLA_Q_RANK), BF16),
                   jax.ShapeDtypeStruct((t, MLA_KV_RANK), BF16),
                   jax.ShapeDtypeStruct((t, HEAD_LANES), BF16)],
        compiler_params=_cp(("parallel",)),
    )(xb, w_c, gq, gkv, cos_t, sin_t)


def _qup_kernel(a_ref, w_ref, cos_ref, sin_ref, o_ref, *, scale):
    z = _dot(a_ref[...], w_ref[...])
    c = cos_ref[...] * scale
    s = sin_ref[...] * scale
    hl = HEAD_LANES
    for h in range(MLA_HEADS):
        o_ref[:, h * hl:(h + 1) * hl] = (
            z[:, h * hl:(h + 1) * hl] * c
            + z[:, (MLA_HEADS + h) * hl:(MLA_HEADS + h + 1) * hl] * s).astype(BF16)


def _qup(cq, w_q2, cos_t, sin_t, tm):
    t = cq.shape[0]
    n = MLA_HEADS * HEAD_LANES
    scale = (MLA_NOPE_DIM + MLA_ROPE_DIM) ** -0.5 * LOG2E
    return pl.pallas_call(
        functools.partial(_qup_kernel, scale=scale),
        grid=(t // tm,),
        in_specs=[pl.BlockSpec((tm, MLA_Q_RANK), lambda i: (i, 0)),
                  pl.BlockSpec(w_q2.shape, lambda i: (0, 0)),
                  pl.BlockSpec((tm, HEAD_LANES), lambda i: (i, 0)),
                  pl.BlockSpec((tm, HEAD_LANES), lambda i: (i, 0))],
        out_specs=pl.BlockSpec((tm, n), lambda i: (i, 0)),
        out_shape=jax.ShapeDtypeStruct((t, n), BF16),
        compiler_params=_cp(("parallel",)),
    )(cq, w_q2, cos_t, sin_t)


ONES_ROWS = 16


def _scores_t(k, q):
    return lax.dot_general(k, q, (((1,), (1,)), ((), ())), preferred_element_type=F32)


def _softmax_step_t(s, vt, m_ref, acc_ref):
    m_old = m_ref[...]
    m_new = jnp.maximum(m_old, jnp.max(s, axis=0, keepdims=True))
    p = jnp.exp2(s - m_new).astype(BF16)
    alpha = jnp.exp2(m_old - m_new)
    acc_ref[...] = alpha * acc_ref[...] + _dot(vt, p)
    m_ref[...] = m_new


def _fill_vt(v_ref, vt_scr, lane0, row0, v_rows, tile):
    for j in range(vt_scr.shape[0]):
        vt = v_ref[0, j * tile:(j + 1) * tile, lane0:lane0 + HEAD_LANES].astype(F32).T
        vt_scr[j, 0:v_rows, :] = vt[row0:row0 + v_rows].astype(BF16)
        vt_scr[j, v_rows:v_rows + ONES_ROWS, :] = jnp.ones((ONES_ROWS, tile), BF16)


def _normalised(acc, v_rows):
    return acc[0:v_rows] * (1.0 / acc[v_rows:v_rows + 1])


def _diff_attn_kernel(q_ref, k_ref, v_ref, bias_ref, lq1_ref, lk1_ref, lq2_ref, lk2_ref, g_ref, o_ref,
                      vt_scr, s_scr, m_scr, acc_scr, *, lam_init, tile, heads):
    qi = pl.program_id(2)
    hd = 2 * DIFF_HEAD_DIM
    hl = HEAD_LANES

    @pl.when(qi == 0)
    def _():
        for h in range(heads):
            _fill_vt(v_ref, vt_scr.at[h], h * hl, 0, hd, tile)

    q_cat = []
    for h in range(heads):
        q = q_ref[0, :, h * hl:(h + 1) * hl]
        lane = lax.broadcasted_iota(jnp.int32, q.shape, 1)
        zero = jnp.zeros_like(q)
        q_cat.append(jnp.concatenate([jnp.where(lane < DIFF_HEAD_DIM, q, zero),
                                      jnp.where(lane >= DIFF_HEAD_DIM, q, zero)], axis=0))
    m_scr[...] = jnp.full(m_scr.shape, NEG, F32)
    acc_scr[...] = jnp.zeros(acc_scr.shape, F32)

    def scores(slot, kj):
        rows = pl.ds(pl.multiple_of(kj * tile, tile), tile)
        for h in range(heads):
            s_scr[slot, h] = _scores_t(k_ref[0, rows, h * hl:(h + 1) * hl], q_cat[h])

    def update(slot, kj, bias_idx=None):
        for h in range(heads):
            s = s_scr[slot, h]
            if bias_idx is not None:
                s = s + bias_ref[h, bias_idx]
            _softmax_step_t(s, vt_scr[h, kj], m_scr.at[h], acc_scr.at[h])

    n_far = jnp.maximum(qi - 1, 0)
    pairs = n_far // 2
    scores(0, 0)

    def pair_body(j2, carry):
        a = 2 * j2
        scores(1, a + 1)
        update(0, a)
        scores(0, a + 2)
        update(1, a + 1)
        return carry

    lax.fori_loop(0, pairs, pair_body, 0)
    t0 = 2 * pairs

    @pl.when(n_far - t0 == 1)
    def _():
        scores(1, t0 + 1)
        update(0, t0)
        scores(0, t0 + 2)
        update(1, t0 + 1, 1)
        update(0, t0 + 2, 0)

    @pl.when(jnp.logical_and(n_far == t0, qi >= 1))
    def _():
        scores(1, t0 + 1)
        update(0, t0, 1)
        update(1, t0 + 1, 0)

    @pl.when(qi == 0)
    def _():
        update(0, 0, 0)

    lam = (jnp.exp(jnp.sum(lq1_ref[...] * lk1_ref[...], axis=-1, keepdims=True))
           - jnp.exp(jnp.sum(lq2_ref[...] * lk2_ref[...], axis=-1, keepdims=True)) + lam_init)
    for h in range(heads):
        acc = acc_scr[h]
        o = _normalised(acc[:, :tile], hd) - lam * _normalised(acc[:, tile:], hd)
        o = o * lax.rsqrt(jnp.mean(o * o, axis=0, keepdims=True) + EPS)
        o_ref[0, :, h * hl:(h + 1) * hl] = (o.T * (g_ref[...] * (1.0 - lam_init))).astype(BF16)


def _diff_attention(qkv, bias, lq1, lk1, lq2, lk2, g, lam_init, batch, seq):
    tile = ATT_TILE
    hl = HEAD_LANES
    nh = DIFF_HEADS
    hps = ATT_HEADS_PER_STEP
    ng = nh // hps
    vec = lambda a: pl.BlockSpec(a.shape, lambda b, h, i: (0, 0))
    return pl.pallas_call(
        functools.partial(_diff_attn_kernel, lam_init=lam_init, tile=tile, heads=hps),
        grid=(batch, ng, seq // tile),
        in_specs=[pl.BlockSpec((1, tile, hps * hl), lambda b, h, i: (b, i, h)),
                  pl.BlockSpec((1, seq, hps * hl), lambda b, h, i: (b, 0, ng + h)),
                  pl.BlockSpec((1, seq, hps * hl), lambda b, h, i: (b, 0, 2 * ng + h)),
                  pl.BlockSpec((hps, 2, tile, 2 * tile), lambda b, h, i: (h, 0, 0, 0)),
                  vec(lq1), vec(lk1), vec(lq2), vec(lk2), vec(g)],
        out_specs=pl.BlockSpec((1, tile, hps * hl), lambda b, h, i: (b, i, h)),
        out_shape=jax.ShapeDtypeStruct((batch, seq, nh * hl), BF16),
        scratch_shapes=[pltpu.VMEM((hps, seq // tile, hl + ONES_ROWS, tile), BF16),
                        pltpu.VMEM((2, hps, tile, 2 * tile), F32),
                        pltpu.VMEM((hps, 1, 2 * tile), F32),
                        pltpu.VMEM((hps, hl + ONES_ROWS, 2 * tile), F32)],
        compiler_params=_cp(("parallel", "parallel", "arbitrary")),
    )(qkv, qkv, qkv, bias, lq1, lk1, lq2, lk2, g)


MLA_PAIR = 2


def _mla_attn_kernel(q_ref, k_ref, kr_ref, v_ref, o_ref, kf_scr, vt_scr, s_scr, m_scr, acc_scr, *, tile):
    qi = pl.program_id(2)
    hl = HEAD_LANES

    @pl.when(qi == 0)
    def _():
        for h in range(MLA_PAIR):
            kf_scr[h] = k_ref[0, :, h * hl:(h + 1) * hl] + kr_ref[0]
            _fill_vt(v_ref, vt_scr.at[h], 0, h * MLA_V_DIM, MLA_V_DIM, tile)

    q = [q_ref[0, :, h * hl:(h + 1) * hl] for h in range(MLA_PAIR)]
    m_scr[...] = jnp.full(m_scr.shape, NEG, F32)
    acc_scr[...] = jnp.zeros(acc_scr.shape, F32)

    def scores(slot, kj):
        rows = pl.ds(pl.multiple_of(kj * tile, tile), tile)
        for h in range(MLA_PAIR):
            s_scr[slot, h] = _scores_t(kf_scr[h, rows, :], q[h])

    def update(slot, kj, key_offset=None):
        for h in range(MLA_PAIR):
            s = s_scr[slot, h]
            if key_offset is not None:
                key = lax.broadcasted_iota(jnp.int32, s.shape, 0)
                qry = lax.broadcasted_iota(jnp.int32, s.shape, 1)
                s = jnp.where(key + key_offset <= qry, s, NEG)
            _softmax_step_t(s, vt_scr[h, kj], m_scr.at[h], acc_scr.at[h])

    scores(0, 0)

    def pair_body(j2, carry):
        a = 2 * j2
        scores(1, a + 1)
        update(0, a)
        scores(0, a + 2)
        update(1, a + 1)
        return carry

    lax.fori_loop(0, qi, pair_body, 0)
    t0 = 2 * qi
    scores(1, t0 + 1)
    update(0, t0, 0)
    update(1, t0 + 1, tile)
    o = jnp.concatenate([_normalised(acc_scr[h], MLA_V_DIM) for h in range(MLA_PAIR)], axis=0)
    o_ref[0] = o.T.astype(BF16)


def _mla_attention(q, kv, kr, batch, seq):
    tile = ATT_TILE
    tq = 2 * tile
    hl = HEAD_LANES
    ng = MLA_HEADS // MLA_PAIR
    v_rows = MLA_V_DIM + ONES_ROWS
    return pl.pallas_call(
        functools.partial(_mla_attn_kernel, tile=tile),
        grid=(batch, ng, seq // tq),
        in_specs=[pl.BlockSpec((1, tq, MLA_PAIR * hl), lambda b, h, i: (b, i, h)),
                  pl.BlockSpec((1, seq, MLA_PAIR * hl), lambda b, h, i: (b, 0, h)),
                  pl.BlockSpec((1, seq, hl), lambda b, h, i: (b, 0, 0)),
                  pl.BlockSpec((1, seq, hl), lambda b, h, i: (b, 0, MLA_HEADS + h))],
        out_specs=pl.BlockSpec((1, tq, hl), lambda b, h, i: (b, i, h)),
        out_shape=jax.ShapeDtypeStruct((batch, seq, MLA_HEADS * MLA_V_DIM), BF16),
        scratch_shapes=[pltpu.VMEM((MLA_PAIR, seq, hl), BF16),
                        pltpu.VMEM((MLA_PAIR, seq // tile, v_rows, tile), BF16),
                        pltpu.VMEM((2, MLA_PAIR, tile, tq), F32),
                        pltpu.VMEM((MLA_PAIR, 1, tq), F32),
                        pltpu.VMEM((MLA_PAIR, v_rows, tq), F32)],
        compiler_params=_cp(("parallel", "parallel", "arbitrary")),
    )(q, kv, kr, kv)


def _merge_kernel(xb_ref, oa_ref, ob_ref, wga_ref, wgb_ref, bga_ref, bgb_ref, wbd_ref, wbm_ref, o_ref):
    xb = xb_ref[...]
    g_a = _sigmoid(_dot(xb, wga_ref[...]) + bga_ref[...])
    g_b = _sigmoid(_dot(xb, wgb_ref[...]) + bgb_ref[...])
    m = g_a * _dot(oa_ref[...], wbd_ref[...]) + g_b * _dot(ob_ref[...], wbm_ref[...])
    o_ref[...] = m.astype(BF16)


def _merge(xb, o_a, o_b, w_g, b_g, w_bd, w_bm, tm, tn):
    t = xb.shape[0]
    nj = D_MODEL // tn
    row = lambda a: pl.BlockSpec((tm, a.shape[1]), lambda i, j: (i, 0))
    col = lambda w, off: pl.BlockSpec((w.shape[0], tn), lambda i, j: (0, j + off))
    vec = lambda off: pl.BlockSpec((1, tn), lambda i, j: (0, j + off))
    return pl.pallas_call(
        _merge_kernel,
        grid=(t // tm, nj),
        in_specs=[row(xb), row(o_a), row(o_b), col(w_g, 0), col(w_g, nj), vec(0), vec(nj), col(w_bd, 0),
                  col(w_bm, 0)],
        out_specs=pl.BlockSpec((tm, tn), lambda i, j: (i, j)),
        out_shape=jax.ShapeDtypeStruct((t, D_MODEL), BF16),
        compiler_params=_cp(("parallel", "arbitrary")),
    )(xb, o_a, o_b, w_g, w_g, b_g, b_g, w_bd, w_bm)


def _out_ln_kernel(m_ref, w_ref, x_ref, g_ref, b_ref, xo_ref, xbo_ref, *, alpha):
    r = alpha * x_ref[...] + _dot(m_ref[...], w_ref[...])
    y = _layernorm(r, g_ref[...], b_ref[...])
    xo_ref[...] = y
    xbo_ref[...] = y.astype(BF16)


def _out_ln(m, w_out, x, g, b, alpha, tm):
    t = x.shape[0]
    row = pl.BlockSpec((tm, D_MODEL), lambda i: (i, 0))
    full = lambda a: pl.BlockSpec(a.shape, lambda i: (0, 0))
    return pl.pallas_call(
        functools.partial(_out_ln_kernel, alpha=alpha),
        grid=(t // tm,),
        in_specs=[row, full(w_out), row, full(g), full(b)],
        out_specs=[row, row],
        out_shape=[jax.ShapeDtypeStruct((t, D_MODEL), F32), jax.ShapeDtypeStruct((t, D_MODEL), BF16)],
        compiler_params=_cp(("parallel",)),
    )(m, w_out, x, g, b)


def _swiglu_up_kernel(a_ref, w1_ref, w3_ref, o_ref):
    a = a_ref[...]
    u = _dot(a, w1_ref[...])
    o_ref[...] = (u * _sigmoid(u) * _dot(a, w3_ref[...])).astype(BF16)


def _swiglu_up(xb, w1, w3, tm, tn):
    t = xb.shape[0]
    ff = w1.shape[1]
    return pl.pallas_call(
        _swiglu_up_kernel,
        grid=(t // tm, ff // tn),
        in_specs=[pl.BlockSpec((tm, D_MODEL), lambda i, j: (i, 0)),
                  pl.BlockSpec((D_MODEL, tn), lambda i, j: (0, j)),
                  pl.BlockSpec((D_MODEL, tn), lambda i, j: (0, j))],
        out_specs=pl.BlockSpec((tm, tn), lambda i, j: (i, j)),
        out_shape=jax.ShapeDtypeStruct((t, ff), BF16),
        compiler_params=_cp(("parallel", "arbitrary")),
    )(xb, w1, w3)


def _ffn_ln_kernel(f_ref, w2_ref, x_ref, xb_ref, p_ref, wpg_ref, wpp_ref, g_ref, b_ref, xo_ref, xbo_ref, *,
                   alpha, project):
    f = _dot(f_ref[...], w2_ref[...]) if project else f_ref[...]
    e = _sigmoid(_dot(xb_ref[...], wpg_ref[...])) * _dot(p_ref[...].astype(BF16), wpp_ref[...])
    y = _layernorm(alpha * x_ref[...] + f + e, g_ref[...], b_ref[...])
    xo_ref[...] = y
    xbo_ref[...] = y.astype(BF16)


def _ffn_ln(f, w2, x, xb, p, w_pg, w_pp, g, b, alpha, tm, project):
    t = x.shape[0]
    row = lambda n: pl.BlockSpec((tm, n), lambda i: (i, 0))
    full = lambda a: pl.BlockSpec(a.shape, lambda i: (0, 0))
    return pl.pallas_call(
        functools.partial(_ffn_ln_kernel, alpha=alpha, project=project),
        grid=(t // tm,),
        in_specs=[row(f.shape[1]), full(w2), row(D_MODEL), row(D_MODEL), row(PLE_DIM), full(w_pg), full(w_pp),
                  full(g), full(b)],
        out_specs=[row(D_MODEL), row(D_MODEL)],
        out_shape=[jax.ShapeDtypeStruct((t, D_MODEL), F32), jax.ShapeDtypeStruct((t, D_MODEL), BF16)],
        compiler_params=_cp(("parallel",)),
    )(f, w2, x, xb, p, w_pg, w_pp, g, b)


def _router_kernel(x_ref, w_ref, o_ref):
    logits = jnp.dot(x_ref[...], w_ref[...], preferred_element_type=F32, precision=lax.Precision.HIGHEST)
    lane = lax.broadcasted_iota(jnp.int32, logits.shape, 1)
    lg = jnp.where(lane < N_EXPERTS, logits, -jnp.inf)
    v1 = jnp.max(lg, axis=-1, keepdims=True)
    i1 = jnp.min(jnp.where(lg == v1, lane, HEAD_LANES), axis=-1, keepdims=True)
    lg2 = jnp.where(lane == i1, -jnp.inf, lg)
    v2 = jnp.max(lg2, axis=-1, keepdims=True)
    i2 = jnp.min(jnp.where(lg2 == v2, lane, HEAD_LANES), axis=-1, keepdims=True)
    e2 = jnp.exp(v2 - v1)
    g1 = 1.0 / (1.0 + e2)
    g2 = e2 / (1.0 + e2)
    o_ref[...] = jnp.where(lane == 0, i1.astype(F32),
                           jnp.where(lane == 1, i2.astype(F32),
                                     jnp.where(lane == 2, g1, jnp.where(lane == 3, g2, 0.0))))


def _router(x, w_r, tm):
    t = x.shape[0]
    return pl.pallas_call(
        _router_kernel,
        grid=(t // tm,),
        in_specs=[pl.BlockSpec((tm, D_MODEL), lambda i: (i, 0)),
                  pl.BlockSpec(w_r.shape, lambda i: (0, 0))],
        out_specs=pl.BlockSpec((tm, HEAD_LANES), lambda i: (i, 0)),
        out_shape=jax.ShapeDtypeStruct((t, HEAD_LANES), F32),
        compiler_params=_cp(("parallel",)),
    )(x, w_r)


def _expert_up_kernel(be_ref, nb_ref, a_ref, w1_ref, w3_ref, o_ref):
    @pl.when(pl.program_id(0) < nb_ref[0])
    def _():
        a = a_ref[...]
        u = _dot(a, w1_ref[0])
        o_ref[...] = (u * _sigmoid(u) * _dot(a, w3_ref[0])).astype(BF16)

    @pl.when(pl.program_id(0) >= nb_ref[0])
    def _():
        o_ref[...] = jnp.zeros(o_ref.shape, BF16)


def _expert_up(a, w1, w3, blk_e, n_used, tm, tn):
    rows = a.shape[0]
    ff = w1.shape[2]
    grid_spec = pltpu.PrefetchScalarGridSpec(
        num_scalar_prefetch=2,
        grid=(rows // tm, ff // tn),
        in_specs=[pl.BlockSpec((tm, D_MODEL), lambda i, j, be, nb: (i, 0)),
                  pl.BlockSpec((1, D_MODEL, tn), lambda i, j, be, nb: (be[i], 0, j)),
                  pl.BlockSpec((1, D_MODEL, tn), lambda i, j, be, nb: (be[i], 0, j))],
        out_specs=pl.BlockSpec((tm, tn), lambda i, j, be, nb: (i, j)),
    )
    return pl.pallas_call(
        _expert_up_kernel,
        grid_spec=grid_spec,
        out_shape=jax.ShapeDtypeStruct((rows, ff), BF16),
        compiler_params=_cp(("parallel", "arbitrary")),
    )(blk_e, n_used, a, w1, w3)


def _expert_down_kernel(be_ref, nb_ref, h_ref, w2_ref, o_ref):
    @pl.when(pl.program_id(0) < nb_ref[0])
    def _():
        o_ref[...] = _dot(h_ref[...], w2_ref[0])

    @pl.when(pl.program_id(0) >= nb_ref[0])
    def _():
        o_ref[...] = jnp.zeros(o_ref.shape, F32)


def _expert_down(h, w2, blk_e, n_used, tm):
    rows, ff = h.shape
    grid_spec = pltpu.PrefetchScalarGridSpec(
        num_scalar_prefetch=2,
        grid=(rows // tm,),
        in_specs=[pl.BlockSpec((tm, ff), lambda i, be, nb: (i, 0)),
                  pl.BlockSpec((1, ff, D_MODEL), lambda i, be, nb: (be[i], 0, 0))],
        out_specs=pl.BlockSpec((tm, D_MODEL), lambda i, be, nb: (i, 0)),
    )
    return pl.pallas_call(
        _expert_down_kernel,
        grid_spec=grid_spec,
        out_shape=jax.ShapeDtypeStruct((rows, D_MODEL), F32),
        compiler_params=_cp(("parallel",)),
    )(blk_e, n_used, h, w2)


def _moe(x, xb, w_r, w1, w3, w2, tm):
    t = x.shape[0]
    m = t * TOP_K
    gb = GROUP_ROWS
    routed = _router(x, w_r, tm)
    flat_e = routed[:, :TOP_K].astype(jnp.int32).reshape(-1)
    gates = routed[:, TOP_K:2 * TOP_K]
    onehot = (flat_e[:, None] == jnp.arange(N_EXPERTS, dtype=jnp.int32)[None, :]).astype(jnp.int32)
    csum = jnp.cumsum(onehot, axis=0)
    counts = csum[-1]
    rank = jnp.take_along_axis(csum, flat_e[:, None], axis=1)[:, 0] - 1
    padded = ((counts + gb - 1) // gb) * gb
    pend = jnp.cumsum(padded)
    dest = (pend - padded)[flat_e] + rank
    n_blocks = m // gb + N_EXPERTS
    src_tok = jnp.zeros((n_blocks * gb,), jnp.int32).at[dest].set(jnp.arange(m, dtype=jnp.int32) // TOP_K)
    blk_e = jnp.minimum(jnp.searchsorted(pend, jnp.arange(n_blocks, dtype=jnp.int32) * gb, side='right'),
                        N_EXPERTS - 1).astype(jnp.int32)
    n_used = (pend[-1:] // gb).astype(jnp.int32)
    a = jnp.take(xb, src_tok, axis=0)
    h = _expert_up(a, w1, w3, blk_e, n_used, gb, EXPERT_FF // 2)
    yb = _expert_down(h, w2, blk_e, n_used, gb)
    dest2 = dest.reshape(t, TOP_K)
    return (jnp.take(yb, dest2[:, 0], axis=0) * gates[:, 0:1]
            + jnp.take(yb, dest2[:, 1], axis=0) * gates[:, 1:2])


def _t5_bucket(dist):
    n = jnp.maximum(dist, 0)
    max_exact = REL_BUCKETS // 2
    large = max_exact + (jnp.log(jnp.maximum(n, 1).astype(F32) / max_exact)
                         / math.log(REL_MAX_DIST / max_exact) * (REL_BUCKETS - max_exact)).astype(jnp.int32)
    large = jnp.minimum(large, REL_BUCKETS - 1)
    return jnp.where(n < max_exact, n, large)


def _bias_tiles(table, tile):
    a = jnp.arange(tile, dtype=jnp.int32)
    d0 = a[None, :] - a[:, None]
    far = table[REL_BUCKETS - 1].astype(F32)
    f0 = jnp.transpose(table[_t5_bucket(d0)].astype(F32) - far, (2, 0, 1))
    f1 = jnp.transpose(table[_t5_bucket(d0 + tile)].astype(F32) - far, (2, 0, 1))
    f0 = jnp.where((d0 >= 0)[None], f0, NEG)
    tiles = jnp.stack([f0, f1], axis=1)
    return jnp.concatenate([tiles, tiles], axis=-1)


def _rope_tables(positions):
    half = MLA_ROPE_DIM // 2
    inv_freq = ROPE_THETA ** (-jnp.arange(half, dtype=F32) / half)
    ang = positions.astype(F32)[:, :, None] * inv_freq
    c, s = jnp.cos(ang), jnp.sin(ang)
    b, sq = positions.shape
    ones = jnp.ones((b, sq, MLA_NOPE_DIM), F32)
    z_nope = jnp.zeros((b, sq, MLA_NOPE_DIM), F32)
    z_pad = jnp.zeros((b, sq, HEAD_LANES - MLA_NOPE_DIM - MLA_ROPE_DIM), F32)
    cos_t = jnp.concatenate([ones, c, c, z_pad], axis=-1).reshape(b * sq, HEAD_LANES)
    sin_t = jnp.concatenate([z_nope, s, s, z_pad], axis=-1).reshape(b * sq, HEAD_LANES)
    return cos_t, sin_t


def _rot_cols(w):
    half = MLA_ROPE_DIM // 2
    return jnp.concatenate([-w[..., half:], w[..., :half]], axis=-1)


def _mixer_weights(w_in, w_uq, w_ukv):
    d = D_MODEL
    hq = DIFF_HEADS * 2 * DIFF_HEAD_DIM
    o = 0
    w_dq = w_in[:, o:o + hq]; o += hq
    w_dk = w_in[:, o:o + hq]; o += hq
    w_dv = w_in[:, o:o + hq]; o += hq
    w_cq = w_in[:, o:o + MLA_Q_RANK]; o += MLA_Q_RANK
    w_ckv = w_in[:, o:o + MLA_KV_RANK]; o += MLA_KV_RANK
    w_kr = w_in[:, o:o + MLA_ROPE_DIM]; o += MLA_ROPE_DIM
    w_g = w_in[:, o:]
    w_qkv = jnp.concatenate([w_dq, w_dk, w_dv], axis=1).astype(BF16)
    pad = HEAD_LANES - MLA_NOPE_DIM - MLA_ROPE_DIM
    z = lambda n: jnp.zeros((d, n), F32)
    w_c = jnp.concatenate([w_cq, w_ckv, z(MLA_NOPE_DIM), w_kr, z(pad), z(MLA_NOPE_DIM), _rot_cols(w_kr), z(pad)],
                          axis=1).astype(BF16)
    wq = w_uq.reshape(MLA_Q_RANK, MLA_HEADS, MLA_NOPE_DIM + MLA_ROPE_DIM)
    nope, rope = wq[..., :MLA_NOPE_DIM], wq[..., MLA_NOPE_DIM:]
    zq = lambda n: jnp.zeros((MLA_Q_RANK, MLA_HEADS, n), F32)
    plain = jnp.concatenate([nope, rope, zq(pad)], axis=-1).reshape(MLA_Q_RANK, -1)
    rot = jnp.concatenate([zq(MLA_NOPE_DIM), _rot_cols(rope), zq(pad)], axis=-1).reshape(MLA_Q_RANK, -1)
    w_q2 = jnp.concatenate([plain, rot], axis=1).astype(BF16)
    wkv = w_ukv.reshape(MLA_KV_RANK, MLA_HEADS, MLA_NOPE_DIM + MLA_V_DIM)
    zk = jnp.zeros((MLA_KV_RANK, MLA_HEADS, HEAD_LANES - MLA_NOPE_DIM), F32)
    w_kv2 = jnp.concatenate([
        jnp.concatenate([wkv[..., :MLA_NOPE_DIM], zk], axis=-1).reshape(MLA_KV_RANK, -1),
        wkv[..., MLA_NOPE_DIM:].reshape(MLA_KV_RANK, -1)], axis=1).astype(BF16)
    return w_qkv, w_c, w_g.astype(BF16), w_q2, w_kv2


def kernel(x, p, positions, rel_bias_table, w_in, b_gate, lambda_q1, lambda_k1, lambda_q2, lambda_k2,
           diff_subln_g, mla_q_norm_g, w_uq, mla_kv_norm_g, w_ukv, w_branch_diff, w_branch_mla, w_out,
           ln_mix_g, ln_mix_b, dense_w1, dense_w3, dense_w2, router_w, expert_w1, expert_w3, expert_w2,
           w_ple_gate, w_ple_proj, ln_ffn_g, ln_ffn_b):
    batch, seq, d = x.shape
    depth = w_in.shape[0]
    t = batch * seq
    tm = min(ROW_TILE, t)
    alpha = (2.0 * depth) ** 0.25
    row = lambda v: v.reshape(1, -1).astype(F32)

    cos_t, sin_t = _rope_tables(positions)
    bias = _bias_tiles(rel_bias_table * LOG2E, ATT_TILE)
    xf = x.reshape(t, d)
    xb = xf.astype(BF16)

    for i in range(depth):
        lam_init = 0.8 - 0.6 * math.exp(-0.3 * i)
        w_qkv, w_c, w_g, w_q2, w_kv2 = _mixer_weights(w_in[i], w_uq[i], w_ukv[i])
        qkv = _matmul(xb, w_qkv, BF16, tm, 1024, scaled_tiles=1, scale=DIFF_HEAD_DIM ** -0.5 * LOG2E)
        cq, ckv, kr = _latent(xb, w_c, row(mla_q_norm_g[i]), row(mla_kv_norm_g[i]), cos_t, sin_t, tm)
        q_mla = _qup(cq, w_q2, cos_t, sin_t, tm)
        kv_mla = _matmul(ckv, w_kv2, BF16, tm, w_kv2.shape[1] // 2)
        o_a = _diff_attention(qkv.reshape(batch, seq, -1), bias, row(lambda_q1[i]), row(lambda_k1[i]),
                              row(lambda_q2[i]), row(lambda_k2[i]), row(diff_subln_g[i]), lam_init, batch, seq)
        o_b = _mla_attention(q_mla.reshape(batch, seq, -1), kv_mla.reshape(batch, seq, -1),
                             kr.reshape(batch, seq, -1), batch, seq)
        mix = _merge(xb, o_a.reshape(t, -1), o_b.reshape(t, -1), w_g, row(b_gate[i]),
                     w_branch_diff[i].astype(BF16), w_branch_mla[i].astype(BF16), tm, 512)
        xf, xb = _out_ln(mix, w_out[i].astype(BF16), xf, row(ln_mix_g[i]), row(ln_mix_b[i]), alpha, tm)
        j = i // 2
        p_i = p[i].reshape(t, PLE_DIM)
        w_pg = w_ple_gate[i].astype(BF16)
        w_pp = w_ple_proj[i].astype(BF16)
        if i % 2 == 0:
            hmid = _swiglu_up(xb, dense_w1[j].astype(BF16), dense_w3[j].astype(BF16), tm, DENSE_FF // 2)
            xf, xb = _ffn_ln(hmid, dense_w2[j].astype(BF16), xf, xb, p_i, w_pg, w_pp,
                             row(ln_ffn_g[i]), row(ln_ffn_b[i]), alpha, tm, True)
        else:
            w_r = jnp.concatenate([router_w[j], jnp.zeros((d, HEAD_LANES - N_EXPERTS), F32)], axis=1)
            f = _moe(xf, xb, w_r, expert_w1[j].astype(BF16), expert_w3[j].astype(BF16),
                     expert_w2[j].astype(BF16), tm)
            xf, xb = _ffn_ln(f, jnp.zeros((8, HEAD_LANES), BF16), xf, xb, p_i, w_pg, w_pp,
                             row(ln_ffn_g[i]), row(ln_ffn_b[i]), alpha, tm, False)
    return xf.reshape(batch, seq, d)
```

```python
import functools
import math

import jax
import jax.numpy as jnp
from jax import lax
from jax.experimental import pallas as pl
from jax.experimental.pallas import tpu as pltpu

F32 = jnp.float32
BF16 = jnp.bfloat16

D_MODEL = 1024
PLE_DIM = 256
DIFF_HEADS = 8
DIFF_HEAD_DIM = 64
MLA_HEADS = 8
MLA_Q_RANK = 384
MLA_KV_RANK = 256
MLA_NOPE_DIM = 64
MLA_ROPE_DIM = 32
MLA_V_DIM = 64
ROPE_THETA = 10000.0
REL_BUCKETS = 32
REL_MAX_DIST = 128
DENSE_FF = 2816
N_EXPERTS = 8
TOP_K = 2
EXPERT_FF = 3584
EPS = 1e-5

HEAD_LANES = 128
NEG = -1e30
LOG2E = math.log2(math.e)
ATT_TILE = 256
ATT_HEADS_PER_STEP = 2
ROW_TILE = 512
BIG_ROW_TILE = 1024
GROUP_ROWS = 512
VMEM_LIMIT = 52 * 1024 * 1024


def _cp(sem):
    return pltpu.CompilerParams(dimension_semantics=sem, vmem_limit_bytes=VMEM_LIMIT)


def _dot(a, b):
    return jnp.dot(a, b, preferred_element_type=F32)


def _sigmoid(x):
    return 1.0 / (1.0 + jnp.exp(-x))


def _layernorm(r, g, b):
    mu = jnp.mean(r, axis=-1, keepdims=True)
    d = r - mu
    var = jnp.mean(d * d, axis=-1, keepdims=True)
    return d * lax.rsqrt(var + EPS) * g + b


def _rmsnorm(x, g):
    return x * lax.rsqrt(jnp.mean(x * x, axis=-1, keepdims=True) + EPS) * g


def _mm_kernel(a_ref, w_ref, o_ref, *, scaled_tiles, scale):
    acc = _dot(a_ref[...], w_ref[...])
    if scaled_tiles:
        acc = acc * jnp.where(pl.program_id(1) < scaled_tiles, scale, 1.0)
    o_ref[...] = acc.astype(o_ref.dtype)


def _matmul(a, w, out_dtype, tm, tn, scaled_tiles=0, scale=1.0):
    m, k = a.shape
    n = w.shape[1]
    return pl.pallas_call(
        functools.partial(_mm_kernel, scaled_tiles=scaled_tiles, scale=scale),
        grid=(m // tm, n // tn),
        in_specs=[pl.BlockSpec((tm, k), lambda i, j: (i, 0)),
                  pl.BlockSpec((k, tn), lambda i, j: (0, j))],
        out_specs=pl.BlockSpec((tm, tn), lambda i, j: (i, j)),
        out_shape=jax.ShapeDtypeStruct((m, n), out_dtype),
        compiler_params=_cp(("parallel", "arbitrary")),
    )(a, w)


C_COLS = MLA_Q_RANK + MLA_KV_RANK + 2 * HEAD_LANES


def _latent_kernel(a_ref, w_ref, gq_ref, gkv_ref, cos_ref, sin_ref, cq_ref, ckv_ref, kr_ref):
    z = _dot(a_ref[...], w_ref[...])
    cq = z[:, :MLA_Q_RANK]
    ckv = z[:, MLA_Q_RANK:MLA_Q_RANK + MLA_KV_RANK]
    kr = z[:, MLA_Q_RANK + MLA_KV_RANK:MLA_Q_RANK + MLA_KV_RANK + HEAD_LANES]
    kr_rot = z[:, MLA_Q_RANK + MLA_KV_RANK + HEAD_LANES:]
    cq_ref[...] = _rmsnorm(cq, gq_ref[...]).astype(BF16)
    ckv_ref[...] = _rmsnorm(ckv, gkv_ref[...]).astype(BF16)
    kr_ref[...] = (kr * cos_ref[...] + kr_rot * sin_ref[...]).astype(BF16)


def _latent(xb, w_c, gq, gkv, cos_t, sin_t, tm):
    t = xb.shape[0]
    row = lambda n: pl.BlockSpec((tm, n), lambda i: (i, 0))
    full = lambda a: pl.BlockSpec(a.shape, lambda i: (0, 0))
    return pl.pallas_call(
        _latent_kernel,
        grid=(t // tm,),
        in_specs=[row(D_MODEL), full(w_c), full(gq), full(gkv), row(HEAD_LANES), row(HEAD_LANES)],
        out_specs=[row(MLA_Q_RANK), row(MLA_KV_RANK), row(HEAD_LANES)],
        out_shape=[jax.ShapeDtypeStruct((t, MLA_Q_RANK), BF16),
                   jax.ShapeDtypeStruct((t, MLA_KV_RANK), BF16),
                   jax.ShapeDtypeStruct((t, HEAD_LANES), BF16)],
        compiler_params=_cp(("parallel",)),
    )(xb, w_c, gq, gkv, cos_t, sin_t)


def _qup_kernel(a_ref, w_ref, cos_ref, sin_ref, o_ref, *, scale):
    z = _dot(a_ref[...], w_ref[...])
    c = cos_ref[...] * scale
    s = sin_ref[...] * scale
    hl = HEAD_LANES
    for h in range(MLA_HEADS):
        o_ref[:, h * hl:(h + 1) * hl] = (
            z[:, h * hl:(h + 1) * hl] * c
            + z[:, (MLA_HEADS + h) * hl:(MLA_HEADS + h + 1) * hl] * s).astype(BF16)


def _qup(cq, w_q2, cos_t, sin_t, tm):
    t = cq.shape[0]
    n = MLA_HEADS * HEAD_LANES
    scale = (MLA_NOPE_DIM + MLA_ROPE_DIM) ** -0.5 * LOG2E
    return pl.pallas_call(
        functools.partial(_qup_kernel, scale=scale),
        grid=(t // tm,),
        in_specs=[pl.BlockSpec((tm, MLA_Q_RANK), lambda i: (i, 0)),
                  pl.BlockSpec(w_q2.shape, lambda i: (0, 0)),
                  pl.BlockSpec((tm, HEAD_LANES), lambda i: (i, 0)),
                  pl.BlockSpec((tm, HEAD_LANES), lambda i: (i, 0))],
        out_specs=pl.BlockSpec((tm, n), lambda i: (i, 0)),
        out_shape=jax.ShapeDtypeStruct((t, n), BF16),
        compiler_params=_cp(("parallel",)),
    )(cq, w_q2, cos_t, sin_t)


ONES_ROWS = 16


def _scores_t(k, q):
    return lax.dot_general(k, q, (((1,), (1,)), ((), ())), preferred_element_type=F32)


def _softmax_step_t(s, vt, m_ref, acc_ref):
    m_old = m_ref[...]
    m_new = jnp.maximum(m_old, jnp.max(s, axis=0, keepdims=True))
    p = jnp.exp2(s - m_new).astype(BF16)
    alpha = jnp.exp2(m_old - m_new)
    acc_ref[...] = alpha * acc_ref[...] + _dot(vt, p)
    m_ref[...] = m_new


def _fill_vt(v_ref, vt_scr, lane0, row0, v_rows, tile):
    for j in range(vt_scr.shape[0]):
        vt = v_ref[0, j * tile:(j + 1) * tile, lane0:lane0 + HEAD_LANES].astype(F32).T
        vt_scr[j, 0:v_rows, :] = vt[row0:row0 + v_rows].astype(BF16)
        vt_scr[j, v_rows:v_rows + ONES_ROWS, :] = jnp.ones((ONES_ROWS, tile), BF16)


def _normalised(acc, v_rows):
    return acc[0:v_rows] * (1.0 / acc[v_rows:v_rows + 1])


def _diff_attn_kernel(q_ref, k_ref, v_ref, bias_ref, lq1_ref, lk1_ref, lq2_ref, lk2_ref, g_ref, o_ref,
                      vt_scr, s_scr, m_scr, acc_scr, *, lam_init, tile, heads):
    qi = pl.program_id(2)
    hd = 2 * DIFF_HEAD_DIM
    hl = HEAD_LANES

    @pl.when(qi == 0)
    def _():
        for h in range(heads):
            _fill_vt(v_ref, vt_scr.at[h], h * hl, 0, hd, tile)

    q_cat = []
    for h in range(heads):
        q = q_ref[0, :, h * hl:(h + 1) * hl]
        lane = lax.broadcasted_iota(jnp.int32, q.shape, 1)
        zero = jnp.zeros_like(q)
        q_cat.append(jnp.concatenate([jnp.where(lane < DIFF_HEAD_DIM, q, zero),
                                      jnp.where(lane >= DIFF_HEAD_DIM, q, zero)], axis=0))
    m_scr[...] = jnp.full(m_scr.shape, NEG, F32)
    acc_scr[...] = jnp.zeros(acc_scr.shape, F32)

    def scores(slot, kj):
        rows = pl.ds(pl.multiple_of(kj * tile, tile), tile)
        for h in range(heads):
            s_scr[slot, h] = _scores_t(k_ref[0, rows, h * hl:(h + 1) * hl], q_cat[h])

    def update(slot, kj, bias_idx=None):
        for h in range(heads):
            s = s_scr[slot, h]
            if bias_idx is not None:
                s = s + bias_ref[h, bias_idx]
            _softmax_step_t(s, vt_scr[h, kj], m_scr.at[h], acc_scr.at[h])

    n_far = jnp.maximum(qi - 1, 0)
    pairs = n_far // 2
    scores(0, 0)

    def pair_body(j2, carry):
        a = 2 * j2
        scores(1, a + 1)
        update(0, a)
        scores(0, a + 2)
        update(1, a + 1)
        return carry

    lax.fori_loop(0, pairs, pair_body, 0)
    t0 = 2 * pairs

    @pl.when(n_far - t0 == 1)
    def _():
        scores(1, t0 + 1)
        update(0, t0)
        scores(0, t0 + 2)
        update(1, t0 + 1, 1)
        update(0, t0 + 2, 0)

    @pl.when(jnp.logical_and(n_far == t0, qi >= 1))
    def _():
        scores(1, t0 + 1)
        update(0, t0, 1)
        update(1, t0 + 1, 0)

    @pl.when(qi == 0)
    def _():
        update(0, 0, 0)

    lam = (jnp.exp(jnp.sum(lq1_ref[...] * lk1_ref[...], axis=-1, keepdims=True))
           - jnp.exp(jnp.sum(lq2_ref[...] * lk2_ref[...], axis=-1, keepdims=True)) + lam_init)
    for h in range(heads):
        acc = acc_scr[h]
        o = _normalised(acc[:, :tile], hd) - lam * _normalised(acc[:, tile:], hd)
        o = o * lax.rsqrt(jnp.mean(o * o, axis=0, keepdims=True) + EPS)
        o_ref[0, :, h * hl:(h + 1) * hl] = (o.T * (g_ref[...] * (1.0 - lam_init))).astype(BF16)


def _diff_attention(qkv, bias, lq1, lk1, lq2, lk2, g, lam_init, batch, seq):
    tile = ATT_TILE
    hl = HEAD_LANES
    nh = DIFF_HEADS
    hps = ATT_HEADS_PER_STEP
    ng = nh // hps
    vec = lambda a: pl.BlockSpec(a.shape, lambda b, h, i: (0, 0))
    return pl.pallas_call(
        functools.partial(_diff_attn_kernel, lam_init=lam_init, tile=tile, heads=hps),
        grid=(batch, ng, seq // tile),
        in_specs=[pl.BlockSpec((1, tile, hps * hl), lambda b, h, i: (b, i, h)),
                  pl.BlockSpec((1, seq, hps * hl), lambda b, h, i: (b, 0, ng + h)),
                  pl.BlockSpec((1, seq, hps * hl), lambda b, h, i: (b, 0, 2 * ng + h)),
                  pl.BlockSpec((hps, 2, tile, 2 * tile), lambda b, h, i: (h, 0, 0, 0)),
                  vec(lq1), vec(lk1), vec(lq2), vec(lk2), vec(g)],
        out_specs=pl.BlockSpec((1, tile, hps * hl), lambda b, h, i: (b, i, h)),
        out_shape=jax.ShapeDtypeStruct((batch, seq, nh * hl), BF16),
        scratch_shapes=[pltpu.VMEM((hps, seq // tile, hl + ONES_ROWS, tile), BF16),
                        pltpu.VMEM((2, hps, tile, 2 * tile), F32),
                        pltpu.VMEM((hps, 1, 2 * tile), F32),
                        pltpu.VMEM((hps, hl + ONES_ROWS, 2 * tile), F32)],
        compiler_params=_cp(("parallel", "parallel", "arbitrary")),
    )(qkv, qkv, qkv, bias, lq1, lk1, lq2, lk2, g)


MLA_PAIR = 2


def _mla_attn_kernel(q_ref, k_ref, kr_ref, v_ref, o_ref, kf_scr, vt_scr, s_scr, m_scr, acc_scr, *, tile):
    qi = pl.program_id(2)
    hl = HEAD_LANES

    @pl.when(qi == 0)
    def _():
        for h in range(MLA_PAIR):
            kf_scr[h] = k_ref[0, :, h * hl:(h + 1) * hl] + kr_ref[0]
            _fill_vt(v_ref, vt_scr.at[h], 0, h * MLA_V_DIM, MLA_V_DIM, tile)

    q = [q_ref[0, :, h * hl:(h + 1) * hl] for h in range(MLA_PAIR)]
    m_scr[...] = jnp.full(m_scr.shape, NEG, F32)
    acc_scr[...] = jnp.zeros(acc_scr.shape, F32)

    def scores(slot, kj):
        rows = pl.ds(pl.multiple_of(kj * tile, tile), tile)
        for h in range(MLA_PAIR):
            s_scr[slot, h] = _scores_t(kf_scr[h, rows, :], q[h])

    def update(slot, kj, key_offset=None):
        for h in range(MLA_PAIR):
            s = s_scr[slot, h]
            if key_offset is not None:
                key = lax.broadcasted_iota(jnp.int32, s.shape, 0)
                qry = lax.broadcasted_iota(jnp.int32, s.shape, 1)
                s = jnp.where(key + key_offset <= qry, s, NEG)
            _softmax_step_t(s, vt_scr[h, kj], m_scr.at[h], acc_scr.at[h])

    scores(0, 0)

    def pair_body(j2, carry):
        a = 2 * j2
        scores(1, a + 1)
        update(0, a)
        scores(0, a + 2)
        update(1, a + 1)
        return carry

    lax.fori_loop(0, qi, pair_body, 0)
    t0 = 2 * qi
    scores(1, t0 + 1)
    update(0, t0, 0)
    update(1, t0 + 1, tile)
    o = jnp.concatenate([_normalised(acc_scr[h], MLA_V_DIM) for h in range(MLA_PAIR)], axis=0)
    o_ref[0] = o.T.astype(BF16)


def _mla_attention(q, kv, kr, batch, seq):
    tile = ATT_TILE
    tq = 2 * tile
    hl = HEAD_LANES
    ng = MLA_HEADS // MLA_PAIR
    v_rows = MLA_V_DIM + ONES_ROWS
    return pl.pallas_call(
        functools.partial(_mla_attn_kernel, tile=tile),
        grid=(batch, ng, seq // tq),
        in_specs=[pl.BlockSpec((1, tq, MLA_PAIR * hl), lambda b, h, i: (b, i, h)),
                  pl.BlockSpec((1, seq, MLA_PAIR * hl), lambda b, h, i: (b, 0, h)),
                  pl.BlockSpec((1, seq, hl), lambda b, h, i: (b, 0, 0)),
                  pl.BlockSpec((1, seq, hl), lambda b, h, i: (b, 0, MLA_HEADS + h))],
        out_specs=pl.BlockSpec((1, tq, hl), lambda b, h, i: (b, i, h)),
        out_shape=jax.ShapeDtypeStruct((batch, seq, MLA_HEADS * MLA_V_DIM), BF16),
        scratch_shapes=[pltpu.VMEM((MLA_PAIR, seq, hl), BF16),
                        pltpu.VMEM((MLA_PAIR, seq // tile, v_rows, tile), BF16),
                        pltpu.VMEM((2, MLA_PAIR, tile, tq), F32),
                        pltpu.VMEM((MLA_PAIR, 1, tq), F32),
                        pltpu.VMEM((MLA_PAIR, v_rows, tq), F32)],
        compiler_params=_cp(("parallel", "parallel", "arbitrary")),
    )(q, kv, kr, kv)


def _merge_kernel(xb_ref, oa_ref, ob_ref, wga_ref, wgb_ref, bga_ref, bgb_ref, wbd_ref, wbm_ref, o_ref):
    xb = xb_ref[...]
    g_a = _sigmoid(_dot(xb, wga_ref[...]) + bga_ref[...])
    g_b = _sigmoid(_dot(xb, wgb_ref[...]) + bgb_ref[...])
    m = g_a * _dot(oa_ref[...], wbd_ref[...]) + g_b * _dot(ob_ref[...], wbm_ref[...])
    o_ref[...] = m.astype(BF16)


def _merge(xb, o_a, o_b, w_g, b_g, w_bd, w_bm, tm, tn):
    t = xb.shape[0]
    nj = D_MODEL // tn
    row = lambda a: pl.BlockSpec((tm, a.shape[1]), lambda i, j: (i, 0))
    col = lambda w, off: pl.BlockSpec((w.shape[0], tn), lambda i, j: (0, j + off))
    vec = lambda off: pl.BlockSpec((1, tn), lambda i, j: (0, j + off))
    return pl.pallas_call(
        _merge_kernel,
        grid=(t // tm, nj),
        in_specs=[row(xb), row(o_a), row(o_b), col(w_g, 0), col(w_g, nj), vec(0), vec(nj), col(w_bd, 0),
                  col(w_bm, 0)],
        out_specs=pl.BlockSpec((tm, tn), lambda i, j: (i, j)),
        out_shape=jax.ShapeDtypeStruct((t, D_MODEL), BF16),
        compiler_params=_cp(("parallel", "arbitrary")),
    )(xb, o_a, o_b, w_g, w_g, b_g, b_g, w_bd, w_bm)


def _out_ln_kernel(m_ref, w_ref, x_ref, g_ref, b_ref, xo_ref, xbo_ref, *, alpha):
    r = alpha * x_ref[...] + _dot(m_ref[...], w_ref[...])
    y = _layernorm(r, g_ref[...], b_ref[...])
    xo_ref[...] = y
    xbo_ref[...] = y.astype(BF16)


def _out_ln(m, w_out, x, g, b, alpha, tm):
    t = x.shape[0]
    row = pl.BlockSpec((tm, D_MODEL), lambda i: (i, 0))
    full = lambda a: pl.BlockSpec(a.shape, lambda i: (0, 0))
    return pl.pallas_call(
        functools.partial(_out_ln_kernel, alpha=alpha),
        grid=(t // tm,),
        in_specs=[row, full(w_out), row, full(g), full(b)],
        out_specs=[row, row],
        out_shape=[jax.ShapeDtypeStruct((t, D_MODEL), F32), jax.ShapeDtypeStruct((t, D_MODEL), BF16)],
        compiler_params=_cp(("parallel",)),
    )(m, w_out, x, g, b)


def _swiglu_up_kernel(a_ref, w1_ref, w3_ref, o_ref):
    a = a_ref[...]
    u = _dot(a, w1_ref[...])
    o_ref[...] = (u * _sigmoid(u) * _dot(a, w3_ref[...])).astype(BF16)


def _swiglu_up(xb, w1, w3, tm, tn):
    t = xb.shape[0]
    ff = w1.shape[1]
    return pl.pallas_call(
        _swiglu_up_kernel,
        grid=(t // tm, ff // tn),
        in_specs=[pl.BlockSpec((tm, D_MODEL), lambda i, j: (i, 0)),
                  pl.BlockSpec((D_MODEL, tn), lambda i, j: (0, j)),
                  pl.BlockSpec((D_MODEL, tn), lambda i, j: (0, j))],
        out_specs=pl.BlockSpec((tm, tn), lambda i, j: (i, j)),
        out_shape=jax.ShapeDtypeStruct((t, ff), BF16),
        compiler_params=_cp(("parallel", "arbitrary")),
    )(xb, w1, w3)


def _ple_ln(f, x_ref, xb_ref, p_ref, wpg_ref, wpp_ref, g_ref, b_ref, xo_ref, xbo_ref, alpha):
    e = _sigmoid(_dot(xb_ref[...], wpg_ref[...])) * _dot(p_ref[...].astype(BF16), wpp_ref[...])
    y = _layernorm(alpha * x_ref[...] + f + e, g_ref[...], b_ref[...])
    xo_ref[...] = y
    xbo_ref[...] = y.astype(BF16)


def _dense_ln_kernel(h_ref, w2_ref, x_ref, xb_ref, p_ref, wpg_ref, wpp_ref, g_ref, b_ref, xo_ref, xbo_ref, *, alpha):
    _ple_ln(_dot(h_ref[...], w2_ref[...]), x_ref, xb_ref, p_ref, wpg_ref, wpp_ref, g_ref, b_ref, xo_ref, xbo_ref,
            alpha)


def _moe_ln_kernel(y1_ref, y2_ref, gates_ref, x_ref, xb_ref, p_ref, wpg_ref, wpp_ref, g_ref, b_ref, xo_ref, xbo_ref,
                   *, alpha):
    gates = gates_ref[...]
    f = (y1_ref[...].astype(F32) * gates[:, TOP_K:TOP_K + 1]
         + y2_ref[...].astype(F32) * gates[:, TOP_K + 1:TOP_K + 2])
    _ple_ln(f, x_ref, xb_ref, p_ref, wpg_ref, wpp_ref, g_ref, b_ref, xo_ref, xbo_ref, alpha)


def _channel_ln(body, lead, x, xb, p, w_pg, w_pp, g, b, alpha, tm):
    t = x.shape[0]
    row = lambda a: pl.BlockSpec((tm, a.shape[1]), lambda i: (i, 0))
    full = lambda a: pl.BlockSpec(a.shape, lambda i: (0, 0))
    tail = [x, xb, p, w_pg, w_pp, g, b]
    return pl.pallas_call(
        functools.partial(body, alpha=alpha),
        grid=(t // tm,),
        in_specs=[row(a) if tiled else full(a) for a, tiled in lead]
        + [row(x), row(xb), row(p), full(w_pg), full(w_pp), full(g), full(b)],
        out_specs=[row(x), row(xb)],
        out_shape=[jax.ShapeDtypeStruct((t, D_MODEL), F32), jax.ShapeDtypeStruct((t, D_MODEL), BF16)],
        compiler_params=_cp(("parallel",)),
    )(*[a for a, _ in lead], *tail)


def _router_kernel(x_ref, w_ref, o_ref):
    logits = jnp.dot(x_ref[...], w_ref[...], preferred_element_type=F32, precision=lax.Precision.HIGHEST)
    lane = lax.broadcasted_iota(jnp.int32, logits.shape, 1)
    lg = jnp.where(lane < N_EXPERTS, logits, -jnp.inf)
    v1 = jnp.max(lg, axis=-1, keepdims=True)
    i1 = jnp.min(jnp.where(lg == v1, lane, HEAD_LANES), axis=-1, keepdims=True)
    lg2 = jnp.where(lane == i1, -jnp.inf, lg)
    v2 = jnp.max(lg2, axis=-1, keepdims=True)
    i2 = jnp.min(jnp.where(lg2 == v2, lane, HEAD_LANES), axis=-1, keepdims=True)
    e2 = jnp.exp(v2 - v1)
    g1 = 1.0 / (1.0 + e2)
    g2 = e2 / (1.0 + e2)
    o_ref[...] = jnp.where(lane == 0, i1.astype(F32),
                           jnp.where(lane == 1, i2.astype(F32),
                                     jnp.where(lane == 2, g1, jnp.where(lane == 3, g2, 0.0))))


def _router(x, w_r, tm):
    t = x.shape[0]
    return pl.pallas_call(
        _router_kernel,
        grid=(t // tm,),
        in_specs=[pl.BlockSpec((tm, D_MODEL), lambda i: (i, 0)),
                  pl.BlockSpec(w_r.shape, lambda i: (0, 0))],
        out_specs=pl.BlockSpec((tm, HEAD_LANES), lambda i: (i, 0)),
        out_shape=jax.ShapeDtypeStruct((t, HEAD_LANES), F32),
        compiler_params=_cp(("parallel",)),
    )(x, w_r)


def _expert_changed(be_ref):
    i = pl.program_id(1)
    return jnp.logical_or(i == 0, be_ref[i] != be_ref[jnp.maximum(i - 1, 0)])


def _expert_up_kernel(be_ref, nb_ref, a_ref, w1_ref, w3_ref, o_ref, w1_scr, w3_scr):
    @pl.when(_expert_changed(be_ref))
    def _():
        w1_scr[...] = w1_ref[0].astype(BF16)
        w3_scr[...] = w3_ref[0].astype(BF16)

    @pl.when(pl.program_id(1) < nb_ref[0])
    def _():
        a = a_ref[...]
        u = _dot(a, w1_scr[...])
        o_ref[...] = (u * _sigmoid(u) * _dot(a, w3_scr[...])).astype(BF16)

    @pl.when(pl.program_id(1) >= nb_ref[0])
    def _():
        o_ref[...] = jnp.zeros(o_ref.shape, BF16)


def _expert_up(a, w1, w3, blk_e, n_used, tm, tn):
    rows = a.shape[0]
    ff = w1.shape[2]
    wspec = pl.BlockSpec((1, D_MODEL, tn), lambda j, i, be, nb: (be[i], 0, j))
    grid_spec = pltpu.PrefetchScalarGridSpec(
        num_scalar_prefetch=2,
        grid=(ff // tn, rows // tm),
        in_specs=[pl.BlockSpec((tm, D_MODEL), lambda j, i, be, nb: (i, 0)), wspec, wspec],
        out_specs=pl.BlockSpec((tm, tn), lambda j, i, be, nb: (i, j)),
        scratch_shapes=[pltpu.VMEM((D_MODEL, tn), BF16), pltpu.VMEM((D_MODEL, tn), BF16)],
    )
    return pl.pallas_call(
        _expert_up_kernel,
        grid_spec=grid_spec,
        out_shape=jax.ShapeDtypeStruct((rows, ff), BF16),
        compiler_params=_cp(("arbitrary", "arbitrary")),
    )(blk_e, n_used, a, w1, w3)


def _expert_down_kernel(be_ref, nb_ref, h_ref, w2_ref, o_ref, w2_scr):
    @pl.when(_expert_changed(be_ref))
    def _():
        w2_scr[...] = w2_ref[0].astype(BF16)

    @pl.when(pl.program_id(1) < nb_ref[0])
    def _():
        o_ref[...] = _dot(h_ref[...], w2_scr[...]).astype(BF16)

    @pl.when(pl.program_id(1) >= nb_ref[0])
    def _():
        o_ref[...] = jnp.zeros(o_ref.shape, BF16)


def _expert_down(h, w2, blk_e, n_used, tm, tn):
    rows, ff = h.shape
    grid_spec = pltpu.PrefetchScalarGridSpec(
        num_scalar_prefetch=2,
        grid=(D_MODEL // tn, rows // tm),
        in_specs=[pl.BlockSpec((tm, ff), lambda j, i, be, nb: (i, 0)),
                  pl.BlockSpec((1, ff, tn), lambda j, i, be, nb: (be[i], 0, j))],
        out_specs=pl.BlockSpec((tm, tn), lambda j, i, be, nb: (i, j)),
        scratch_shapes=[pltpu.VMEM((ff, tn), BF16)],
    )
    return pl.pallas_call(
        _expert_down_kernel,
        grid_spec=grid_spec,
        out_shape=jax.ShapeDtypeStruct((rows, D_MODEL), BF16),
        compiler_params=_cp(("arbitrary", "arbitrary")),
    )(blk_e, n_used, h, w2)


def _moe(x, xb, w_r, w1, w3, w2, tm):
    t = x.shape[0]
    m = t * TOP_K
    gb = GROUP_ROWS
    routed = _router(x, w_r, tm)
    flat_e = routed[:, :TOP_K].astype(jnp.int32).reshape(-1)
    onehot = (flat_e[:, None] == jnp.arange(N_EXPERTS, dtype=jnp.int32)[None, :]).astype(jnp.int32)
    csum = jnp.cumsum(onehot, axis=0)
    counts = csum[-1]
    rank = jnp.sum(csum * onehot, axis=1) - 1
    start = jnp.cumsum(counts) - counts
    padded = ((counts + gb - 1) // gb) * gb
    pend = jnp.cumsum(padded)
    pstart = pend - padded
    dest = jnp.sum(pstart[None, :] * onehot, axis=1) + rank
    n_blocks = m // gb + N_EXPERTS
    blk_e = jnp.minimum(jnp.sum((jnp.arange(n_blocks, dtype=jnp.int32)[:, None] * gb >= pend[None, :])
                                .astype(jnp.int32), axis=1), N_EXPERTS - 1)
    n_used = (pend[-1:] // gb).astype(jnp.int32)
    order_tok = (jnp.sort(flat_e * m + jnp.arange(m, dtype=jnp.int32)) % m) // TOP_K
    order_tok = jnp.concatenate([order_tok, jnp.zeros((gb,), jnp.int32)])
    blk_first = start[blk_e] + jnp.arange(n_blocks, dtype=jnp.int32) * gb - pstart[blk_e]
    blk_first = jnp.clip(blk_first, 0, m)
    src_tok = jax.vmap(lambda s: lax.dynamic_slice(order_tok, (s,), (gb,)))(blk_first).reshape(-1)
    a = jnp.take(xb, src_tok, axis=0)
    h = _expert_up(a, w1, w3, blk_e, n_used, gb, EXPERT_FF // 4)
    yb = _expert_down(h, w2, blk_e, n_used, gb, D_MODEL // 2)
    dest2 = dest.reshape(t, TOP_K)
    return jnp.take(yb, dest2[:, 0], axis=0), jnp.take(yb, dest2[:, 1], axis=0), routed


def _t5_bucket(dist):
    n = jnp.maximum(dist, 0)
    max_exact = REL_BUCKETS // 2
    large = max_exact + (jnp.log(jnp.maximum(n, 1).astype(F32) / max_exact)
                         / math.log(REL_MAX_DIST / max_exact) * (REL_BUCKETS - max_exact)).astype(jnp.int32)
    large = jnp.minimum(large, REL_BUCKETS - 1)
    return jnp.where(n < max_exact, n, large)


def _bias_tiles(table, tile):
    a = jnp.arange(tile, dtype=jnp.int32)
    d0 = a[None, :] - a[:, None]
    far = table[REL_BUCKETS - 1].astype(F32)
    f0 = jnp.transpose(table[_t5_bucket(d0)].astype(F32) - far, (2, 0, 1))
    f1 = jnp.transpose(table[_t5_bucket(d0 + tile)].astype(F32) - far, (2, 0, 1))
    f0 = jnp.where((d0 >= 0)[None], f0, NEG)
    tiles = jnp.stack([f0, f1], axis=1)
    return jnp.concatenate([tiles, tiles], axis=-1)


def _rope_tables(positions):
    half = MLA_ROPE_DIM // 2
    inv_freq = ROPE_THETA ** (-jnp.arange(half, dtype=F32) / half)
    ang = positions.astype(F32)[:, :, None] * inv_freq
    c, s = jnp.cos(ang), jnp.sin(ang)
    b, sq = positions.shape
    ones = jnp.ones((b, sq, MLA_NOPE_DIM), F32)
    z_nope = jnp.zeros((b, sq, MLA_NOPE_DIM), F32)
    z_pad = jnp.zeros((b, sq, HEAD_LANES - MLA_NOPE_DIM - MLA_ROPE_DIM), F32)
    cos_t = jnp.concatenate([ones, c, c, z_pad], axis=-1).reshape(b * sq, HEAD_LANES)
    sin_t = jnp.concatenate([z_nope, s, s, z_pad], axis=-1).reshape(b * sq, HEAD_LANES)
    return cos_t, sin_t


def _rot_cols(w):
    half = MLA_ROPE_DIM // 2
    return jnp.concatenate([-w[..., half:], w[..., :half]], axis=-1)


def _mixer_weights(w_in, w_uq, w_ukv):
    d = D_MODEL
    hq = DIFF_HEADS * 2 * DIFF_HEAD_DIM
    o = 0
    w_dq = w_in[:, o:o + hq]; o += hq
    w_dk = w_in[:, o:o + hq]; o += hq
    w_dv = w_in[:, o:o + hq]; o += hq
    w_cq = w_in[:, o:o + MLA_Q_RANK]; o += MLA_Q_RANK
    w_ckv = w_in[:, o:o + MLA_KV_RANK]; o += MLA_KV_RANK
    w_kr = w_in[:, o:o + MLA_ROPE_DIM]; o += MLA_ROPE_DIM
    w_g = w_in[:, o:]
    w_qkv = jnp.concatenate([w_dq, w_dk, w_dv], axis=1).astype(BF16)
    pad = HEAD_LANES - MLA_NOPE_DIM - MLA_ROPE_DIM
    z = lambda n: jnp.zeros((d, n), F32)
    w_c = jnp.concatenate([w_cq, w_ckv, z(MLA_NOPE_DIM), w_kr, z(pad), z(MLA_NOPE_DIM), _rot_cols(w_kr), z(pad)],
                          axis=1).astype(BF16)
    wq = w_uq.reshape(MLA_Q_RANK, MLA_HEADS, MLA_NOPE_DIM + MLA_ROPE_DIM)
    nope, rope = wq[..., :MLA_NOPE_DIM], wq[..., MLA_NOPE_DIM:]
    zq = lambda n: jnp.zeros((MLA_Q_RANK, MLA_HEADS, n), F32)
    plain = jnp.concatenate([nope, rope, zq(pad)], axis=-1).reshape(MLA_Q_RANK, -1)
    rot = jnp.concatenate([zq(MLA_NOPE_DIM), _rot_cols(rope), zq(pad)], axis=-1).reshape(MLA_Q_RANK, -1)
    w_q2 = jnp.concatenate([plain, rot], axis=1).astype(BF16)
    wkv = w_ukv.reshape(MLA_KV_RANK, MLA_HEADS, MLA_NOPE_DIM + MLA_V_DIM)
    zk = jnp.zeros((MLA_KV_RANK, MLA_HEADS, HEAD_LANES - MLA_NOPE_DIM), F32)
    w_kv2 = jnp.concatenate([
        jnp.concatenate([wkv[..., :MLA_NOPE_DIM], zk], axis=-1).reshape(MLA_KV_RANK, -1),
        wkv[..., MLA_NOPE_DIM:].reshape(MLA_KV_RANK, -1)], axis=1).astype(BF16)
    return w_qkv, w_c, w_g.astype(BF16), w_q2, w_kv2


def kernel(x, p, positions, rel_bias_table, w_in, b_gate, lambda_q1, lambda_k1, lambda_q2, lambda_k2,
           diff_subln_g, mla_q_norm_g, w_uq, mla_kv_norm_g, w_ukv, w_branch_diff, w_branch_mla, w_out,
           ln_mix_g, ln_mix_b, dense_w1, dense_w3, dense_w2, router_w, expert_w1, expert_w3, expert_w2,
           w_ple_gate, w_ple_proj, ln_ffn_g, ln_ffn_b):
    batch, seq, d = x.shape
    depth = w_in.shape[0]
    t = batch * seq
    tm = min(ROW_TILE, t)
    tm_mm = BIG_ROW_TILE if t % BIG_ROW_TILE == 0 else tm
    alpha = (2.0 * depth) ** 0.25
    row = lambda v: v.reshape(1, -1).astype(F32)

    cos_t, sin_t = _rope_tables(positions)
    bias = _bias_tiles(rel_bias_table * LOG2E, ATT_TILE)
    xf = x.reshape(t, d)
    xb = xf.astype(BF16)

    for i in range(depth):
        lam_init = 0.8 - 0.6 * math.exp(-0.3 * i)
        w_qkv, w_c, w_g, w_q2, w_kv2 = _mixer_weights(w_in[i], w_uq[i], w_ukv[i])
        qkv = _matmul(xb, w_qkv, BF16, tm_mm, 1024, scaled_tiles=1, scale=DIFF_HEAD_DIM ** -0.5 * LOG2E)
        cq, ckv, kr = _latent(xb, w_c, row(mla_q_norm_g[i]), row(mla_kv_norm_g[i]), cos_t, sin_t, tm)
        q_mla = _qup(cq, w_q2, cos_t, sin_t, tm)
        kv_mla = _matmul(ckv, w_kv2, BF16, tm, w_kv2.shape[1] // 2)
        o_a = _diff_attention(qkv.reshape(batch, seq, -1), bias, row(lambda_q1[i]), row(lambda_k1[i]),
                              row(lambda_q2[i]), row(lambda_k2[i]), row(diff_subln_g[i]), lam_init, batch, seq)
        o_b = _mla_attention(q_mla.reshape(batch, seq, -1), kv_mla.reshape(batch, seq, -1),
                             kr.reshape(batch, seq, -1), batch, seq)
        mix = _merge(xb, o_a.reshape(t, -1), o_b.reshape(t, -1), w_g, row(b_gate[i]),
                     w_branch_diff[i].astype(BF16), w_branch_mla[i].astype(BF16), tm_mm, 512)
        xf, xb = _out_ln(mix, w_out[i].astype(BF16), xf, row(ln_mix_g[i]), row(ln_mix_b[i]), alpha, tm)
        j = i // 2
        p_i = p[i].reshape(t, PLE_DIM)
        w_pg = w_ple_gate[i].astype(BF16)
        w_pp = w_ple_proj[i].astype(BF16)
        if i % 2 == 0:
            hmid = _swiglu_up(xb, dense_w1[j].astype(BF16), dense_w3[j].astype(BF16), tm_mm, DENSE_FF // 2)
            lead = [(hmid, True), (dense_w2[j].astype(BF16), False)]
            body = _dense_ln_kernel
        else:
            w_r = jnp.concatenate([router_w[j], jnp.zeros((d, HEAD_LANES - N_EXPERTS), F32)], axis=1)
            y1, y2, routed = _moe(xf, xb, w_r, expert_w1[j], expert_w3[j], expert_w2[j], tm)
            lead = [(y1, True), (y2, True), (routed, True)]
            body = _moe_ln_kernel
        xf, xb = _channel_ln(body, lead, xf, xb, p_i, w_pg, w_pp, row(ln_ffn_g[i]), row(ln_ffn_b[i]), alpha, tm)
    return xf.reshape(batch, seq, d)
```

```python
import functools
import math

import jax
import jax.numpy as jnp
from jax import lax
from jax.experimental import pallas as pl
from jax.experimental.pallas import tpu as pltpu

F32 = jnp.float32
BF16 = jnp.bfloat16

D_MODEL = 1024
PLE_DIM = 256
DIFF_HEADS = 8
DIFF_HEAD_DIM = 64
MLA_HEADS = 8
MLA_Q_RANK = 384
MLA_KV_RANK = 256
MLA_NOPE_DIM = 64
MLA_ROPE_DIM = 32
MLA_V_DIM = 64
ROPE_THETA = 10000.0
REL_BUCKETS = 32
REL_MAX_DIST = 128
DENSE_FF = 2816
N_EXPERTS = 8
TOP_K = 2
EXPERT_FF = 3584
EPS = 1e-5

HEAD_LANES = 128
NEG = -1e30
LOG2E = math.log2(math.e)
ATT_TILE = 256
ATT_HEADS_PER_STEP = 2
ROW_TILE = 512
BIG_ROW_TILE = 1024
GROUP_ROWS = 512
VMEM_LIMIT = 52 * 1024 * 1024


def _cp(sem):
    return pltpu.CompilerParams(dimension_semantics=sem, vmem_limit_bytes=VMEM_LIMIT)


def _dot(a, b):
    return jnp.dot(a, b, preferred_element_type=F32)


def _sigmoid(x):
    return 1.0 / (1.0 + jnp.exp(-x))


def _layernorm(r, g, b):
    mu = jnp.mean(r, axis=-1, keepdims=True)
    d = r - mu
    var = jnp.mean(d * d, axis=-1, keepdims=True)
    return d * lax.rsqrt(var + EPS) * g + b


def _rmsnorm(x, g):
    return x * lax.rsqrt(jnp.mean(x * x, axis=-1, keepdims=True) + EPS) * g


def _mm_kernel(a_ref, w_ref, o_ref, *, scaled_tiles, scale):
    acc = _dot(a_ref[...], w_ref[...])
    if scaled_tiles:
        acc = acc * jnp.where(pl.program_id(1) < scaled_tiles, scale, 1.0)
    o_ref[...] = acc.astype(o_ref.dtype)


def _matmul(a, w, out_dtype, tm, tn, scaled_tiles=0, scale=1.0):
    m, k = a.shape
    n = w.shape[1]
    return pl.pallas_call(
        functools.partial(_mm_kernel, scaled_tiles=scaled_tiles, scale=scale),
        grid=(m // tm, n // tn),
        in_specs=[pl.BlockSpec((tm, k), lambda i, j: (i, 0)),
                  pl.BlockSpec((k, tn), lambda i, j: (0, j))],
        out_specs=pl.BlockSpec((tm, tn), lambda i, j: (i, j)),
        out_shape=jax.ShapeDtypeStruct((m, n), out_dtype),
        compiler_params=_cp(("parallel", "arbitrary")),
    )(a, w)


C_COLS = MLA_Q_RANK + MLA_KV_RANK + 2 * HEAD_LANES


def _latent_kernel(a_ref, w_ref, gq_ref, gkv_ref, cos_ref, sin_ref, cq_ref, ckv_ref, kr_ref):
    z = _dot(a_ref[...], w_ref[...])
    cq = z[:, :MLA_Q_RANK]
    ckv = z[:, MLA_Q_RANK:MLA_Q_RANK + MLA_KV_RANK]
    kr = z[:, MLA_Q_RANK + MLA_KV_RANK:MLA_Q_RANK + MLA_KV_RANK + HEAD_LANES]
    kr_rot = z[:, MLA_Q_RANK + MLA_KV_RANK + HEAD_LANES:]
    cq_ref[...] = _rmsnorm(cq, gq_ref[...]).astype(BF16)
    ckv_ref[...] = _rmsnorm(ckv, gkv_ref[...]).astype(BF16)
    kr_ref[...] = (kr * cos_ref[...] + kr_rot * sin_ref[...]).astype(BF16)


def _latent(xb, w_c, gq, gkv, cos_t, sin_t, tm):
    t = xb.shape[0]
    row = lambda n: pl.BlockSpec((tm, n), lambda i: (i, 0))
    full = lambda a: pl.BlockSpec(a.shape, lambda i: (0, 0))
    return pl.pallas_call(
        _latent_kernel,
        grid=(t // tm,),
        in_specs=[row(D_MODEL), full(w_c), full(gq), full(gkv), row(HEAD_LANES), row(HEAD_LANES)],
        out_specs=[row(MLA_Q_RANK), row(MLA_KV_RANK), row(HEAD_LANES)],
        out_shape=[jax.ShapeDtypeStruct((t, MLA_Q_RANK), BF16),
                   jax.ShapeDtypeStruct((t, MLA_KV_RANK), BF16),
                   jax.ShapeDtypeStruct((t, HEAD_LANES), BF16)],
        compiler_params=_cp(("parallel",)),
    )(xb, w_c, gq, gkv, cos_t, sin_t)


def _qup_kernel(a_ref, w_ref, cos_ref, sin_ref, o_ref, *, scale):
    z = _dot(a_ref[...], w_ref[...])
    c = cos_ref[...] * scale
    s = sin_ref[...] * scale
    hl = HEAD_LANES
    for h in range(MLA_HEADS):
        o_ref[:, h * hl:(h + 1) * hl] = (
            z[:, h * hl:(h + 1) * hl] * c
            + z[:, (MLA_HEADS + h) * hl:(MLA_HEADS + h + 1) * hl] * s).astype(BF16)


def _qup(cq, w_q2, cos_t, sin_t, tm):
    t = cq.shape[0]
    n = MLA_HEADS * HEAD_LANES
    scale = (MLA_NOPE_DIM + MLA_ROPE_DIM) ** -0.5 * LOG2E
    return pl.pallas_call(
        functools.partial(_qup_kernel, scale=scale),
        grid=(t // tm,),
        in_specs=[pl.BlockSpec((tm, MLA_Q_RANK), lambda i: (i, 0)),
                  pl.BlockSpec(w_q2.shape, lambda i: (0, 0)),
                  pl.BlockSpec((tm, HEAD_LANES), lambda i: (i, 0)),
                  pl.BlockSpec((tm, HEAD_LANES), lambda i: (i, 0))],
        out_specs=pl.BlockSpec((tm, n), lambda i: (i, 0)),
        out_shape=jax.ShapeDtypeStruct((t, n), BF16),
        compiler_params=_cp(("parallel",)),
    )(cq, w_q2, cos_t, sin_t)


ONES_ROWS = 16


def _scores_t(k, q):
    return lax.dot_general(k, q, (((1,), (1,)), ((), ())), preferred_element_type=F32)


def _softmax_step_t(s, vt, m_ref, acc_ref):
    m_old = m_ref[...]
    m_new = jnp.maximum(m_old, jnp.max(s, axis=0, keepdims=True))
    p = jnp.exp2(s - m_new).astype(BF16)
    alpha = jnp.exp2(m_old - m_new)
    acc_ref[...] = alpha * acc_ref[...] + _dot(vt, p)
    m_ref[...] = m_new


def _fill_vt(v_ref, vt_scr, lane0, row0, v_rows, tile):
    for j in range(vt_scr.shape[0]):
        vt = v_ref[0, j * tile:(j + 1) * tile, lane0:lane0 + HEAD_LANES].astype(F32).T
        vt_scr[j, 0:v_rows, :] = vt[row0:row0 + v_rows].astype(BF16)
        vt_scr[j, v_rows:v_rows + ONES_ROWS, :] = jnp.ones((ONES_ROWS, tile), BF16)


def _normalised(acc, v_rows):
    return acc[0:v_rows] * (1.0 / acc[v_rows:v_rows + 1])


def _diff_attn_kernel(q_ref, k_ref, v_ref, bias_ref, lq1_ref, lk1_ref, lq2_ref, lk2_ref, g_ref, o_ref,
                      vt_scr, s_scr, m_scr, acc_scr, *, lam_init, tile, heads):
    qi = pl.program_id(2)
    hd = 2 * DIFF_HEAD_DIM
    hl = HEAD_LANES

    @pl.when(qi == 0)
    def _():
        for h in range(heads):
            _fill_vt(v_ref, vt_scr.at[h], h * hl, 0, hd, tile)

    q_cat = []
    for h in range(heads):
        q = q_ref[0, :, h * hl:(h + 1) * hl]
        lane = lax.broadcasted_iota(jnp.int32, q.shape, 1)
        zero = jnp.zeros_like(q)
        q_cat.append(jnp.concatenate([jnp.where(lane < DIFF_HEAD_DIM, q, zero),
                                      jnp.where(lane >= DIFF_HEAD_DIM, q, zero)], axis=0))
    m_scr[...] = jnp.full(m_scr.shape, NEG, F32)
    acc_scr[...] = jnp.zeros(acc_scr.shape, F32)

    def scores(slot, kj):
        rows = pl.ds(pl.multiple_of(kj * tile, tile), tile)
        for h in range(heads):
            s_scr[slot, h] = _scores_t(k_ref[0, rows, h * hl:(h + 1) * hl], q_cat[h])

    def update(slot, kj, bias_idx=None):
        for h in range(heads):
            s = s_scr[slot, h]
            if bias_idx is not None:
                s = s + bias_ref[h, bias_idx]
            _softmax_step_t(s, vt_scr[h, kj], m_scr.at[h], acc_scr.at[h])

    n_far = jnp.maximum(qi - 1, 0)
    pairs = n_far // 2
    scores(0, 0)

    def pair_body(j2, carry):
        a = 2 * j2
        scores(1, a + 1)
        update(0, a)
        scores(0, a + 2)
        update(1, a + 1)
        return carry

    lax.fori_loop(0, pairs, pair_body, 0)
    t0 = 2 * pairs

    @pl.when(n_far - t0 == 1)
    def _():
        scores(1, t0 + 1)
        update(0, t0)
        scores(0, t0 + 2)
        update(1, t0 + 1, 1)
        update(0, t0 + 2, 0)

    @pl.when(jnp.logical_and(n_far == t0, qi >= 1))
    def _():
        scores(1, t0 + 1)
        update(0, t0, 1)
        update(1, t0 + 1, 0)

    @pl.when(qi == 0)
    def _():
        update(0, 0, 0)

    lam = (jnp.exp(jnp.sum(lq1_ref[...] * lk1_ref[...], axis=-1, keepdims=True))
           - jnp.exp(jnp.sum(lq2_ref[...] * lk2_ref[...], axis=-1, keepdims=True)) + lam_init)
    for h in range(heads):
        acc = acc_scr[h]
        o = _normalised(acc[:, :tile], hd) - lam * _normalised(acc[:, tile:], hd)
        o = o * lax.rsqrt(jnp.mean(o * o, axis=0, keepdims=True) + EPS)
        o_ref[0, :, h * hl:(h + 1) * hl] = (o.T * (g_ref[...] * (1.0 - lam_init))).astype(BF16)


def _diff_attention(qkv, bias, lq1, lk1, lq2, lk2, g, lam_init, batch, seq):
    tile = ATT_TILE
    hl = HEAD_LANES
    nh = DIFF_HEADS
    hps = ATT_HEADS_PER_STEP
    ng = nh // hps
    vec = lambda a: pl.BlockSpec(a.shape, lambda b, h, i: (0, 0))
    return pl.pallas_call(
        functools.partial(_diff_attn_kernel, lam_init=lam_init, tile=tile, heads=hps),
        grid=(batch, ng, seq // tile),
        in_specs=[pl.BlockSpec((1, tile, hps * hl), lambda b, h, i: (b, i, h)),
                  pl.BlockSpec((1, seq, hps * hl), lambda b, h, i: (b, 0, ng + h)),
                  pl.BlockSpec((1, seq, hps * hl), lambda b, h, i: (b, 0, 2 * ng + h)),
                  pl.BlockSpec((hps, 2, tile, 2 * tile), lambda b, h, i: (h, 0, 0, 0)),
                  vec(lq1), vec(lk1), vec(lq2), vec(lk2), vec(g)],
        out_specs=pl.BlockSpec((1, tile, hps * hl), lambda b, h, i: (b, i, h)),
        out_shape=jax.ShapeDtypeStruct((batch, seq, nh * hl), BF16),
        scratch_shapes=[pltpu.VMEM((hps, seq // tile, hl + ONES_ROWS, tile), BF16),
                        pltpu.VMEM((2, hps, tile, 2 * tile), F32),
                        pltpu.VMEM((hps, 1, 2 * tile), F32),
                        pltpu.VMEM((hps, hl + ONES_ROWS, 2 * tile), F32)],
        compiler_params=_cp(("parallel", "parallel", "arbitrary")),
    )(qkv, qkv, qkv, bias, lq1, lk1, lq2, lk2, g)


MLA_PAIR = 2


def _mla_attn_kernel(q_ref, k_ref, kr_ref, v_ref, o_ref, kf_scr, vt_scr, s_scr, m_scr, acc_scr, *, tile):
    qi = pl.program_id(2)
    hl = HEAD_LANES

    @pl.when(qi == 0)
    def _():
        for h in range(MLA_PAIR):
            kf_scr[h] = k_ref[0, :, h * hl:(h + 1) * hl] + kr_ref[0]
            _fill_vt(v_ref, vt_scr.at[h], 0, h * MLA_V_DIM, MLA_V_DIM, tile)

    q = [q_ref[0, :, h * hl:(h + 1) * hl] for h in range(MLA_PAIR)]
    m_scr[...] = jnp.full(m_scr.shape, NEG, F32)
    acc_scr[...] = jnp.zeros(acc_scr.shape, F32)

    def scores(slot, kj):
        rows = pl.ds(pl.multiple_of(kj * tile, tile), tile)
        for h in range(MLA_PAIR):
            s_scr[slot, h] = _scores_t(kf_scr[h, rows, :], q[h])

    def update(slot, kj, key_offset=None):
        for h in range(MLA_PAIR):
            s = s_scr[slot, h]
            if key_offset is not None:
                key = lax.broadcasted_iota(jnp.int32, s.shape, 0)
                qry = lax.broadcasted_iota(jnp.int32, s.shape, 1)
                s = jnp.where(key + key_offset <= qry, s, NEG)
            _softmax_step_t(s, vt_scr[h, kj], m_scr.at[h], acc_scr.at[h])

    scores(0, 0)

    def pair_body(j2, carry):
        a = 2 * j2
        scores(1, a + 1)
        update(0, a)
        scores(0, a + 2)
        update(1, a + 1)
        return carry

    lax.fori_loop(0, qi, pair_body, 0)
    t0 = 2 * qi
    scores(1, t0 + 1)
    update(0, t0, 0)
    update(1, t0 + 1, tile)
    o = jnp.concatenate([_normalised(acc_scr[h], MLA_V_DIM) for h in range(MLA_PAIR)], axis=0)
    o_ref[0] = o.T.astype(BF16)


def _mla_attention(q, kv, kr, batch, seq):
    tile = ATT_TILE
    tq = 2 * tile
    hl = HEAD_LANES
    ng = MLA_HEADS // MLA_PAIR
    v_rows = MLA_V_DIM + ONES_ROWS
    return pl.pallas_call(
        functools.partial(_mla_attn_kernel, tile=tile),
        grid=(batch, ng, seq // tq),
        in_specs=[pl.BlockSpec((1, tq, MLA_PAIR * hl), lambda b, h, i: (b, i, h)),
                  pl.BlockSpec((1, seq, MLA_PAIR * hl), lambda b, h, i: (b, 0, h)),
                  pl.BlockSpec((1, seq, hl), lambda b, h, i: (b, 0, 0)),
                  pl.BlockSpec((1, seq, hl), lambda b, h, i: (b, 0, MLA_HEADS + h))],
        out_specs=pl.BlockSpec((1, tq, hl), lambda b, h, i: (b, i, h)),
        out_shape=jax.ShapeDtypeStruct((batch, seq, MLA_HEADS * MLA_V_DIM), BF16),
        scratch_shapes=[pltpu.VMEM((MLA_PAIR, seq, hl), BF16),
                        pltpu.VMEM((MLA_PAIR, seq // tile, v_rows, tile), BF16),
                        pltpu.VMEM((2, MLA_PAIR, tile, tq), F32),
                        pltpu.VMEM((MLA_PAIR, 1, tq), F32),
                        pltpu.VMEM((MLA_PAIR, v_rows, tq), F32)],
        compiler_params=_cp(("parallel", "parallel", "arbitrary")),
    )(q, kv, kr, kv)


def _merge_kernel(xb_ref, oa_ref, ob_ref, wga_ref, wgb_ref, bga_ref, bgb_ref, wbd_ref, wbm_ref, o_ref):
    xb = xb_ref[...]
    g_a = _sigmoid(_dot(xb, wga_ref[...]) + bga_ref[...])
    g_b = _sigmoid(_dot(xb, wgb_ref[...]) + bgb_ref[...])
    m = g_a * _dot(oa_ref[...], wbd_ref[...]) + g_b * _dot(ob_ref[...], wbm_ref[...])
    o_ref[...] = m.astype(BF16)


def _merge(xb, o_a, o_b, w_g, b_g, w_bd, w_bm, tm, tn):
    t = xb.shape[0]
    nj = D_MODEL // tn
    row = lambda a: pl.BlockSpec((tm, a.shape[1]), lambda i, j: (i, 0))
    col = lambda w, off: pl.BlockSpec((w.shape[0], tn), lambda i, j: (0, j + off))
    vec = lambda off: pl.BlockSpec((1, tn), lambda i, j: (0, j + off))
    return pl.pallas_call(
        _merge_kernel,
        grid=(t // tm, nj),
        in_specs=[row(xb), row(o_a), row(o_b), col(w_g, 0), col(w_g, nj), vec(0), vec(nj), col(w_bd, 0),
                  col(w_bm, 0)],
        out_specs=pl.BlockSpec((tm, tn), lambda i, j: (i, j)),
        out_shape=jax.ShapeDtypeStruct((t, D_MODEL), BF16),
        compiler_params=_cp(("parallel", "arbitrary")),
    )(xb, o_a, o_b, w_g, w_g, b_g, b_g, w_bd, w_bm)


def _out_ln_kernel(m_ref, w_ref, x_ref, g_ref, b_ref, xo_ref, xbo_ref, *, alpha):
    r = alpha * x_ref[...] + _dot(m_ref[...], w_ref[...])
    y = _layernorm(r, g_ref[...], b_ref[...])
    xo_ref[...] = y
    xbo_ref[...] = y.astype(BF16)


def _out_ln(m, w_out, x, g, b, alpha, tm):
    t = x.shape[0]
    row = pl.BlockSpec((tm, D_MODEL), lambda i: (i, 0))
    full = lambda a: pl.BlockSpec(a.shape, lambda i: (0, 0))
    return pl.pallas_call(
        functools.partial(_out_ln_kernel, alpha=alpha),
        grid=(t // tm,),
        in_specs=[row, full(w_out), row, full(g), full(b)],
        out_specs=[row, row],
        out_shape=[jax.ShapeDtypeStruct((t, D_MODEL), F32), jax.ShapeDtypeStruct((t, D_MODEL), BF16)],
        compiler_params=_cp(("parallel",)),
    )(m, w_out, x, g, b)


def _swiglu_up_kernel(a_ref, w1_ref, w3_ref, o_ref):
    a = a_ref[...]
    u = _dot(a, w1_ref[...])
    o_ref[...] = (u * _sigmoid(u) * _dot(a, w3_ref[...])).astype(BF16)


def _swiglu_up(xb, w1, w3, tm, tn):
    t = xb.shape[0]
    ff = w1.shape[1]
    return pl.pallas_call(
        _swiglu_up_kernel,
        grid=(t // tm, ff // tn),
        in_specs=[pl.BlockSpec((tm, D_MODEL), lambda i, j: (i, 0)),
                  pl.BlockSpec((D_MODEL, tn), lambda i, j: (0, j)),
                  pl.BlockSpec((D_MODEL, tn), lambda i, j: (0, j))],
        out_specs=pl.BlockSpec((tm, tn), lambda i, j: (i, j)),
        out_shape=jax.ShapeDtypeStruct((t, ff), BF16),
        compiler_params=_cp(("parallel", "arbitrary")),
    )(xb, w1, w3)


def _ple_ln(f, x_ref, xb_ref, p_ref, wpg_ref, wpp_ref, g_ref, b_ref, xo_ref, xbo_ref, alpha):
    e = _sigmoid(_dot(xb_ref[...], wpg_ref[...])) * _dot(p_ref[...].astype(BF16), wpp_ref[...])
    y = _layernorm(alpha * x_ref[...] + f + e, g_ref[...], b_ref[...])
    xo_ref[...] = y
    xbo_ref[...] = y.astype(BF16)


def _dense_ln_kernel(h_ref, w2_ref, x_ref, xb_ref, p_ref, wpg_ref, wpp_ref, g_ref, b_ref, xo_ref, xbo_ref, *, alpha):
    _ple_ln(_dot(h_ref[...], w2_ref[...]), x_ref, xb_ref, p_ref, wpg_ref, wpp_ref, g_ref, b_ref, xo_ref, xbo_ref,
            alpha)


def _moe_ln_kernel(y1_ref, y2_ref, gates_ref, x_ref, xb_ref, p_ref, wpg_ref, wpp_ref, g_ref, b_ref, xo_ref, xbo_ref,
                   *, alpha):
    gates = gates_ref[...]
    f = (y1_ref[...].astype(F32) * gates[:, TOP_K:TOP_K + 1]
         + y2_ref[...].astype(F32) * gates[:, TOP_K + 1:TOP_K + 2])
    _ple_ln(f, x_ref, xb_ref, p_ref, wpg_ref, wpp_ref, g_ref, b_ref, xo_ref, xbo_ref, alpha)


def _channel_ln(body, lead, x, xb, p, w_pg, w_pp, g, b, alpha, tm):
    t = x.shape[0]
    row = lambda a: pl.BlockSpec((tm, a.shape[1]), lambda i: (i, 0))
    full = lambda a: pl.BlockSpec(a.shape, lambda i: (0, 0))
    tail = [x, xb, p, w_pg, w_pp, g, b]
    return pl.pallas_call(
        functools.partial(body, alpha=alpha),
        grid=(t // tm,),
        in_specs=[row(a) if tiled else full(a) for a, tiled in lead]
        + [row(x), row(xb), row(p), full(w_pg), full(w_pp), full(g), full(b)],
        out_specs=[row(x), row(xb)],
        out_shape=[jax.ShapeDtypeStruct((t, D_MODEL), F32), jax.ShapeDtypeStruct((t, D_MODEL), BF16)],
        compiler_params=_cp(("parallel",)),
    )(*[a for a, _ in lead], *tail)


def _router_kernel(x_ref, w_ref, o_ref):
    logits = jnp.dot(x_ref[...], w_ref[...], preferred_element_type=F32, precision=lax.Precision.HIGHEST)
    lane = lax.broadcasted_iota(jnp.int32, logits.shape, 1)
    lg = jnp.where(lane < N_EXPERTS, logits, -jnp.inf)
    v1 = jnp.max(lg, axis=-1, keepdims=True)
    i1 = jnp.min(jnp.where(lg == v1, lane, HEAD_LANES), axis=-1, keepdims=True)
    lg2 = jnp.where(lane == i1, -jnp.inf, lg)
    v2 = jnp.max(lg2, axis=-1, keepdims=True)
    i2 = jnp.min(jnp.where(lg2 == v2, lane, HEAD_LANES), axis=-1, keepdims=True)
    e2 = jnp.exp(v2 - v1)
    g1 = 1.0 / (1.0 + e2)
    g2 = e2 / (1.0 + e2)
    o_ref[...] = jnp.where(lane == 0, i1.astype(F32),
                           jnp.where(lane == 1, i2.astype(F32),
                                     jnp.where(lane == 2, g1, jnp.where(lane == 3, g2, 0.0))))


def _router(x, w_r, tm):
    t = x.shape[0]
    return pl.pallas_call(
        _router_kernel,
        grid=(t // tm,),
        in_specs=[pl.BlockSpec((tm, D_MODEL), lambda i: (i, 0)),
                  pl.BlockSpec(w_r.shape, lambda i: (0, 0))],
        out_specs=pl.BlockSpec((tm, HEAD_LANES), lambda i: (i, 0)),
        out_shape=jax.ShapeDtypeStruct((t, HEAD_LANES), F32),
        compiler_params=_cp(("parallel",)),
    )(x, w_r)


def _expert_changed(be_ref):
    i = pl.program_id(1)
    return jnp.logical_or(i == 0, be_ref[i] != be_ref[jnp.maximum(i - 1, 0)])


def _expert_up_kernel(be_ref, nb_ref, a_ref, w1_ref, w3_ref, o_ref, w1_scr, w3_scr):
    @pl.when(_expert_changed(be_ref))
    def _():
        w1_scr[...] = w1_ref[0].astype(BF16)
        w3_scr[...] = w3_ref[0].astype(BF16)

    @pl.when(pl.program_id(1) < nb_ref[0])
    def _():
        a = a_ref[...]
        u = _dot(a, w1_scr[...])
        o_ref[...] = (u * _sigmoid(u) * _dot(a, w3_scr[...])).astype(BF16)

    @pl.when(pl.program_id(1) >= nb_ref[0])
    def _():
        o_ref[...] = jnp.zeros(o_ref.shape, BF16)


def _expert_up(a, w1, w3, layer, blk_e, n_used, tm, tn):
    rows = a.shape[0]
    ff = w1.shape[3]
    wspec = pl.BlockSpec((None, 1, D_MODEL, tn), lambda j, i, be, nb: (layer, be[i], 0, j),
                         pipeline_mode=pl.Buffered(1))
    grid_spec = pltpu.PrefetchScalarGridSpec(
        num_scalar_prefetch=2,
        grid=(ff // tn, rows // tm),
        in_specs=[pl.BlockSpec((tm, D_MODEL), lambda j, i, be, nb: (i, 0)), wspec, wspec],
        out_specs=pl.BlockSpec((tm, tn), lambda j, i, be, nb: (i, j)),
        scratch_shapes=[pltpu.VMEM((D_MODEL, tn), BF16), pltpu.VMEM((D_MODEL, tn), BF16)],
    )
    return pl.pallas_call(
        _expert_up_kernel,
        grid_spec=grid_spec,
        out_shape=jax.ShapeDtypeStruct((rows, ff), BF16),
        compiler_params=_cp(("arbitrary", "arbitrary")),
    )(blk_e, n_used, a, w1, w3)


def _expert_down_kernel(be_ref, nb_ref, h_ref, w2_ref, o_ref, w2_scr):
    @pl.when(_expert_changed(be_ref))
    def _():
        w2_scr[...] = w2_ref[0].astype(BF16)

    @pl.when(pl.program_id(1) < nb_ref[0])
    def _():
        o_ref[...] = _dot(h_ref[...], w2_scr[...]).astype(BF16)

    @pl.when(pl.program_id(1) >= nb_ref[0])
    def _():
        o_ref[...] = jnp.zeros(o_ref.shape, BF16)


def _expert_down(h, w2, layer, blk_e, n_used, tm, tn):
    rows, ff = h.shape
    grid_spec = pltpu.PrefetchScalarGridSpec(
        num_scalar_prefetch=2,
        grid=(D_MODEL // tn, rows // tm),
        in_specs=[pl.BlockSpec((tm, ff), lambda j, i, be, nb: (i, 0)),
                  pl.BlockSpec((None, 1, ff, tn), lambda j, i, be, nb: (layer, be[i], 0, j),
                               pipeline_mode=pl.Buffered(1))],
        out_specs=pl.BlockSpec((tm, tn), lambda j, i, be, nb: (i, j)),
        scratch_shapes=[pltpu.VMEM((ff, tn), BF16)],
    )
    return pl.pallas_call(
        _expert_down_kernel,
        grid_spec=grid_spec,
        out_shape=jax.ShapeDtypeStruct((rows, D_MODEL), BF16),
        compiler_params=_cp(("arbitrary", "arbitrary")),
    )(blk_e, n_used, h, w2)


def _moe(x, xb, w_r, w1, w3, w2, layer, tm):
    t = x.shape[0]
    m = t * TOP_K
    gb = GROUP_ROWS
    routed = _router(x, w_r, tm)
    flat_e = routed[:, :TOP_K].astype(jnp.int32).reshape(-1)
    onehot = (flat_e[:, None] == jnp.arange(N_EXPERTS, dtype=jnp.int32)[None, :]).astype(jnp.int32)
    csum = jnp.cumsum(onehot, axis=0)
    counts = csum[-1]
    rank = jnp.sum(csum * onehot, axis=1) - 1
    padded = ((counts + gb - 1) // gb) * gb
    pend = jnp.cumsum(padded)
    pstart = pend - padded
    dest = jnp.sum(pstart[None, :] * onehot, axis=1) + rank
    n_blocks = m // gb + N_EXPERTS
    blk_e = jnp.minimum(jnp.sum((jnp.arange(n_blocks, dtype=jnp.int32)[:, None] * gb >= pend[None, :])
                                .astype(jnp.int32), axis=1), N_EXPERTS - 1)
    n_used = (pend[-1:] // gb).astype(jnp.int32)
    n_rows = n_blocks * gb
    gap = padded - counts
    gap_end = jnp.cumsum(gap)
    i_dummy = jnp.arange(n_rows - m, dtype=jnp.int32)
    in_gap = (i_dummy[:, None] >= (gap_end - gap)[None, :]) & (i_dummy[:, None] < gap_end[None, :])
    dummy_key = jnp.where(i_dummy < gap_end[-1],
                          jnp.sum(jnp.where(in_gap, (pstart + counts - (gap_end - gap))[None, :], 0), axis=1),
                          pend[-1] - gap_end[-1]) + i_dummy
    keys = jnp.concatenate([dest, dummy_key])
    toks = jnp.concatenate([jnp.arange(m, dtype=jnp.int32) // TOP_K, jnp.zeros((n_rows - m,), jnp.int32)])
    _, src_tok = lax.sort((keys, toks), num_keys=1)
    take = lambda arr, rows: arr.at[rows].get(mode="promise_in_bounds")
    a = take(xb, src_tok)
    h = _expert_up(a, w1, w3, layer, blk_e, n_used, gb, EXPERT_FF // 2)
    yb = _expert_down(h, w2, layer, blk_e, n_used, gb, D_MODEL)
    dest2 = dest.reshape(t, TOP_K)
    return take(yb, dest2[:, 0]), take(yb, dest2[:, 1]), routed


def _t5_bucket(dist):
    n = jnp.maximum(dist, 0)
    max_exact = REL_BUCKETS // 2
    large = max_exact + (jnp.log(jnp.maximum(n, 1).astype(F32) / max_exact)
                         / math.log(REL_MAX_DIST / max_exact) * (REL_BUCKETS - max_exact)).astype(jnp.int32)
    large = jnp.minimum(large, REL_BUCKETS - 1)
    return jnp.where(n < max_exact, n, large)


def _bias_tiles(table, tile):
    a = jnp.arange(tile, dtype=jnp.int32)
    d0 = a[None, :] - a[:, None]
    rel = table.astype(F32) - table[REL_BUCKETS - 1].astype(F32)[None, :]

    def lookup(dist):
        onehot = (_t5_bucket(dist)[None, :, :] == jnp.arange(REL_BUCKETS, dtype=jnp.int32)[:, None, None])
        return jnp.sum(jnp.where(onehot[:, None], rel[:, :, None, None], 0.0), axis=0)

    f0 = lookup(d0)
    f1 = lookup(d0 + tile)
    f0 = jnp.where((d0 >= 0)[None], f0, NEG)
    tiles = jnp.stack([f0, f1], axis=1)
    return jnp.concatenate([tiles, tiles], axis=-1)


def _rope_tables(positions):
    half = MLA_ROPE_DIM // 2
    inv_freq = ROPE_THETA ** (-jnp.arange(half, dtype=F32) / half)
    ang = positions.astype(F32)[:, :, None] * inv_freq
    c, s = jnp.cos(ang), jnp.sin(ang)
    b, sq = positions.shape
    ones = jnp.ones((b, sq, MLA_NOPE_DIM), F32)
    z_nope = jnp.zeros((b, sq, MLA_NOPE_DIM), F32)
    z_pad = jnp.zeros((b, sq, HEAD_LANES - MLA_NOPE_DIM - MLA_ROPE_DIM), F32)
    cos_t = jnp.concatenate([ones, c, c, z_pad], axis=-1).reshape(b * sq, HEAD_LANES)
    sin_t = jnp.concatenate([z_nope, s, s, z_pad], axis=-1).reshape(b * sq, HEAD_LANES)
    return cos_t, sin_t


def _rot_cols(w):
    half = MLA_ROPE_DIM // 2
    return jnp.concatenate([-w[..., half:], w[..., :half]], axis=-1)


def _mixer_weights(w_in, w_uq, w_ukv):
    d = D_MODEL
    hq = DIFF_HEADS * 2 * DIFF_HEAD_DIM
    o = 0
    w_dq = w_in[:, o:o + hq]; o += hq
    w_dk = w_in[:, o:o + hq]; o += hq
    w_dv = w_in[:, o:o + hq]; o += hq
    w_cq = w_in[:, o:o + MLA_Q_RANK]; o += MLA_Q_RANK
    w_ckv = w_in[:, o:o + MLA_KV_RANK]; o += MLA_KV_RANK
    w_kr = w_in[:, o:o + MLA_ROPE_DIM]; o += MLA_ROPE_DIM
    w_g = w_in[:, o:]
    w_qkv = jnp.concatenate([w_dq, w_dk, w_dv], axis=1).astype(BF16)
    pad = HEAD_LANES - MLA_NOPE_DIM - MLA_ROPE_DIM
    z = lambda n: jnp.zeros((d, n), F32)
    w_c = jnp.concatenate([w_cq, w_ckv, z(MLA_NOPE_DIM), w_kr, z(pad), z(MLA_NOPE_DIM), _rot_cols(w_kr), z(pad)],
                          axis=1).astype(BF16)
    wq = w_uq.reshape(MLA_Q_RANK, MLA_HEADS, MLA_NOPE_DIM + MLA_ROPE_DIM)
    nope, rope = wq[..., :MLA_NOPE_DIM], wq[..., MLA_NOPE_DIM:]
    zq = lambda n: jnp.zeros((MLA_Q_RANK, MLA_HEADS, n), F32)
    plain = jnp.concatenate([nope, rope, zq(pad)], axis=-1).reshape(MLA_Q_RANK, -1)
    rot = jnp.concatenate([zq(MLA_NOPE_DIM), _rot_cols(rope), zq(pad)], axis=-1).reshape(MLA_Q_RANK, -1)
    w_q2 = jnp.concatenate([plain, rot], axis=1).astype(BF16)
    wkv = w_ukv.reshape(MLA_KV_RANK, MLA_HEADS, MLA_NOPE_DIM + MLA_V_DIM)
    zk = jnp.zeros((MLA_KV_RANK, MLA_HEADS, HEAD_LANES - MLA_NOPE_DIM), F32)
    w_kv2 = jnp.concatenate([
        jnp.concatenate([wkv[..., :MLA_NOPE_DIM], zk], axis=-1).reshape(MLA_KV_RANK, -1),
        wkv[..., MLA_NOPE_DIM:].reshape(MLA_KV_RANK, -1)], axis=1).astype(BF16)
    return w_qkv, w_c, w_g.astype(BF16), w_q2, w_kv2


def kernel(x, p, positions, rel_bias_table, w_in, b_gate, lambda_q1, lambda_k1, lambda_q2, lambda_k2,
           diff_subln_g, mla_q_norm_g, w_uq, mla_kv_norm_g, w_ukv, w_branch_diff, w_branch_mla, w_out,
           ln_mix_g, ln_mix_b, dense_w1, dense_w3, dense_w2, router_w, expert_w1, expert_w3, expert_w2,
           w_ple_gate, w_ple_proj, ln_ffn_g, ln_ffn_b):
    batch, seq, d = x.shape
    depth = w_in.shape[0]
    t = batch * seq
    tm = min(ROW_TILE, t)
    tm_mm = BIG_ROW_TILE if t % BIG_ROW_TILE == 0 else tm
    alpha = (2.0 * depth) ** 0.25
    row = lambda v: v.reshape(1, -1).astype(F32)

    cos_t, sin_t = _rope_tables(positions)
    bias = _bias_tiles(rel_bias_table * LOG2E, ATT_TILE)
    xf = x.reshape(t, d)
    xb = xf.astype(BF16)

    for i in range(depth):
        lam_init = 0.8 - 0.6 * math.exp(-0.3 * i)
        w_qkv, w_c, w_g, w_q2, w_kv2 = _mixer_weights(w_in[i], w_uq[i], w_ukv[i])
        qkv = _matmul(xb, w_qkv, BF16, tm_mm, 1024, scaled_tiles=1, scale=DIFF_HEAD_DIM ** -0.5 * LOG2E)
        cq, ckv, kr = _latent(xb, w_c, row(mla_q_norm_g[i]), row(mla_kv_norm_g[i]), cos_t, sin_t, tm)
        q_mla = _qup(cq, w_q2, cos_t, sin_t, tm)
        kv_mla = _matmul(ckv, w_kv2, BF16, tm, w_kv2.shape[1] // 2)
        o_a = _diff_attention(qkv.reshape(batch, seq, -1), bias, row(lambda_q1[i]), row(lambda_k1[i]),
                              row(lambda_q2[i]), row(lambda_k2[i]), row(diff_subln_g[i]), lam_init, batch, seq)
        o_b = _mla_attention(q_mla.reshape(batch, seq, -1), kv_mla.reshape(batch, seq, -1),
                             kr.reshape(batch, seq, -1), batch, seq)
        mix = _merge(xb, o_a.reshape(t, -1), o_b.reshape(t, -1), w_g, row(b_gate[i]),
                     w_branch_diff[i].astype(BF16), w_branch_mla[i].astype(BF16), tm_mm, 512)
        xf, xb = _out_ln(mix, w_out[i].astype(BF16), xf, row(ln_mix_g[i]), row(ln_mix_b[i]), alpha, tm)
        j = i // 2
        p_i = p[i].reshape(t, PLE_DIM)
        w_pg = w_ple_gate[i].astype(BF16)
        w_pp = w_ple_proj[i].astype(BF16)
        if i % 2 == 0:
            hmid = _swiglu_up(xb, dense_w1[j].astype(BF16), dense_w3[j].astype(BF16), tm_mm, DENSE_FF // 2)
            lead = [(hmid, True), (dense_w2[j].astype(BF16), False)]
            body = _dense_ln_kernel
        else:
            w_r = jnp.concatenate([router_w[j], jnp.zeros((d, HEAD_LANES - N_EXPERTS), F32)], axis=1)
            y1, y2, routed = _moe(xf, xb, w_r, expert_w1, expert_w3, expert_w2, j, tm)
            lead = [(y1, True), (y2, True), (routed, True)]
            body = _moe_ln_kernel
        xf, xb = _channel_ln(body, lead, xf, xb, p_i, w_pg, w_pp, row(ln_ffn_g[i]), row(ln_ffn_b[i]), alpha, tm)
    return xf.reshape(batch, seq, d)
```

```python
import functools
import math

import jax
import jax.numpy as jnp
from jax import lax
from jax.experimental import pallas as pl
from jax.experimental.pallas import tpu as pltpu

F32 = jnp.float32
BF16 = jnp.bfloat16

D_MODEL = 1024
PLE_DIM = 256
DIFF_HEADS = 8
DIFF_HEAD_DIM = 64
MLA_HEADS = 8
MLA_Q_RANK = 384
MLA_KV_RANK = 256
MLA_NOPE_DIM = 64
MLA_ROPE_DIM = 32
MLA_V_DIM = 64
ROPE_THETA = 10000.0
REL_BUCKETS = 32
REL_MAX_DIST = 128
DENSE_FF = 2816
N_EXPERTS = 8
TOP_K = 2
EXPERT_FF = 3584
EPS = 1e-5

HEAD_LANES = 128
NEG = -1e30
LOG2E = math.log2(math.e)
ATT_TILE = 256
ATT_HEADS_PER_STEP = 4
ROW_TILE = 512
BIG_ROW_TILE = 1024
GROUP_ROWS = 512
VMEM_LIMIT = 52 * 1024 * 1024


def _cp(sem):
    return pltpu.CompilerParams(dimension_semantics=sem, vmem_limit_bytes=VMEM_LIMIT)


def _dot(a, b):
    return jnp.dot(a, b, preferred_element_type=F32)


def _sigmoid(x):
    return 1.0 / (1.0 + jnp.exp(-x))


def _layernorm(r, g, b):
    mu = jnp.mean(r, axis=-1, keepdims=True)
    d = r - mu
    var = jnp.mean(d * d, axis=-1, keepdims=True)
    return d * lax.rsqrt(var + EPS) * g + b


def _rmsnorm(x, g):
    return x * lax.rsqrt(jnp.mean(x * x, axis=-1, keepdims=True) + EPS) * g


def _mm_kernel(a_ref, w_ref, o_ref, *, scaled_tiles, scale):
    acc = _dot(a_ref[...], w_ref[...])
    if scaled_tiles:
        acc = acc * jnp.where(pl.program_id(1) < scaled_tiles, scale, 1.0)
    o_ref[...] = acc.astype(o_ref.dtype)


def _matmul(a, w, out_dtype, tm, tn, scaled_tiles=0, scale=1.0):
    m, k = a.shape
    n = w.shape[1]
    return pl.pallas_call(
        functools.partial(_mm_kernel, scaled_tiles=scaled_tiles, scale=scale),
        grid=(m // tm, n // tn),
        in_specs=[pl.BlockSpec((tm, k), lambda i, j: (i, 0)),
                  pl.BlockSpec((k, tn), lambda i, j: (0, j))],
        out_specs=pl.BlockSpec((tm, tn), lambda i, j: (i, j)),
        out_shape=jax.ShapeDtypeStruct((m, n), out_dtype),
        compiler_params=_cp(("parallel", "arbitrary")),
    )(a, w)


C_COLS = MLA_Q_RANK + MLA_KV_RANK + 2 * HEAD_LANES


def _resident(a):
    return pl.BlockSpec(a.shape, lambda *_: (0,) * a.ndim, pipeline_mode=pl.Buffered(1))


def _latent_kernel(a_ref, wc_ref, wq_ref, wkv_ref, gq_ref, gkv_ref, cos_ref, sin_ref, q_ref, kv_ref, kr_ref, *,
                   scale):
    z = _dot(a_ref[...], wc_ref[...])
    cq = z[:, :MLA_Q_RANK]
    ckv = z[:, MLA_Q_RANK:MLA_Q_RANK + MLA_KV_RANK]
    kr = z[:, MLA_Q_RANK + MLA_KV_RANK:MLA_Q_RANK + MLA_KV_RANK + HEAD_LANES]
    kr_rot = z[:, MLA_Q_RANK + MLA_KV_RANK + HEAD_LANES:]
    cos, sin = cos_ref[...], sin_ref[...]
    kr_ref[...] = (kr * cos + kr_rot * sin).astype(BF16)
    kv_ref[...] = _dot(_rmsnorm(ckv, gkv_ref[...]).astype(BF16), wkv_ref[...]).astype(BF16)
    zq = _dot(_rmsnorm(cq, gq_ref[...]).astype(BF16), wq_ref[...])
    c = cos * scale
    s = sin * scale
    hl = HEAD_LANES
    for h in range(MLA_HEADS):
        q_ref[:, h * hl:(h + 1) * hl] = (
            zq[:, h * hl:(h + 1) * hl] * c
            + zq[:, (MLA_HEADS + h) * hl:(MLA_HEADS + h + 1) * hl] * s).astype(BF16)


def _latent(xb, w_c, w_q2, w_kv2, gq, gkv, cos_t, sin_t, tm):
    t = xb.shape[0]
    row = lambda n: pl.BlockSpec((tm, n), lambda i: (i, 0))
    nq = MLA_HEADS * HEAD_LANES
    nkv = w_kv2.shape[1]
    scale = (MLA_NOPE_DIM + MLA_ROPE_DIM) ** -0.5 * LOG2E
    return pl.pallas_call(
        functools.partial(_latent_kernel, scale=scale),
        grid=(t // tm,),
        in_specs=[row(D_MODEL), _resident(w_c), _resident(w_q2), _resident(w_kv2), _resident(gq), _resident(gkv),
                  row(HEAD_LANES), row(HEAD_LANES)],
        out_specs=[row(nq), row(nkv), row(HEAD_LANES)],
        out_shape=[jax.ShapeDtypeStruct((t, nq), BF16),
                   jax.ShapeDtypeStruct((t, nkv), BF16),
                   jax.ShapeDtypeStruct((t, HEAD_LANES), BF16)],
        compiler_params=_cp(("parallel",)),
    )(xb, w_c, w_q2, w_kv2, gq, gkv, cos_t, sin_t)


ONES_ROWS = 16


def _scores_t(k, q):
    return lax.dot_general(k, q, (((1,), (1,)), ((), ())), preferred_element_type=F32)


def _softmax_step_t(s, vt, m_ref, acc_ref):
    m_old = m_ref[...]
    m_new = jnp.maximum(m_old, jnp.max(s, axis=0, keepdims=True))
    p = jnp.exp2(s - m_new).astype(BF16)
    alpha = jnp.exp2(m_old - m_new)
    acc_ref[...] = alpha * acc_ref[...] + _dot(vt, p)
    m_ref[...] = m_new


def _fill_vt(v_ref, vt_scr, lane0, row0, v_rows, tile):
    for j in range(vt_scr.shape[0]):
        vt = v_ref[0, j * tile:(j + 1) * tile, lane0:lane0 + HEAD_LANES].astype(F32).T
        vt_scr[j, 0:v_rows, :] = vt[row0:row0 + v_rows].astype(BF16)
        vt_scr[j, v_rows:v_rows + ONES_ROWS, :] = jnp.ones((ONES_ROWS, tile), BF16)


def _normalised(acc, v_rows):
    return acc[0:v_rows] * (1.0 / acc[v_rows:v_rows + 1])


def _diff_attn_kernel(q_ref, k_ref, v_ref, bias_ref, lq1_ref, lk1_ref, lq2_ref, lk2_ref, g_ref, o_ref,
                      vt_scr, s_scr, m_scr, acc_scr, *, lam_init, tile, heads):
    qi = pl.program_id(2)
    hd = 2 * DIFF_HEAD_DIM
    hl = HEAD_LANES

    @pl.when(qi == 0)
    def _():
        for h in range(heads):
            _fill_vt(v_ref, vt_scr.at[h], h * hl, 0, hd, tile)

    q_cat = []
    for h in range(heads):
        q = q_ref[0, :, h * hl:(h + 1) * hl]
        lane = lax.broadcasted_iota(jnp.int32, q.shape, 1)
        zero = jnp.zeros_like(q)
        q_cat.append(jnp.concatenate([jnp.where(lane < DIFF_HEAD_DIM, q, zero),
                                      jnp.where(lane >= DIFF_HEAD_DIM, q, zero)], axis=0))
    m_scr[...] = jnp.full(m_scr.shape, NEG, F32)
    acc_scr[...] = jnp.zeros(acc_scr.shape, F32)

    def scores(slot, kj):
        rows = pl.ds(pl.multiple_of(kj * tile, tile), tile)
        for h in range(heads):
            s_scr[slot, h] = _scores_t(k_ref[0, rows, h * hl:(h + 1) * hl], q_cat[h])

    def update(slot, kj, bias_idx=None):
        for h in range(heads):
            s = s_scr[slot, h]
            if bias_idx is not None:
                s = s + bias_ref[h, bias_idx]
            _softmax_step_t(s, vt_scr[h, kj], m_scr.at[h], acc_scr.at[h])

    n_far = jnp.maximum(qi - 1, 0)
    pairs = n_far // 2
    scores(0, 0)

    def pair_body(j2, carry):
        a = 2 * j2
        scores(1, a + 1)
        update(0, a)
        scores(0, a + 2)
        update(1, a + 1)
        return carry

    lax.fori_loop(0, pairs, pair_body, 0)
    t0 = 2 * pairs

    @pl.when(n_far - t0 == 1)
    def _():
        scores(1, t0 + 1)
        update(0, t0)
        scores(0, t0 + 2)
        update(1, t0 + 1, 1)
        update(0, t0 + 2, 0)

    @pl.when(jnp.logical_and(n_far == t0, qi >= 1))
    def _():
        scores(1, t0 + 1)
        update(0, t0, 1)
        update(1, t0 + 1, 0)

    @pl.when(qi == 0)
    def _():
        update(0, 0, 0)

    lam = (jnp.exp(jnp.sum(lq1_ref[...] * lk1_ref[...], axis=-1, keepdims=True))
           - jnp.exp(jnp.sum(lq2_ref[...] * lk2_ref[...], axis=-1, keepdims=True)) + lam_init)
    for h in range(heads):
        acc = acc_scr[h]
        o = _normalised(acc[:, :tile], hd) - lam * _normalised(acc[:, tile:], hd)
        o = o * lax.rsqrt(jnp.mean(o * o, axis=0, keepdims=True) + EPS)
        o_ref[0, :, h * hl:(h + 1) * hl] = (o.T * (g_ref[...] * (1.0 - lam_init))).astype(BF16)


def _diff_attention(qkv, bias, lq1, lk1, lq2, lk2, g, lam_init, batch, seq):
    tile = ATT_TILE
    hl = HEAD_LANES
    nh = DIFF_HEADS
    hps = ATT_HEADS_PER_STEP
    ng = nh // hps
    vec = lambda a: pl.BlockSpec(a.shape, lambda b, h, i: (0, 0))
    return pl.pallas_call(
        functools.partial(_diff_attn_kernel, lam_init=lam_init, tile=tile, heads=hps),
        grid=(batch, ng, seq // tile),
        in_specs=[pl.BlockSpec((1, tile, hps * hl), lambda b, h, i: (b, i, h)),
                  pl.BlockSpec((1, seq, hps * hl), lambda b, h, i: (b, 0, ng + h)),
                  pl.BlockSpec((1, seq, hps * hl), lambda b, h, i: (b, 0, 2 * ng + h)),
                  pl.BlockSpec((hps, 2, tile, 2 * tile), lambda b, h, i: (h, 0, 0, 0)),
                  vec(lq1), vec(lk1), vec(lq2), vec(lk2), vec(g)],
        out_specs=pl.BlockSpec((1, tile, hps * hl), lambda b, h, i: (b, i, h)),
        out_shape=jax.ShapeDtypeStruct((batch, seq, nh * hl), BF16),
        scratch_shapes=[pltpu.VMEM((hps, seq // tile, hl + ONES_ROWS, tile), BF16),
                        pltpu.VMEM((2, hps, tile, 2 * tile), F32),
                        pltpu.VMEM((hps, 1, 2 * tile), F32),
                        pltpu.VMEM((hps, hl + ONES_ROWS, 2 * tile), F32)],
        compiler_params=_cp(("parallel", "parallel", "arbitrary")),
    )(qkv, qkv, qkv, bias, lq1, lk1, lq2, lk2, g)


MLA_PAIR = 2
MLA_STEP_HEADS = 4


def _mla_attn_kernel(q_ref, k_ref, kr_ref, v_ref, o_ref, kf_scr, vt_scr, s_scr, m_scr, acc_scr, *, tile):
    qi = pl.program_id(2)
    hl = HEAD_LANES

    @pl.when(qi == 0)
    def _():
        for h in range(MLA_STEP_HEADS):
            kf_scr[h] = k_ref[0, :, h * hl:(h + 1) * hl] + kr_ref[0]
            _fill_vt(v_ref, vt_scr.at[h], (h // MLA_PAIR) * hl, (h % MLA_PAIR) * MLA_V_DIM, MLA_V_DIM, tile)

    q = [q_ref[0, :, h * hl:(h + 1) * hl] for h in range(MLA_STEP_HEADS)]
    m_scr[...] = jnp.full(m_scr.shape, NEG, F32)
    acc_scr[...] = jnp.zeros(acc_scr.shape, F32)

    def scores(slot, kj):
        rows = pl.ds(pl.multiple_of(kj * tile, tile), tile)
        for h in range(MLA_STEP_HEADS):
            s_scr[slot, h] = _scores_t(kf_scr[h, rows, :], q[h])

    def update(slot, kj, key_offset=None):
        for h in range(MLA_STEP_HEADS):
            s = s_scr[slot, h]
            if key_offset is not None:
                key = lax.broadcasted_iota(jnp.int32, s.shape, 0)
                qry = lax.broadcasted_iota(jnp.int32, s.shape, 1)
                s = jnp.where(key + key_offset <= qry, s, NEG)
            _softmax_step_t(s, vt_scr[h, kj], m_scr.at[h], acc_scr.at[h])

    scores(0, 0)

    def pair_body(j2, carry):
        a = 2 * j2
        scores(1, a + 1)
        update(0, a)
        scores(0, a + 2)
        update(1, a + 1)
        return carry

    lax.fori_loop(0, qi, pair_body, 0)
    t0 = 2 * qi
    scores(1, t0 + 1)
    update(0, t0, 0)
    update(1, t0 + 1, tile)
    o = jnp.concatenate([_normalised(acc_scr[h], MLA_V_DIM) for h in range(MLA_STEP_HEADS)], axis=0)
    o_ref[0] = o.T.astype(BF16)


def _mla_attention(q, kv, kr, batch, seq):
    tile = ATT_TILE
    tq = 2 * tile
    hl = HEAD_LANES
    nh = MLA_STEP_HEADS
    ng = MLA_HEADS // nh
    v_lanes = nh * MLA_V_DIM
    v_rows = MLA_V_DIM + ONES_ROWS
    return pl.pallas_call(
        functools.partial(_mla_attn_kernel, tile=tile),
        grid=(batch, ng, seq // tq),
        in_specs=[pl.BlockSpec((1, tq, nh * hl), lambda b, h, i: (b, i, h)),
                  pl.BlockSpec((1, seq, nh * hl), lambda b, h, i: (b, 0, h)),
                  pl.BlockSpec((1, seq, hl), lambda b, h, i: (b, 0, 0)),
                  pl.BlockSpec((1, seq, v_lanes), lambda b, h, i: (b, 0, MLA_HEADS * hl // v_lanes + h))],
        out_specs=pl.BlockSpec((1, tq, v_lanes), lambda b, h, i: (b, i, h)),
        out_shape=jax.ShapeDtypeStruct((batch, seq, MLA_HEADS * MLA_V_DIM), BF16),
        scratch_shapes=[pltpu.VMEM((nh, seq, hl), BF16),
                        pltpu.VMEM((nh, seq // tile, v_rows, tile), BF16),
                        pltpu.VMEM((2, nh, tile, tq), F32),
                        pltpu.VMEM((nh, 1, tq), F32),
                        pltpu.VMEM((nh, v_rows, tq), F32)],
        compiler_params=_cp(("parallel", "parallel", "arbitrary")),
    )(q, kv, kr, kv)


def _mix_ln_kernel(xb_ref, oa_ref, ob_ref, x_ref, wg_ref, bg_ref, wbd_ref, wbm_ref, wo_ref, g_ref, b_ref,
                   xo_ref, xbo_ref, *, alpha):
    d = D_MODEL
    xb = xb_ref[...]
    g_a = _sigmoid(_dot(xb, wg_ref[:, :d]) + bg_ref[:, :d])
    m = g_a * _dot(oa_ref[...], wbd_ref[...])
    g_b = _sigmoid(_dot(xb, wg_ref[:, d:]) + bg_ref[:, d:])
    m = (m + g_b * _dot(ob_ref[...], wbm_ref[...])).astype(BF16)
    y = _layernorm(alpha * x_ref[...] + _dot(m, wo_ref[...]), g_ref[...], b_ref[...])
    xo_ref[...] = y
    xbo_ref[...] = y.astype(BF16)


def _mix_ln(xb, o_a, o_b, x, w_g, b_g, w_bd, w_bm, w_out, g, b, alpha, tm):
    t = x.shape[0]
    row = lambda a: pl.BlockSpec((tm, a.shape[1]), lambda i: (i, 0))
    return pl.pallas_call(
        functools.partial(_mix_ln_kernel, alpha=alpha),
        grid=(t // tm,),
        in_specs=[row(xb), row(o_a), row(o_b), row(x), _resident(w_g), _resident(b_g), _resident(w_bd),
                  _resident(w_bm), _resident(w_out), _resident(g), _resident(b)],
        out_specs=[row(x), row(xb)],
        out_shape=[jax.ShapeDtypeStruct((t, D_MODEL), F32), jax.ShapeDtypeStruct((t, D_MODEL), BF16)],
        compiler_params=_cp(("parallel",)),
    )(xb, o_a, o_b, x, w_g, b_g, w_bd, w_bm, w_out, g, b)


def _swiglu_up_kernel(a_ref, w1_ref, w3_ref, o_ref):
    a = a_ref[...]
    u = _dot(a, w1_ref[...])
    o_ref[...] = (u * _sigmoid(u) * _dot(a, w3_ref[...])).astype(BF16)


def _swiglu_up(xb, w1, w3, tm, tn):
    t = xb.shape[0]
    ff = w1.shape[1]
    return pl.pallas_call(
        _swiglu_up_kernel,
        grid=(t // tm, ff // tn),
        in_specs=[pl.BlockSpec((tm, D_MODEL), lambda i, j: (i, 0)),
                  pl.BlockSpec((D_MODEL, tn), lambda i, j: (0, j)),
                  pl.BlockSpec((D_MODEL, tn), lambda i, j: (0, j))],
        out_specs=pl.BlockSpec((tm, tn), lambda i, j: (i, j)),
        out_shape=jax.ShapeDtypeStruct((t, ff), BF16),
        compiler_params=_cp(("parallel", "arbitrary")),
    )(xb, w1, w3)


def _ple_ln(f, x_ref, xb_ref, p_ref, wpg_ref, wpp_ref, g_ref, b_ref, xo_ref, xbo_ref, alpha):
    e = _sigmoid(_dot(xb_ref[...], wpg_ref[...])) * _dot(p_ref[...].astype(BF16), wpp_ref[...])
    y = _layernorm(alpha * x_ref[...] + f + e, g_ref[...], b_ref[...])
    xo_ref[...] = y
    xbo_ref[...] = y.astype(BF16)


def _dense_ln_kernel(h_ref, w2_ref, x_ref, xb_ref, p_ref, wpg_ref, wpp_ref, g_ref, b_ref, xo_ref, xbo_ref, *, alpha):
    _ple_ln(_dot(h_ref[...], w2_ref[...]), x_ref, xb_ref, p_ref, wpg_ref, wpp_ref, g_ref, b_ref, xo_ref, xbo_ref,
            alpha)


def _moe_ln_kernel(y1_ref, y2_ref, gates_ref, x_ref, xb_ref, p_ref, wpg_ref, wpp_ref, g_ref, b_ref, xo_ref, xbo_ref,
                   *, alpha):
    gates = gates_ref[...]
    f = (y1_ref[...].astype(F32) * gates[:, TOP_K:TOP_K + 1]
         + y2_ref[...].astype(F32) * gates[:, TOP_K + 1:TOP_K + 2])
    _ple_ln(f, x_ref, xb_ref, p_ref, wpg_ref, wpp_ref, g_ref, b_ref, xo_ref, xbo_ref, alpha)


def _channel_ln(body, lead, x, xb, p, w_pg, w_pp, g, b, alpha, tm):
    t = x.shape[0]
    row = lambda a: pl.BlockSpec((tm, a.shape[1]), lambda i: (i, 0))
    full = _resident
    tail = [x, xb, p, w_pg, w_pp, g, b]
    return pl.pallas_call(
        functools.partial(body, alpha=alpha),
        grid=(t // tm,),
        in_specs=[row(a) if tiled else full(a) for a, tiled in lead]
        + [row(x), row(xb), row(p), full(w_pg), full(w_pp), full(g), full(b)],
        out_specs=[row(x), row(xb)],
        out_shape=[jax.ShapeDtypeStruct((t, D_MODEL), F32), jax.ShapeDtypeStruct((t, D_MODEL), BF16)],
        compiler_params=_cp(("parallel",)),
    )(*[a for a, _ in lead], *tail)


def _router_kernel(x_ref, w_ref, o_ref):
    logits = jnp.dot(x_ref[...], w_ref[...], preferred_element_type=F32, precision=lax.Precision.HIGHEST)
    lane = lax.broadcasted_iota(jnp.int32, logits.shape, 1)
    lg = jnp.where(lane < N_EXPERTS, logits, -jnp.inf)
    v1 = jnp.max(lg, axis=-1, keepdims=True)
    i1 = jnp.min(jnp.where(lg == v1, lane, HEAD_LANES), axis=-1, keepdims=True)
    lg2 = jnp.where(lane == i1, -jnp.inf, lg)
    v2 = jnp.max(lg2, axis=-1, keepdims=True)
    i2 = jnp.min(jnp.where(lg2 == v2, lane, HEAD_LANES), axis=-1, keepdims=True)
    e2 = jnp.exp(v2 - v1)
    g1 = 1.0 / (1.0 + e2)
    g2 = e2 / (1.0 + e2)
    o_ref[...] = jnp.where(lane == 0, i1.astype(F32),
                           jnp.where(lane == 1, i2.astype(F32),
                                     jnp.where(lane == 2, g1, jnp.where(lane == 3, g2, 0.0))))


def _router(x, w_r, tm):
    t = x.shape[0]
    return pl.pallas_call(
        _router_kernel,
        grid=(t // tm,),
        in_specs=[pl.BlockSpec((tm, D_MODEL), lambda i: (i, 0)),
                  pl.BlockSpec(w_r.shape, lambda i: (0, 0))],
        out_specs=pl.BlockSpec((tm, HEAD_LANES), lambda i: (i, 0)),
        out_shape=jax.ShapeDtypeStruct((t, HEAD_LANES), F32),
        compiler_params=_cp(("parallel",)),
    )(x, w_r)


def _expert_changed(be_ref):
    i = pl.program_id(1)
    return jnp.logical_or(i == 0, be_ref[i] != be_ref[jnp.maximum(i - 1, 0)])


def _expert_up_kernel(be_ref, nb_ref, a_ref, w1_ref, w3_ref, o_ref, w1_scr, w3_scr):
    @pl.when(_expert_changed(be_ref))
    def _():
        w1_scr[...] = w1_ref[0].astype(BF16)
        w3_scr[...] = w3_ref[0].astype(BF16)

    @pl.when(pl.program_id(1) < nb_ref[0])
    def _():
        a = a_ref[...]
        u = _dot(a, w1_scr[...])
        o_ref[...] = (u * _sigmoid(u) * _dot(a, w3_scr[...])).astype(BF16)

    @pl.when(pl.program_id(1) >= nb_ref[0])
    def _():
        o_ref[...] = jnp.zeros(o_ref.shape, BF16)


def _expert_up(a, w1, w3, layer, blk_e, n_used, tm, tn):
    rows = a.shape[0]
    ff = w1.shape[3]
    wspec = pl.BlockSpec((None, 1, D_MODEL, tn), lambda j, i, be, nb: (layer, be[i], 0, j),
                         pipeline_mode=pl.Buffered(1))
    grid_spec = pltpu.PrefetchScalarGridSpec(
        num_scalar_prefetch=2,
        grid=(ff // tn, rows // tm),
        in_specs=[pl.BlockSpec((tm, D_MODEL), lambda j, i, be, nb: (i, 0)), wspec, wspec],
        out_specs=pl.BlockSpec((tm, tn), lambda j, i, be, nb: (i, j)),
        scratch_shapes=[pltpu.VMEM((D_MODEL, tn), BF16), pltpu.VMEM((D_MODEL, tn), BF16)],
    )
    return pl.pallas_call(
        _expert_up_kernel,
        grid_spec=grid_spec,
        out_shape=jax.ShapeDtypeStruct((rows, ff), BF16),
        compiler_params=_cp(("arbitrary", "arbitrary")),
    )(blk_e, n_used, a, w1, w3)


def _expert_down_kernel(be_ref, nb_ref, h_ref, w2_ref, o_ref, w2_scr):
    @pl.when(_expert_changed(be_ref))
    def _():
        w2_scr[...] = w2_ref[0].astype(BF16)

    @pl.when(pl.program_id(1) < nb_ref[0])
    def _():
        o_ref[...] = _dot(h_ref[...], w2_scr[...]).astype(BF16)

    @pl.when(pl.program_id(1) >= nb_ref[0])
    def _():
        o_ref[...] = jnp.zeros(o_ref.shape, BF16)


def _expert_down(h, w2, layer, blk_e, n_used, tm, tn):
    rows, ff = h.shape
    grid_spec = pltpu.PrefetchScalarGridSpec(
        num_scalar_prefetch=2,
        grid=(D_MODEL // tn, rows // tm),
        in_specs=[pl.BlockSpec((tm, ff), lambda j, i, be, nb: (i, 0)),
                  pl.BlockSpec((None, 1, ff, tn), lambda j, i, be, nb: (layer, be[i], 0, j),
                               pipeline_mode=pl.Buffered(1))],
        out_specs=pl.BlockSpec((tm, tn), lambda j, i, be, nb: (i, j)),
        scratch_shapes=[pltpu.VMEM((ff, tn), BF16)],
    )
    return pl.pallas_call(
        _expert_down_kernel,
        grid_spec=grid_spec,
        out_shape=jax.ShapeDtypeStruct((rows, D_MODEL), BF16),
        compiler_params=_cp(("arbitrary", "arbitrary")),
    )(blk_e, n_used, h, w2)


def _moe(x, xb, w_r, w1, w3, w2, layer, tm):
    t = x.shape[0]
    m = t * TOP_K
    gb = GROUP_ROWS
    routed = _router(x, w_r, tm)
    flat_e = routed[:, :TOP_K].astype(jnp.int32).reshape(-1)
    onehot = (flat_e[:, None] == jnp.arange(N_EXPERTS, dtype=jnp.int32)[None, :]).astype(jnp.int32)
    csum = jnp.cumsum(onehot, axis=0)
    counts = csum[-1]
    rank = jnp.sum(csum * onehot, axis=1) - 1
    padded = ((counts + gb - 1) // gb) * gb
    pend = jnp.cumsum(padded)
    pstart = pend - padded
    dest = jnp.sum(pstart[None, :] * onehot, axis=1) + rank
    n_blocks = m // gb + N_EXPERTS
    blk_e = jnp.minimum(jnp.sum((jnp.arange(n_blocks, dtype=jnp.int32)[:, None] * gb >= pend[None, :])
                                .astype(jnp.int32), axis=1), N_EXPERTS - 1)
    n_used = (pend[-1:] // gb).astype(jnp.int32)
    n_rows = n_blocks * gb
    gap = padded - counts
    gap_end = jnp.cumsum(gap)
    i_dummy = jnp.arange(n_rows - m, dtype=jnp.int32)
    in_gap = (i_dummy[:, None] >= (gap_end - gap)[None, :]) & (i_dummy[:, None] < gap_end[None, :])
    dummy_key = jnp.where(i_dummy < gap_end[-1],
                          jnp.sum(jnp.where(in_gap, (pstart + counts - (gap_end - gap))[None, :], 0), axis=1),
                          pend[-1] - gap_end[-1]) + i_dummy
    keys = jnp.concatenate([dest, dummy_key])
    toks = jnp.concatenate([jnp.arange(m, dtype=jnp.int32) // TOP_K, jnp.zeros((n_rows - m,), jnp.int32)])
    _, src_tok = lax.sort((keys, toks), num_keys=1)
    take = lambda arr, rows: arr.at[rows].get(mode="promise_in_bounds")
    a = take(xb, src_tok)
    h = _expert_up(a, w1, w3, layer, blk_e, n_used, gb, EXPERT_FF // 2)
    yb = _expert_down(h, w2, layer, blk_e, n_used, gb, D_MODEL)
    dest2 = dest.reshape(t, TOP_K)
    return take(yb, dest2[:, 0]), take(yb, dest2[:, 1]), routed


def _t5_bucket(dist):
    n = jnp.maximum(dist, 0)
    max_exact = REL_BUCKETS // 2
    large = max_exact + (jnp.log(jnp.maximum(n, 1).astype(F32) / max_exact)
                         / math.log(REL_MAX_DIST / max_exact) * (REL_BUCKETS - max_exact)).astype(jnp.int32)
    large = jnp.minimum(large, REL_BUCKETS - 1)
    return jnp.where(n < max_exact, n, large)


def _bias_tiles(table, tile):
    a = jnp.arange(tile, dtype=jnp.int32)
    d0 = a[None, :] - a[:, None]
    rel = table.astype(F32) - table[REL_BUCKETS - 1].astype(F32)[None, :]

    def lookup(dist):
        onehot = (_t5_bucket(dist)[None, :, :] == jnp.arange(REL_BUCKETS, dtype=jnp.int32)[:, None, None])
        return jnp.sum(jnp.where(onehot[:, None], rel[:, :, None, None], 0.0), axis=0)

    f0 = lookup(d0)
    f1 = lookup(d0 + tile)
    f0 = jnp.where((d0 >= 0)[None], f0, NEG)
    tiles = jnp.stack([f0, f1], axis=1)
    return jnp.concatenate([tiles, tiles], axis=-1)


def _rope_tables(positions):
    half = MLA_ROPE_DIM // 2
    inv_freq = ROPE_THETA ** (-jnp.arange(half, dtype=F32) / half)
    ang = positions.astype(F32)[:, :, None] * inv_freq
    c, s = jnp.cos(ang), jnp.sin(ang)
    b, sq = positions.shape
    ones = jnp.ones((b, sq, MLA_NOPE_DIM), F32)
    z_nope = jnp.zeros((b, sq, MLA_NOPE_DIM), F32)
    z_pad = jnp.zeros((b, sq, HEAD_LANES - MLA_NOPE_DIM - MLA_ROPE_DIM), F32)
    cos_t = jnp.concatenate([ones, c, c, z_pad], axis=-1).reshape(b * sq, HEAD_LANES)
    sin_t = jnp.concatenate([z_nope, s, s, z_pad], axis=-1).reshape(b * sq, HEAD_LANES)
    return cos_t, sin_t


def _rot_cols(w):
    half = MLA_ROPE_DIM // 2
    return jnp.concatenate([-w[..., half:], w[..., :half]], axis=-1)


def _mixer_weights(w_in, w_uq, w_ukv):
    d = D_MODEL
    hq = DIFF_HEADS * 2 * DIFF_HEAD_DIM
    o = 0
    w_dq = w_in[:, o:o + hq]; o += hq
    w_dk = w_in[:, o:o + hq]; o += hq
    w_dv = w_in[:, o:o + hq]; o += hq
    w_cq = w_in[:, o:o + MLA_Q_RANK]; o += MLA_Q_RANK
    w_ckv = w_in[:, o:o + MLA_KV_RANK]; o += MLA_KV_RANK
    w_kr = w_in[:, o:o + MLA_ROPE_DIM]; o += MLA_ROPE_DIM
    w_g = w_in[:, o:]
    w_qkv = jnp.concatenate([w_dq, w_dk, w_dv], axis=1).astype(BF16)
    pad = HEAD_LANES - MLA_NOPE_DIM - MLA_ROPE_DIM
    z = lambda n: jnp.zeros((d, n), F32)
    w_c = jnp.concatenate([w_cq, w_ckv, z(MLA_NOPE_DIM), w_kr, z(pad), z(MLA_NOPE_DIM), _rot_cols(w_kr), z(pad)],
                          axis=1).astype(BF16)
    wq = w_uq.reshape(MLA_Q_RANK, MLA_HEADS, MLA_NOPE_DIM + MLA_ROPE_DIM)
    nope, rope = wq[..., :MLA_NOPE_DIM], wq[..., MLA_NOPE_DIM:]
    zq = lambda n: jnp.zeros((MLA_Q_RANK, MLA_HEADS, n), F32)
    plain = jnp.concatenate([nope, rope, zq(pad)], axis=-1).reshape(MLA_Q_RANK, -1)
    rot = jnp.concatenate([zq(MLA_NOPE_DIM), _rot_cols(rope), zq(pad)], axis=-1).reshape(MLA_Q_RANK, -1)
    w_q2 = jnp.concatenate([plain, rot], axis=1).astype(BF16)
    wkv = w_ukv.reshape(MLA_KV_RANK, MLA_HEADS, MLA_NOPE_DIM + MLA_V_DIM)
    zk = jnp.zeros((MLA_KV_RANK, MLA_HEADS, HEAD_LANES - MLA_NOPE_DIM), F32)
    w_kv2 = jnp.concatenate([
        jnp.concatenate([wkv[..., :MLA_NOPE_DIM], zk], axis=-1).reshape(MLA_KV_RANK, -1),
        wkv[..., MLA_NOPE_DIM:].reshape(MLA_KV_RANK, -1)], axis=1).astype(BF16)
    return w_qkv, w_c, w_g.astype(BF16), w_q2, w_kv2


def kernel(x, p, positions, rel_bias_table, w_in, b_gate, lambda_q1, lambda_k1, lambda_q2, lambda_k2,
           diff_subln_g, mla_q_norm_g, w_uq, mla_kv_norm_g, w_ukv, w_branch_diff, w_branch_mla, w_out,
           ln_mix_g, ln_mix_b, dense_w1, dense_w3, dense_w2, router_w, expert_w1, expert_w3, expert_w2,
           w_ple_gate, w_ple_proj, ln_ffn_g, ln_ffn_b):
    batch, seq, d = x.shape
    depth = w_in.shape[0]
    t = batch * seq
    tm = min(ROW_TILE, t)
    tm_mm = BIG_ROW_TILE if t % BIG_ROW_TILE == 0 else tm
    alpha = (2.0 * depth) ** 0.25
    row = lambda v: v.reshape(1, -1).astype(F32)

    cos_t, sin_t = _rope_tables(positions)
    bias = _bias_tiles(rel_bias_table * LOG2E, ATT_TILE)
    xf = x.reshape(t, d)
    xb = xf.astype(BF16)

    for i in range(depth):
        lam_init = 0.8 - 0.6 * math.exp(-0.3 * i)
        w_qkv, w_c, w_g, w_q2, w_kv2 = _mixer_weights(w_in[i], w_uq[i], w_ukv[i])
        qkv = _matmul(xb, w_qkv, BF16, tm_mm, 1024, scaled_tiles=1, scale=DIFF_HEAD_DIM ** -0.5 * LOG2E)
        q_mla, kv_mla, kr = _latent(xb, w_c, w_q2, w_kv2, row(mla_q_norm_g[i]), row(mla_kv_norm_g[i]),
                                    cos_t, sin_t, tm)
        o_a = _diff_attention(qkv.reshape(batch, seq, -1), bias, row(lambda_q1[i]), row(lambda_k1[i]),
                              row(lambda_q2[i]), row(lambda_k2[i]), row(diff_subln_g[i]), lam_init, batch, seq)
        o_b = _mla_attention(q_mla.reshape(batch, seq, -1), kv_mla.reshape(batch, seq, -1),
                             kr.reshape(batch, seq, -1), batch, seq)
        xf, xb = _mix_ln(xb, o_a.reshape(t, -1), o_b.reshape(t, -1), xf, w_g, row(b_gate[i]),
                         w_branch_diff[i].astype(BF16), w_branch_mla[i].astype(BF16), w_out[i].astype(BF16),
                         row(ln_mix_g[i]), row(ln_mix_b[i]), alpha, tm)
        j = i // 2
        p_i = p[i].reshape(t, PLE_DIM)
        w_pg = w_ple_gate[i].astype(BF16)
        w_pp = w_ple_proj[i].astype(BF16)
        if i % 2 == 0:
            hmid = _swiglu_up(xb, dense_w1[j].astype(BF16), dense_w3[j].astype(BF16), tm_mm, DENSE_FF // 2)
            lead = [(hmid, True), (dense_w2[j].astype(BF16), False)]
            body = _dense_ln_kernel
        else:
            w_r = jnp.concatenate([router_w[j], jnp.zeros((d, HEAD_LANES - N_EXPERTS), F32)], axis=1)
            y1, y2, routed = _moe(xf, xb, w_r, expert_w1, expert_w3, expert_w2, j, tm)
            lead = [(y1, True), (y2, True), (routed, True)]
            body = _moe_ln_kernel
        xf, xb = _channel_ln(body, lead, xf, xb, p_i, w_pg, w_pp, row(ln_ffn_g[i]), row(ln_ffn_b[i]), alpha, tm)
    return xf.reshape(batch, seq, d)
```

```python
import functools
import math

import jax
import jax.numpy as jnp
from jax import lax
from jax.experimental import pallas as pl
from jax.experimental.pallas import tpu as pltpu

F32 = jnp.float32
BF16 = jnp.bfloat16

D_MODEL = 1024
PLE_DIM = 256
DIFF_HEADS = 8
DIFF_HEAD_DIM = 64
MLA_HEADS = 8
MLA_Q_RANK = 384
MLA_KV_RANK = 256
MLA_NOPE_DIM = 64
MLA_ROPE_DIM = 32
MLA_V_DIM = 64
ROPE_THETA = 10000.0
REL_BUCKETS = 32
REL_MAX_DIST = 128
DENSE_FF = 2816
N_EXPERTS = 8
TOP_K = 2
EXPERT_FF = 3584
EPS = 1e-5

HEAD_LANES = 128
NEG = -1e30
LOG2E = math.log2(math.e)
ATT_TILE = 256
ATT_HEADS_PER_STEP = 4
ROW_TILE = 512
BIG_ROW_TILE = 1024
GROUP_ROWS = 512
DISPATCH_CHUNKS = 4
VMEM_LIMIT = 52 * 1024 * 1024


def _cp(sem):
    return pltpu.CompilerParams(dimension_semantics=sem, vmem_limit_bytes=VMEM_LIMIT)


def _dot(a, b):
    return jnp.dot(a, b, preferred_element_type=F32)


def _sigmoid(x):
    return 1.0 / (1.0 + jnp.exp(-x))


def _layernorm(r, g, b):
    mu = jnp.mean(r, axis=-1, keepdims=True)
    d = r - mu
    var = jnp.mean(d * d, axis=-1, keepdims=True)
    return d * lax.rsqrt(var + EPS) * g + b


def _rmsnorm(x, g):
    return x * lax.rsqrt(jnp.mean(x * x, axis=-1, keepdims=True) + EPS) * g


def _mm_kernel(a_ref, w_ref, o_ref, *, scaled_tiles, scale):
    acc = _dot(a_ref[...], w_ref[...])
    if scaled_tiles:
        acc = acc * jnp.where(pl.program_id(1) < scaled_tiles, scale, 1.0)
    o_ref[...] = acc.astype(o_ref.dtype)


def _matmul(a, w, out_dtype, tm, tn, scaled_tiles=0, scale=1.0):
    m, k = a.shape
    n = w.shape[1]
    return pl.pallas_call(
        functools.partial(_mm_kernel, scaled_tiles=scaled_tiles, scale=scale),
        grid=(m // tm, n // tn),
        in_specs=[pl.BlockSpec((tm, k), lambda i, j: (i, 0)),
                  pl.BlockSpec((k, tn), lambda i, j: (0, j))],
        out_specs=pl.BlockSpec((tm, tn), lambda i, j: (i, j)),
        out_shape=jax.ShapeDtypeStruct((m, n), out_dtype),
        compiler_params=_cp(("parallel", "arbitrary")),
    )(a, w)


C_COLS = MLA_Q_RANK + MLA_KV_RANK + 2 * HEAD_LANES


def _resident(a):
    return pl.BlockSpec(a.shape, lambda *_: (0,) * a.ndim, pipeline_mode=pl.Buffered(1))


def _latent_kernel(a_ref, wc_ref, wq_ref, wkv_ref, gq_ref, gkv_ref, cos_ref, sin_ref, q_ref, kv_ref, kr_ref, *,
                   scale):
    z = _dot(a_ref[...], wc_ref[...])
    cq = z[:, :MLA_Q_RANK]
    ckv = z[:, MLA_Q_RANK:MLA_Q_RANK + MLA_KV_RANK]
    kr = z[:, MLA_Q_RANK + MLA_KV_RANK:MLA_Q_RANK + MLA_KV_RANK + HEAD_LANES]
    kr_rot = z[:, MLA_Q_RANK + MLA_KV_RANK + HEAD_LANES:]
    cos, sin = cos_ref[...], sin_ref[...]
    kr_ref[...] = (kr * cos + kr_rot * sin).astype(BF16)
    kv_ref[...] = _dot(_rmsnorm(ckv, gkv_ref[...]).astype(BF16), wkv_ref[...]).astype(BF16)
    zq = _dot(_rmsnorm(cq, gq_ref[...]).astype(BF16), wq_ref[...])
    c = cos * scale
    s = sin * scale
    hl = HEAD_LANES
    for h in range(MLA_HEADS):
        q_ref[:, h * hl:(h + 1) * hl] = (
            zq[:, h * hl:(h + 1) * hl] * c
            + zq[:, (MLA_HEADS + h) * hl:(MLA_HEADS + h + 1) * hl] * s).astype(BF16)


def _latent(xb, w_c, w_q2, w_kv2, gq, gkv, cos_t, sin_t, tm):
    t = xb.shape[0]
    row = lambda n: pl.BlockSpec((tm, n), lambda i: (i, 0))
    nq = MLA_HEADS * HEAD_LANES
    nkv = w_kv2.shape[1]
    scale = (MLA_NOPE_DIM + MLA_ROPE_DIM) ** -0.5 * LOG2E
    return pl.pallas_call(
        functools.partial(_latent_kernel, scale=scale),
        grid=(t // tm,),
        in_specs=[row(D_MODEL), _resident(w_c), _resident(w_q2), _resident(w_kv2), _resident(gq), _resident(gkv),
                  row(HEAD_LANES), row(HEAD_LANES)],
        out_specs=[row(nq), row(nkv), row(HEAD_LANES)],
        out_shape=[jax.ShapeDtypeStruct((t, nq), BF16),
                   jax.ShapeDtypeStruct((t, nkv), BF16),
                   jax.ShapeDtypeStruct((t, HEAD_LANES), BF16)],
        compiler_params=_cp(("parallel",)),
    )(xb, w_c, w_q2, w_kv2, gq, gkv, cos_t, sin_t)


ONES_ROWS = 16


def _scores_t(k, q):
    return lax.dot_general(k, q, (((1,), (1,)), ((), ())), preferred_element_type=F32)


def _softmax_step_t(s, vt, m_ref, acc_ref):
    m_old = m_ref[...]
    m_new = jnp.maximum(m_old, jnp.max(s, axis=0, keepdims=True))
    p = jnp.exp2(s - m_new).astype(BF16)
    alpha = jnp.exp2(m_old - m_new)
    acc_ref[...] = alpha * acc_ref[...] + _dot(vt, p)
    m_ref[...] = m_new


def _fill_vt(v_ref, vt_scr, lane0, row0, v_rows, tile):
    for j in range(vt_scr.shape[0]):
        vt = v_ref[0, j * tile:(j + 1) * tile, lane0:lane0 + HEAD_LANES].astype(F32).T
        vt_scr[j, 0:v_rows, :] = vt[row0:row0 + v_rows].astype(BF16)
        vt_scr[j, v_rows:v_rows + ONES_ROWS, :] = jnp.ones((ONES_ROWS, tile), BF16)


def _normalised(acc, v_rows):
    return acc[0:v_rows] * (1.0 / acc[v_rows:v_rows + 1])


def _diff_attn_kernel(q_ref, k_ref, v_ref, bias_ref, lq1_ref, lk1_ref, lq2_ref, lk2_ref, g_ref, o_ref,
                      vt_scr, s_scr, m_scr, acc_scr, *, lam_init, tile, heads):
    qi = pl.program_id(2)
    hd = 2 * DIFF_HEAD_DIM
    hl = HEAD_LANES

    @pl.when(qi == 0)
    def _():
        for h in range(heads):
            _fill_vt(v_ref, vt_scr.at[h], h * hl, 0, hd, tile)

    q_cat = []
    for h in range(heads):
        q = q_ref[0, :, h * hl:(h + 1) * hl]
        lane = lax.broadcasted_iota(jnp.int32, q.shape, 1)
        zero = jnp.zeros_like(q)
        q_cat.append(jnp.concatenate([jnp.where(lane < DIFF_HEAD_DIM, q, zero),
                                      jnp.where(lane >= DIFF_HEAD_DIM, q, zero)], axis=0))
    m_scr[...] = jnp.full(m_scr.shape, NEG, F32)
    acc_scr[...] = jnp.zeros(acc_scr.shape, F32)

    def scores(slot, kj):
        rows = pl.ds(pl.multiple_of(kj * tile, tile), tile)
        for h in range(heads):
            s_scr[slot, h] = _scores_t(k_ref[0, rows, h * hl:(h + 1) * hl], q_cat[h])

    def update(slot, kj, bias_idx=None):
        for h in range(heads):
            s = s_scr[slot, h]
            if bias_idx is not None:
                s = s + bias_ref[h, bias_idx]
            _softmax_step_t(s, vt_scr[h, kj], m_scr.at[h], acc_scr.at[h])

    n_far = jnp.maximum(qi - 1, 0)
    quads = n_far // 4
    scores(0, 0)

    def pair_step(a):
        scores(1, a + 1)
        update(0, a)
        scores(0, a + 2)
        update(1, a + 1)

    def quad_body(j4, carry):
        pair_step(4 * j4)
        pair_step(4 * j4 + 2)
        return carry

    lax.fori_loop(0, quads, quad_body, 0)
    odd_pair = (n_far - 4 * quads) >= 2

    @pl.when(odd_pair)
    def _():
        pair_step(4 * quads)

    t0 = 4 * quads + jnp.where(odd_pair, 2, 0)

    @pl.when(n_far - t0 == 1)
    def _():
        scores(1, t0 + 1)
        update(0, t0)
        scores(0, t0 + 2)
        update(1, t0 + 1, 1)
        update(0, t0 + 2, 0)

    @pl.when(jnp.logical_and(n_far == t0, qi >= 1))
    def _():
        scores(1, t0 + 1)
        update(0, t0, 1)
        update(1, t0 + 1, 0)

    @pl.when(qi == 0)
    def _():
        update(0, 0, 0)

    lam = (jnp.exp(jnp.sum(lq1_ref[...] * lk1_ref[...], axis=-1, keepdims=True))
           - jnp.exp(jnp.sum(lq2_ref[...] * lk2_ref[...], axis=-1, keepdims=True)) + lam_init)
    for h in range(heads):
        acc = acc_scr[h]
        o = _normalised(acc[:, :tile], hd) - lam * _normalised(acc[:, tile:], hd)
        o = o * lax.rsqrt(jnp.mean(o * o, axis=0, keepdims=True) + EPS)
        o_ref[0, :, h * hl:(h + 1) * hl] = (o.T * (g_ref[...] * (1.0 - lam_init))).astype(BF16)


def _diff_attention(qkv, bias, lq1, lk1, lq2, lk2, g, lam_init, batch, seq):
    tile = ATT_TILE
    hl = HEAD_LANES
    nh = DIFF_HEADS
    hps = ATT_HEADS_PER_STEP
    ng = nh // hps
    vec = lambda a: pl.BlockSpec(a.shape, lambda b, h, i: (0, 0))
    return pl.pallas_call(
        functools.partial(_diff_attn_kernel, lam_init=lam_init, tile=tile, heads=hps),
        grid=(batch, ng, seq // tile),
        in_specs=[pl.BlockSpec((1, tile, hps * hl), lambda b, h, i: (b, i, h)),
                  pl.BlockSpec((1, seq, hps * hl), lambda b, h, i: (b, 0, ng + h)),
                  pl.BlockSpec((1, seq, hps * hl), lambda b, h, i: (b, 0, 2 * ng + h)),
                  pl.BlockSpec((hps, 2, tile, 2 * tile), lambda b, h, i: (h, 0, 0, 0)),
                  vec(lq1), vec(lk1), vec(lq2), vec(lk2), vec(g)],
        out_specs=pl.BlockSpec((1, tile, hps * hl), lambda b, h, i: (b, i, h)),
        out_shape=jax.ShapeDtypeStruct((batch, seq, nh * hl), BF16),
        scratch_shapes=[pltpu.VMEM((hps, seq // tile, hl + ONES_ROWS, tile), BF16),
                        pltpu.VMEM((2, hps, tile, 2 * tile), F32),
                        pltpu.VMEM((hps, 1, 2 * tile), F32),
                        pltpu.VMEM((hps, hl + ONES_ROWS, 2 * tile), F32)],
        compiler_params=_cp(("parallel", "parallel", "arbitrary")),
    )(qkv, qkv, qkv, bias, lq1, lk1, lq2, lk2, g)


MLA_PAIR = 2
MLA_STEP_HEADS = 4


def _mla_attn_kernel(q_ref, k_ref, kr_ref, v_ref, o_ref, kf_scr, vt_scr, s_scr, m_scr, acc_scr, *, tile):
    qi = pl.program_id(2)
    hl = HEAD_LANES

    @pl.when(qi == 0)
    def _():
        for h in range(MLA_STEP_HEADS):
            kf_scr[h] = k_ref[0, :, h * hl:(h + 1) * hl] + kr_ref[0]
            _fill_vt(v_ref, vt_scr.at[h], (h // MLA_PAIR) * hl, (h % MLA_PAIR) * MLA_V_DIM, MLA_V_DIM, tile)

    q = [q_ref[0, :, h * hl:(h + 1) * hl] for h in range(MLA_STEP_HEADS)]
    m_scr[...] = jnp.full(m_scr.shape, NEG, F32)
    acc_scr[...] = jnp.zeros(acc_scr.shape, F32)

    def scores(slot, kj):
        rows = pl.ds(pl.multiple_of(kj * tile, tile), tile)
        for h in range(MLA_STEP_HEADS):
            s_scr[slot, h] = _scores_t(kf_scr[h, rows, :], q[h])

    def update(slot, kj, key_offset=None):
        for h in range(MLA_STEP_HEADS):
            s = s_scr[slot, h]
            if key_offset is not None:
                key = lax.broadcasted_iota(jnp.int32, s.shape, 0)
                qry = lax.broadcasted_iota(jnp.int32, s.shape, 1)
                s = jnp.where(key + key_offset <= qry, s, NEG)
            _softmax_step_t(s, vt_scr[h, kj], m_scr.at[h], acc_scr.at[h])

    scores(0, 0)

    def pair_step(a):
        scores(1, a + 1)
        update(0, a)
        scores(0, a + 2)
        update(1, a + 1)

    def quad_body(j4, carry):
        pair_step(4 * j4)
        pair_step(4 * j4 + 2)
        return carry

    lax.fori_loop(0, qi // 2, quad_body, 0)

    @pl.when(qi % 2 == 1)
    def _():
        pair_step(2 * qi - 2)

    t0 = 2 * qi
    scores(1, t0 + 1)
    update(0, t0, 0)
    update(1, t0 + 1, tile)
    o = jnp.concatenate([_normalised(acc_scr[h], MLA_V_DIM) for h in range(MLA_STEP_HEADS)], axis=0)
    o_ref[0] = o.T.astype(BF16)


def _mla_attention(q, kv, kr, batch, seq):
    tile = ATT_TILE
    tq = 2 * tile
    hl = HEAD_LANES
    nh = MLA_STEP_HEADS
    ng = MLA_HEADS // nh
    v_lanes = nh * MLA_V_DIM
    v_rows = MLA_V_DIM + ONES_ROWS
    return pl.pallas_call(
        functools.partial(_mla_attn_kernel, tile=tile),
        grid=(batch, ng, seq // tq),
        in_specs=[pl.BlockSpec((1, tq, nh * hl), lambda b, h, i: (b, i, h)),
                  pl.BlockSpec((1, seq, nh * hl), lambda b, h, i: (b, 0, h)),
                  pl.BlockSpec((1, seq, hl), lambda b, h, i: (b, 0, 0)),
                  pl.BlockSpec((1, seq, v_lanes), lambda b, h, i: (b, 0, MLA_HEADS * hl // v_lanes + h))],
        out_specs=pl.BlockSpec((1, tq, v_lanes), lambda b, h, i: (b, i, h)),
        out_shape=jax.ShapeDtypeStruct((batch, seq, MLA_HEADS * MLA_V_DIM), BF16),
        scratch_shapes=[pltpu.VMEM((nh, seq, hl), BF16),
                        pltpu.VMEM((nh, seq // tile, v_rows, tile), BF16),
                        pltpu.VMEM((2, nh, tile, tq), F32),
                        pltpu.VMEM((nh, 1, tq), F32),
                        pltpu.VMEM((nh, v_rows, tq), F32)],
        compiler_params=_cp(("parallel", "parallel", "arbitrary")),
    )(q, kv, kr, kv)


def _mix_ln_kernel(xb_ref, oa_ref, ob_ref, x_ref, wg_ref, bg_ref, wbd_ref, wbm_ref, wo_ref, g_ref, b_ref,
                   xo_ref, xbo_ref, *, alpha):
    d = D_MODEL
    xb = xb_ref[...]
    g_a = _sigmoid(_dot(xb, wg_ref[:, :d]) + bg_ref[:, :d])
    m = g_a * _dot(oa_ref[...], wbd_ref[...])
    g_b = _sigmoid(_dot(xb, wg_ref[:, d:]) + bg_ref[:, d:])
    m = (m + g_b * _dot(ob_ref[...], wbm_ref[...])).astype(BF16)
    y = _layernorm(alpha * x_ref[...] + _dot(m, wo_ref[...]), g_ref[...], b_ref[...])
    xo_ref[...] = y
    xbo_ref[...] = y.astype(BF16)


def _mix_ln(xb, o_a, o_b, x, w_g, b_g, w_bd, w_bm, w_out, g, b, alpha, tm):
    t = x.shape[0]
    row = lambda a: pl.BlockSpec((tm, a.shape[1]), lambda i: (i, 0))
    return pl.pallas_call(
        functools.partial(_mix_ln_kernel, alpha=alpha),
        grid=(t // tm,),
        in_specs=[row(xb), row(o_a), row(o_b), row(x), _resident(w_g), _resident(b_g), _resident(w_bd),
                  _resident(w_bm), _resident(w_out), _resident(g), _resident(b)],
        out_specs=[row(x), row(xb)],
        out_shape=[jax.ShapeDtypeStruct((t, D_MODEL), F32), jax.ShapeDtypeStruct((t, D_MODEL), BF16)],
        compiler_params=_cp(("parallel",)),
    )(xb, o_a, o_b, x, w_g, b_g, w_bd, w_bm, w_out, g, b)


def _swiglu_up_kernel(a_ref, w1_ref, w3_ref, o_ref):
    a = a_ref[...]
    u = _dot(a, w1_ref[...])
    o_ref[...] = (u * _sigmoid(u) * _dot(a, w3_ref[...])).astype(BF16)


def _swiglu_up(xb, w1, w3, tm, tn):
    t = xb.shape[0]
    ff = w1.shape[1]
    return pl.pallas_call(
        _swiglu_up_kernel,
        grid=(t // tm, ff // tn),
        in_specs=[pl.BlockSpec((tm, D_MODEL), lambda i, j: (i, 0)),
                  pl.BlockSpec((D_MODEL, tn), lambda i, j: (0, j)),
                  pl.BlockSpec((D_MODEL, tn), lambda i, j: (0, j))],
        out_specs=pl.BlockSpec((tm, tn), lambda i, j: (i, j)),
        out_shape=jax.ShapeDtypeStruct((t, ff), BF16),
        compiler_params=_cp(("parallel", "arbitrary")),
    )(xb, w1, w3)


def _ple_ln(f, x_ref, xb_ref, p_ref, wpg_ref, wpp_ref, g_ref, b_ref, xo_ref, xbo_ref, alpha):
    e = _sigmoid(_dot(xb_ref[...], wpg_ref[...])) * _dot(p_ref[...].astype(BF16), wpp_ref[...])
    y = _layernorm(alpha * x_ref[...] + f + e, g_ref[...], b_ref[...])
    xo_ref[...] = y
    xbo_ref[...] = y.astype(BF16)


def _dense_ln_kernel(h_ref, w2_ref, x_ref, xb_ref, p_ref, wpg_ref, wpp_ref, g_ref, b_ref, xo_ref, xbo_ref, *, alpha):
    _ple_ln(_dot(h_ref[...], w2_ref[...]), x_ref, xb_ref, p_ref, wpg_ref, wpp_ref, g_ref, b_ref, xo_ref, xbo_ref,
            alpha)


def _moe_ln_kernel(y1_ref, y2_ref, gates_ref, x_ref, xb_ref, p_ref, wpg_ref, wpp_ref, g_ref, b_ref, xo_ref, xbo_ref,
                   *, alpha):
    gates = gates_ref[...]
    f = (y1_ref[...].astype(F32) * gates[:, TOP_K:TOP_K + 1]
         + y2_ref[...].astype(F32) * gates[:, TOP_K + 1:TOP_K + 2])
    _ple_ln(f, x_ref, xb_ref, p_ref, wpg_ref, wpp_ref, g_ref, b_ref, xo_ref, xbo_ref, alpha)


def _channel_ln(body, lead, x, xb, p, w_pg, w_pp, g, b, alpha, tm):
    t = x.shape[0]
    row = lambda a: pl.BlockSpec((tm, a.shape[1]), lambda i: (i, 0))
    full = _resident
    tail = [x, xb, p, w_pg, w_pp, g, b]
    return pl.pallas_call(
        functools.partial(body, alpha=alpha),
        grid=(t // tm,),
        in_specs=[row(a) if tiled else full(a) for a, tiled in lead]
        + [row(x), row(xb), row(p), full(w_pg), full(w_pp), full(g), full(b)],
        out_specs=[row(x), row(xb)],
        out_shape=[jax.ShapeDtypeStruct((t, D_MODEL), F32), jax.ShapeDtypeStruct((t, D_MODEL), BF16)],
        compiler_params=_cp(("parallel",)),
    )(*[a for a, _ in lead], *tail)


def _router_kernel(x_ref, w_ref, o_ref):
    logits = jnp.dot(x_ref[...], w_ref[...], preferred_element_type=F32, precision=lax.Precision.HIGHEST)
    lane = lax.broadcasted_iota(jnp.int32, logits.shape, 1)
    lg = jnp.where(lane < N_EXPERTS, logits, -jnp.inf)
    v1 = jnp.max(lg, axis=-1, keepdims=True)
    i1 = jnp.min(jnp.where(lg == v1, lane, HEAD_LANES), axis=-1, keepdims=True)
    lg2 = jnp.where(lane == i1, -jnp.inf, lg)
    v2 = jnp.max(lg2, axis=-1, keepdims=True)
    i2 = jnp.min(jnp.where(lg2 == v2, lane, HEAD_LANES), axis=-1, keepdims=True)
    e2 = jnp.exp(v2 - v1)
    g1 = 1.0 / (1.0 + e2)
    g2 = e2 / (1.0 + e2)
    o_ref[...] = jnp.where(lane == 0, i1.astype(F32),
                           jnp.where(lane == 1, i2.astype(F32),
                                     jnp.where(lane == 2, g1, jnp.where(lane == 3, g2, 0.0))))


def _router(x, w_r, tm):
    t = x.shape[0]
    return pl.pallas_call(
        _router_kernel,
        grid=(t // tm,),
        in_specs=[pl.BlockSpec((tm, D_MODEL), lambda i: (i, 0)),
                  pl.BlockSpec(w_r.shape, lambda i: (0, 0))],
        out_specs=pl.BlockSpec((tm, HEAD_LANES), lambda i: (i, 0)),
        out_shape=jax.ShapeDtypeStruct((t, HEAD_LANES), F32),
        compiler_params=_cp(("parallel",)),
    )(x, w_r)


def _expert_changed(be_ref):
    i = pl.program_id(1)
    return jnp.logical_or(i == 0, be_ref[i] != be_ref[jnp.maximum(i - 1, 0)])


def _expert_up_kernel(be_ref, nb_ref, a_ref, w1_ref, w3_ref, *rest):
    o_ref, w1_scr, w3_scr = rest[-3:]

    @pl.when(_expert_changed(be_ref))
    def _():
        w1_scr[...] = w1_ref[0].astype(BF16)
        w3_scr[...] = w3_ref[0].astype(BF16)

    @pl.when(pl.program_id(1) < nb_ref[0])
    def _():
        a = a_ref[...]
        u = _dot(a, w1_scr[...])
        o_ref[...] = (u * _sigmoid(u) * _dot(a, w3_scr[...])).astype(BF16)

    @pl.when(pl.program_id(1) >= nb_ref[0])
    def _():
        o_ref[...] = jnp.zeros(o_ref.shape, BF16)


def _expert_up(a, w1, w3, layer, blk_e, n_used, tm, tn, total_rows, first_block, into):
    rows = a.shape[0]
    ff = w1.shape[3]
    wspec = pl.BlockSpec((None, 1, D_MODEL, tn), lambda j, i, be, nb: (layer, be[i], 0, j),
                         pipeline_mode=pl.Buffered(1))
    chained = [] if into is None else [into]
    grid_spec = pltpu.PrefetchScalarGridSpec(
        num_scalar_prefetch=2,
        grid=(ff // tn, rows // tm),
        in_specs=[pl.BlockSpec((tm, D_MODEL), lambda j, i, be, nb: (i, 0)), wspec, wspec]
        + [pl.BlockSpec(memory_space=pl.ANY) for _ in chained],
        out_specs=pl.BlockSpec((tm, tn), lambda j, i, be, nb: (first_block + i, j)),
        scratch_shapes=[pltpu.VMEM((D_MODEL, tn), BF16), pltpu.VMEM((D_MODEL, tn), BF16)],
    )
    return pl.pallas_call(
        _expert_up_kernel,
        grid_spec=grid_spec,
        out_shape=jax.ShapeDtypeStruct((total_rows, ff), BF16),
        input_output_aliases={5: 0} if chained else {},
        compiler_params=_cp(("arbitrary", "arbitrary")),
    )(blk_e, n_used, a, w1, w3, *chained)


def _expert_down_kernel(be_ref, nb_ref, h_ref, w2_ref, o_ref, w2_scr):
    @pl.when(_expert_changed(be_ref))
    def _():
        w2_scr[...] = w2_ref[0].astype(BF16)

    @pl.when(pl.program_id(1) < nb_ref[0])
    def _():
        o_ref[...] = _dot(h_ref[...], w2_scr[...]).astype(BF16)

    @pl.when(pl.program_id(1) >= nb_ref[0])
    def _():
        o_ref[...] = jnp.zeros(o_ref.shape, BF16)


def _expert_down(h, w2, layer, blk_e, n_used, tm, tn):
    rows, ff = h.shape
    grid_spec = pltpu.PrefetchScalarGridSpec(
        num_scalar_prefetch=2,
        grid=(D_MODEL // tn, rows // tm),
        in_specs=[pl.BlockSpec((tm, ff), lambda j, i, be, nb: (i, 0)),
                  pl.BlockSpec((None, 1, ff, tn), lambda j, i, be, nb: (layer, be[i], 0, j),
                               pipeline_mode=pl.Buffered(1))],
        out_specs=pl.BlockSpec((tm, tn), lambda j, i, be, nb: (i, j)),
        scratch_shapes=[pltpu.VMEM((ff, tn), BF16)],
    )
    return pl.pallas_call(
        _expert_down_kernel,
        grid_spec=grid_spec,
        out_shape=jax.ShapeDtypeStruct((rows, D_MODEL), BF16),
        compiler_params=_cp(("arbitrary", "arbitrary")),
    )(blk_e, n_used, h, w2)


def _moe(x, xb, w_r, w1, w3, w2, layer, tm):
    t = x.shape[0]
    m = t * TOP_K
    gb = GROUP_ROWS
    routed = _router(x, w_r, tm)
    flat_e = routed[:, :TOP_K].astype(jnp.int32).reshape(-1)
    onehot = (flat_e[:, None] == jnp.arange(N_EXPERTS, dtype=jnp.int32)[None, :]).astype(jnp.int32)
    csum = jnp.cumsum(onehot, axis=0)
    counts = csum[-1]
    rank = jnp.sum(csum * onehot, axis=1) - 1
    padded = ((counts + gb - 1) // gb) * gb
    pend = jnp.cumsum(padded)
    pstart = pend - padded
    dest = jnp.sum(pstart[None, :] * onehot, axis=1) + rank
    n_blocks = m // gb + N_EXPERTS
    blk_e = jnp.minimum(jnp.sum((jnp.arange(n_blocks, dtype=jnp.int32)[:, None] * gb >= pend[None, :])
                                .astype(jnp.int32), axis=1), N_EXPERTS - 1)
    n_used = (pend[-1:] // gb).astype(jnp.int32)
    n_rows = n_blocks * gb
    gap = padded - counts
    gap_end = jnp.cumsum(gap)
    i_dummy = jnp.arange(n_rows - m, dtype=jnp.int32)
    in_gap = (i_dummy[:, None] >= (gap_end - gap)[None, :]) & (i_dummy[:, None] < gap_end[None, :])
    dummy_key = jnp.where(i_dummy < gap_end[-1],
                          jnp.sum(jnp.where(in_gap, (pstart + counts - (gap_end - gap))[None, :], 0), axis=1),
                          pend[-1] - gap_end[-1]) + i_dummy
    keys = jnp.concatenate([dest, dummy_key])
    toks = jnp.concatenate([jnp.arange(m, dtype=jnp.int32) // TOP_K, jnp.zeros((n_rows - m,), jnp.int32)])
    _, src_tok = lax.sort((keys, toks), num_keys=1)
    take = lambda arr, rows: arr.at[rows].get(mode="promise_in_bounds")
    nbc = n_blocks // DISPATCH_CHUNKS
    h = None
    for c in range(DISPATCH_CHUNKS):
        a = take(xb, src_tok[c * nbc * gb:(c + 1) * nbc * gb])
        h = _expert_up(a, w1, w3, layer, blk_e[c * nbc:(c + 1) * nbc], jnp.clip(n_used - c * nbc, 0, nbc),
                       gb, EXPERT_FF // 2, n_rows, c * nbc, h)
    yb = _expert_down(h, w2, layer, blk_e, n_used, gb, D_MODEL)
    dest2 = dest.reshape(t, TOP_K)
    return take(yb, dest2[:, 0]), take(yb, dest2[:, 1]), routed


def _t5_bucket(dist):
    n = jnp.maximum(dist, 0)
    max_exact = REL_BUCKETS // 2
    large = max_exact + (jnp.log(jnp.maximum(n, 1).astype(F32) / max_exact)
                         / math.log(REL_MAX_DIST / max_exact) * (REL_BUCKETS - max_exact)).astype(jnp.int32)
    large = jnp.minimum(large, REL_BUCKETS - 1)
    return jnp.where(n < max_exact, n, large)


def _bias_tiles(table, tile):
    a = jnp.arange(tile, dtype=jnp.int32)
    d0 = a[None, :] - a[:, None]
    rel = table.astype(F32) - table[REL_BUCKETS - 1].astype(F32)[None, :]

    def lookup(dist):
        onehot = (_t5_bucket(dist)[None, :, :] == jnp.arange(REL_BUCKETS, dtype=jnp.int32)[:, None, None])
        return jnp.sum(jnp.where(onehot[:, None], rel[:, :, None, None], 0.0), axis=0)

    f0 = lookup(d0)
    f1 = lookup(d0 + tile)
    f0 = jnp.where((d0 >= 0)[None], f0, NEG)
    tiles = jnp.stack([f0, f1], axis=1)
    return jnp.concatenate([tiles, tiles], axis=-1)


def _rope_tables(positions):
    half = MLA_ROPE_DIM // 2
    inv_freq = ROPE_THETA ** (-jnp.arange(half, dtype=F32) / half)
    ang = positions.astype(F32)[:, :, None] * inv_freq
    c, s = jnp.cos(ang), jnp.sin(ang)
    b, sq = positions.shape
    ones = jnp.ones((b, sq, MLA_NOPE_DIM), F32)
    z_nope = jnp.zeros((b, sq, MLA_NOPE_DIM), F32)
    z_pad = jnp.zeros((b, sq, HEAD_LANES - MLA_NOPE_DIM - MLA_ROPE_DIM), F32)
    cos_t = jnp.concatenate([ones, c, c, z_pad], axis=-1).reshape(b * sq, HEAD_LANES)
    sin_t = jnp.concatenate([z_nope, s, s, z_pad], axis=-1).reshape(b * sq, HEAD_LANES)
    return cos_t, sin_t


def _rot_cols(w):
    half = MLA_ROPE_DIM // 2
    return jnp.concatenate([-w[..., half:], w[..., :half]], axis=-1)


def _mixer_weights(w_in, w_uq, w_ukv):
    d = D_MODEL
    hq = DIFF_HEADS * 2 * DIFF_HEAD_DIM
    o = 0
    w_dq = w_in[:, o:o + hq]; o += hq
    w_dk = w_in[:, o:o + hq]; o += hq
    w_dv = w_in[:, o:o + hq]; o += hq
    w_cq = w_in[:, o:o + MLA_Q_RANK]; o += MLA_Q_RANK
    w_ckv = w_in[:, o:o + MLA_KV_RANK]; o += MLA_KV_RANK
    w_kr = w_in[:, o:o + MLA_ROPE_DIM]; o += MLA_ROPE_DIM
    w_g = w_in[:, o:]
    w_qkv = jnp.concatenate([w_dq, w_dk, w_dv], axis=1).astype(BF16)
    pad = HEAD_LANES - MLA_NOPE_DIM - MLA_ROPE_DIM
    z = lambda n: jnp.zeros((d, n), F32)
    w_c = jnp.concatenate([w_cq, w_ckv, z(MLA_NOPE_DIM), w_kr, z(pad), z(MLA_NOPE_DIM), _rot_cols(w_kr), z(pad)],
                          axis=1).astype(BF16)
    wq = w_uq.reshape(MLA_Q_RANK, MLA_HEADS, MLA_NOPE_DIM + MLA_ROPE_DIM)
    nope, rope = wq[..., :MLA_NOPE_DIM], wq[..., MLA_NOPE_DIM:]
    zq = lambda n: jnp.zeros((MLA_Q_RANK, MLA_HEADS, n), F32)
    plain = jnp.concatenate([nope, rope, zq(pad)], axis=-1).reshape(MLA_Q_RANK, -1)
    rot = jnp.concatenate([zq(MLA_NOPE_DIM), _rot_cols(rope), zq(pad)], axis=-1).reshape(MLA_Q_RANK, -1)
    w_q2 = jnp.concatenate([plain, rot], axis=1).astype(BF16)
    wkv = w_ukv.reshape(MLA_KV_RANK, MLA_HEADS, MLA_NOPE_DIM + MLA_V_DIM)
    zk = jnp.zeros((MLA_KV_RANK, MLA_HEADS, HEAD_LANES - MLA_NOPE_DIM), F32)
    w_kv2 = jnp.concatenate([
        jnp.concatenate([wkv[..., :MLA_NOPE_DIM], zk], axis=-1).reshape(MLA_KV_RANK, -1),
        wkv[..., MLA_NOPE_DIM:].reshape(MLA_KV_RANK, -1)], axis=1).astype(BF16)
    return w_qkv, w_c, w_g.astype(BF16), w_q2, w_kv2


def kernel(x, p, positions, rel_bias_table, w_in, b_gate, lambda_q1, lambda_k1, lambda_q2, lambda_k2,
           diff_subln_g, mla_q_norm_g, w_uq, mla_kv_norm_g, w_ukv, w_branch_diff, w_branch_mla, w_out,
           ln_mix_g, ln_mix_b, dense_w1, dense_w3, dense_w2, router_w, expert_w1, expert_w3, expert_w2,
           w_ple_gate, w_ple_proj, ln_ffn_g, ln_ffn_b):
    batch, seq, d = x.shape
    depth = w_in.shape[0]
    t = batch * seq
    tm = min(ROW_TILE, t)
    tm_mm = BIG_ROW_TILE if t % BIG_ROW_TILE == 0 else tm
    alpha = (2.0 * depth) ** 0.25
    row = lambda v: v.reshape(1, -1).astype(F32)

    cos_t, sin_t = _rope_tables(positions)
    bias = _bias_tiles(rel_bias_table * LOG2E, ATT_TILE)
    xf = x.reshape(t, d)
    xb = xf.astype(BF16)

    for i in range(depth):
        lam_init = 0.8 - 0.6 * math.exp(-0.3 * i)
        w_qkv, w_c, w_g, w_q2, w_kv2 = _mixer_weights(w_in[i], w_uq[i], w_ukv[i])
        qkv = _matmul(xb, w_qkv, BF16, tm_mm, 1024, scaled_tiles=1, scale=DIFF_HEAD_DIM ** -0.5 * LOG2E)
        q_mla, kv_mla, kr = _latent(xb, w_c, w_q2, w_kv2, row(mla_q_norm_g[i]), row(mla_kv_norm_g[i]),
                                    cos_t, sin_t, tm)
        o_a = _diff_attention(qkv.reshape(batch, seq, -1), bias, row(lambda_q1[i]), row(lambda_k1[i]),
                              row(lambda_q2[i]), row(lambda_k2[i]), row(diff_subln_g[i]), lam_init, batch, seq)
        o_b = _mla_attention(q_mla.reshape(batch, seq, -1), kv_mla.reshape(batch, seq, -1),
                             kr.reshape(batch, seq, -1), batch, seq)
        xf, xb = _mix_ln(xb, o_a.reshape(t, -1), o_b.reshape(t, -1), xf, w_g, row(b_gate[i]),
                         w_branch_diff[i].astype(BF16), w_branch_mla[i].astype(BF16), w_out[i].astype(BF16),
                         row(ln_mix_g[i]), row(ln_mix_b[i]), alpha, tm)
        j = i // 2
        p_i = p[i].reshape(t, PLE_DIM)
        w_pg = w_ple_gate[i].astype(BF16)
        w_pp = w_ple_proj[i].astype(BF16)
        if i % 2 == 0:
            hmid = _swiglu_up(xb, dense_w1[j].astype(BF16), dense_w3[j].astype(BF16), tm_mm, DENSE_FF // 2)
            lead = [(hmid, True), (dense_w2[j].astype(BF16), False)]
            body = _dense_ln_kernel
        else:
            w_r = jnp.concatenate([router_w[j], jnp.zeros((d, HEAD_LANES - N_EXPERTS), F32)], axis=1)
            y1, y2, routed = _moe(xf, xb, w_r, expert_w1, expert_w3, expert_w2, j, tm)
            lead = [(y1, True), (y2, True), (routed, True)]
            body = _moe_ln_kernel
        xf, xb = _channel_ln(body, lead, xf, xb, p_i, w_pg, w_pp, row(ln_ffn_g[i]), row(ln_ffn_b[i]), alpha, tm)
    return xf.reshape(batch, seq, d)
```

```python
import functools
import math

import jax
import jax.numpy as jnp
from jax import lax
from jax.experimental import pallas as pl
from jax.experimental.pallas import tpu as pltpu

F32 = jnp.float32
BF16 = jnp.bfloat16

D_MODEL = 1024
PLE_DIM = 256
DIFF_HEADS = 8
DIFF_HEAD_DIM = 64
MLA_HEADS = 8
MLA_Q_RANK = 384
MLA_KV_RANK = 256
MLA_NOPE_DIM = 64
MLA_ROPE_DIM = 32
MLA_V_DIM = 64
ROPE_THETA = 10000.0
REL_BUCKETS = 32
REL_MAX_DIST = 128
DENSE_FF = 2816
N_EXPERTS = 8
TOP_K = 2
EXPERT_FF = 3584
EPS = 1e-5

HEAD_LANES = 128
NEG = -1e30
LOG2E = math.log2(math.e)
ATT_TILE = 256
ATT_HEADS_PER_STEP = 4
ROW_TILE = 512
BIG_ROW_TILE = 1024
GROUP_ROWS = 512
DISPATCH_CHUNKS = 4
VMEM_LIMIT = 52 * 1024 * 1024


def _cp(sem):
    return pltpu.CompilerParams(dimension_semantics=sem, vmem_limit_bytes=VMEM_LIMIT)


def _dot(a, b):
    return jnp.dot(a, b, preferred_element_type=F32)


def _sigmoid(x):
    return 1.0 / (1.0 + jnp.exp(-x))


def _layernorm(r, g, b):
    mu = jnp.mean(r, axis=-1, keepdims=True)
    d = r - mu
    var = jnp.mean(d * d, axis=-1, keepdims=True)
    return d * lax.rsqrt(var + EPS) * g + b


def _rmsnorm(x, g):
    return x * lax.rsqrt(jnp.mean(x * x, axis=-1, keepdims=True) + EPS) * g


def _mm_kernel(a_ref, w_ref, o_ref, *, scaled_tiles, scale):
    acc = _dot(a_ref[...], w_ref[...])
    if scaled_tiles:
        acc = acc * jnp.where(pl.program_id(1) < scaled_tiles, scale, 1.0)
    o_ref[...] = acc.astype(o_ref.dtype)


def _matmul(a, w, out_dtype, tm, tn, scaled_tiles=0, scale=1.0):
    m, k = a.shape
    n = w.shape[1]
    return pl.pallas_call(
        functools.partial(_mm_kernel, scaled_tiles=scaled_tiles, scale=scale),
        grid=(m // tm, n // tn),
        in_specs=[pl.BlockSpec((tm, k), lambda i, j: (i, 0)),
                  pl.BlockSpec((k, tn), lambda i, j: (0, j))],
        out_specs=pl.BlockSpec((tm, tn), lambda i, j: (i, j)),
        out_shape=jax.ShapeDtypeStruct((m, n), out_dtype),
        compiler_params=_cp(("parallel", "arbitrary")),
    )(a, w)


C_COLS = MLA_Q_RANK + MLA_KV_RANK + 2 * HEAD_LANES


def _resident(a):
    return pl.BlockSpec(a.shape, lambda *_: (0,) * a.ndim, pipeline_mode=pl.Buffered(1))


def _latent_kernel(a_ref, wc_ref, wq_ref, wk_ref, wvt_ref, gq_ref, gkv_ref, cos_ref, sin_ref, q_ref, k_ref, vt_ref, *,
                   scale):
    z = _dot(a_ref[...], wc_ref[...])
    cq = z[:, :MLA_Q_RANK]
    ckv = z[:, MLA_Q_RANK:MLA_Q_RANK + MLA_KV_RANK]
    kr = z[:, MLA_Q_RANK + MLA_KV_RANK:MLA_Q_RANK + MLA_KV_RANK + HEAD_LANES]
    kr_rot = z[:, MLA_Q_RANK + MLA_KV_RANK + HEAD_LANES:]
    cos, sin = cos_ref[...], sin_ref[...]
    k_rope = kr * cos + kr_rot * sin
    ckv_n = _rmsnorm(ckv, gkv_ref[...]).astype(BF16)
    zk = _dot(ckv_n, wk_ref[...])
    vt_ref[...] = lax.dot_general(wvt_ref[...], ckv_n, (((1,), (1,)), ((), ())),
                                  preferred_element_type=F32).astype(BF16)
    zq = _dot(_rmsnorm(cq, gq_ref[...]).astype(BF16), wq_ref[...])
    c = cos * scale
    s = sin * scale
    hl = HEAD_LANES
    for h in range(MLA_HEADS):
        k_ref[:, h * hl:(h + 1) * hl] = (zk[:, h * hl:(h + 1) * hl] + k_rope).astype(BF16)
        q_ref[:, h * hl:(h + 1) * hl] = (
            zq[:, h * hl:(h + 1) * hl] * c
            + zq[:, (MLA_HEADS + h) * hl:(MLA_HEADS + h + 1) * hl] * s).astype(BF16)


def _latent(xb, w_c, w_q2, w_k2, w_vt, gq, gkv, cos_t, sin_t, tm):
    t = xb.shape[0]
    row = lambda n: pl.BlockSpec((tm, n), lambda i: (i, 0))
    nq = MLA_HEADS * HEAD_LANES
    nv = MLA_HEADS * MLA_V_DIM
    scale = (MLA_NOPE_DIM + MLA_ROPE_DIM) ** -0.5 * LOG2E
    return pl.pallas_call(
        functools.partial(_latent_kernel, scale=scale),
        grid=(t // tm,),
        in_specs=[row(D_MODEL), _resident(w_c), _resident(w_q2), _resident(w_k2), _resident(w_vt), _resident(gq),
                  _resident(gkv), row(HEAD_LANES), row(HEAD_LANES)],
        out_specs=[row(nq), row(nq), pl.BlockSpec((nv, tm), lambda i: (0, i))],
        out_shape=[jax.ShapeDtypeStruct((t, nq), BF16),
                   jax.ShapeDtypeStruct((t, nq), BF16),
                   jax.ShapeDtypeStruct((nv, t), BF16)],
        compiler_params=_cp(("parallel",)),
    )(xb, w_c, w_q2, w_k2, w_vt, gq, gkv, cos_t, sin_t)


ONES_ROWS = 16


def _scores_t(k, q):
    return lax.dot_general(k, q, (((1,), (1,)), ((), ())), preferred_element_type=F32)


def _softmax_step_t(s, vt, m_ref, acc_ref):
    m_old = m_ref[...]
    m_new = jnp.maximum(m_old, jnp.max(s, axis=0, keepdims=True))
    p = jnp.exp2(s - m_new).astype(BF16)
    alpha = jnp.exp2(m_old - m_new)
    acc_ref[...] = alpha * acc_ref[...] + _dot(vt, p)
    m_ref[...] = m_new


def _fill_vt(v_ref, vt_scr, lane0, row0, v_rows, tile):
    for j in range(vt_scr.shape[0]):
        vt = v_ref[0, j * tile:(j + 1) * tile, lane0:lane0 + HEAD_LANES].astype(F32).T
        vt_scr[j, 0:v_rows, :] = vt[row0:row0 + v_rows].astype(BF16)
        vt_scr[j, v_rows:v_rows + ONES_ROWS, :] = jnp.ones((ONES_ROWS, tile), BF16)


def _normalised(acc, v_rows):
    return acc[0:v_rows] * (1.0 / acc[v_rows:v_rows + 1])


def _diff_attn_kernel(q_ref, k_ref, v_ref, bias_ref, lq1_ref, lk1_ref, lq2_ref, lk2_ref, g_ref, o_ref,
                      vt_scr, qcat_scr, s_scr, m_scr, acc_scr, *, lam_init, tile, heads):
    qi = pl.program_id(2)
    n_q = pl.num_programs(2)
    hd = 2 * DIFF_HEAD_DIM
    hl = HEAD_LANES
    cur = qi % 2

    def first_scores(q_tile, slot):
        rows = pl.ds(pl.multiple_of(q_tile * tile, tile), tile)
        for h in range(heads):
            q = q_ref[0, rows, h * hl:(h + 1) * hl]
            lane = lax.broadcasted_iota(jnp.int32, q.shape, 1)
            zero = jnp.zeros_like(q)
            qcat_scr[slot, h] = jnp.concatenate([jnp.where(lane < DIFF_HEAD_DIM, q, zero),
                                                 jnp.where(lane >= DIFF_HEAD_DIM, q, zero)], axis=0)
            s_scr[0, h] = _scores_t(k_ref[0, 0:tile, h * hl:(h + 1) * hl], qcat_scr[slot, h])

    @pl.when(qi == 0)
    def _():
        for h in range(heads):
            _fill_vt(v_ref, vt_scr.at[h], h * hl, 0, hd, tile)
        first_scores(0, 0)

    m_scr[...] = jnp.full(m_scr.shape, NEG, F32)
    acc_scr[...] = jnp.zeros(acc_scr.shape, F32)

    def scores(slot, kj):
        rows = pl.ds(pl.multiple_of(kj * tile, tile), tile)
        for h in range(heads):
            s_scr[slot, h] = _scores_t(k_ref[0, rows, h * hl:(h + 1) * hl], qcat_scr[cur, h])

    def update(slot, kj, bias_idx=None):
        for h in range(heads):
            s = s_scr[slot, h]
            if bias_idx is not None:
                s = s + bias_ref[h, bias_idx]
            _softmax_step_t(s, vt_scr[h, kj], m_scr.at[h], acc_scr.at[h])

    n_far = jnp.maximum(qi - 1, 0)
    quads = n_far // 4

    def pair_step(a):
        scores(1, a + 1)
        update(0, a)
        scores(0, a + 2)
        update(1, a + 1)

    def quad_body(j4, carry):
        pair_step(4 * j4)
        pair_step(4 * j4 + 2)
        return carry

    lax.fori_loop(0, quads, quad_body, 0)
    odd_pair = (n_far - 4 * quads) >= 2

    @pl.when(odd_pair)
    def _():
        pair_step(4 * quads)

    t0 = 4 * quads + jnp.where(odd_pair, 2, 0)

    @pl.when(n_far - t0 == 1)
    def _():
        scores(1, t0 + 1)
        update(0, t0)
        scores(0, t0 + 2)
        update(1, t0 + 1, 1)
        update(0, t0 + 2, 0)

    @pl.when(jnp.logical_and(n_far == t0, qi >= 1))
    def _():
        scores(1, t0 + 1)
        update(0, t0, 1)
        update(1, t0 + 1, 0)

    @pl.when(qi == 0)
    def _():
        update(0, 0, 0)

    first_scores(jnp.minimum(qi + 1, n_q - 1), 1 - cur)

    lam = (jnp.exp(jnp.sum(lq1_ref[...] * lk1_ref[...], axis=-1, keepdims=True))
           - jnp.exp(jnp.sum(lq2_ref[...] * lk2_ref[...], axis=-1, keepdims=True)) + lam_init)
    for h in range(heads):
        acc = acc_scr[h]
        o = _normalised(acc[:, :tile], hd) - lam * _normalised(acc[:, tile:], hd)
        o = o * lax.rsqrt(jnp.mean(o * o, axis=0, keepdims=True) + EPS)
        o_ref[0, :, h * hl:(h + 1) * hl] = (o.T * (g_ref[...] * (1.0 - lam_init))).astype(BF16)


def _diff_attention(qkv, bias, lq1, lk1, lq2, lk2, g, lam_init, batch, seq):
    tile = ATT_TILE
    hl = HEAD_LANES
    nh = DIFF_HEADS
    hps = ATT_HEADS_PER_STEP
    ng = nh // hps
    vec = lambda a: pl.BlockSpec(a.shape, lambda b, h, i: (0, 0))
    return pl.pallas_call(
        functools.partial(_diff_attn_kernel, lam_init=lam_init, tile=tile, heads=hps),
        grid=(batch, ng, seq // tile),
        in_specs=[pl.BlockSpec((1, seq, hps * hl), lambda b, h, i: (b, 0, h)),
                  pl.BlockSpec((1, seq, hps * hl), lambda b, h, i: (b, 0, ng + h)),
                  pl.BlockSpec((1, seq, hps * hl), lambda b, h, i: (b, 0, 2 * ng + h)),
                  pl.BlockSpec((hps, 2, tile, 2 * tile), lambda b, h, i: (h, 0, 0, 0)),
                  vec(lq1), vec(lk1), vec(lq2), vec(lk2), vec(g)],
        out_specs=pl.BlockSpec((1, tile, hps * hl), lambda b, h, i: (b, i, h)),
        out_shape=jax.ShapeDtypeStruct((batch, seq, nh * hl), BF16),
        scratch_shapes=[pltpu.VMEM((hps, seq // tile, hl + ONES_ROWS, tile), BF16),
                        pltpu.VMEM((2, hps, 2 * tile, hl), BF16),
                        pltpu.VMEM((2, hps, tile, 2 * tile), F32),
                        pltpu.VMEM((hps, 1, 2 * tile), F32),
                        pltpu.VMEM((hps, hl + ONES_ROWS, 2 * tile), F32)],
        compiler_params=_cp(("parallel", "parallel", "arbitrary")),
    )(qkv, qkv, qkv, bias, lq1, lk1, lq2, lk2, g)


MLA_STEP_HEADS = 4


def _mla_attn_kernel(q_ref, k_ref, vt_ref, o_ref, vt_scr, s_scr, m_scr, acc_scr, *, tile):
    qi = pl.program_id(2)
    n_q = pl.num_programs(2)
    hl = HEAD_LANES
    vd = MLA_V_DIM
    tq = 2 * tile

    def scores(slot, kj, q_tile):
        rows = pl.ds(pl.multiple_of(kj * tile, tile), tile)
        q_rows = pl.ds(pl.multiple_of(q_tile * tq, tq), tq)
        for h in range(MLA_STEP_HEADS):
            s_scr[slot, h] = _scores_t(k_ref[0, rows, h * hl:(h + 1) * hl], q_ref[0, q_rows, h * hl:(h + 1) * hl])

    @pl.when(qi == 0)
    def _():
        for h in range(MLA_STEP_HEADS):
            for j in range(vt_scr.shape[1]):
                vt_scr[h, j, 0:vd, :] = vt_ref[h * vd:(h + 1) * vd, j * tile:(j + 1) * tile]
                vt_scr[h, j, vd:vd + ONES_ROWS, :] = jnp.ones((ONES_ROWS, tile), BF16)
        scores(0, 0, 0)

    m_scr[...] = jnp.full(m_scr.shape, NEG, F32)
    acc_scr[...] = jnp.zeros(acc_scr.shape, F32)

    def update(slot, kj, key_offset=None):
        for h in range(MLA_STEP_HEADS):
            s = s_scr[slot, h]
            if key_offset is not None:
                key = lax.broadcasted_iota(jnp.int32, s.shape, 0)
                qry = lax.broadcasted_iota(jnp.int32, s.shape, 1)
                s = jnp.where(key + key_offset <= qry, s, NEG)
            _softmax_step_t(s, vt_scr[h, kj], m_scr.at[h], acc_scr.at[h])

    def pair_step(a):
        scores(1, a + 1, qi)
        update(0, a)
        scores(0, a + 2, qi)
        update(1, a + 1)

    def quad_body(j4, carry):
        pair_step(4 * j4)
        pair_step(4 * j4 + 2)
        return carry

    lax.fori_loop(0, qi // 2, quad_body, 0)

    @pl.when(qi % 2 == 1)
    def _():
        pair_step(2 * qi - 2)

    t0 = 2 * qi
    scores(1, t0 + 1, qi)
    update(0, t0, 0)
    update(1, t0 + 1, tile)
    scores(0, 0, jnp.minimum(qi + 1, n_q - 1))
    o = jnp.concatenate([_normalised(acc_scr[h], MLA_V_DIM) for h in range(MLA_STEP_HEADS)], axis=0)
    o_ref[0] = o.T.astype(BF16)


def _mla_attention(q, k, vt, batch, seq):
    tile = ATT_TILE
    tq = 2 * tile
    hl = HEAD_LANES
    nh = MLA_STEP_HEADS
    ng = MLA_HEADS // nh
    v_lanes = nh * MLA_V_DIM
    v_rows = MLA_V_DIM + ONES_ROWS
    return pl.pallas_call(
        functools.partial(_mla_attn_kernel, tile=tile),
        grid=(batch, ng, seq // tq),
        in_specs=[pl.BlockSpec((1, seq, nh * hl), lambda b, h, i: (b, 0, h)),
                  pl.BlockSpec((1, seq, nh * hl), lambda b, h, i: (b, 0, h)),
                  pl.BlockSpec((v_lanes, seq), lambda b, h, i: (h, b))],
        out_specs=pl.BlockSpec((1, tq, v_lanes), lambda b, h, i: (b, i, h)),
        out_shape=jax.ShapeDtypeStruct((batch, seq, MLA_HEADS * MLA_V_DIM), BF16),
        scratch_shapes=[pltpu.VMEM((nh, seq // tile, v_rows, tile), BF16),
                        pltpu.VMEM((2, nh, tile, tq), F32),
                        pltpu.VMEM((nh, 1, tq), F32),
                        pltpu.VMEM((nh, v_rows, tq), F32)],
        compiler_params=_cp(("parallel", "parallel", "arbitrary")),
    )(q, k, vt)


def _mix_ln_kernel(xb_ref, oa_ref, ob_ref, x_ref, wg_ref, bg_ref, wbd_ref, wbm_ref, wo_ref, g_ref, b_ref,
                   xo_ref, xbo_ref, *, alpha):
    d = D_MODEL
    xb = xb_ref[...]
    g_a = _sigmoid(_dot(xb, wg_ref[:, :d]) + bg_ref[:, :d])
    m = g_a * _dot(oa_ref[...], wbd_ref[...])
    g_b = _sigmoid(_dot(xb, wg_ref[:, d:]) + bg_ref[:, d:])
    m = (m + g_b * _dot(ob_ref[...], wbm_ref[...])).astype(BF16)
    y = _layernorm(alpha * x_ref[...] + _dot(m, wo_ref[...]), g_ref[...], b_ref[...])
    xo_ref[...] = y
    xbo_ref[...] = y.astype(BF16)


def _mix_ln(xb, o_a, o_b, x, w_g, b_g, w_bd, w_bm, w_out, g, b, alpha, tm):
    t = x.shape[0]
    row = lambda a: pl.BlockSpec((tm, a.shape[1]), lambda i: (i, 0))
    return pl.pallas_call(
        functools.partial(_mix_ln_kernel, alpha=alpha),
        grid=(t // tm,),
        in_specs=[row(xb), row(o_a), row(o_b), row(x), _resident(w_g), _resident(b_g), _resident(w_bd),
                  _resident(w_bm), _resident(w_out), _resident(g), _resident(b)],
        out_specs=[row(x), row(xb)],
        out_shape=[jax.ShapeDtypeStruct((t, D_MODEL), F32), jax.ShapeDtypeStruct((t, D_MODEL), BF16)],
        compiler_params=_cp(("parallel",)),
    )(xb, o_a, o_b, x, w_g, b_g, w_bd, w_bm, w_out, g, b)


def _swiglu_up_kernel(a_ref, w1_ref, w3_ref, o_ref):
    a = a_ref[...]
    u = _dot(a, w1_ref[...])
    o_ref[...] = (u * _sigmoid(u) * _dot(a, w3_ref[...])).astype(BF16)


def _swiglu_up(xb, w1, w3, tm, tn):
    t = xb.shape[0]
    ff = w1.shape[1]
    return pl.pallas_call(
        _swiglu_up_kernel,
        grid=(t // tm, ff // tn),
        in_specs=[pl.BlockSpec((tm, D_MODEL), lambda i, j: (i, 0)),
                  pl.BlockSpec((D_MODEL, tn), lambda i, j: (0, j)),
                  pl.BlockSpec((D_MODEL, tn), lambda i, j: (0, j))],
        out_specs=pl.BlockSpec((tm, tn), lambda i, j: (i, j)),
        out_shape=jax.ShapeDtypeStruct((t, ff), BF16),
        compiler_params=_cp(("parallel", "arbitrary")),
    )(xb, w1, w3)


def _ple_ln(f, x_ref, xb_ref, p_ref, wpg_ref, wpp_ref, g_ref, b_ref, xo_ref, xbo_ref, alpha):
    e = _sigmoid(_dot(xb_ref[...], wpg_ref[...])) * _dot(p_ref[...].astype(BF16), wpp_ref[...])
    y = _layernorm(alpha * x_ref[...] + f + e, g_ref[...], b_ref[...])
    xo_ref[...] = y
    xbo_ref[...] = y.astype(BF16)


def _dense_ln_kernel(h_ref, w2_ref, x_ref, xb_ref, p_ref, wpg_ref, wpp_ref, g_ref, b_ref, xo_ref, xbo_ref, *, alpha):
    _ple_ln(_dot(h_ref[...], w2_ref[...]), x_ref, xb_ref, p_ref, wpg_ref, wpp_ref, g_ref, b_ref, xo_ref, xbo_ref,
            alpha)


def _moe_ln_kernel(y1_ref, y2_ref, gates_ref, x_ref, xb_ref, p_ref, wpg_ref, wpp_ref, g_ref, b_ref, xo_ref, xbo_ref,
                   *, alpha):
    gates = gates_ref[...]
    f = (y1_ref[...].astype(F32) * gates[:, TOP_K:TOP_K + 1]
         + y2_ref[...].astype(F32) * gates[:, TOP_K + 1:TOP_K + 2])
    _ple_ln(f, x_ref, xb_ref, p_ref, wpg_ref, wpp_ref, g_ref, b_ref, xo_ref, xbo_ref, alpha)


def _channel_ln(body, lead, x, xb, p, w_pg, w_pp, g, b, alpha, tm):
    t = x.shape[0]
    row = lambda a: pl.BlockSpec((tm, a.shape[1]), lambda i: (i, 0))
    full = _resident
    tail = [x, xb, p, w_pg, w_pp, g, b]
    return pl.pallas_call(
        functools.partial(body, alpha=alpha),
        grid=(t // tm,),
        in_specs=[row(a) if tiled else full(a) for a, tiled in lead]
        + [row(x), row(xb), row(p), full(w_pg), full(w_pp), full(g), full(b)],
        out_specs=[row(x), row(xb)],
        out_shape=[jax.ShapeDtypeStruct((t, D_MODEL), F32), jax.ShapeDtypeStruct((t, D_MODEL), BF16)],
        compiler_params=_cp(("parallel",)),
    )(*[a for a, _ in lead], *tail)


def _router_kernel(x_ref, w_ref, o_ref):
    logits = jnp.dot(x_ref[...], w_ref[...], preferred_element_type=F32, precision=lax.Precision.HIGHEST)
    lane = lax.broadcasted_iota(jnp.int32, logits.shape, 1)
    lg = jnp.where(lane < N_EXPERTS, logits, -jnp.inf)
    v1 = jnp.max(lg, axis=-1, keepdims=True)
    i1 = jnp.min(jnp.where(lg == v1, lane, HEAD_LANES), axis=-1, keepdims=True)
    lg2 = jnp.where(lane == i1, -jnp.inf, lg)
    v2 = jnp.max(lg2, axis=-1, keepdims=True)
    i2 = jnp.min(jnp.where(lg2 == v2, lane, HEAD_LANES), axis=-1, keepdims=True)
    e2 = jnp.exp(v2 - v1)
    g1 = 1.0 / (1.0 + e2)
    g2 = e2 / (1.0 + e2)
    o_ref[...] = jnp.where(lane == 0, i1.astype(F32),
                           jnp.where(lane == 1, i2.astype(F32),
                                     jnp.where(lane == 2, g1, jnp.where(lane == 3, g2, 0.0))))


def _router(x, w_r, tm):
    t = x.shape[0]
    return pl.pallas_call(
        _router_kernel,
        grid=(t // tm,),
        in_specs=[pl.BlockSpec((tm, D_MODEL), lambda i: (i, 0)),
                  pl.BlockSpec(w_r.shape, lambda i: (0, 0))],
        out_specs=pl.BlockSpec((tm, HEAD_LANES), lambda i: (i, 0)),
        out_shape=jax.ShapeDtypeStruct((t, HEAD_LANES), F32),
        compiler_params=_cp(("parallel",)),
    )(x, w_r)


def _expert_changed(be_ref):
    i = pl.program_id(1)
    return jnp.logical_or(i == 0, be_ref[i] != be_ref[jnp.maximum(i - 1, 0)])


def _expert_up_kernel(be_ref, nb_ref, a_ref, w1_ref, w3_ref, *rest):
    o_ref, w1_scr, w3_scr = rest[-3:]

    @pl.when(_expert_changed(be_ref))
    def _():
        w1_scr[...] = w1_ref[0].astype(BF16)
        w3_scr[...] = w3_ref[0].astype(BF16)

    @pl.when(pl.program_id(1) < nb_ref[0])
    def _():
        a = a_ref[...]
        u = _dot(a, w1_scr[...])
        o_ref[...] = (u * _sigmoid(u) * _dot(a, w3_scr[...])).astype(BF16)

    @pl.when(pl.program_id(1) >= nb_ref[0])
    def _():
        o_ref[...] = jnp.zeros(o_ref.shape, BF16)


def _expert_up(a, w1, w3, layer, blk_e, n_used, tm, tn, total_rows, first_block, into):
    rows = a.shape[0]
    ff = w1.shape[3]
    wspec = pl.BlockSpec((None, 1, D_MODEL, tn), lambda j, i, be, nb: (layer, be[i], 0, j),
                         pipeline_mode=pl.Buffered(1))
    chained = [] if into is None else [into]
    grid_spec = pltpu.PrefetchScalarGridSpec(
        num_scalar_prefetch=2,
        grid=(ff // tn, rows // tm),
        in_specs=[pl.BlockSpec((tm, D_MODEL), lambda j, i, be, nb: (i, 0)), wspec, wspec]
        + [pl.BlockSpec(memory_space=pl.ANY) for _ in chained],
        out_specs=pl.BlockSpec((tm, tn), lambda j, i, be, nb: (first_block + i, j)),
        scratch_shapes=[pltpu.VMEM((D_MODEL, tn), BF16), pltpu.VMEM((D_MODEL, tn), BF16)],
    )
    return pl.pallas_call(
        _expert_up_kernel,
        grid_spec=grid_spec,
        out_shape=jax.ShapeDtypeStruct((total_rows, ff), BF16),
        input_output_aliases={5: 0} if chained else {},
        compiler_params=_cp(("arbitrary", "arbitrary")),
    )(blk_e, n_used, a, w1, w3, *chained)


def _expert_down_kernel(be_ref, nb_ref, h_ref, w2_ref, o_ref, w2_scr):
    @pl.when(_expert_changed(be_ref))
    def _():
        w2_scr[...] = w2_ref[0].astype(BF16)

    @pl.when(pl.program_id(1) < nb_ref[0])
    def _():
        o_ref[...] = _dot(h_ref[...], w2_scr[...]).astype(BF16)

    @pl.when(pl.program_id(1) >= nb_ref[0])
    def _():
        o_ref[...] = jnp.zeros(o_ref.shape, BF16)


def _expert_down(h, w2, layer, blk_e, n_used, tm, tn):
    rows, ff = h.shape
    grid_spec = pltpu.PrefetchScalarGridSpec(
        num_scalar_prefetch=2,
        grid=(D_MODEL // tn, rows // tm),
        in_specs=[pl.BlockSpec((tm, ff), lambda j, i, be, nb: (i, 0)),
                  pl.BlockSpec((None, 1, ff, tn), lambda j, i, be, nb: (layer, be[i], 0, j),
                               pipeline_mode=pl.Buffered(1))],
        out_specs=pl.BlockSpec((tm, tn), lambda j, i, be, nb: (i, j)),
        scratch_shapes=[pltpu.VMEM((ff, tn), BF16)],
    )
    return pl.pallas_call(
        _expert_down_kernel,
        grid_spec=grid_spec,
        out_shape=jax.ShapeDtypeStruct((rows, D_MODEL), BF16),
        compiler_params=_cp(("arbitrary", "arbitrary")),
    )(blk_e, n_used, h, w2)


def _moe(x, xb, w_r, w1, w3, w2, layer, tm):
    t = x.shape[0]
    m = t * TOP_K
    gb = GROUP_ROWS
    routed = _router(x, w_r, tm)
    flat_e = routed[:, :TOP_K].astype(jnp.int32).reshape(-1)
    onehot = (flat_e[:, None] == jnp.arange(N_EXPERTS, dtype=jnp.int32)[None, :]).astype(jnp.int32)
    csum = jnp.cumsum(onehot, axis=0)
    counts = csum[-1]
    rank = jnp.sum(csum * onehot, axis=1) - 1
    padded = ((counts + gb - 1) // gb) * gb
    pend = jnp.cumsum(padded)
    pstart = pend - padded
    dest = jnp.sum(pstart[None, :] * onehot, axis=1) + rank
    n_blocks = m // gb + N_EXPERTS
    blk_e = jnp.minimum(jnp.sum((jnp.arange(n_blocks, dtype=jnp.int32)[:, None] * gb >= pend[None, :])
                                .astype(jnp.int32), axis=1), N_EXPERTS - 1)
    n_used = (pend[-1:] // gb).astype(jnp.int32)
    n_rows = n_blocks * gb
    gap = padded - counts
    gap_end = jnp.cumsum(gap)
    i_dummy = jnp.arange(n_rows - m, dtype=jnp.int32)
    in_gap = (i_dummy[:, None] >= (gap_end - gap)[None, :]) & (i_dummy[:, None] < gap_end[None, :])
    dummy_key = jnp.where(i_dummy < gap_end[-1],
                          jnp.sum(jnp.where(in_gap, (pstart + counts - (gap_end - gap))[None, :], 0), axis=1),
                          pend[-1] - gap_end[-1]) + i_dummy
    keys = jnp.concatenate([dest, dummy_key]).astype(jnp.uint32)
    toks = jnp.concatenate([jnp.arange(m, dtype=jnp.uint32) // TOP_K, jnp.zeros((n_rows - m,), jnp.uint32)])
    assert n_rows * t < 2 ** 32
    src_tok = (jnp.sort(keys * t + toks) % t).astype(jnp.int32)
    take = lambda arr, rows: arr.at[rows].get(mode="promise_in_bounds")
    nbc = n_blocks // DISPATCH_CHUNKS
    h = None
    for c in range(DISPATCH_CHUNKS):
        a = take(xb, src_tok[c * nbc * gb:(c + 1) * nbc * gb])
        h = _expert_up(a, w1, w3, layer, blk_e[c * nbc:(c + 1) * nbc], jnp.clip(n_used - c * nbc, 0, nbc),
                       gb, EXPERT_FF // 2, n_rows, c * nbc, h)
    yb = _expert_down(h, w2, layer, blk_e, n_used, gb, D_MODEL)
    dest2 = dest.reshape(t, TOP_K)
    return take(yb, dest2[:, 0]), take(yb, dest2[:, 1]), routed


def _t5_bucket(dist):
    n = jnp.maximum(dist, 0)
    max_exact = REL_BUCKETS // 2
    large = max_exact + (jnp.log(jnp.maximum(n, 1).astype(F32) / max_exact)
                         / math.log(REL_MAX_DIST / max_exact) * (REL_BUCKETS - max_exact)).astype(jnp.int32)
    large = jnp.minimum(large, REL_BUCKETS - 1)
    return jnp.where(n < max_exact, n, large)


def _bias_tiles(table, tile):
    a = jnp.arange(tile, dtype=jnp.int32)
    d0 = a[None, :] - a[:, None]
    rel = table.astype(F32) - table[REL_BUCKETS - 1].astype(F32)[None, :]

    def lookup(dist):
        onehot = (_t5_bucket(dist)[None, :, :] == jnp.arange(REL_BUCKETS, dtype=jnp.int32)[:, None, None])
        return jnp.sum(jnp.where(onehot[:, None], rel[:, :, None, None], 0.0), axis=0)

    f0 = lookup(d0)
    f1 = lookup(d0 + tile)
    f0 = jnp.where((d0 >= 0)[None], f0, NEG)
    tiles = jnp.stack([f0, f1], axis=1)
    return jnp.concatenate([tiles, tiles], axis=-1)


def _rope_tables(positions):
    half = MLA_ROPE_DIM // 2
    inv_freq = ROPE_THETA ** (-jnp.arange(half, dtype=F32) / half)
    ang = positions.astype(F32)[:, :, None] * inv_freq
    c, s = jnp.cos(ang), jnp.sin(ang)
    b, sq = positions.shape
    ones = jnp.ones((b, sq, MLA_NOPE_DIM), F32)
    z_nope = jnp.zeros((b, sq, MLA_NOPE_DIM), F32)
    z_pad = jnp.zeros((b, sq, HEAD_LANES - MLA_NOPE_DIM - MLA_ROPE_DIM), F32)
    cos_t = jnp.concatenate([ones, c, c, z_pad], axis=-1).reshape(b * sq, HEAD_LANES)
    sin_t = jnp.concatenate([z_nope, s, s, z_pad], axis=-1).reshape(b * sq, HEAD_LANES)
    return cos_t, sin_t


def _rot_cols(w):
    half = MLA_ROPE_DIM // 2
    return jnp.concatenate([-w[..., half:], w[..., :half]], axis=-1)


def _mixer_weights(w_in, w_uq, w_ukv):
    d = D_MODEL
    hq = DIFF_HEADS * 2 * DIFF_HEAD_DIM
    o = 0
    w_dq = w_in[:, o:o + hq]; o += hq
    w_dk = w_in[:, o:o + hq]; o += hq
    w_dv = w_in[:, o:o + hq]; o += hq
    w_cq = w_in[:, o:o + MLA_Q_RANK]; o += MLA_Q_RANK
    w_ckv = w_in[:, o:o + MLA_KV_RANK]; o += MLA_KV_RANK
    w_kr = w_in[:, o:o + MLA_ROPE_DIM]; o += MLA_ROPE_DIM
    w_g = w_in[:, o:]
    w_qkv = jnp.concatenate([w_dq, w_dk, w_dv], axis=1).astype(BF16)
    pad = HEAD_LANES - MLA_NOPE_DIM - MLA_ROPE_DIM
    z = lambda n: jnp.zeros((d, n), F32)
    w_c = jnp.concatenate([w_cq, w_ckv, z(MLA_NOPE_DIM), w_kr, z(pad), z(MLA_NOPE_DIM), _rot_cols(w_kr), z(pad)],
                          axis=1).astype(BF16)
    wq = w_uq.reshape(MLA_Q_RANK, MLA_HEADS, MLA_NOPE_DIM + MLA_ROPE_DIM)
    nope, rope = wq[..., :MLA_NOPE_DIM], wq[..., MLA_NOPE_DIM:]
    zq = lambda n: jnp.zeros((MLA_Q_RANK, MLA_HEADS, n), F32)
    plain = jnp.concatenate([nope, rope, zq(pad)], axis=-1).reshape(MLA_Q_RANK, -1)
    rot = jnp.concatenate([zq(MLA_NOPE_DIM), _rot_cols(rope), zq(pad)], axis=-1).reshape(MLA_Q_RANK, -1)
    w_q2 = jnp.concatenate([plain, rot], axis=1).astype(BF16)
    wkv = w_ukv.reshape(MLA_KV_RANK, MLA_HEADS, MLA_NOPE_DIM + MLA_V_DIM)
    zk = jnp.zeros((MLA_KV_RANK, MLA_HEADS, HEAD_LANES - MLA_NOPE_DIM), F32)
    w_k2 = jnp.concatenate([wkv[..., :MLA_NOPE_DIM], zk], axis=-1).reshape(MLA_KV_RANK, -1).astype(BF16)
    w_vt = wkv[..., MLA_NOPE_DIM:].reshape(MLA_KV_RANK, -1).T.astype(BF16)
    return w_qkv, w_c, w_g.astype(BF16), w_q2, w_k2, w_vt


def kernel(x, p, positions, rel_bias_table, w_in, b_gate, lambda_q1, lambda_k1, lambda_q2, lambda_k2,
           diff_subln_g, mla_q_norm_g, w_uq, mla_kv_norm_g, w_ukv, w_branch_diff, w_branch_mla, w_out,
           ln_mix_g, ln_mix_b, dense_w1, dense_w3, dense_w2, router_w, expert_w1, expert_w3, expert_w2,
           w_ple_gate, w_ple_proj, ln_ffn_g, ln_ffn_b):
    batch, seq, d = x.shape
    depth = w_in.shape[0]
    t = batch * seq
    tm = min(ROW_TILE, t)
    tm_mm = BIG_ROW_TILE if t % BIG_ROW_TILE == 0 else tm
    alpha = (2.0 * depth) ** 0.25
    row = lambda v: v.reshape(1, -1).astype(F32)

    cos_t, sin_t = _rope_tables(positions)
    bias = _bias_tiles(rel_bias_table * LOG2E, ATT_TILE)
    xf = x.reshape(t, d)
    xb = xf.astype(BF16)

    for i in range(depth):
        lam_init = 0.8 - 0.6 * math.exp(-0.3 * i)
        w_qkv, w_c, w_g, w_q2, w_k2, w_vt = _mixer_weights(w_in[i], w_uq[i], w_ukv[i])
        qkv = _matmul(xb, w_qkv, BF16, tm_mm, 1024, scaled_tiles=1, scale=DIFF_HEAD_DIM ** -0.5 * LOG2E)
        q_mla, k_mla, vt_mla = _latent(xb, w_c, w_q2, w_k2, w_vt, row(mla_q_norm_g[i]), row(mla_kv_norm_g[i]),
                                       cos_t, sin_t, tm)
        o_a = _diff_attention(qkv.reshape(batch, seq, -1), bias, row(lambda_q1[i]), row(lambda_k1[i]),
                              row(lambda_q2[i]), row(lambda_k2[i]), row(diff_subln_g[i]), lam_init, batch, seq)
        o_b = _mla_attention(q_mla.reshape(batch, seq, -1), k_mla.reshape(batch, seq, -1), vt_mla, batch, seq)
        xf, xb = _mix_ln(xb, o_a.reshape(t, -1), o_b.reshape(t, -1), xf, w_g, row(b_gate[i]),
                         w_branch_diff[i].astype(BF16), w_branch_mla[i].astype(BF16), w_out[i].astype(BF16),
                         row(ln_mix_g[i]), row(ln_mix_b[i]), alpha, tm)
        j = i // 2
        p_i = p[i].reshape(t, PLE_DIM)
        w_pg = w_ple_gate[i].astype(BF16)
        w_pp = w_ple_proj[i].astype(BF16)
        if i % 2 == 0:
            hmid = _swiglu_up(xb, dense_w1[j].astype(BF16), dense_w3[j].astype(BF16), tm_mm, DENSE_FF // 2)
            lead = [(hmid, True), (dense_w2[j].astype(BF16), False)]
            body = _dense_ln_kernel
        else:
            w_r = jnp.concatenate([router_w[j], jnp.zeros((d, HEAD_LANES - N_EXPERTS), F32)], axis=1)
            y1, y2, routed = _moe(xf, xb, w_r, expert_w1, expert_w3, expert_w2, j, tm)
            lead = [(y1, True), (y2, True), (routed, True)]
            body = _moe_ln_kernel
        xf, xb = _channel_ln(body, lead, xf, xb, p_i, w_pg, w_pp, row(ln_ffn_g[i]), row(ln_ffn_b[i]), alpha, tm)
    return xf.reshape(batch, seq, d)
```

```python
import functools
import math

import jax
import jax.numpy as jnp
from jax import lax
from jax.experimental import pallas as pl
from jax.experimental.pallas import tpu as pltpu

F32 = jnp.float32
BF16 = jnp.bfloat16

D_MODEL = 1024
PLE_DIM = 256
DIFF_HEADS = 8
DIFF_HEAD_DIM = 64
MLA_HEADS = 8
MLA_Q_RANK = 384
MLA_KV_RANK = 256
MLA_NOPE_DIM = 64
MLA_ROPE_DIM = 32
MLA_V_DIM = 64
ROPE_THETA = 10000.0
REL_BUCKETS = 32
REL_MAX_DIST = 128
DENSE_FF = 2816
N_EXPERTS = 8
TOP_K = 2
EXPERT_FF = 3584
EPS = 1e-5

HEAD_LANES = 128
NEG = -1e30
LOG2E = math.log2(math.e)
ATT_TILE = 256
ATT_HEADS_PER_STEP = 4
ROW_TILE = 512
BIG_ROW_TILE = 1024
GROUP_ROWS = 512
DISPATCH_CHUNKS = 4
VMEM_LIMIT = 52 * 1024 * 1024


def _cp(sem):
    return pltpu.CompilerParams(dimension_semantics=sem, vmem_limit_bytes=VMEM_LIMIT)


def _dot(a, b):
    return jnp.dot(a, b, preferred_element_type=F32)


def _sigmoid(x):
    return 1.0 / (1.0 + jnp.exp(-x))


def _layernorm(r, g, b):
    mu = jnp.mean(r, axis=-1, keepdims=True)
    d = r - mu
    var = jnp.mean(d * d, axis=-1, keepdims=True)
    return d * lax.rsqrt(var + EPS) * g + b


def _rmsnorm(x, g):
    return x * lax.rsqrt(jnp.mean(x * x, axis=-1, keepdims=True) + EPS) * g


def _mm_kernel(a_ref, w_ref, o_ref, *, scaled_tiles, scale):
    acc = _dot(a_ref[...].astype(BF16), w_ref[...])
    if scaled_tiles:
        acc = acc * jnp.where(pl.program_id(1) < scaled_tiles, scale, 1.0)
    o_ref[...] = acc.astype(o_ref.dtype)


def _matmul(a, w, out_dtype, tm, tn, scaled_tiles=0, scale=1.0):
    m, k = a.shape
    n = w.shape[1]
    return pl.pallas_call(
        functools.partial(_mm_kernel, scaled_tiles=scaled_tiles, scale=scale),
        grid=(m // tm, n // tn),
        in_specs=[pl.BlockSpec((tm, k), lambda i, j: (i, 0)),
                  pl.BlockSpec((k, tn), lambda i, j: (0, j))],
        out_specs=pl.BlockSpec((tm, tn), lambda i, j: (i, j)),
        out_shape=jax.ShapeDtypeStruct((m, n), out_dtype),
        compiler_params=_cp(("parallel", "arbitrary")),
    )(a, w)


C_COLS = MLA_Q_RANK + MLA_KV_RANK + 2 * HEAD_LANES


def _resident(a):
    return pl.BlockSpec(a.shape, lambda *_: (0,) * a.ndim, pipeline_mode=pl.Buffered(1))


def _latent_kernel(a_ref, wc_ref, wq_ref, wk_ref, wvt_ref, gq_ref, gkv_ref, cos_ref, sin_ref, q_ref, k_ref, vt_ref, *,
                   scale):
    z = _dot(a_ref[...].astype(BF16), wc_ref[...])
    cq = z[:, :MLA_Q_RANK]
    ckv = z[:, MLA_Q_RANK:MLA_Q_RANK + MLA_KV_RANK]
    kr = z[:, MLA_Q_RANK + MLA_KV_RANK:MLA_Q_RANK + MLA_KV_RANK + HEAD_LANES]
    kr_rot = z[:, MLA_Q_RANK + MLA_KV_RANK + HEAD_LANES:]
    cos, sin = cos_ref[...], sin_ref[...]
    k_rope = kr * cos + kr_rot * sin
    ckv_n = _rmsnorm(ckv, gkv_ref[...]).astype(BF16)
    zk = _dot(ckv_n, wk_ref[...])
    vt_ref[...] = lax.dot_general(wvt_ref[...], ckv_n, (((1,), (1,)), ((), ())),
                                  preferred_element_type=F32).astype(BF16)
    zq = _dot(_rmsnorm(cq, gq_ref[...]).astype(BF16), wq_ref[...])
    c = cos * scale
    s = sin * scale
    hl = HEAD_LANES
    for h in range(MLA_HEADS):
        k_ref[:, h * hl:(h + 1) * hl] = (zk[:, h * hl:(h + 1) * hl] + k_rope).astype(BF16)
        q_ref[:, h * hl:(h + 1) * hl] = (
            zq[:, h * hl:(h + 1) * hl] * c
            + zq[:, (MLA_HEADS + h) * hl:(MLA_HEADS + h + 1) * hl] * s).astype(BF16)


def _latent(xb, w_c, w_q2, w_k2, w_vt, gq, gkv, cos_t, sin_t, tm):
    t = xb.shape[0]
    row = lambda n: pl.BlockSpec((tm, n), lambda i: (i, 0))
    nq = MLA_HEADS * HEAD_LANES
    nv = MLA_HEADS * MLA_V_DIM
    scale = (MLA_NOPE_DIM + MLA_ROPE_DIM) ** -0.5 * LOG2E
    return pl.pallas_call(
        functools.partial(_latent_kernel, scale=scale),
        grid=(t // tm,),
        in_specs=[row(D_MODEL), _resident(w_c), _resident(w_q2), _resident(w_k2), _resident(w_vt), _resident(gq),
                  _resident(gkv), row(HEAD_LANES), row(HEAD_LANES)],
        out_specs=[row(nq), row(nq), pl.BlockSpec((nv, tm), lambda i: (0, i))],
        out_shape=[jax.ShapeDtypeStruct((t, nq), BF16),
                   jax.ShapeDtypeStruct((t, nq), BF16),
                   jax.ShapeDtypeStruct((nv, t), BF16)],
        compiler_params=_cp(("parallel",)),
    )(xb, w_c, w_q2, w_k2, w_vt, gq, gkv, cos_t, sin_t)


ONES_ROWS = 16


def _scores_t(k, q):
    return lax.dot_general(k, q, (((1,), (1,)), ((), ())), preferred_element_type=F32)


def _softmax_step_t(s, vt, m_ref, acc_ref):
    m_old = m_ref[...]
    m_new = jnp.maximum(m_old, jnp.max(s, axis=0, keepdims=True))
    p = jnp.exp2(s - m_new).astype(BF16)
    alpha = jnp.exp2(m_old - m_new)
    acc_ref[...] = alpha * acc_ref[...] + _dot(vt, p)
    m_ref[...] = m_new


def _fill_vt(v_ref, vt_scr, lane0, row0, v_rows, tile):
    for j in range(vt_scr.shape[0]):
        vt = v_ref[0, j * tile:(j + 1) * tile, lane0:lane0 + HEAD_LANES].astype(F32).T
        vt_scr[j, 0:v_rows, :] = vt[row0:row0 + v_rows].astype(BF16)
        vt_scr[j, v_rows:v_rows + ONES_ROWS, :] = jnp.ones((ONES_ROWS, tile), BF16)


def _normalised(acc, v_rows):
    return acc[0:v_rows] * (1.0 / acc[v_rows:v_rows + 1])


def _diff_attn_kernel(q_ref, k_ref, v_ref, bias_ref, lq1_ref, lk1_ref, lq2_ref, lk2_ref, g_ref, o_ref,
                      vt_scr, qcat_scr, s_scr, m_scr, acc_scr, *, lam_init, tile, heads):
    qi = pl.program_id(2)
    n_q = pl.num_programs(2)
    hd = 2 * DIFF_HEAD_DIM
    hl = HEAD_LANES
    cur = qi % 2

    def first_scores(q_tile, slot):
        rows = pl.ds(pl.multiple_of(q_tile * tile, tile), tile)
        for h in range(heads):
            q = q_ref[0, rows, h * hl:(h + 1) * hl]
            lane = lax.broadcasted_iota(jnp.int32, q.shape, 1)
            zero = jnp.zeros_like(q)
            qcat_scr[slot, h] = jnp.concatenate([jnp.where(lane < DIFF_HEAD_DIM, q, zero),
                                                 jnp.where(lane >= DIFF_HEAD_DIM, q, zero)], axis=0)
            s_scr[0, h] = _scores_t(k_ref[0, 0:tile, h * hl:(h + 1) * hl], qcat_scr[slot, h])

    @pl.when(qi == 0)
    def _():
        for h in range(heads):
            _fill_vt(v_ref, vt_scr.at[h], h * hl, 0, hd, tile)
        first_scores(0, 0)

    m_scr[...] = jnp.full(m_scr.shape, NEG, F32)
    acc_scr[...] = jnp.zeros(acc_scr.shape, F32)

    def scores(slot, kj):
        rows = pl.ds(pl.multiple_of(kj * tile, tile), tile)
        for h in range(heads):
            s_scr[slot, h] = _scores_t(k_ref[0, rows, h * hl:(h + 1) * hl], qcat_scr[cur, h])

    def update(slot, kj, bias_idx=None):
        for h in range(heads):
            s = s_scr[slot, h]
            if bias_idx is not None:
                s = s + bias_ref[h, bias_idx]
            _softmax_step_t(s, vt_scr[h, kj], m_scr.at[h], acc_scr.at[h])

    n_far = jnp.maximum(qi - 1, 0)
    quads = n_far // 4

    def pair_step(a):
        scores(1, a + 1)
        update(0, a)
        scores(0, a + 2)
        update(1, a + 1)

    def quad_body(j4, carry):
        pair_step(4 * j4)
        pair_step(4 * j4 + 2)
        return carry

    lax.fori_loop(0, quads, quad_body, 0)
    odd_pair = (n_far - 4 * quads) >= 2

    @pl.when(odd_pair)
    def _():
        pair_step(4 * quads)

    t0 = 4 * quads + jnp.where(odd_pair, 2, 0)

    @pl.when(n_far - t0 == 1)
    def _():
        scores(1, t0 + 1)
        update(0, t0)
        scores(0, t0 + 2)
        update(1, t0 + 1, 1)
        update(0, t0 + 2, 0)

    @pl.when(jnp.logical_and(n_far == t0, qi >= 1))
    def _():
        scores(1, t0 + 1)
        update(0, t0, 1)
        update(1, t0 + 1, 0)

    @pl.when(qi == 0)
    def _():
        update(0, 0, 0)

    first_scores(jnp.minimum(qi + 1, n_q - 1), 1 - cur)

    lam = (jnp.exp(jnp.sum(lq1_ref[...] * lk1_ref[...], axis=-1, keepdims=True))
           - jnp.exp(jnp.sum(lq2_ref[...] * lk2_ref[...], axis=-1, keepdims=True)) + lam_init)
    for h in range(heads):
        acc = acc_scr[h]
        o = _normalised(acc[:, :tile], hd) - lam * _normalised(acc[:, tile:], hd)
        o = o * lax.rsqrt(jnp.mean(o * o, axis=0, keepdims=True) + EPS)
        o_ref[0, :, h * hl:(h + 1) * hl] = (o.T * (g_ref[...] * (1.0 - lam_init))).astype(BF16)


def _diff_attention(qkv, bias, lq1, lk1, lq2, lk2, g, lam_init, batch, seq):
    tile = ATT_TILE
    hl = HEAD_LANES
    nh = DIFF_HEADS
    hps = ATT_HEADS_PER_STEP
    ng = nh // hps
    vec = lambda a: pl.BlockSpec(a.shape, lambda b, h, i: (0, 0))
    return pl.pallas_call(
        functools.partial(_diff_attn_kernel, lam_init=lam_init, tile=tile, heads=hps),
        grid=(batch, ng, seq // tile),
        in_specs=[pl.BlockSpec((1, seq, hps * hl), lambda b, h, i: (b, 0, h)),
                  pl.BlockSpec((1, seq, hps * hl), lambda b, h, i: (b, 0, ng + h)),
                  pl.BlockSpec((1, seq, hps * hl), lambda b, h, i: (b, 0, 2 * ng + h)),
                  pl.BlockSpec((hps, 2, tile, 2 * tile), lambda b, h, i: (h, 0, 0, 0)),
                  vec(lq1), vec(lk1), vec(lq2), vec(lk2), vec(g)],
        out_specs=pl.BlockSpec((1, tile, hps * hl), lambda b, h, i: (b, i, h)),
        out_shape=jax.ShapeDtypeStruct((batch, seq, nh * hl), BF16),
        scratch_shapes=[pltpu.VMEM((hps, seq // tile, hl + ONES_ROWS, tile), BF16),
                        pltpu.VMEM((2, hps, 2 * tile, hl), BF16),
                        pltpu.VMEM((2, hps, tile, 2 * tile), F32),
                        pltpu.VMEM((hps, 1, 2 * tile), F32),
                        pltpu.VMEM((hps, hl + ONES_ROWS, 2 * tile), F32)],
        compiler_params=_cp(("parallel", "parallel", "arbitrary")),
    )(qkv, qkv, qkv, bias, lq1, lk1, lq2, lk2, g)


MLA_STEP_HEADS = 4


def _mla_attn_kernel(q_ref, k_ref, vt_ref, o_ref, vt_scr, s_scr, m_scr, acc_scr, *, tile):
    qi = pl.program_id(2)
    n_q = pl.num_programs(2)
    hl = HEAD_LANES
    vd = MLA_V_DIM
    tq = 2 * tile

    def scores(slot, kj, q_tile):
        rows = pl.ds(pl.multiple_of(kj * tile, tile), tile)
        q_rows = pl.ds(pl.multiple_of(q_tile * tq, tq), tq)
        for h in range(MLA_STEP_HEADS):
            s_scr[slot, h] = _scores_t(k_ref[0, rows, h * hl:(h + 1) * hl], q_ref[0, q_rows, h * hl:(h + 1) * hl])

    @pl.when(qi == 0)
    def _():
        for h in range(MLA_STEP_HEADS):
            for j in range(vt_scr.shape[1]):
                vt_scr[h, j, 0:vd, :] = vt_ref[h * vd:(h + 1) * vd, j * tile:(j + 1) * tile]
                vt_scr[h, j, vd:vd + ONES_ROWS, :] = jnp.ones((ONES_ROWS, tile), BF16)
        scores(0, 0, 0)

    m_scr[...] = jnp.full(m_scr.shape, NEG, F32)
    acc_scr[...] = jnp.zeros(acc_scr.shape, F32)

    def update(slot, kj, key_offset=None):
        for h in range(MLA_STEP_HEADS):
            s = s_scr[slot, h]
            if key_offset is not None:
                key = lax.broadcasted_iota(jnp.int32, s.shape, 0)
                qry = lax.broadcasted_iota(jnp.int32, s.shape, 1)
                s = jnp.where(key + key_offset <= qry, s, NEG)
            _softmax_step_t(s, vt_scr[h, kj], m_scr.at[h], acc_scr.at[h])

    def pair_step(a):
        scores(1, a + 1, qi)
        update(0, a)
        scores(0, a + 2, qi)
        update(1, a + 1)

    def quad_body(j4, carry):
        pair_step(4 * j4)
        pair_step(4 * j4 + 2)
        return carry

    lax.fori_loop(0, qi // 2, quad_body, 0)

    @pl.when(qi % 2 == 1)
    def _():
        pair_step(2 * qi - 2)

    t0 = 2 * qi
    scores(1, t0 + 1, qi)
    update(0, t0, 0)
    update(1, t0 + 1, tile)
    scores(0, 0, jnp.minimum(qi + 1, n_q - 1))
    o = jnp.concatenate([_normalised(acc_scr[h], MLA_V_DIM) for h in range(MLA_STEP_HEADS)], axis=0)
    o_ref[0] = o.T.astype(BF16)


def _mla_attention(q, k, vt, batch, seq):
    tile = ATT_TILE
    tq = 2 * tile
    hl = HEAD_LANES
    nh = MLA_STEP_HEADS
    ng = MLA_HEADS // nh
    v_lanes = nh * MLA_V_DIM
    v_rows = MLA_V_DIM + ONES_ROWS
    return pl.pallas_call(
        functools.partial(_mla_attn_kernel, tile=tile),
        grid=(batch, ng, seq // tq),
        in_specs=[pl.BlockSpec((1, seq, nh * hl), lambda b, h, i: (b, 0, h)),
                  pl.BlockSpec((1, seq, nh * hl), lambda b, h, i: (b, 0, h)),
                  pl.BlockSpec((v_lanes, seq), lambda b, h, i: (h, b))],
        out_specs=pl.BlockSpec((1, tq, v_lanes), lambda b, h, i: (b, i, h)),
        out_shape=jax.ShapeDtypeStruct((batch, seq, MLA_HEADS * MLA_V_DIM), BF16),
        scratch_shapes=[pltpu.VMEM((nh, seq // tile, v_rows, tile), BF16),
                        pltpu.VMEM((2, nh, tile, tq), F32),
                        pltpu.VMEM((nh, 1, tq), F32),
                        pltpu.VMEM((nh, v_rows, tq), F32)],
        compiler_params=_cp(("parallel", "parallel", "arbitrary")),
    )(q, k, vt)


def _mix_ln_kernel(oa_ref, ob_ref, x_ref, wg_ref, bg_ref, wbd_ref, wbm_ref, wo_ref, g_ref, b_ref,
                   xo_ref, xbo_ref, *, alpha):
    d = D_MODEL
    x = x_ref[...]
    xb = x.astype(BF16)
    g_a = _sigmoid(_dot(xb, wg_ref[:, :d]) + bg_ref[:, :d])
    m = g_a * _dot(oa_ref[...], wbd_ref[...])
    g_b = _sigmoid(_dot(xb, wg_ref[:, d:]) + bg_ref[:, d:])
    m = (m + g_b * _dot(ob_ref[...], wbm_ref[...])).astype(BF16)
    y = _layernorm(alpha * x + _dot(m, wo_ref[...]), g_ref[...], b_ref[...])
    xo_ref[...] = y
    xbo_ref[...] = y.astype(BF16)


def _mix_ln(o_a, o_b, x, w_g, b_g, w_bd, w_bm, w_out, g, b, alpha, tm):
    t = x.shape[0]
    row = lambda a: pl.BlockSpec((tm, a.shape[1]), lambda i: (i, 0))
    return pl.pallas_call(
        functools.partial(_mix_ln_kernel, alpha=alpha),
        grid=(t // tm,),
        in_specs=[row(o_a), row(o_b), row(x), _resident(w_g), _resident(b_g), _resident(w_bd),
                  _resident(w_bm), _resident(w_out), _resident(g), _resident(b)],
        out_specs=[row(x), row(o_a)],
        out_shape=[jax.ShapeDtypeStruct((t, D_MODEL), F32), jax.ShapeDtypeStruct((t, D_MODEL), BF16)],
        compiler_params=_cp(("parallel",)),
    )(o_a, o_b, x, w_g, b_g, w_bd, w_bm, w_out, g, b)


def _swiglu_up_kernel(a_ref, w1_ref, w3_ref, o_ref):
    a = a_ref[...]
    u = _dot(a, w1_ref[...])
    o_ref[...] = (u * _sigmoid(u) * _dot(a, w3_ref[...])).astype(BF16)


def _swiglu_up(xb, w1, w3, tm, tn):
    t = xb.shape[0]
    ff = w1.shape[1]
    return pl.pallas_call(
        _swiglu_up_kernel,
        grid=(t // tm, ff // tn),
        in_specs=[pl.BlockSpec((tm, D_MODEL), lambda i, j: (i, 0)),
                  pl.BlockSpec((D_MODEL, tn), lambda i, j: (0, j)),
                  pl.BlockSpec((D_MODEL, tn), lambda i, j: (0, j))],
        out_specs=pl.BlockSpec((tm, tn), lambda i, j: (i, j)),
        out_shape=jax.ShapeDtypeStruct((t, ff), BF16),
        compiler_params=_cp(("parallel", "arbitrary")),
    )(xb, w1, w3)


def _ple_ln(f, x_ref, xb_ref, p_ref, wpg_ref, wpp_ref, g_ref, b_ref, xo_ref, xbo_ref, alpha):
    e = _sigmoid(_dot(xb_ref[...], wpg_ref[...])) * _dot(p_ref[...].astype(BF16), wpp_ref[...])
    y = _layernorm(alpha * x_ref[...] + f + e, g_ref[...], b_ref[...])
    xo_ref[...] = y
    xbo_ref[...] = y.astype(BF16)


def _dense_ln_kernel(h_ref, w2_ref, x_ref, xb_ref, p_ref, wpg_ref, wpp_ref, g_ref, b_ref, xo_ref, xbo_ref, *, alpha):
    _ple_ln(_dot(h_ref[...], w2_ref[...]), x_ref, xb_ref, p_ref, wpg_ref, wpp_ref, g_ref, b_ref, xo_ref, xbo_ref,
            alpha)


def _moe_ln_kernel(y1_ref, y2_ref, gates_ref, x_ref, xb_ref, p_ref, wpg_ref, wpp_ref, g_ref, b_ref, xo_ref, xbo_ref,
                   *, alpha):
    gates = gates_ref[...]
    f = (y1_ref[...].astype(F32) * gates[:, TOP_K:TOP_K + 1]
         + y2_ref[...].astype(F32) * gates[:, TOP_K + 1:TOP_K + 2])
    _ple_ln(f, x_ref, xb_ref, p_ref, wpg_ref, wpp_ref, g_ref, b_ref, xo_ref, xbo_ref, alpha)


def _channel_ln(body, lead, x, xb, p, w_pg, w_pp, g, b, alpha, tm):
    t = x.shape[0]
    row = lambda a: pl.BlockSpec((tm, a.shape[1]), lambda i: (i, 0))
    full = _resident
    tail = [x, xb, p, w_pg, w_pp, g, b]
    return pl.pallas_call(
        functools.partial(body, alpha=alpha),
        grid=(t // tm,),
        in_specs=[row(a) if tiled else full(a) for a, tiled in lead]
        + [row(x), row(xb), row(p), full(w_pg), full(w_pp), full(g), full(b)],
        out_specs=[row(x), row(xb)],
        out_shape=[jax.ShapeDtypeStruct((t, D_MODEL), F32), jax.ShapeDtypeStruct((t, D_MODEL), BF16)],
        compiler_params=_cp(("parallel",)),
    )(*[a for a, _ in lead], *tail)


def _router_kernel(x_ref, whi_ref, wlo_ref, o_ref):
    x = x_ref[...]
    x_hi = x.astype(BF16)
    x_lo = (x - x_hi.astype(F32)).astype(BF16)
    w_hi = whi_ref[...]
    logits = _dot(x_hi, w_hi) + (_dot(x_lo, w_hi) + _dot(x_hi, wlo_ref[...]))
    lane = lax.broadcasted_iota(jnp.int32, logits.shape, 1)
    lg = jnp.where(lane < N_EXPERTS, logits, -jnp.inf)
    v1 = jnp.max(lg, axis=-1, keepdims=True)
    i1 = jnp.min(jnp.where(lg == v1, lane, HEAD_LANES), axis=-1, keepdims=True)
    lg2 = jnp.where(lane == i1, -jnp.inf, lg)
    v2 = jnp.max(lg2, axis=-1, keepdims=True)
    i2 = jnp.min(jnp.where(lg2 == v2, lane, HEAD_LANES), axis=-1, keepdims=True)
    e2 = jnp.exp(v2 - v1)
    g1 = 1.0 / (1.0 + e2)
    g2 = e2 / (1.0 + e2)
    o_ref[...] = jnp.where(lane == 0, i1.astype(F32),
                           jnp.where(lane == 1, i2.astype(F32),
                                     jnp.where(lane == 2, g1, jnp.where(lane == 3, g2, 0.0))))


def _router(x, w_r, tm):
    t = x.shape[0]
    w_hi = w_r.astype(BF16)
    w_lo = (w_r - w_hi.astype(F32)).astype(BF16)
    return pl.pallas_call(
        _router_kernel,
        grid=(t // tm,),
        in_specs=[pl.BlockSpec((tm, D_MODEL), lambda i: (i, 0)), _resident(w_hi), _resident(w_lo)],
        out_specs=pl.BlockSpec((tm, HEAD_LANES), lambda i: (i, 0)),
        out_shape=jax.ShapeDtypeStruct((t, HEAD_LANES), F32),
        compiler_params=_cp(("parallel",)),
    )(x, w_hi, w_lo)


def _expert_changed(be_ref):
    i = pl.program_id(1)
    return jnp.logical_or(i == 0, be_ref[i] != be_ref[jnp.maximum(i - 1, 0)])


def _expert_up_kernel(be_ref, nb_ref, a_ref, w1_ref, w3_ref, *rest):
    o_ref, w1_scr, w3_scr = rest[-3:]

    @pl.when(_expert_changed(be_ref))
    def _():
        w1_scr[...] = w1_ref[0].astype(BF16)
        w3_scr[...] = w3_ref[0].astype(BF16)

    @pl.when(pl.program_id(1) < nb_ref[0])
    def _():
        a = a_ref[...]
        u = _dot(a, w1_scr[...])
        o_ref[...] = (u * _sigmoid(u) * _dot(a, w3_scr[...])).astype(BF16)

    @pl.when(pl.program_id(1) >= nb_ref[0])
    def _():
        o_ref[...] = jnp.zeros(o_ref.shape, BF16)


def _expert_up(a, w1, w3, layer, blk_e, n_used, tm, tn, total_rows, first_block, into):
    rows = a.shape[0]
    ff = w1.shape[3]
    wspec = pl.BlockSpec((None, 1, D_MODEL, tn), lambda j, i, be, nb: (layer, be[i], 0, j),
                         pipeline_mode=pl.Buffered(1))
    chained = [] if into is None else [into]
    grid_spec = pltpu.PrefetchScalarGridSpec(
        num_scalar_prefetch=2,
        grid=(ff // tn, rows // tm),
        in_specs=[pl.BlockSpec((tm, D_MODEL), lambda j, i, be, nb: (i, 0)), wspec, wspec]
        + [pl.BlockSpec(memory_space=pl.ANY) for _ in chained],
        out_specs=pl.BlockSpec((tm, tn), lambda j, i, be, nb: (first_block + i, j)),
        scratch_shapes=[pltpu.VMEM((D_MODEL, tn), BF16), pltpu.VMEM((D_MODEL, tn), BF16)],
    )
    return pl.pallas_call(
        _expert_up_kernel,
        grid_spec=grid_spec,
        out_shape=jax.ShapeDtypeStruct((total_rows, ff), BF16),
        input_output_aliases={5: 0} if chained else {},
        compiler_params=_cp(("arbitrary", "arbitrary")),
    )(blk_e, n_used, a, w1, w3, *chained)


def _expert_down_kernel(be_ref, nb_ref, h_ref, w2_ref, o_ref, w2_scr):
    @pl.when(_expert_changed(be_ref))
    def _():
        w2_scr[...] = w2_ref[0].astype(BF16)

    @pl.when(pl.program_id(1) < nb_ref[0])
    def _():
        o_ref[...] = _dot(h_ref[...], w2_scr[...]).astype(BF16)

    @pl.when(pl.program_id(1) >= nb_ref[0])
    def _():
        o_ref[...] = jnp.zeros(o_ref.shape, BF16)


def _expert_down(h, w2, layer, blk_e, n_used, tm, tn):
    rows, ff = h.shape
    grid_spec = pltpu.PrefetchScalarGridSpec(
        num_scalar_prefetch=2,
        grid=(D_MODEL // tn, rows // tm),
        in_specs=[pl.BlockSpec((tm, ff), lambda j, i, be, nb: (i, 0)),
                  pl.BlockSpec((None, 1, ff, tn), lambda j, i, be, nb: (layer, be[i], 0, j),
                               pipeline_mode=pl.Buffered(1))],
        out_specs=pl.BlockSpec((tm, tn), lambda j, i, be, nb: (i, j)),
        scratch_shapes=[pltpu.VMEM((ff, tn), BF16)],
    )
    return pl.pallas_call(
        _expert_down_kernel,
        grid_spec=grid_spec,
        out_shape=jax.ShapeDtypeStruct((rows, D_MODEL), BF16),
        compiler_params=_cp(("arbitrary", "arbitrary")),
    )(blk_e, n_used, h, w2)


def _moe(x, xb, w_r, w1, w3, w2, layer, tm):
    t = x.shape[0]
    m = t * TOP_K
    gb = GROUP_ROWS
    routed = _router(x, w_r, tm)
    flat_e = routed[:, :TOP_K].astype(jnp.int32).reshape(-1)
    onehot = (flat_e[:, None] == jnp.arange(N_EXPERTS, dtype=jnp.int32)[None, :]).astype(jnp.int32)
    csum = jnp.cumsum(onehot, axis=0)
    counts = csum[-1]
    rank = jnp.sum(csum * onehot, axis=1) - 1
    padded = ((counts + gb - 1) // gb) * gb
    pend = jnp.cumsum(padded)
    pstart = pend - padded
    dest = jnp.sum(pstart[None, :] * onehot, axis=1) + rank
    n_blocks = m // gb + N_EXPERTS
    blk_e = jnp.minimum(jnp.sum((jnp.arange(n_blocks, dtype=jnp.int32)[:, None] * gb >= pend[None, :])
                                .astype(jnp.int32), axis=1), N_EXPERTS - 1)
    n_used = (pend[-1:] // gb).astype(jnp.int32)
    n_rows = n_blocks * gb
    gap = padded - counts
    gap_end = jnp.cumsum(gap)
    i_dummy = jnp.arange(n_rows - m, dtype=jnp.int32)
    in_gap = (i_dummy[:, None] >= (gap_end - gap)[None, :]) & (i_dummy[:, None] < gap_end[None, :])
    dummy_key = jnp.where(i_dummy < gap_end[-1],
                          jnp.sum(jnp.where(in_gap, (pstart + counts - (gap_end - gap))[None, :], 0), axis=1),
                          pend[-1] - gap_end[-1]) + i_dummy
    keys = jnp.concatenate([dest, dummy_key]).astype(jnp.uint32)
    toks = jnp.concatenate([jnp.arange(m, dtype=jnp.uint32) // TOP_K, jnp.zeros((n_rows - m,), jnp.uint32)])
    assert n_rows * t < 2 ** 32
    src_tok = (jnp.sort(keys * t + toks) % t).astype(jnp.int32)
    take = lambda arr, rows: arr.at[rows].get(mode="promise_in_bounds")
    nbc = n_blocks // DISPATCH_CHUNKS
    h = None
    for c in range(DISPATCH_CHUNKS):
        a = take(xb, src_tok[c * nbc * gb:(c + 1) * nbc * gb])
        h = _expert_up(a, w1, w3, layer, blk_e[c * nbc:(c + 1) * nbc], jnp.clip(n_used - c * nbc, 0, nbc),
                       gb, EXPERT_FF // 2, n_rows, c * nbc, h)
    yb = _expert_down(h, w2, layer, blk_e, n_used, gb, D_MODEL)
    dest2 = dest.reshape(t, TOP_K)
    return take(yb, dest2[:, 0]), take(yb, dest2[:, 1]), routed


def _t5_bucket(dist):
    n = jnp.maximum(dist, 0)
    max_exact = REL_BUCKETS // 2
    large = max_exact + (jnp.log(jnp.maximum(n, 1).astype(F32) / max_exact)
                         / math.log(REL_MAX_DIST / max_exact) * (REL_BUCKETS - max_exact)).astype(jnp.int32)
    large = jnp.minimum(large, REL_BUCKETS - 1)
    return jnp.where(n < max_exact, n, large)


def _bias_tiles(table, tile):
    period = 3 * tile
    k = jnp.arange(period, dtype=jnp.int32)
    dist = jnp.where(k < 2 * tile, k, k - period)
    rel = table.astype(F32) - table[REL_BUCKETS - 1].astype(F32)[None, :]
    onehot = _t5_bucket(dist)[:, None] == jnp.arange(REL_BUCKETS, dtype=jnp.int32)[None, :]
    f = jnp.sum(jnp.where(onehot[:, :, None], rel[None], 0.0), axis=1)
    f = jnp.where((dist >= 0)[:, None], f, NEG)
    heads = table.shape[1]
    toep = jnp.tile(f.T, (1, tile))[:, :tile * (period - 1)].reshape(heads, tile, period - 1)
    tiles = jnp.stack([toep[:, :, :tile], toep[:, :, tile:2 * tile]], axis=1)
    return jnp.concatenate([tiles, tiles], axis=-1)


def _rope_tables(positions):
    half = MLA_ROPE_DIM // 2
    inv_freq = ROPE_THETA ** (-jnp.arange(half, dtype=F32) / half)
    ang = positions.astype(F32)[:, :, None] * inv_freq
    c, s = jnp.cos(ang), jnp.sin(ang)
    b, sq = positions.shape
    ones = jnp.ones((b, sq, MLA_NOPE_DIM), F32)
    z_nope = jnp.zeros((b, sq, MLA_NOPE_DIM), F32)
    z_pad = jnp.zeros((b, sq, HEAD_LANES - MLA_NOPE_DIM - MLA_ROPE_DIM), F32)
    cos_t = jnp.concatenate([ones, c, c, z_pad], axis=-1).reshape(b * sq, HEAD_LANES)
    sin_t = jnp.concatenate([z_nope, s, s, z_pad], axis=-1).reshape(b * sq, HEAD_LANES)
    return cos_t, sin_t


def _rot_cols(w):
    half = MLA_ROPE_DIM // 2
    return jnp.concatenate([-w[..., half:], w[..., :half]], axis=-1)


def _mixer_weights(w_in, w_uq, w_ukv):
    d = D_MODEL
    hq = DIFF_HEADS * 2 * DIFF_HEAD_DIM
    o = 0
    w_dq = w_in[:, o:o + hq]; o += hq
    w_dk = w_in[:, o:o + hq]; o += hq
    w_dv = w_in[:, o:o + hq]; o += hq
    w_cq = w_in[:, o:o + MLA_Q_RANK]; o += MLA_Q_RANK
    w_ckv = w_in[:, o:o + MLA_KV_RANK]; o += MLA_KV_RANK
    w_kr = w_in[:, o:o + MLA_ROPE_DIM]; o += MLA_ROPE_DIM
    w_g = w_in[:, o:]
    w_qkv = jnp.concatenate([w_dq, w_dk, w_dv], axis=1).astype(BF16)
    pad = HEAD_LANES - MLA_NOPE_DIM - MLA_ROPE_DIM
    z = lambda n: jnp.zeros((d, n), F32)
    w_c = jnp.concatenate([w_cq, w_ckv, z(MLA_NOPE_DIM), w_kr, z(pad), z(MLA_NOPE_DIM), _rot_cols(w_kr), z(pad)],
                          axis=1).astype(BF16)
    wq = w_uq.reshape(MLA_Q_RANK, MLA_HEADS, MLA_NOPE_DIM + MLA_ROPE_DIM)
    nope, rope = wq[..., :MLA_NOPE_DIM], wq[..., MLA_NOPE_DIM:]
    zq = lambda n: jnp.zeros((MLA_Q_RANK, MLA_HEADS, n), F32)
    plain = jnp.concatenate([nope, rope, zq(pad)], axis=-1).reshape(MLA_Q_RANK, -1)
    rot = jnp.concatenate([zq(MLA_NOPE_DIM), _rot_cols(rope), zq(pad)], axis=-1).reshape(MLA_Q_RANK, -1)
    w_q2 = jnp.concatenate([plain, rot], axis=1).astype(BF16)
    wkv = w_ukv.reshape(MLA_KV_RANK, MLA_HEADS, MLA_NOPE_DIM + MLA_V_DIM)
    zk = jnp.zeros((MLA_KV_RANK, MLA_HEADS, HEAD_LANES - MLA_NOPE_DIM), F32)
    w_k2 = jnp.concatenate([wkv[..., :MLA_NOPE_DIM], zk], axis=-1).reshape(MLA_KV_RANK, -1).astype(BF16)
    w_vt = wkv[..., MLA_NOPE_DIM:].reshape(MLA_KV_RANK, -1).T.astype(BF16)
    return w_qkv, w_c, w_g.astype(BF16), w_q2, w_k2, w_vt


def kernel(x, p, positions, rel_bias_table, w_in, b_gate, lambda_q1, lambda_k1, lambda_q2, lambda_k2,
           diff_subln_g, mla_q_norm_g, w_uq, mla_kv_norm_g, w_ukv, w_branch_diff, w_branch_mla, w_out,
           ln_mix_g, ln_mix_b, dense_w1, dense_w3, dense_w2, router_w, expert_w1, expert_w3, expert_w2,
           w_ple_gate, w_ple_proj, ln_ffn_g, ln_ffn_b):
    batch, seq, d = x.shape
    depth = w_in.shape[0]
    t = batch * seq
    tm = min(ROW_TILE, t)
    tm_mm = BIG_ROW_TILE if t % BIG_ROW_TILE == 0 else tm
    alpha = (2.0 * depth) ** 0.25
    row = lambda v: v.reshape(1, -1).astype(F32)

    cos_t, sin_t = _rope_tables(positions)
    bias = _bias_tiles(rel_bias_table * LOG2E, ATT_TILE)
    xf = x.reshape(t, d)
    xb = xf

    for i in range(depth):
        lam_init = 0.8 - 0.6 * math.exp(-0.3 * i)
        w_qkv, w_c, w_g, w_q2, w_k2, w_vt = _mixer_weights(w_in[i], w_uq[i], w_ukv[i])
        qkv = _matmul(xb, w_qkv, BF16, tm_mm, 1024, scaled_tiles=1, scale=DIFF_HEAD_DIM ** -0.5 * LOG2E)
        q_mla, k_mla, vt_mla = _latent(xb, w_c, w_q2, w_k2, w_vt, row(mla_q_norm_g[i]), row(mla_kv_norm_g[i]),
                                       cos_t, sin_t, tm)
        o_a = _diff_attention(qkv.reshape(batch, seq, -1), bias, row(lambda_q1[i]), row(lambda_k1[i]),
                              row(lambda_q2[i]), row(lambda_k2[i]), row(diff_subln_g[i]), lam_init, batch, seq)
        o_b = _mla_attention(q_mla.reshape(batch, seq, -1), k_mla.reshape(batch, seq, -1), vt_mla, batch, seq)
        xf, xb = _mix_ln(o_a.reshape(t, -1), o_b.reshape(t, -1), xf, w_g, row(b_gate[i]),
                         w_branch_diff[i].astype(BF16), w_branch_mla[i].astype(BF16), w_out[i].astype(BF16),
                         row(ln_mix_g[i]), row(ln_mix_b[i]), alpha, tm)
        j = i // 2
        p_i = p[i].reshape(t, PLE_DIM)
        w_pg = w_ple_gate[i].astype(BF16)
        w_pp = w_ple_proj[i].astype(BF16)
        if i % 2 == 0:
            hmid = _swiglu_up(xb, dense_w1[j].astype(BF16), dense_w3[j].astype(BF16), tm_mm, DENSE_FF // 2)
            lead = [(hmid, True), (dense_w2[j].astype(BF16), False)]
            body = _dense_ln_kernel
        else:
            w_r = jnp.concatenate([router_w[j], jnp.zeros((d, HEAD_LANES - N_EXPERTS), F32)], axis=1)
            y1, y2, routed = _moe(xf, xb, w_r, expert_w1, expert_w3, expert_w2, j, tm)
            lead = [(y1, True), (y2, True), (routed, True)]
            body = _moe_ln_kernel
        xf, xb = _channel_ln(body, lead, xf, xb, p_i, w_pg, w_pp, row(ln_ffn_g[i]), row(ln_ffn_b[i]), alpha, tm)
    return xf.reshape(batch, seq, d)
```

```python
import functools
import math

import jax
import jax.numpy as jnp
from jax import lax
from jax.experimental import pallas as pl
from jax.experimental.pallas import tpu as pltpu

F32 = jnp.float32
BF16 = jnp.bfloat16

D_MODEL = 1024
PLE_DIM = 256
DIFF_HEADS = 8
DIFF_HEAD_DIM = 64
MLA_HEADS = 8
MLA_Q_RANK = 384
MLA_KV_RANK = 256
MLA_NOPE_DIM = 64
MLA_ROPE_DIM = 32
MLA_V_DIM = 64
ROPE_THETA = 10000.0
REL_BUCKETS = 32
REL_MAX_DIST = 128
DENSE_FF = 2816
N_EXPERTS = 8
TOP_K = 2
EXPERT_FF = 3584
EPS = 1e-5

HEAD_LANES = 128
NEG = -1e30
LOG2E = math.log2(math.e)
ATT_TILE = 256
ATT_HEADS_PER_STEP = 4
ROW_TILE = 512
BIG_ROW_TILE = 1024
GROUP_ROWS = 512
COMBINE_CHUNKS = 4
DISPATCH_SPLIT = (0, 1, 4, 10, 16)
VMEM_LIMIT = 52 * 1024 * 1024


def _cp(sem):
    return pltpu.CompilerParams(dimension_semantics=sem, vmem_limit_bytes=VMEM_LIMIT)


def _dot(a, b):
    return jnp.dot(a, b, preferred_element_type=F32)


def _sigmoid(x):
    return 1.0 / (1.0 + jnp.exp(-x))


def _layernorm(r, g, b):
    mu = jnp.mean(r, axis=-1, keepdims=True)
    d = r - mu
    var = jnp.mean(d * d, axis=-1, keepdims=True)
    return d * lax.rsqrt(var + EPS) * g + b


def _rmsnorm(x, g):
    return x * lax.rsqrt(jnp.mean(x * x, axis=-1, keepdims=True) + EPS) * g


def _mm_kernel(a_ref, w_ref, o_ref, *, scaled_tiles, scale):
    acc = _dot(a_ref[...].astype(BF16), w_ref[...])
    if scaled_tiles:
        acc = acc * jnp.where(pl.program_id(1) < scaled_tiles, scale, 1.0)
    o_ref[...] = acc.astype(o_ref.dtype)


def _matmul(a, w, out_dtype, tm, tn, scaled_tiles=0, scale=1.0):
    m, k = a.shape
    n = w.shape[1]
    return pl.pallas_call(
        functools.partial(_mm_kernel, scaled_tiles=scaled_tiles, scale=scale),
        grid=(m // tm, n // tn),
        in_specs=[pl.BlockSpec((tm, k), lambda i, j: (i, 0)),
                  pl.BlockSpec((k, tn), lambda i, j: (0, j))],
        out_specs=pl.BlockSpec((tm, tn), lambda i, j: (i, j)),
        out_shape=jax.ShapeDtypeStruct((m, n), out_dtype),
        compiler_params=_cp(("parallel", "arbitrary")),
    )(a, w)


C_COLS = MLA_Q_RANK + MLA_KV_RANK + 2 * HEAD_LANES


def _resident(a):
    return pl.BlockSpec(a.shape, lambda *_: (0,) * a.ndim, pipeline_mode=pl.Buffered(1))


def _latent_kernel(a_ref, wc_ref, wq_ref, wk_ref, wvt_ref, gq_ref, gkv_ref, cos_ref, sin_ref, q_ref, k_ref, vt_ref, *,
                   scale):
    z = _dot(a_ref[...].astype(BF16), wc_ref[...])
    cq = z[:, :MLA_Q_RANK]
    ckv = z[:, MLA_Q_RANK:MLA_Q_RANK + MLA_KV_RANK]
    kr = z[:, MLA_Q_RANK + MLA_KV_RANK:MLA_Q_RANK + MLA_KV_RANK + HEAD_LANES]
    kr_rot = z[:, MLA_Q_RANK + MLA_KV_RANK + HEAD_LANES:]
    cos, sin = cos_ref[...], sin_ref[...]
    k_rope = kr * cos + kr_rot * sin
    ckv_n = _rmsnorm(ckv, gkv_ref[...]).astype(BF16)
    zk = _dot(ckv_n, wk_ref[...])
    vt_ref[...] = lax.dot_general(wvt_ref[...], ckv_n, (((1,), (1,)), ((), ())),
                                  preferred_element_type=F32).astype(BF16)
    zq = _dot(_rmsnorm(cq, gq_ref[...]).astype(BF16), wq_ref[...])
    c = cos * scale
    s = sin * scale
    hl = HEAD_LANES
    for h in range(MLA_HEADS):
        k_ref[:, h * hl:(h + 1) * hl] = (zk[:, h * hl:(h + 1) * hl] + k_rope).astype(BF16)
        q_ref[:, h * hl:(h + 1) * hl] = (
            zq[:, h * hl:(h + 1) * hl] * c
            + zq[:, (MLA_HEADS + h) * hl:(MLA_HEADS + h + 1) * hl] * s).astype(BF16)


def _latent(xb, w_c, w_q2, w_k2, w_vt, gq, gkv, cos_t, sin_t, tm):
    t = xb.shape[0]
    row = lambda n: pl.BlockSpec((tm, n), lambda i: (i, 0))
    nq = MLA_HEADS * HEAD_LANES
    nv = MLA_HEADS * MLA_V_DIM
    scale = (MLA_NOPE_DIM + MLA_ROPE_DIM) ** -0.5 * LOG2E
    return pl.pallas_call(
        functools.partial(_latent_kernel, scale=scale),
        grid=(t // tm,),
        in_specs=[row(D_MODEL), _resident(w_c), _resident(w_q2), _resident(w_k2), _resident(w_vt), _resident(gq),
                  _resident(gkv), row(HEAD_LANES), row(HEAD_LANES)],
        out_specs=[row(nq), row(nq), pl.BlockSpec((nv, tm), lambda i: (0, i))],
        out_shape=[jax.ShapeDtypeStruct((t, nq), BF16),
                   jax.ShapeDtypeStruct((t, nq), BF16),
                   jax.ShapeDtypeStruct((nv, t), BF16)],
        compiler_params=_cp(("parallel",)),
    )(xb, w_c, w_q2, w_k2, w_vt, gq, gkv, cos_t, sin_t)


ONES_ROWS = 16


def _scores_t(k, q):
    return lax.dot_general(k, q, (((1,), (1,)), ((), ())), preferred_element_type=F32)


def _softmax_step_t(s, vt, m_ref, acc_ref):
    m_old = m_ref[...]
    m_new = jnp.maximum(m_old, jnp.max(s, axis=0, keepdims=True))
    p = jnp.exp2(s - m_new).astype(BF16)
    alpha = jnp.exp2(m_old - m_new)
    acc_ref[...] = alpha * acc_ref[...] + _dot(vt, p)
    m_ref[...] = m_new


def _fill_vt(v_ref, vt_scr, lane0, row0, v_rows, tile):
    for j in range(vt_scr.shape[0]):
        vt = v_ref[0, j * tile:(j + 1) * tile, lane0:lane0 + HEAD_LANES].astype(F32).T
        vt_scr[j, 0:v_rows, :] = vt[row0:row0 + v_rows].astype(BF16)
        vt_scr[j, v_rows:v_rows + ONES_ROWS, :] = jnp.ones((ONES_ROWS, tile), BF16)


def _normalised(acc, v_rows):
    return acc[0:v_rows] * (1.0 / acc[v_rows:v_rows + 1])


def _diff_attn_kernel(q_ref, k_ref, v_ref, bias_ref, lq1_ref, lk1_ref, lq2_ref, lk2_ref, g_ref, o_ref,
                      vt_scr, qcat_scr, s_scr, m_scr, acc_scr, *, lam_init, tile, heads):
    qi = pl.program_id(2)
    n_q = pl.num_programs(2)
    hd = 2 * DIFF_HEAD_DIM
    hl = HEAD_LANES
    cur = qi % 2

    def first_scores(q_tile, slot):
        rows = pl.ds(pl.multiple_of(q_tile * tile, tile), tile)
        for h in range(heads):
            q = q_ref[0, rows, h * hl:(h + 1) * hl]
            lane = lax.broadcasted_iota(jnp.int32, q.shape, 1)
            zero = jnp.zeros_like(q)
            qcat_scr[slot, h] = jnp.concatenate([jnp.where(lane < DIFF_HEAD_DIM, q, zero),
                                                 jnp.where(lane >= DIFF_HEAD_DIM, q, zero)], axis=0)
            s_scr[0, h] = _scores_t(k_ref[0, 0:tile, h * hl:(h + 1) * hl], qcat_scr[slot, h])

    @pl.when(qi == 0)
    def _():
        for h in range(heads):
            _fill_vt(v_ref, vt_scr.at[h], h * hl, 0, hd, tile)
        first_scores(0, 0)

    m_scr[...] = jnp.full(m_scr.shape, NEG, F32)
    acc_scr[...] = jnp.zeros(acc_scr.shape, F32)

    def scores(slot, kj):
        rows = pl.ds(pl.multiple_of(kj * tile, tile), tile)
        for h in range(heads):
            s_scr[slot, h] = _scores_t(k_ref[0, rows, h * hl:(h + 1) * hl], qcat_scr[cur, h])

    def update(slot, kj, bias_idx=None):
        for h in range(heads):
            s = s_scr[slot, h]
            if bias_idx is not None:
                s = s + bias_ref[h, bias_idx]
            _softmax_step_t(s, vt_scr[h, kj], m_scr.at[h], acc_scr.at[h])

    n_far = jnp.maximum(qi - 1, 0)
    quads = n_far // 4

    def pair_step(a):
        scores(1, a + 1)
        update(0, a)
        scores(0, a + 2)
        update(1, a + 1)

    def quad_body(j4, carry):
        pair_step(4 * j4)
        pair_step(4 * j4 + 2)
        return carry

    lax.fori_loop(0, quads, quad_body, 0)
    odd_pair = (n_far - 4 * quads) >= 2

    @pl.when(odd_pair)
    def _():
        pair_step(4 * quads)

    t0 = 4 * quads + jnp.where(odd_pair, 2, 0)

    @pl.when(n_far - t0 == 1)
    def _():
        scores(1, t0 + 1)
        update(0, t0)
        scores(0, t0 + 2)
        update(1, t0 + 1, 1)
        update(0, t0 + 2, 0)

    @pl.when(jnp.logical_and(n_far == t0, qi >= 1))
    def _():
        scores(1, t0 + 1)
        update(0, t0, 1)
        update(1, t0 + 1, 0)

    @pl.when(qi == 0)
    def _():
        update(0, 0, 0)

    first_scores(jnp.minimum(qi + 1, n_q - 1), 1 - cur)

    lam = (jnp.exp(jnp.sum(lq1_ref[...] * lk1_ref[...], axis=-1, keepdims=True))
           - jnp.exp(jnp.sum(lq2_ref[...] * lk2_ref[...], axis=-1, keepdims=True)) + lam_init)
    for h in range(heads):
        acc = acc_scr[h]
        o = _normalised(acc[:, :tile], hd) - lam * _normalised(acc[:, tile:], hd)
        o = o * lax.rsqrt(jnp.mean(o * o, axis=0, keepdims=True) + EPS)
        o_ref[0, :, h * hl:(h + 1) * hl] = (o.T * (g_ref[...] * (1.0 - lam_init))).astype(BF16)


def _diff_attention(qkv, bias, lq1, lk1, lq2, lk2, g, lam_init, batch, seq):
    tile = ATT_TILE
    hl = HEAD_LANES
    nh = DIFF_HEADS
    hps = ATT_HEADS_PER_STEP
    ng = nh // hps
    vec = lambda a: pl.BlockSpec(a.shape, lambda b, h, i: (0, 0))
    return pl.pallas_call(
        functools.partial(_diff_attn_kernel, lam_init=lam_init, tile=tile, heads=hps),
        grid=(batch, ng, seq // tile),
        in_specs=[pl.BlockSpec((1, seq, hps * hl), lambda b, h, i: (b, 0, h)),
                  pl.BlockSpec((1, seq, hps * hl), lambda b, h, i: (b, 0, ng + h)),
                  pl.BlockSpec((1, seq, hps * hl), lambda b, h, i: (b, 0, 2 * ng + h)),
                  pl.BlockSpec((hps, 2, tile, 2 * tile), lambda b, h, i: (h, 0, 0, 0)),
                  vec(lq1), vec(lk1), vec(lq2), vec(lk2), vec(g)],
        out_specs=pl.BlockSpec((1, tile, hps * hl), lambda b, h, i: (b, i, h)),
        out_shape=jax.ShapeDtypeStruct((batch, seq, nh * hl), BF16),
        scratch_shapes=[pltpu.VMEM((hps, seq // tile, hl + ONES_ROWS, tile), BF16),
                        pltpu.VMEM((2, hps, 2 * tile, hl), BF16),
                        pltpu.VMEM((2, hps, tile, 2 * tile), F32),
                        pltpu.VMEM((hps, 1, 2 * tile), F32),
                        pltpu.VMEM((hps, hl + ONES_ROWS, 2 * tile), F32)],
        compiler_params=_cp(("parallel", "parallel", "arbitrary")),
    )(qkv, qkv, qkv, bias, lq1, lk1, lq2, lk2, g)


MLA_STEP_HEADS = 4


def _mla_attn_kernel(q_ref, k_ref, vt_ref, o_ref, vt_scr, s_scr, m_scr, acc_scr, *, tile):
    qi = pl.program_id(2)
    n_q = pl.num_programs(2)
    hl = HEAD_LANES
    vd = MLA_V_DIM
    tq = 2 * tile

    def scores(slot, kj, q_tile):
        rows = pl.ds(pl.multiple_of(kj * tile, tile), tile)
        q_rows = pl.ds(pl.multiple_of(q_tile * tq, tq), tq)
        for h in range(MLA_STEP_HEADS):
            s_scr[slot, h] = _scores_t(k_ref[0, rows, h * hl:(h + 1) * hl], q_ref[0, q_rows, h * hl:(h + 1) * hl])

    @pl.when(qi == 0)
    def _():
        for h in range(MLA_STEP_HEADS):
            for j in range(vt_scr.shape[1]):
                vt_scr[h, j, 0:vd, :] = vt_ref[h * vd:(h + 1) * vd, j * tile:(j + 1) * tile]
                vt_scr[h, j, vd:vd + ONES_ROWS, :] = jnp.ones((ONES_ROWS, tile), BF16)
        scores(0, 0, 0)

    m_scr[...] = jnp.full(m_scr.shape, NEG, F32)
    acc_scr[...] = jnp.zeros(acc_scr.shape, F32)

    def update(slot, kj, key_offset=None):
        for h in range(MLA_STEP_HEADS):
            s = s_scr[slot, h]
            if key_offset is not None:
                key = lax.broadcasted_iota(jnp.int32, s.shape, 0)
                qry = lax.broadcasted_iota(jnp.int32, s.shape, 1)
                s = jnp.where(key + key_offset <= qry, s, NEG)
            _softmax_step_t(s, vt_scr[h, kj], m_scr.at[h], acc_scr.at[h])

    def pair_step(a):
        scores(1, a + 1, qi)
        update(0, a)
        scores(0, a + 2, qi)
        update(1, a + 1)

    def quad_body(j4, carry):
        pair_step(4 * j4)
        pair_step(4 * j4 + 2)
        return carry

    lax.fori_loop(0, qi // 2, quad_body, 0)

    @pl.when(qi % 2 == 1)
    def _():
        pair_step(2 * qi - 2)

    t0 = 2 * qi
    scores(1, t0 + 1, qi)
    update(0, t0, 0)
    update(1, t0 + 1, tile)
    scores(0, 0, jnp.minimum(qi + 1, n_q - 1))
    o = jnp.concatenate([_normalised(acc_scr[h], MLA_V_DIM) for h in range(MLA_STEP_HEADS)], axis=0)
    o_ref[0] = o.T.astype(BF16)


def _mla_attention(q, k, vt, batch, seq):
    tile = ATT_TILE
    tq = 2 * tile
    hl = HEAD_LANES
    nh = MLA_STEP_HEADS
    ng = MLA_HEADS // nh
    v_lanes = nh * MLA_V_DIM
    v_rows = MLA_V_DIM + ONES_ROWS
    return pl.pallas_call(
        functools.partial(_mla_attn_kernel, tile=tile),
        grid=(batch, ng, seq // tq),
        in_specs=[pl.BlockSpec((1, seq, nh * hl), lambda b, h, i: (b, 0, h)),
                  pl.BlockSpec((1, seq, nh * hl), lambda b, h, i: (b, 0, h)),
                  pl.BlockSpec((v_lanes, seq), lambda b, h, i: (h, b))],
        out_specs=pl.BlockSpec((1, tq, v_lanes), lambda b, h, i: (b, i, h)),
        out_shape=jax.ShapeDtypeStruct((batch, seq, MLA_HEADS * MLA_V_DIM), BF16),
        scratch_shapes=[pltpu.VMEM((nh, seq // tile, v_rows, tile), BF16),
                        pltpu.VMEM((2, nh, tile, tq), F32),
                        pltpu.VMEM((nh, 1, tq), F32),
                        pltpu.VMEM((nh, v_rows, tq), F32)],
        compiler_params=_cp(("parallel", "parallel", "arbitrary")),
    )(q, k, vt)


def _mix_ln_kernel(oa_ref, ob_ref, x_ref, wg_ref, bg_ref, wbd_ref, wbm_ref, wo_ref, g_ref, b_ref,
                   xo_ref, xbo_ref, *, alpha):
    d = D_MODEL
    x = x_ref[...]
    xb = x.astype(BF16)
    g_a = _sigmoid(_dot(xb, wg_ref[:, :d]) + bg_ref[:, :d])
    m = g_a * _dot(oa_ref[...], wbd_ref[...])
    g_b = _sigmoid(_dot(xb, wg_ref[:, d:]) + bg_ref[:, d:])
    m = (m + g_b * _dot(ob_ref[...], wbm_ref[...])).astype(BF16)
    y = _layernorm(alpha * x + _dot(m, wo_ref[...]), g_ref[...], b_ref[...])
    xo_ref[...] = y
    xbo_ref[...] = y.astype(BF16)


def _mix_ln(o_a, o_b, x, w_g, b_g, w_bd, w_bm, w_out, g, b, alpha, tm):
    t = x.shape[0]
    row = lambda a: pl.BlockSpec((tm, a.shape[1]), lambda i: (i, 0))
    return pl.pallas_call(
        functools.partial(_mix_ln_kernel, alpha=alpha),
        grid=(t // tm,),
        in_specs=[row(o_a), row(o_b), row(x), _resident(w_g), _resident(b_g), _resident(w_bd),
                  _resident(w_bm), _resident(w_out), _resident(g), _resident(b)],
        out_specs=[row(x), row(o_a)],
        out_shape=[jax.ShapeDtypeStruct((t, D_MODEL), F32), jax.ShapeDtypeStruct((t, D_MODEL), BF16)],
        compiler_params=_cp(("parallel",)),
    )(o_a, o_b, x, w_g, b_g, w_bd, w_bm, w_out, g, b)


def _swiglu_up_kernel(a_ref, w1_ref, w3_ref, o_ref):
    a = a_ref[...]
    u = _dot(a, w1_ref[...])
    o_ref[...] = (u * _sigmoid(u) * _dot(a, w3_ref[...])).astype(BF16)


def _swiglu_up(xb, w1, w3, tm, tn):
    t = xb.shape[0]
    ff = w1.shape[1]
    return pl.pallas_call(
        _swiglu_up_kernel,
        grid=(t // tm, ff // tn),
        in_specs=[pl.BlockSpec((tm, D_MODEL), lambda i, j: (i, 0)),
                  pl.BlockSpec((D_MODEL, tn), lambda i, j: (0, j)),
                  pl.BlockSpec((D_MODEL, tn), lambda i, j: (0, j))],
        out_specs=pl.BlockSpec((tm, tn), lambda i, j: (i, j)),
        out_shape=jax.ShapeDtypeStruct((t, ff), BF16),
        compiler_params=_cp(("parallel", "arbitrary")),
    )(xb, w1, w3)


def _ple_ln(f, x_ref, xb_ref, p_ref, wpg_ref, wpp_ref, g_ref, b_ref, xo_ref, xbo_ref, alpha):
    e = _sigmoid(_dot(xb_ref[...], wpg_ref[...])) * _dot(p_ref[...].astype(BF16), wpp_ref[...])
    y = _layernorm(alpha * x_ref[...] + f + e, g_ref[...], b_ref[...])
    xo_ref[...] = y
    xbo_ref[...] = y.astype(BF16)


def _dense_ln_kernel(h_ref, w2_ref, x_ref, xb_ref, p_ref, wpg_ref, wpp_ref, g_ref, b_ref, xo_ref, xbo_ref, *, alpha):
    _ple_ln(_dot(h_ref[...], w2_ref[...]), x_ref, xb_ref, p_ref, wpg_ref, wpp_ref, g_ref, b_ref, xo_ref, xbo_ref,
            alpha)


def _moe_ln_kernel(y1_ref, y2_ref, gates_ref, x_ref, xb_ref, p_ref, wpg_ref, wpp_ref, g_ref, b_ref, xo_ref, xbo_ref,
                   *, alpha):
    gates = gates_ref[...]
    f = (y1_ref[...].astype(F32) * gates[:, TOP_K:TOP_K + 1]
         + y2_ref[...].astype(F32) * gates[:, TOP_K + 1:TOP_K + 2])
    _ple_ln(f, x_ref, xb_ref, p_ref, wpg_ref, wpp_ref, g_ref, b_ref, xo_ref, xbo_ref, alpha)


def _channel_ln(body, lead, x, xb, p, w_pg, w_pp, g, b, alpha, tm, first_block=0, n_row_blocks=None, into=()):
    t = x.shape[0]
    n_row_blocks = t // tm if n_row_blocks is None else n_row_blocks
    rows = lambda a: pl.BlockSpec((tm, a.shape[1]), lambda i: (first_block + i, 0))
    local = lambda a: pl.BlockSpec((tm, a.shape[1]), lambda i: (i, 0))
    spec = {"local": local, "rows": rows, "full": _resident}
    operands = [a for a, _ in lead] + [x, xb, p, w_pg, w_pp, g, b]
    n_in = len(operands)

    def kernel_body(*refs):
        body(*refs[:n_in], *refs[n_in + len(into):], alpha=alpha)

    return pl.pallas_call(
        kernel_body,
        grid=(n_row_blocks,),
        in_specs=[spec[kind](a) for a, kind in lead]
        + [rows(x), rows(xb), rows(p), _resident(w_pg), _resident(w_pp), _resident(g), _resident(b)]
        + [pl.BlockSpec(memory_space=pl.ANY) for _ in into],
        out_specs=[rows(x), rows(xb)],
        out_shape=[jax.ShapeDtypeStruct((t, D_MODEL), F32), jax.ShapeDtypeStruct((t, D_MODEL), BF16)],
        input_output_aliases={n_in + k: k for k in range(len(into))},
        compiler_params=_cp(("parallel",)),
    )(*operands, *into)


def _router_kernel(x_ref, whi_ref, wlo_ref, o_ref):
    x = x_ref[...]
    x_hi = x.astype(BF16)
    x_lo = (x - x_hi.astype(F32)).astype(BF16)
    w_hi = whi_ref[...]
    logits = _dot(x_hi, w_hi) + (_dot(x_lo, w_hi) + _dot(x_hi, wlo_ref[...]))
    lane = lax.broadcasted_iota(jnp.int32, logits.shape, 1)
    lg = jnp.where(lane < N_EXPERTS, logits, -jnp.inf)
    v1 = jnp.max(lg, axis=-1, keepdims=True)
    i1 = jnp.min(jnp.where(lg == v1, lane, HEAD_LANES), axis=-1, keepdims=True)
    lg2 = jnp.where(lane == i1, -jnp.inf, lg)
    v2 = jnp.max(lg2, axis=-1, keepdims=True)
    i2 = jnp.min(jnp.where(lg2 == v2, lane, HEAD_LANES), axis=-1, keepdims=True)
    e2 = jnp.exp(v2 - v1)
    g1 = 1.0 / (1.0 + e2)
    g2 = e2 / (1.0 + e2)
    o_ref[...] = jnp.where(lane == 0, i1.astype(F32),
                           jnp.where(lane == 1, i2.astype(F32),
                                     jnp.where(lane == 2, g1, jnp.where(lane == 3, g2, 0.0))))


def _router(x, w_r, tm):
    t = x.shape[0]
    w_hi = w_r.astype(BF16)
    w_lo = (w_r - w_hi.astype(F32)).astype(BF16)
    return pl.pallas_call(
        _router_kernel,
        grid=(t // tm,),
        in_specs=[pl.BlockSpec((tm, D_MODEL), lambda i: (i, 0)), _resident(w_hi), _resident(w_lo)],
        out_specs=pl.BlockSpec((tm, HEAD_LANES), lambda i: (i, 0)),
        out_shape=jax.ShapeDtypeStruct((t, HEAD_LANES), F32),
        compiler_params=_cp(("parallel",)),
    )(x, w_hi, w_lo)


def _expert_changed(be_ref):
    i = pl.program_id(1)
    return jnp.logical_or(i == 0, be_ref[i] != be_ref[jnp.maximum(i - 1, 0)])


def _expert_up_kernel(be_ref, nb_ref, a_ref, w1_ref, w3_ref, *rest):
    o_ref, w1_scr, w3_scr = rest[-3:]

    @pl.when(_expert_changed(be_ref))
    def _():
        w1_scr[...] = w1_ref[0].astype(BF16)
        w3_scr[...] = w3_ref[0].astype(BF16)

    @pl.when(pl.program_id(1) < nb_ref[0])
    def _():
        a = a_ref[...]
        u = _dot(a, w1_scr[...])
        o_ref[...] = (u * _sigmoid(u) * _dot(a, w3_scr[...])).astype(BF16)

    @pl.when(pl.program_id(1) >= nb_ref[0])
    def _():
        o_ref[...] = jnp.zeros(o_ref.shape, BF16)


def _expert_up(a, w1, w3, layer, blk_e, n_used, tm, tn, total_rows, first_block, into):
    rows = a.shape[0]
    ff = w1.shape[3]
    wspec = pl.BlockSpec((None, 1, D_MODEL, tn), lambda j, i, be, nb: (layer, be[i], 0, j),
                         pipeline_mode=pl.Buffered(1))
    chained = [] if into is None else [into]
    grid_spec = pltpu.PrefetchScalarGridSpec(
        num_scalar_prefetch=2,
        grid=(ff // tn, rows // tm),
        in_specs=[pl.BlockSpec((tm, D_MODEL), lambda j, i, be, nb: (i, 0)), wspec, wspec]
        + [pl.BlockSpec(memory_space=pl.ANY) for _ in chained],
        out_specs=pl.BlockSpec((tm, tn), lambda j, i, be, nb: (first_block + i, j)),
        scratch_shapes=[pltpu.VMEM((D_MODEL, tn), BF16), pltpu.VMEM((D_MODEL, tn), BF16)],
    )
    return pl.pallas_call(
        _expert_up_kernel,
        grid_spec=grid_spec,
        out_shape=jax.ShapeDtypeStruct((total_rows, ff), BF16),
        input_output_aliases={5: 0} if chained else {},
        compiler_params=_cp(("arbitrary", "arbitrary")),
    )(blk_e, n_used, a, w1, w3, *chained)


def _expert_down_kernel(be_ref, nb_ref, h_ref, w2_ref, o_ref, w2_scr):
    @pl.when(_expert_changed(be_ref))
    def _():
        w2_scr[...] = w2_ref[0].astype(BF16)

    @pl.when(pl.program_id(1) < nb_ref[0])
    def _():
        o_ref[...] = _dot(h_ref[...], w2_scr[...]).astype(BF16)

    @pl.when(pl.program_id(1) >= nb_ref[0])
    def _():
        o_ref[...] = jnp.zeros(o_ref.shape, BF16)


def _expert_down(h, w2, layer, blk_e, n_used, tm, tn):
    rows, ff = h.shape
    grid_spec = pltpu.PrefetchScalarGridSpec(
        num_scalar_prefetch=2,
        grid=(D_MODEL // tn, rows // tm),
        in_specs=[pl.BlockSpec((tm, ff), lambda j, i, be, nb: (i, 0)),
                  pl.BlockSpec((None, 1, ff, tn), lambda j, i, be, nb: (layer, be[i], 0, j),
                               pipeline_mode=pl.Buffered(1))],
        out_specs=pl.BlockSpec((tm, tn), lambda j, i, be, nb: (i, j)),
        scratch_shapes=[pltpu.VMEM((ff, tn), BF16)],
    )
    return pl.pallas_call(
        _expert_down_kernel,
        grid_spec=grid_spec,
        out_shape=jax.ShapeDtypeStruct((rows, D_MODEL), BF16),
        compiler_params=_cp(("arbitrary", "arbitrary")),
    )(blk_e, n_used, h, w2)


def _take_rows(arr, rows):
    return arr.at[rows].get(mode="promise_in_bounds")


def _moe(x, xb, w_r, w1, w3, w2, layer, tm):
    t = x.shape[0]
    m = t * TOP_K
    gb = GROUP_ROWS
    routed = _router(x, w_r, tm)
    flat_e = routed[:, :TOP_K].astype(jnp.int32).reshape(-1)
    onehot = (flat_e[:, None] == jnp.arange(N_EXPERTS, dtype=jnp.int32)[None, :]).astype(jnp.int32)
    csum = jnp.cumsum(onehot, axis=0)
    counts = csum[-1]
    rank = jnp.sum(csum * onehot, axis=1) - 1
    padded = ((counts + gb - 1) // gb) * gb
    pend = jnp.cumsum(padded)
    pstart = pend - padded
    dest = jnp.sum(pstart[None, :] * onehot, axis=1) + rank
    n_blocks = m // gb + N_EXPERTS
    blk_e = jnp.minimum(jnp.sum((jnp.arange(n_blocks, dtype=jnp.int32)[:, None] * gb >= pend[None, :])
                                .astype(jnp.int32), axis=1), N_EXPERTS - 1)
    n_used = (pend[-1:] // gb).astype(jnp.int32)
    n_rows = n_blocks * gb
    gap = padded - counts
    gap_end = jnp.cumsum(gap)
    i_dummy = jnp.arange(n_rows - m, dtype=jnp.int32)
    in_gap = (i_dummy[:, None] >= (gap_end - gap)[None, :]) & (i_dummy[:, None] < gap_end[None, :])
    dummy_key = jnp.where(i_dummy < gap_end[-1],
                          jnp.sum(jnp.where(in_gap, (pstart + counts - (gap_end - gap))[None, :], 0), axis=1),
                          pend[-1] - gap_end[-1]) + i_dummy
    keys = jnp.concatenate([dest, dummy_key]).astype(jnp.uint32)
    toks = jnp.concatenate([jnp.arange(m, dtype=jnp.uint32) // TOP_K, jnp.zeros((n_rows - m,), jnp.uint32)])
    assert n_rows * t < 2 ** 32
    src_tok = (jnp.sort(keys * t + toks) % t).astype(jnp.int32)
    bounds = [n_blocks * f // 16 for f in DISPATCH_SPLIT]
    h = None
    for lo, hi in zip(bounds[:-1], bounds[1:]):
        a = _take_rows(xb, src_tok[lo * gb:hi * gb])
        h = _expert_up(a, w1, w3, layer, blk_e[lo:hi], jnp.clip(n_used - lo, 0, hi - lo),
                       gb, EXPERT_FF // 2, n_rows, lo, h)
    yb = _expert_down(h, w2, layer, blk_e, n_used, gb, D_MODEL)
    return yb, dest.reshape(t, TOP_K), routed


def _t5_bucket(dist):
    n = jnp.maximum(dist, 0)
    max_exact = REL_BUCKETS // 2
    large = max_exact + (jnp.log(jnp.maximum(n, 1).astype(F32) / max_exact)
                         / math.log(REL_MAX_DIST / max_exact) * (REL_BUCKETS - max_exact)).astype(jnp.int32)
    large = jnp.minimum(large, REL_BUCKETS - 1)
    return jnp.where(n < max_exact, n, large)


def _bias_tiles(table, tile):
    period = 3 * tile
    k = jnp.arange(period, dtype=jnp.int32)
    dist = jnp.where(k < 2 * tile, k, k - period)
    rel = table.astype(F32) - table[REL_BUCKETS - 1].astype(F32)[None, :]
    onehot = _t5_bucket(dist)[:, None] == jnp.arange(REL_BUCKETS, dtype=jnp.int32)[None, :]
    f = jnp.sum(jnp.where(onehot[:, :, None], rel[None], 0.0), axis=1)
    f = jnp.where((dist >= 0)[:, None], f, NEG)
    heads = table.shape[1]
    toep = jnp.tile(f.T, (1, tile))[:, :tile * (period - 1)].reshape(heads, tile, period - 1)
    tiles = jnp.stack([toep[:, :, :tile], toep[:, :, tile:2 * tile]], axis=1)
    return jnp.concatenate([tiles, tiles], axis=-1)


def _rope_tables(positions):
    half = MLA_ROPE_DIM // 2
    inv_freq = ROPE_THETA ** (-jnp.arange(half, dtype=F32) / half)
    ang = positions.astype(F32)[:, :, None] * inv_freq
    c, s = jnp.cos(ang), jnp.sin(ang)
    b, sq = positions.shape
    ones = jnp.ones((b, sq, MLA_NOPE_DIM), F32)
    z_nope = jnp.zeros((b, sq, MLA_NOPE_DIM), F32)
    z_pad = jnp.zeros((b, sq, HEAD_LANES - MLA_NOPE_DIM - MLA_ROPE_DIM), F32)
    cos_t = jnp.concatenate([ones, c, c, z_pad], axis=-1).reshape(b * sq, HEAD_LANES)
    sin_t = jnp.concatenate([z_nope, s, s, z_pad], axis=-1).reshape(b * sq, HEAD_LANES)
    return cos_t, sin_t


def _rot_cols(w):
    half = MLA_ROPE_DIM // 2
    return jnp.concatenate([-w[..., half:], w[..., :half]], axis=-1)


def _mixer_weights(w_in, w_uq, w_ukv):
    d = D_MODEL
    hq = DIFF_HEADS * 2 * DIFF_HEAD_DIM
    o = 0
    w_dq = w_in[:, o:o + hq]; o += hq
    w_dk = w_in[:, o:o + hq]; o += hq
    w_dv = w_in[:, o:o + hq]; o += hq
    w_cq = w_in[:, o:o + MLA_Q_RANK]; o += MLA_Q_RANK
    w_ckv = w_in[:, o:o + MLA_KV_RANK]; o += MLA_KV_RANK
    w_kr = w_in[:, o:o + MLA_ROPE_DIM]; o += MLA_ROPE_DIM
    w_g = w_in[:, o:]
    w_qkv = jnp.concatenate([w_dq, w_dk, w_dv], axis=1).astype(BF16)
    pad = HEAD_LANES - MLA_NOPE_DIM - MLA_ROPE_DIM
    z = lambda n: jnp.zeros((d, n), F32)
    w_c = jnp.concatenate([w_cq, w_ckv, z(MLA_NOPE_DIM), w_kr, z(pad), z(MLA_NOPE_DIM), _rot_cols(w_kr), z(pad)],
                          axis=1).astype(BF16)
    wq = w_uq.reshape(MLA_Q_RANK, MLA_HEADS, MLA_NOPE_DIM + MLA_ROPE_DIM)
    nope, rope = wq[..., :MLA_NOPE_DIM], wq[..., MLA_NOPE_DIM:]
    zq = lambda n: jnp.zeros((MLA_Q_RANK, MLA_HEADS, n), F32)
    plain = jnp.concatenate([nope, rope, zq(pad)], axis=-1).reshape(MLA_Q_RANK, -1)
    rot = jnp.concatenate([zq(MLA_NOPE_DIM), _rot_cols(rope), zq(pad)], axis=-1).reshape(MLA_Q_RANK, -1)
    w_q2 = jnp.concatenate([plain, rot], axis=1).astype(BF16)
    wkv = w_ukv.reshape(MLA_KV_RANK, MLA_HEADS, MLA_NOPE_DIM + MLA_V_DIM)
    zk = jnp.zeros((MLA_KV_RANK, MLA_HEADS, HEAD_LANES - MLA_NOPE_DIM), F32)
    w_k2 = jnp.concatenate([wkv[..., :MLA_NOPE_DIM], zk], axis=-1).reshape(MLA_KV_RANK, -1).astype(BF16)
    w_vt = wkv[..., MLA_NOPE_DIM:].reshape(MLA_KV_RANK, -1).T.astype(BF16)
    return w_qkv, w_c, w_g.astype(BF16), w_q2, w_k2, w_vt


def kernel(x, p, positions, rel_bias_table, w_in, b_gate, lambda_q1, lambda_k1, lambda_q2, lambda_k2,
           diff_subln_g, mla_q_norm_g, w_uq, mla_kv_norm_g, w_ukv, w_branch_diff, w_branch_mla, w_out,
           ln_mix_g, ln_mix_b, dense_w1, dense_w3, dense_w2, router_w, expert_w1, expert_w3, expert_w2,
           w_ple_gate, w_ple_proj, ln_ffn_g, ln_ffn_b):
    batch, seq, d = x.shape
    depth = w_in.shape[0]
    t = batch * seq
    tm = min(ROW_TILE, t)
    tm_mm = BIG_ROW_TILE if t % BIG_ROW_TILE == 0 else tm
    alpha = (2.0 * depth) ** 0.25
    row = lambda v: v.reshape(1, -1).astype(F32)

    cos_t, sin_t = _rope_tables(positions)
    bias = _bias_tiles(rel_bias_table * LOG2E, ATT_TILE)
    xf = x.reshape(t, d)
    xb = xf

    for i in range(depth):
        lam_init = 0.8 - 0.6 * math.exp(-0.3 * i)
        w_qkv, w_c, w_g, w_q2, w_k2, w_vt = _mixer_weights(w_in[i], w_uq[i], w_ukv[i])
        qkv = _matmul(xb, w_qkv, BF16, tm_mm, 1024, scaled_tiles=1, scale=DIFF_HEAD_DIM ** -0.5 * LOG2E)
        q_mla, k_mla, vt_mla = _latent(xb, w_c, w_q2, w_k2, w_vt, row(mla_q_norm_g[i]), row(mla_kv_norm_g[i]),
                                       cos_t, sin_t, tm)
        o_a = _diff_attention(qkv.reshape(batch, seq, -1), bias, row(lambda_q1[i]), row(lambda_k1[i]),
                              row(lambda_q2[i]), row(lambda_k2[i]), row(diff_subln_g[i]), lam_init, batch, seq)
        o_b = _mla_attention(q_mla.reshape(batch, seq, -1), k_mla.reshape(batch, seq, -1), vt_mla, batch, seq)
        xf, xb = _mix_ln(o_a.reshape(t, -1), o_b.reshape(t, -1), xf, w_g, row(b_gate[i]),
                         w_branch_diff[i].astype(BF16), w_branch_mla[i].astype(BF16), w_out[i].astype(BF16),
                         row(ln_mix_g[i]), row(ln_mix_b[i]), alpha, tm)
        j = i // 2
        p_i = p[i].reshape(t, PLE_DIM)
        w_pg = w_ple_gate[i].astype(BF16)
        w_pp = w_ple_proj[i].astype(BF16)
        ln_args = (xf, xb, p_i, w_pg, w_pp, row(ln_ffn_g[i]), row(ln_ffn_b[i]), alpha, tm)
        if i % 2 == 0:
            hmid = _swiglu_up(xb, dense_w1[j].astype(BF16), dense_w3[j].astype(BF16), tm_mm, DENSE_FF // 2)
            xf, xb = _channel_ln(_dense_ln_kernel, [(hmid, "rows"), (dense_w2[j].astype(BF16), "full")], *ln_args)
        else:
            w_r = jnp.concatenate([router_w[j], jnp.zeros((d, HEAD_LANES - N_EXPERTS), F32)], axis=1)
            yb, dest2, routed = _moe(xf, xb, w_r, expert_w1, expert_w3, expert_w2, j, tm)
            n_rb = t // tm
            cuts = [n_rb * c // COMBINE_CHUNKS for c in range(COMBINE_CHUNKS + 1)]
            outs = ()
            for lo, hi in zip(cuts[:-1], cuts[1:]):
                y1 = _take_rows(yb, dest2[lo * tm:hi * tm, 0])
                y2 = _take_rows(yb, dest2[lo * tm:hi * tm, 1])
                outs = _channel_ln(_moe_ln_kernel, [(y1, "local"), (y2, "local"), (routed, "rows")], *ln_args,
                                   first_block=lo, n_row_blocks=hi - lo, into=outs)
            xf, xb = outs
    return xf.reshape(batch, seq, d)
```

```python
import functools
import math

import jax
import jax.numpy as jnp
from jax import lax
from jax.experimental import pallas as pl
from jax.experimental.pallas import tpu as pltpu

F32 = jnp.float32
BF16 = jnp.bfloat16

D_MODEL = 1024
PLE_DIM = 256
DIFF_HEADS = 8
DIFF_HEAD_DIM = 64
MLA_HEADS = 8
MLA_Q_RANK = 384
MLA_KV_RANK = 256
MLA_NOPE_DIM = 64
MLA_ROPE_DIM = 32
MLA_V_DIM = 64
ROPE_THETA = 10000.0
REL_BUCKETS = 32
REL_MAX_DIST = 128
DENSE_FF = 2816
N_EXPERTS = 8
TOP_K = 2
EXPERT_FF = 3584
EPS = 1e-5

HEAD_LANES = 128
NEG = -1e30
LOG2E = math.log2(math.e)
ATT_TILE = 256
ATT_HEADS_PER_STEP = 4
ROW_TILE = 512
BIG_ROW_TILE = 1024
GROUP_ROWS = 512
DISPATCH_CHUNKS = 4
VMEM_LIMIT = 52 * 1024 * 1024


def _cp(sem):
    return pltpu.CompilerParams(dimension_semantics=sem, vmem_limit_bytes=VMEM_LIMIT)


def _dot(a, b):
    return jnp.dot(a, b, preferred_element_type=F32)


def _sigmoid(x):
    return 1.0 / (1.0 + jnp.exp(-x))


def _layernorm(r, g, b):
    mu = jnp.mean(r, axis=-1, keepdims=True)
    d = r - mu
    var = jnp.mean(d * d, axis=-1, keepdims=True)
    return d * lax.rsqrt(var + EPS) * g + b


def _rmsnorm(x, g):
    return x * lax.rsqrt(jnp.mean(x * x, axis=-1, keepdims=True) + EPS) * g


def _mm_kernel(a_ref, w_ref, o_ref, *, scaled_tiles, scale):
    acc = _dot(a_ref[...].astype(BF16), w_ref[...])
    if scaled_tiles:
        acc = acc * jnp.where(pl.program_id(1) < scaled_tiles, scale, 1.0)
    o_ref[...] = acc.astype(o_ref.dtype)


def _matmul(a, w, out_dtype, tm, tn, scaled_tiles=0, scale=1.0):
    m, k = a.shape
    n = w.shape[1]
    return pl.pallas_call(
        functools.partial(_mm_kernel, scaled_tiles=scaled_tiles, scale=scale),
        grid=(m // tm, n // tn),
        in_specs=[pl.BlockSpec((tm, k), lambda i, j: (i, 0)),
                  pl.BlockSpec((k, tn), lambda i, j: (0, j))],
        out_specs=pl.BlockSpec((tm, tn), lambda i, j: (i, j)),
        out_shape=jax.ShapeDtypeStruct((m, n), out_dtype),
        compiler_params=_cp(("parallel", "arbitrary")),
    )(a, w)


C_COLS = MLA_Q_RANK + MLA_KV_RANK + 2 * HEAD_LANES


def _resident(a):
    return pl.BlockSpec(a.shape, lambda *_: (0,) * a.ndim, pipeline_mode=pl.Buffered(1))


def _latent_kernel(a_ref, wc_ref, wq_ref, wk_ref, wvt_ref, gq_ref, gkv_ref, cos_ref, sin_ref, q_ref, k_ref, vt_ref, *,
                   scale):
    z = _dot(a_ref[...].astype(BF16), wc_ref[...])
    cq = z[:, :MLA_Q_RANK]
    ckv = z[:, MLA_Q_RANK:MLA_Q_RANK + MLA_KV_RANK]
    kr = z[:, MLA_Q_RANK + MLA_KV_RANK:MLA_Q_RANK + MLA_KV_RANK + HEAD_LANES]
    kr_rot = z[:, MLA_Q_RANK + MLA_KV_RANK + HEAD_LANES:]
    cos, sin = cos_ref[...], sin_ref[...]
    k_rope = kr * cos + kr_rot * sin
    ckv_n = _rmsnorm(ckv, gkv_ref[...]).astype(BF16)
    zk = _dot(ckv_n, wk_ref[...])
    vt_ref[...] = lax.dot_general(wvt_ref[...], ckv_n, (((1,), (1,)), ((), ())),
                                  preferred_element_type=F32).astype(BF16)
    zq = _dot(_rmsnorm(cq, gq_ref[...]).astype(BF16), wq_ref[...])
    c = cos * scale
    s = sin * scale
    hl = HEAD_LANES
    for h in range(MLA_HEADS):
        k_ref[:, h * hl:(h + 1) * hl] = (zk[:, h * hl:(h + 1) * hl] + k_rope).astype(BF16)
        q_ref[:, h * hl:(h + 1) * hl] = (
            zq[:, h * hl:(h + 1) * hl] * c
            + zq[:, (MLA_HEADS + h) * hl:(MLA_HEADS + h + 1) * hl] * s).astype(BF16)


def _latent(xb, w_c, w_q2, w_k2, w_vt, gq, gkv, cos_t, sin_t, tm):
    t = xb.shape[0]
    row = lambda n: pl.BlockSpec((tm, n), lambda i: (i, 0))
    nq = MLA_HEADS * HEAD_LANES
    nv = MLA_HEADS * MLA_V_DIM
    scale = (MLA_NOPE_DIM + MLA_ROPE_DIM) ** -0.5 * LOG2E
    return pl.pallas_call(
        functools.partial(_latent_kernel, scale=scale),
        grid=(t // tm,),
        in_specs=[row(D_MODEL), _resident(w_c), _resident(w_q2), _resident(w_k2), _resident(w_vt), _resident(gq),
                  _resident(gkv), row(HEAD_LANES), row(HEAD_LANES)],
        out_specs=[row(nq), row(nq), pl.BlockSpec((nv, tm), lambda i: (0, i))],
        out_shape=[jax.ShapeDtypeStruct((t, nq), BF16),
                   jax.ShapeDtypeStruct((t, nq), BF16),
                   jax.ShapeDtypeStruct((nv, t), BF16)],
        compiler_params=_cp(("parallel",)),
    )(xb, w_c, w_q2, w_k2, w_vt, gq, gkv, cos_t, sin_t)


ONES_ROWS = 16


def _scores_t(k, q):
    return lax.dot_general(k, q, (((1,), (1,)), ((), ())), preferred_element_type=F32)


def _softmax_step_t(s, vt, m_ref, acc_ref):
    m_old = m_ref[...]
    m_new = jnp.maximum(m_old, jnp.max(s, axis=0, keepdims=True))
    p = jnp.exp2(s - m_new).astype(BF16)
    alpha = jnp.exp2(m_old - m_new)
    acc_ref[...] = alpha * acc_ref[...] + _dot(vt, p)
    m_ref[...] = m_new


def _fill_vt(v_ref, vt_scr, lane0, row0, v_rows, tile):
    for j in range(vt_scr.shape[0]):
        vt = v_ref[0, j * tile:(j + 1) * tile, lane0:lane0 + HEAD_LANES].astype(F32).T
        vt_scr[j, 0:v_rows, :] = vt[row0:row0 + v_rows].astype(BF16)
        vt_scr[j, v_rows:v_rows + ONES_ROWS, :] = jnp.ones((ONES_ROWS, tile), BF16)


def _normalised(acc, v_rows):
    return acc[0:v_rows] * (1.0 / acc[v_rows:v_rows + 1])


def _diff_attn_kernel(q_ref, k_ref, v_ref, bias_ref, lq1_ref, lk1_ref, lq2_ref, lk2_ref, g_ref, o_ref,
                      vt_scr, qcat_scr, s_scr, m_scr, acc_scr, *, lam_init, tile, heads):
    qi = pl.program_id(2)
    n_q = pl.num_programs(2)
    hd = 2 * DIFF_HEAD_DIM
    hl = HEAD_LANES
    cur = qi % 2

    def first_scores(q_tile, slot):
        rows = pl.ds(pl.multiple_of(q_tile * tile, tile), tile)
        for h in range(heads):
            q = q_ref[0, rows, h * hl:(h + 1) * hl]
            lane = lax.broadcasted_iota(jnp.int32, q.shape, 1)
            zero = jnp.zeros_like(q)
            qcat_scr[slot, h] = jnp.concatenate([jnp.where(lane < DIFF_HEAD_DIM, q, zero),
                                                 jnp.where(lane >= DIFF_HEAD_DIM, q, zero)], axis=0)
            s_scr[0, h] = _scores_t(k_ref[0, 0:tile, h * hl:(h + 1) * hl], qcat_scr[slot, h])

    @pl.when(qi == 0)
    def _():
        for h in range(heads):
            _fill_vt(v_ref, vt_scr.at[h], h * hl, 0, hd, tile)
        first_scores(0, 0)

    m_scr[...] = jnp.full(m_scr.shape, NEG, F32)
    acc_scr[...] = jnp.zeros(acc_scr.shape, F32)

    def scores(slot, kj):
        rows = pl.ds(pl.multiple_of(kj * tile, tile), tile)
        for h in range(heads):
            s_scr[slot, h] = _scores_t(k_ref[0, rows, h * hl:(h + 1) * hl], qcat_scr[cur, h])

    def update(slot, kj, bias_idx=None):
        for h in range(heads):
            s = s_scr[slot, h]
            if bias_idx is not None:
                s = s + bias_ref[h, bias_idx]
            _softmax_step_t(s, vt_scr[h, kj], m_scr.at[h], acc_scr.at[h])

    n_far = jnp.maximum(qi - 1, 0)
    quads = n_far // 4

    def pair_step(a):
        scores(1, a + 1)
        update(0, a)
        scores(0, a + 2)
        update(1, a + 1)

    def quad_body(j4, carry):
        pair_step(4 * j4)
        pair_step(4 * j4 + 2)
        return carry

    lax.fori_loop(0, quads, quad_body, 0)
    odd_pair = (n_far - 4 * quads) >= 2

    @pl.when(odd_pair)
    def _():
        pair_step(4 * quads)

    t0 = 4 * quads + jnp.where(odd_pair, 2, 0)

    @pl.when(n_far - t0 == 1)
    def _():
        scores(1, t0 + 1)
        update(0, t0)
        scores(0, t0 + 2)
        update(1, t0 + 1, 1)
        update(0, t0 + 2, 0)

    @pl.when(jnp.logical_and(n_far == t0, qi >= 1))
    def _():
        scores(1, t0 + 1)
        update(0, t0, 1)
        update(1, t0 + 1, 0)

    @pl.when(qi == 0)
    def _():
        update(0, 0, 0)

    first_scores(jnp.minimum(qi + 1, n_q - 1), 1 - cur)

    lam = (jnp.exp(jnp.sum(lq1_ref[...] * lk1_ref[...], axis=-1, keepdims=True))
           - jnp.exp(jnp.sum(lq2_ref[...] * lk2_ref[...], axis=-1, keepdims=True)) + lam_init)
    for h in range(heads):
        acc = acc_scr[h]
        o = _normalised(acc[:, :tile], hd) - lam * _normalised(acc[:, tile:], hd)
        o = o * lax.rsqrt(jnp.mean(o * o, axis=0, keepdims=True) + EPS)
        o_ref[0, :, h * hl:(h + 1) * hl] = (o.T * (g_ref[...] * (1.0 - lam_init))).astype(BF16)


def _diff_attention(qkv, bias, lq1, lk1, lq2, lk2, g, lam_init, batch, seq):
    tile = ATT_TILE
    hl = HEAD_LANES
    nh = DIFF_HEADS
    hps = ATT_HEADS_PER_STEP
    ng = nh // hps
    vec = lambda a: pl.BlockSpec(a.shape, lambda b, h, i: (0, 0))
    return pl.pallas_call(
        functools.partial(_diff_attn_kernel, lam_init=lam_init, tile=tile, heads=hps),
        grid=(batch, ng, seq // tile),
        in_specs=[pl.BlockSpec((1, seq, hps * hl), lambda b, h, i: (b, 0, h)),
                  pl.BlockSpec((1, seq, hps * hl), lambda b, h, i: (b, 0, ng + h)),
                  pl.BlockSpec((1, seq, hps * hl), lambda b, h, i: (b, 0, 2 * ng + h)),
                  pl.BlockSpec((hps, 2, tile, 2 * tile), lambda b, h, i: (h, 0, 0, 0)),
                  vec(lq1), vec(lk1), vec(lq2), vec(lk2), vec(g)],
        out_specs=pl.BlockSpec((1, tile, hps * hl), lambda b, h, i: (b, i, h)),
        out_shape=jax.ShapeDtypeStruct((batch, seq, nh * hl), BF16),
        scratch_shapes=[pltpu.VMEM((hps, seq // tile, hl + ONES_ROWS, tile), BF16),
                        pltpu.VMEM((2, hps, 2 * tile, hl), BF16),
                        pltpu.VMEM((2, hps, tile, 2 * tile), F32),
                        pltpu.VMEM((hps, 1, 2 * tile), F32),
                        pltpu.VMEM((hps, hl + ONES_ROWS, 2 * tile), F32)],
        compiler_params=_cp(("parallel", "parallel", "arbitrary")),
    )(qkv, qkv, qkv, bias, lq1, lk1, lq2, lk2, g)


MLA_STEP_HEADS = 4


def _mla_attn_kernel(q_ref, k_ref, vt_ref, o_ref, vt_scr, s_scr, m_scr, acc_scr, *, tile):
    qi = pl.program_id(2)
    n_q = pl.num_programs(2)
    hl = HEAD_LANES
    vd = MLA_V_DIM
    tq = 2 * tile

    def scores(slot, kj, q_tile):
        rows = pl.ds(pl.multiple_of(kj * tile, tile), tile)
        q_rows = pl.ds(pl.multiple_of(q_tile * tq, tq), tq)
        for h in range(MLA_STEP_HEADS):
            s_scr[slot, h] = _scores_t(k_ref[0, rows, h * hl:(h + 1) * hl], q_ref[0, q_rows, h * hl:(h + 1) * hl])

    @pl.when(qi == 0)
    def _():
        for h in range(MLA_STEP_HEADS):
            for j in range(vt_scr.shape[1]):
                vt_scr[h, j, 0:vd, :] = vt_ref[h * vd:(h + 1) * vd, j * tile:(j + 1) * tile]
                vt_scr[h, j, vd:vd + ONES_ROWS, :] = jnp.ones((ONES_ROWS, tile), BF16)
        scores(0, 0, 0)

    m_scr[...] = jnp.full(m_scr.shape, NEG, F32)
    acc_scr[...] = jnp.zeros(acc_scr.shape, F32)

    def update(slot, kj, key_offset=None):
        for h in range(MLA_STEP_HEADS):
            s = s_scr[slot, h]
            if key_offset is not None:
                key = lax.broadcasted_iota(jnp.int32, s.shape, 0)
                qry = lax.broadcasted_iota(jnp.int32, s.shape, 1)
                s = jnp.where(key + key_offset <= qry, s, NEG)
            _softmax_step_t(s, vt_scr[h, kj], m_scr.at[h], acc_scr.at[h])

    def pair_step(a):
        scores(1, a + 1, qi)
        update(0, a)
        scores(0, a + 2, qi)
        update(1, a + 1)

    def quad_body(j4, carry):
        pair_step(4 * j4)
        pair_step(4 * j4 + 2)
        return carry

    lax.fori_loop(0, qi // 2, quad_body, 0)

    @pl.when(qi % 2 == 1)
    def _():
        pair_step(2 * qi - 2)

    t0 = 2 * qi
    scores(1, t0 + 1, qi)
    update(0, t0, 0)
    update(1, t0 + 1, tile)
    scores(0, 0, jnp.minimum(qi + 1, n_q - 1))
    o = jnp.concatenate([_normalised(acc_scr[h], MLA_V_DIM) for h in range(MLA_STEP_HEADS)], axis=0)
    o_ref[0] = o.T.astype(BF16)


def _mla_attention(q, k, vt, batch, seq):
    tile = ATT_TILE
    tq = 2 * tile
    hl = HEAD_LANES
    nh = MLA_STEP_HEADS
    ng = MLA_HEADS // nh
    v_lanes = nh * MLA_V_DIM
    v_rows = MLA_V_DIM + ONES_ROWS
    return pl.pallas_call(
        functools.partial(_mla_attn_kernel, tile=tile),
        grid=(batch, ng, seq // tq),
        in_specs=[pl.BlockSpec((1, seq, nh * hl), lambda b, h, i: (b, 0, h)),
                  pl.BlockSpec((1, seq, nh * hl), lambda b, h, i: (b, 0, h)),
                  pl.BlockSpec((v_lanes, seq), lambda b, h, i: (h, b))],
        out_specs=pl.BlockSpec((1, tq, v_lanes), lambda b, h, i: (b, i, h)),
        out_shape=jax.ShapeDtypeStruct((batch, seq, MLA_HEADS * MLA_V_DIM), BF16),
        scratch_shapes=[pltpu.VMEM((nh, seq // tile, v_rows, tile), BF16),
                        pltpu.VMEM((2, nh, tile, tq), F32),
                        pltpu.VMEM((nh, 1, tq), F32),
                        pltpu.VMEM((nh, v_rows, tq), F32)],
        compiler_params=_cp(("parallel", "parallel", "arbitrary")),
    )(q, k, vt)


def _mix_ln_kernel(oa_ref, ob_ref, x_ref, wg_ref, bg_ref, wbd_ref, wbm_ref, wo_ref, g_ref, b_ref,
                   xo_ref, xbo_ref, *, alpha):
    d = D_MODEL
    x = x_ref[...]
    xb = x.astype(BF16)
    g_a = _sigmoid(_dot(xb, wg_ref[:, :d]) + bg_ref[:, :d])
    m = g_a * _dot(oa_ref[...], wbd_ref[...])
    g_b = _sigmoid(_dot(xb, wg_ref[:, d:]) + bg_ref[:, d:])
    m = (m + g_b * _dot(ob_ref[...], wbm_ref[...])).astype(BF16)
    y = _layernorm(alpha * x + _dot(m, wo_ref[...]), g_ref[...], b_ref[...])
    xo_ref[...] = y
    xbo_ref[...] = y.astype(BF16)


def _mix_ln(o_a, o_b, x, w_g, b_g, w_bd, w_bm, w_out, g, b, alpha, tm):
    t = x.shape[0]
    row = lambda a: pl.BlockSpec((tm, a.shape[1]), lambda i: (i, 0))
    return pl.pallas_call(
        functools.partial(_mix_ln_kernel, alpha=alpha),
        grid=(t // tm,),
        in_specs=[row(o_a), row(o_b), row(x), _resident(w_g), _resident(b_g), _resident(w_bd),
                  _resident(w_bm), _resident(w_out), _resident(g), _resident(b)],
        out_specs=[row(x), row(o_a)],
        out_shape=[jax.ShapeDtypeStruct((t, D_MODEL), F32), jax.ShapeDtypeStruct((t, D_MODEL), BF16)],
        compiler_params=_cp(("parallel",)),
    )(o_a, o_b, x, w_g, b_g, w_bd, w_bm, w_out, g, b)


def _swiglu_up_kernel(a_ref, w1_ref, w3_ref, o_ref):
    a = a_ref[...]
    u = _dot(a, w1_ref[...])
    o_ref[...] = (u * _sigmoid(u) * _dot(a, w3_ref[...])).astype(BF16)


def _swiglu_up(xb, w1, w3, tm, tn):
    t = xb.shape[0]
    ff = w1.shape[1]
    return pl.pallas_call(
        _swiglu_up_kernel,
        grid=(t // tm, ff // tn),
        in_specs=[pl.BlockSpec((tm, D_MODEL), lambda i, j: (i, 0)),
                  pl.BlockSpec((D_MODEL, tn), lambda i, j: (0, j)),
                  pl.BlockSpec((D_MODEL, tn), lambda i, j: (0, j))],
        out_specs=pl.BlockSpec((tm, tn), lambda i, j: (i, j)),
        out_shape=jax.ShapeDtypeStruct((t, ff), BF16),
        compiler_params=_cp(("parallel", "arbitrary")),
    )(xb, w1, w3)


def _ple_ln(f, x_ref, xb_ref, p_ref, wpg_ref, wpp_ref, g_ref, b_ref, xo_ref, xbo_ref, alpha):
    e = _sigmoid(_dot(xb_ref[...], wpg_ref[...])) * _dot(p_ref[...].astype(BF16), wpp_ref[...])
    y = _layernorm(alpha * x_ref[...] + f + e, g_ref[...], b_ref[...])
    xo_ref[...] = y
    xbo_ref[...] = y.astype(BF16)


def _dense_ln_kernel(h_ref, w2_ref, x_ref, xb_ref, p_ref, wpg_ref, wpp_ref, g_ref, b_ref, xo_ref, xbo_ref, *, alpha):
    _ple_ln(_dot(h_ref[...], w2_ref[...]), x_ref, xb_ref, p_ref, wpg_ref, wpp_ref, g_ref, b_ref, xo_ref, xbo_ref,
            alpha)


def _moe_ln_kernel(y1_ref, y2_ref, gates_ref, x_ref, xb_ref, p_ref, wpg_ref, wpp_ref, g_ref, b_ref, xo_ref, xbo_ref,
                   *, alpha):
    gates = gates_ref[...]
    f = (y1_ref[...].astype(F32) * gates[:, TOP_K:TOP_K + 1]
         + y2_ref[...].astype(F32) * gates[:, TOP_K + 1:TOP_K + 2])
    _ple_ln(f, x_ref, xb_ref, p_ref, wpg_ref, wpp_ref, g_ref, b_ref, xo_ref, xbo_ref, alpha)


def _channel_ln(body, lead, x, xb, p, layer, w_pg, w_pp, g, b, alpha, tm):
    t = x.shape[0]
    row = lambda a: pl.BlockSpec((tm, a.shape[1]), lambda i: (i, 0))
    full = _resident
    tail = [x, xb, p, w_pg, w_pp, g, b]
    return pl.pallas_call(
        functools.partial(body, alpha=alpha),
        grid=(t // tm,),
        in_specs=[row(a) if tiled else full(a) for a, tiled in lead]
        + [row(x), row(xb), pl.BlockSpec((None, tm, p.shape[2]), lambda i: (layer, i, 0)),
           full(w_pg), full(w_pp), full(g), full(b)],
        out_specs=[row(x), row(xb)],
        out_shape=[jax.ShapeDtypeStruct((t, D_MODEL), F32), jax.ShapeDtypeStruct((t, D_MODEL), BF16)],
        compiler_params=_cp(("parallel",)),
    )(*[a for a, _ in lead], *tail)


def _router_kernel(x_ref, whi_ref, wlo_ref, o_ref):
    x = x_ref[...]
    x_hi = x.astype(BF16)
    x_lo = (x - x_hi.astype(F32)).astype(BF16)
    w_hi = whi_ref[...]
    logits = _dot(x_hi, w_hi) + (_dot(x_lo, w_hi) + _dot(x_hi, wlo_ref[...]))
    lane = lax.broadcasted_iota(jnp.int32, logits.shape, 1)
    lg = jnp.where(lane < N_EXPERTS, logits, -jnp.inf)
    v1 = jnp.max(lg, axis=-1, keepdims=True)
    i1 = jnp.min(jnp.where(lg == v1, lane, HEAD_LANES), axis=-1, keepdims=True)
    lg2 = jnp.where(lane == i1, -jnp.inf, lg)
    v2 = jnp.max(lg2, axis=-1, keepdims=True)
    i2 = jnp.min(jnp.where(lg2 == v2, lane, HEAD_LANES), axis=-1, keepdims=True)
    e2 = jnp.exp(v2 - v1)
    g1 = 1.0 / (1.0 + e2)
    g2 = e2 / (1.0 + e2)
    o_ref[...] = jnp.where(lane == 0, i1.astype(F32),
                           jnp.where(lane == 1, i2.astype(F32),
                                     jnp.where(lane == 2, g1, jnp.where(lane == 3, g2, 0.0))))


def _router(x, w_r, tm):
    t = x.shape[0]
    w_hi = w_r.astype(BF16)
    w_lo = (w_r - w_hi.astype(F32)).astype(BF16)
    return pl.pallas_call(
        _router_kernel,
        grid=(t // tm,),
        in_specs=[pl.BlockSpec((tm, D_MODEL), lambda i: (i, 0)), _resident(w_hi), _resident(w_lo)],
        out_specs=pl.BlockSpec((tm, HEAD_LANES), lambda i: (i, 0)),
        out_shape=jax.ShapeDtypeStruct((t, HEAD_LANES), F32),
        compiler_params=_cp(("parallel",)),
    )(x, w_hi, w_lo)


def _expert_changed(be_ref):
    i = pl.program_id(1)
    return jnp.logical_or(i == 0, be_ref[i] != be_ref[jnp.maximum(i - 1, 0)])


def _expert_up_kernel(be_ref, nb_ref, a_ref, w1_ref, w3_ref, *rest):
    o_ref, w1_scr, w3_scr = rest[-3:]

    @pl.when(_expert_changed(be_ref))
    def _():
        w1_scr[...] = w1_ref[0].astype(BF16)
        w3_scr[...] = w3_ref[0].astype(BF16)

    @pl.when(pl.program_id(1) < nb_ref[0])
    def _():
        a = a_ref[...]
        u = _dot(a, w1_scr[...])
        o_ref[...] = (u * _sigmoid(u) * _dot(a, w3_scr[...])).astype(BF16)

    @pl.when(pl.program_id(1) >= nb_ref[0])
    def _():
        o_ref[...] = jnp.zeros(o_ref.shape, BF16)


def _expert_up(a, w1, w3, layer, blk_e, n_used, tm, tn, total_rows, first_block, into):
    rows = a.shape[0]
    ff = w1.shape[3]
    wspec = pl.BlockSpec((None, 1, D_MODEL, tn), lambda j, i, be, nb: (layer, be[i], 0, j),
                         pipeline_mode=pl.Buffered(1))
    chained = [] if into is None else [into]
    grid_spec = pltpu.PrefetchScalarGridSpec(
        num_scalar_prefetch=2,
        grid=(ff // tn, rows // tm),
        in_specs=[pl.BlockSpec((tm, D_MODEL), lambda j, i, be, nb: (i, 0)), wspec, wspec]
        + [pl.BlockSpec(memory_space=pl.ANY) for _ in chained],
        out_specs=pl.BlockSpec((tm, tn), lambda j, i, be, nb: (first_block + i, j)),
        scratch_shapes=[pltpu.VMEM((D_MODEL, tn), BF16), pltpu.VMEM((D_MODEL, tn), BF16)],
    )
    return pl.pallas_call(
        _expert_up_kernel,
        grid_spec=grid_spec,
        out_shape=jax.ShapeDtypeStruct((total_rows, ff), BF16),
        input_output_aliases={5: 0} if chained else {},
        compiler_params=_cp(("arbitrary", "arbitrary")),
    )(blk_e, n_used, a, w1, w3, *chained)


def _expert_down_kernel(be_ref, nb_ref, h_ref, w2_ref, o_ref, w2_scr):
    @pl.when(_expert_changed(be_ref))
    def _():
        w2_scr[...] = w2_ref[0].astype(BF16)

    @pl.when(pl.program_id(1) < nb_ref[0])
    def _():
        o_ref[...] = _dot(h_ref[...], w2_scr[...]).astype(BF16)

    @pl.when(pl.program_id(1) >= nb_ref[0])
    def _():
        o_ref[...] = jnp.zeros(o_ref.shape, BF16)


def _expert_down(h, w2, layer, blk_e, n_used, tm, tn):
    rows, ff = h.shape
    grid_spec = pltpu.PrefetchScalarGridSpec(
        num_scalar_prefetch=2,
        grid=(D_MODEL // tn, rows // tm),
        in_specs=[pl.BlockSpec((tm, ff), lambda j, i, be, nb: (i, 0)),
                  pl.BlockSpec((None, 1, ff, tn), lambda j, i, be, nb: (layer, be[i], 0, j),
                               pipeline_mode=pl.Buffered(1))],
        out_specs=pl.BlockSpec((tm, tn), lambda j, i, be, nb: (i, j)),
        scratch_shapes=[pltpu.VMEM((ff, tn), BF16)],
    )
    return pl.pallas_call(
        _expert_down_kernel,
        grid_spec=grid_spec,
        out_shape=jax.ShapeDtypeStruct((rows, D_MODEL), BF16),
        compiler_params=_cp(("arbitrary", "arbitrary")),
    )(blk_e, n_used, h, w2)


def _moe(x, xb, w_r, w1, w3, w2, layer, tm):
    t = x.shape[0]
    m = t * TOP_K
    gb = GROUP_ROWS
    routed = _router(x, w_r, tm)
    experts = jnp.arange(N_EXPERTS, dtype=jnp.int32)[None, :]
    hot = [(routed[:, k].astype(jnp.int32)[:, None] == experts).astype(jnp.int32) for k in range(TOP_K)]
    both = sum(hot)
    before = jnp.cumsum(both, axis=0) - both
    counts = jnp.sum(both, axis=0)
    padded = ((counts + gb - 1) // gb) * gb
    pend = jnp.cumsum(padded)
    pstart = pend - padded
    dest = [jnp.sum((pstart[None, :] + before) * h, axis=1) for h in hot]
    n_blocks = m // gb + N_EXPERTS
    blk_e = jnp.minimum(jnp.sum((jnp.arange(n_blocks, dtype=jnp.int32)[:, None] * gb >= pend[None, :])
                                .astype(jnp.int32), axis=1), N_EXPERTS - 1)
    n_used = (pend[-1:] // gb).astype(jnp.int32)
    n_rows = n_blocks * gb
    gap = padded - counts
    gap_end = jnp.cumsum(gap)
    i_dummy = jnp.arange(n_rows - m, dtype=jnp.int32)
    in_gap = (i_dummy[:, None] >= (gap_end - gap)[None, :]) & (i_dummy[:, None] < gap_end[None, :])
    dummy_key = jnp.where(i_dummy < gap_end[-1],
                          jnp.sum(jnp.where(in_gap, (pstart + counts - (gap_end - gap))[None, :], 0), axis=1),
                          pend[-1] - gap_end[-1]) + i_dummy
    keys = jnp.concatenate(dest + [dummy_key]).astype(jnp.uint32)
    toks = jnp.concatenate([jnp.arange(t, dtype=jnp.uint32)] * TOP_K + [jnp.zeros((n_rows - m,), jnp.uint32)])
    assert n_rows * t < 2 ** 32
    src_tok = (jnp.sort(keys * t + toks) % t).astype(jnp.int32)
    take = lambda arr, rows: arr.at[rows].get(mode="promise_in_bounds")
    nbc = n_blocks // DISPATCH_CHUNKS
    h = None
    for c in range(DISPATCH_CHUNKS):
        a = take(xb, src_tok[c * nbc * gb:(c + 1) * nbc * gb])
        h = _expert_up(a, w1, w3, layer, blk_e[c * nbc:(c + 1) * nbc], jnp.clip(n_used - c * nbc, 0, nbc),
                       gb, EXPERT_FF // 2, n_rows, c * nbc, h)
    yb = _expert_down(h, w2, layer, blk_e, n_used, gb, D_MODEL)
    return take(yb, dest[0]), take(yb, dest[1]), routed


def _t5_bucket(dist):
    n = jnp.maximum(dist, 0)
    max_exact = REL_BUCKETS // 2
    large = max_exact + (jnp.log(jnp.maximum(n, 1).astype(F32) / max_exact)
                         / math.log(REL_MAX_DIST / max_exact) * (REL_BUCKETS - max_exact)).astype(jnp.int32)
    large = jnp.minimum(large, REL_BUCKETS - 1)
    return jnp.where(n < max_exact, n, large)


def _bias_tiles(table, tile):
    period = 3 * tile
    k = jnp.arange(period, dtype=jnp.int32)
    dist = jnp.where(k < 2 * tile, k, k - period)
    rel = table.astype(F32) - table[REL_BUCKETS - 1].astype(F32)[None, :]
    onehot = _t5_bucket(dist)[:, None] == jnp.arange(REL_BUCKETS, dtype=jnp.int32)[None, :]
    f = jnp.sum(jnp.where(onehot[:, :, None], rel[None], 0.0), axis=1)
    f = jnp.where((dist >= 0)[:, None], f, NEG)
    heads = table.shape[1]
    toep = jnp.tile(f.T, (1, tile))[:, :tile * (period - 1)].reshape(heads, tile, period - 1)
    tiles = jnp.stack([toep[:, :, :tile], toep[:, :, tile:2 * tile]], axis=1)
    return jnp.concatenate([tiles, tiles], axis=-1)


def _rope_tables(positions):
    half = MLA_ROPE_DIM // 2
    b, sq = positions.shape
    inv_freq = ROPE_THETA ** (-jnp.arange(half, dtype=F32) / half)
    per_row = HEAD_LANES // half
    pos = positions.astype(F32).reshape(b * sq // per_row, per_row, 1)
    ang = (pos * inv_freq.reshape(1, 1, half)).reshape(b * sq // per_row, HEAD_LANES)
    c = jnp.cos(ang).reshape(b, sq, half)
    s = jnp.sin(ang).reshape(b, sq, half)
    ones = jnp.ones((b, sq, MLA_NOPE_DIM), F32)
    z_nope = jnp.zeros((b, sq, MLA_NOPE_DIM), F32)
    z_pad = jnp.zeros((b, sq, HEAD_LANES - MLA_NOPE_DIM - MLA_ROPE_DIM), F32)
    cos_t = jnp.concatenate([ones, c, c, z_pad], axis=-1).reshape(b * sq, HEAD_LANES)
    sin_t = jnp.concatenate([z_nope, s, s, z_pad], axis=-1).reshape(b * sq, HEAD_LANES)
    return cos_t, sin_t


def _rot_cols(w):
    half = MLA_ROPE_DIM // 2
    return jnp.concatenate([-w[..., half:], w[..., :half]], axis=-1)


def _mixer_weights(w_in, w_uq, w_ukv):
    d = D_MODEL
    hq = DIFF_HEADS * 2 * DIFF_HEAD_DIM
    o = 0
    w_dq = w_in[:, o:o + hq]; o += hq
    w_dk = w_in[:, o:o + hq]; o += hq
    w_dv = w_in[:, o:o + hq]; o += hq
    w_cq = w_in[:, o:o + MLA_Q_RANK]; o += MLA_Q_RANK
    w_ckv = w_in[:, o:o + MLA_KV_RANK]; o += MLA_KV_RANK
    w_kr = w_in[:, o:o + MLA_ROPE_DIM]; o += MLA_ROPE_DIM
    w_g = w_in[:, o:]
    w_qkv = jnp.concatenate([w_dq, w_dk, w_dv], axis=1).astype(BF16)
    pad = HEAD_LANES - MLA_NOPE_DIM - MLA_ROPE_DIM
    z = lambda n: jnp.zeros((d, n), F32)
    w_c = jnp.concatenate([w_cq, w_ckv, z(MLA_NOPE_DIM), w_kr, z(pad), z(MLA_NOPE_DIM), _rot_cols(w_kr), z(pad)],
                          axis=1).astype(BF16)
    wq = w_uq.reshape(MLA_Q_RANK, MLA_HEADS, MLA_NOPE_DIM + MLA_ROPE_DIM)
    nope, rope = wq[..., :MLA_NOPE_DIM], wq[..., MLA_NOPE_DIM:]
    zq = lambda n: jnp.zeros((MLA_Q_RANK, MLA_HEADS, n), F32)
    plain = jnp.concatenate([nope, rope, zq(pad)], axis=-1).reshape(MLA_Q_RANK, -1)
    rot = jnp.concatenate([zq(MLA_NOPE_DIM), _rot_cols(rope), zq(pad)], axis=-1).reshape(MLA_Q_RANK, -1)
    w_q2 = jnp.concatenate([plain, rot], axis=1).astype(BF16)
    wkv = w_ukv.reshape(MLA_KV_RANK, MLA_HEADS, MLA_NOPE_DIM + MLA_V_DIM)
    zk = jnp.zeros((MLA_KV_RANK, MLA_HEADS, HEAD_LANES - MLA_NOPE_DIM), F32)
    w_k2 = jnp.concatenate([wkv[..., :MLA_NOPE_DIM], zk], axis=-1).reshape(MLA_KV_RANK, -1).astype(BF16)
    w_vt = wkv[..., MLA_NOPE_DIM:].reshape(MLA_KV_RANK, -1).T.astype(BF16)
    return w_qkv, w_c, w_g.astype(BF16), w_q2, w_k2, w_vt


def kernel(x, p, positions, rel_bias_table, w_in, b_gate, lambda_q1, lambda_k1, lambda_q2, lambda_k2,
           diff_subln_g, mla_q_norm_g, w_uq, mla_kv_norm_g, w_ukv, w_branch_diff, w_branch_mla, w_out,
           ln_mix_g, ln_mix_b, dense_w1, dense_w3, dense_w2, router_w, expert_w1, expert_w3, expert_w2,
           w_ple_gate, w_ple_proj, ln_ffn_g, ln_ffn_b):
    batch, seq, d = x.shape
    depth = w_in.shape[0]
    t = batch * seq
    tm = min(ROW_TILE, t)
    tm_mm = BIG_ROW_TILE if t % BIG_ROW_TILE == 0 else tm
    alpha = (2.0 * depth) ** 0.25
    row = lambda v: v.reshape(1, -1).astype(F32)

    cos_t, sin_t = _rope_tables(positions)
    bias = _bias_tiles(rel_bias_table * LOG2E, ATT_TILE)
    xf = x.reshape(t, d)
    xb = xf
    p_all = p.reshape(depth, t, PLE_DIM)

    for i in range(depth):
        lam_init = 0.8 - 0.6 * math.exp(-0.3 * i)
        w_qkv, w_c, w_g, w_q2, w_k2, w_vt = _mixer_weights(w_in[i], w_uq[i], w_ukv[i])
        qkv = _matmul(xb, w_qkv, BF16, tm_mm, 1024, scaled_tiles=1, scale=DIFF_HEAD_DIM ** -0.5 * LOG2E)
        q_mla, k_mla, vt_mla = _latent(xb, w_c, w_q2, w_k2, w_vt, row(mla_q_norm_g[i]), row(mla_kv_norm_g[i]),
                                       cos_t, sin_t, tm)
        o_a = _diff_attention(qkv.reshape(batch, seq, -1), bias, row(lambda_q1[i]), row(lambda_k1[i]),
                              row(lambda_q2[i]), row(lambda_k2[i]), row(diff_subln_g[i]), lam_init, batch, seq)
        o_b = _mla_attention(q_mla.reshape(batch, seq, -1), k_mla.reshape(batch, seq, -1), vt_mla, batch, seq)
        xf, xb = _mix_ln(o_a.reshape(t, -1), o_b.reshape(t, -1), xf, w_g, row(b_gate[i]),
                         w_branch_diff[i].astype(BF16), w_branch_mla[i].astype(BF16), w_out[i].astype(BF16),
                         row(ln_mix_g[i]), row(ln_mix_b[i]), alpha, tm)
        j = i // 2
        w_pg = w_ple_gate[i].astype(BF16)
        w_pp = w_ple_proj[i].astype(BF16)
        if i % 2 == 0:
            hmid = _swiglu_up(xb, dense_w1[j].astype(BF16), dense_w3[j].astype(BF16), tm_mm, DENSE_FF // 2)
            lead = [(hmid, True), (dense_w2[j].astype(BF16), False)]
            body = _dense_ln_kernel
        else:
            w_r = jnp.concatenate([router_w[j], jnp.zeros((d, HEAD_LANES - N_EXPERTS), F32)], axis=1)
            y1, y2, routed = _moe(xf, xb, w_r, expert_w1, expert_w3, expert_w2, j, tm)
            lead = [(y1, True), (y2, True), (routed, True)]
            body = _moe_ln_kernel
        xf, xb = _channel_ln(body, lead, xf, xb, p_all, i, w_pg, w_pp, row(ln_ffn_g[i]), row(ln_ffn_b[i]), alpha, tm)
    return xf.reshape(batch, seq, d)
```

```python
import functools
import math

import jax
import jax.numpy as jnp
from jax import lax
from jax.experimental import pallas as pl
from jax.experimental.pallas import tpu as pltpu

F32 = jnp.float32
BF16 = jnp.bfloat16

D_MODEL = 1024
PLE_DIM = 256
DIFF_HEADS = 8
DIFF_HEAD_DIM = 64
MLA_HEADS = 8
MLA_Q_RANK = 384
MLA_KV_RANK = 256
MLA_NOPE_DIM = 64
MLA_ROPE_DIM = 32
MLA_V_DIM = 64
ROPE_THETA = 10000.0
REL_BUCKETS = 32
REL_MAX_DIST = 128
DENSE_FF = 2816
N_EXPERTS = 8
TOP_K = 2
EXPERT_FF = 3584
EPS = 1e-5

HEAD_LANES = 128
NEG = -1e30
LOG2E = math.log2(math.e)
ATT_TILE = 256
ATT_HEADS_PER_STEP = 4
ROW_TILE = 512
BIG_ROW_TILE = 1024
GROUP_ROWS = 512
DISPATCH_CHUNKS = 4
VMEM_LIMIT = 52 * 1024 * 1024


def _cp(sem):
    return pltpu.CompilerParams(dimension_semantics=sem, vmem_limit_bytes=VMEM_LIMIT)


def _dot(a, b):
    return jnp.dot(a, b, preferred_element_type=F32)


def _sigmoid(x):
    return 1.0 / (1.0 + jnp.exp(-x))


def _layernorm(r, g, b):
    mu = jnp.mean(r, axis=-1, keepdims=True)
    d = r - mu
    var = jnp.mean(d * d, axis=-1, keepdims=True)
    return d * lax.rsqrt(var + EPS) * g + b


def _rmsnorm(x, g):
    return x * lax.rsqrt(jnp.mean(x * x, axis=-1, keepdims=True) + EPS) * g


def _mm_kernel(a_ref, w_ref, o_ref, *, scaled_tiles, scale):
    acc = _dot(a_ref[...].astype(BF16), w_ref[...])
    if scaled_tiles:
        acc = acc * jnp.where(pl.program_id(1) < scaled_tiles, scale, 1.0)
    o_ref[...] = acc.astype(o_ref.dtype)


def _matmul(a, w, out_dtype, tm, tn, scaled_tiles=0, scale=1.0):
    m, k = a.shape
    n = w.shape[1]
    return pl.pallas_call(
        functools.partial(_mm_kernel, scaled_tiles=scaled_tiles, scale=scale),
        grid=(m // tm, n // tn),
        in_specs=[pl.BlockSpec((tm, k), lambda i, j: (i, 0)),
                  pl.BlockSpec((k, tn), lambda i, j: (0, j))],
        out_specs=pl.BlockSpec((tm, tn), lambda i, j: (i, j)),
        out_shape=jax.ShapeDtypeStruct((m, n), out_dtype),
        compiler_params=_cp(("parallel", "arbitrary")),
    )(a, w)


C_COLS = MLA_Q_RANK + MLA_KV_RANK + 2 * HEAD_LANES


def _resident(a):
    return pl.BlockSpec(a.shape, lambda *_: (0,) * a.ndim, pipeline_mode=pl.Buffered(1))


def _latent_kernel(a_ref, wc_ref, wq_ref, wk_ref, wvt_ref, gq_ref, gkv_ref, cos_ref, sin_ref, q_ref, k_ref, vt_ref, *,
                   scale):
    z = _dot(a_ref[...].astype(BF16), wc_ref[...])
    cq = z[:, :MLA_Q_RANK]
    ckv = z[:, MLA_Q_RANK:MLA_Q_RANK + MLA_KV_RANK]
    kr = z[:, MLA_Q_RANK + MLA_KV_RANK:MLA_Q_RANK + MLA_KV_RANK + HEAD_LANES]
    kr_rot = z[:, MLA_Q_RANK + MLA_KV_RANK + HEAD_LANES:]
    cos, sin = cos_ref[...], sin_ref[...]
    k_rope = kr * cos + kr_rot * sin
    ckv_n = _rmsnorm(ckv, gkv_ref[...]).astype(BF16)
    zk = _dot(ckv_n, wk_ref[...])
    vt_ref[...] = lax.dot_general(wvt_ref[...], ckv_n, (((1,), (1,)), ((), ())),
                                  preferred_element_type=F32).astype(BF16)
    zq = _dot(_rmsnorm(cq, gq_ref[...]).astype(BF16), wq_ref[...])
    c = cos * scale
    s = sin * scale
    hl = HEAD_LANES
    for h in range(MLA_HEADS):
        k_ref[:, h * hl:(h + 1) * hl] = (zk[:, h * hl:(h + 1) * hl] + k_rope).astype(BF16)
        q_ref[:, h * hl:(h + 1) * hl] = (
            zq[:, h * hl:(h + 1) * hl] * c
            + zq[:, (MLA_HEADS + h) * hl:(MLA_HEADS + h + 1) * hl] * s).astype(BF16)


def _latent(xb, w_c, w_q2, w_k2, w_vt, gq, gkv, cos_t, sin_t, tm):
    t = xb.shape[0]
    row = lambda n: pl.BlockSpec((tm, n), lambda i: (i, 0))
    nq = MLA_HEADS * HEAD_LANES
    nv = MLA_HEADS * MLA_V_DIM
    scale = (MLA_NOPE_DIM + MLA_ROPE_DIM) ** -0.5 * LOG2E
    return pl.pallas_call(
        functools.partial(_latent_kernel, scale=scale),
        grid=(t // tm,),
        in_specs=[row(D_MODEL), _resident(w_c), _resident(w_q2), _resident(w_k2), _resident(w_vt), _resident(gq),
                  _resident(gkv), row(HEAD_LANES), row(HEAD_LANES)],
        out_specs=[row(nq), row(nq), pl.BlockSpec((nv, tm), lambda i: (0, i))],
        out_shape=[jax.ShapeDtypeStruct((t, nq), BF16),
                   jax.ShapeDtypeStruct((t, nq), BF16),
                   jax.ShapeDtypeStruct((nv, t), BF16)],
        compiler_params=_cp(("parallel",)),
    )(xb, w_c, w_q2, w_k2, w_vt, gq, gkv, cos_t, sin_t)


ONES_ROWS = 16


def _scores_t(k, q):
    return lax.dot_general(k, q, (((1,), (1,)), ((), ())), preferred_element_type=F32)


def _softmax_step_t(s, vt, m_ref, acc_ref):
    m_old = m_ref[...]
    m_new = jnp.maximum(m_old, jnp.max(s, axis=0, keepdims=True))
    p = jnp.exp2(s - m_new).astype(BF16)
    alpha = jnp.exp2(m_old - m_new)
    acc_ref[...] = alpha * acc_ref[...] + _dot(vt, p)
    m_ref[...] = m_new


def _fill_vt(v_ref, vt_scr, lane0, row0, v_rows, tile):
    for j in range(vt_scr.shape[0]):
        vt = v_ref[0, j * tile:(j + 1) * tile, lane0:lane0 + HEAD_LANES].astype(F32).T
        vt_scr[j, 0:v_rows, :] = vt[row0:row0 + v_rows].astype(BF16)
        vt_scr[j, v_rows:v_rows + ONES_ROWS, :] = jnp.ones((ONES_ROWS, tile), BF16)


def _normalised(acc, v_rows):
    return acc[0:v_rows] * (1.0 / acc[v_rows:v_rows + 1])


def _diff_attn_kernel(q_ref, k_ref, v_ref, bias_ref, lq1_ref, lk1_ref, lq2_ref, lk2_ref, g_ref, o_ref,
                      vt_scr, qcat_scr, s_scr, m_scr, acc_scr, *, lam_init, tile, heads):
    qi = pl.program_id(2)
    n_q = pl.num_programs(2)
    hd = 2 * DIFF_HEAD_DIM
    hl = HEAD_LANES
    cur = qi % 2

    def first_scores(q_tile, slot):
        rows = pl.ds(pl.multiple_of(q_tile * tile, tile), tile)
        for h in range(heads):
            q = q_ref[0, rows, h * hl:(h + 1) * hl]
            lane = lax.broadcasted_iota(jnp.int32, q.shape, 1)
            zero = jnp.zeros_like(q)
            qcat_scr[slot, h] = jnp.concatenate([jnp.where(lane < DIFF_HEAD_DIM, q, zero),
                                                 jnp.where(lane >= DIFF_HEAD_DIM, q, zero)], axis=0)
            s_scr[0, h] = _scores_t(k_ref[0, 0:tile, h * hl:(h + 1) * hl], qcat_scr[slot, h])

    @pl.when(qi == 0)
    def _():
        for h in range(heads):
            _fill_vt(v_ref, vt_scr.at[h], h * hl, 0, hd, tile)
        first_scores(0, 0)

    m_scr[...] = jnp.full(m_scr.shape, NEG, F32)
    acc_scr[...] = jnp.zeros(acc_scr.shape, F32)

    def scores(slot, kj):
        rows = pl.ds(pl.multiple_of(kj * tile, tile), tile)
        for h in range(heads):
            s_scr[slot, h] = _scores_t(k_ref[0, rows, h * hl:(h + 1) * hl], qcat_scr[cur, h])

    def update(slot, kj, bias_idx=None):
        for h in range(heads):
            s = s_scr[slot, h]
            if bias_idx is not None:
                s = s + bias_ref[h, bias_idx]
            _softmax_step_t(s, vt_scr[h, kj], m_scr.at[h], acc_scr.at[h])

    n_far = jnp.maximum(qi - 1, 0)
    quads = n_far // 4

    def pair_step(a):
        scores(1, a + 1)
        update(0, a)
        scores(0, a + 2)
        update(1, a + 1)

    def quad_body(j4, carry):
        pair_step(4 * j4)
        pair_step(4 * j4 + 2)
        return carry

    lax.fori_loop(0, quads, quad_body, 0)
    odd_pair = (n_far - 4 * quads) >= 2

    @pl.when(odd_pair)
    def _():
        pair_step(4 * quads)

    t0 = 4 * quads + jnp.where(odd_pair, 2, 0)

    @pl.when(n_far - t0 == 1)
    def _():
        scores(1, t0 + 1)
        update(0, t0)
        scores(0, t0 + 2)
        update(1, t0 + 1, 1)
        update(0, t0 + 2, 0)

    @pl.when(jnp.logical_and(n_far == t0, qi >= 1))
    def _():
        scores(1, t0 + 1)
        update(0, t0, 1)
        update(1, t0 + 1, 0)

    @pl.when(qi == 0)
    def _():
        update(0, 0, 0)

    first_scores(jnp.minimum(qi + 1, n_q - 1), 1 - cur)

    lam = (jnp.exp(jnp.sum(lq1_ref[...] * lk1_ref[...], axis=-1, keepdims=True))
           - jnp.exp(jnp.sum(lq2_ref[...] * lk2_ref[...], axis=-1, keepdims=True)) + lam_init)
    for h in range(heads):
        acc = acc_scr[h]
        o = _normalised(acc[:, :tile], hd) - lam * _normalised(acc[:, tile:], hd)
        o = o * lax.rsqrt(jnp.mean(o * o, axis=0, keepdims=True) + EPS)
        o_ref[0, :, h * hl:(h + 1) * hl] = (o.T * (g_ref[...] * (1.0 - lam_init))).astype(BF16)


def _diff_attention(qkv, bias, lq1, lk1, lq2, lk2, g, lam_init, batch, seq):
    tile = ATT_TILE
    hl = HEAD_LANES
    nh = DIFF_HEADS
    hps = ATT_HEADS_PER_STEP
    ng = nh // hps
    vec = lambda a: pl.BlockSpec(a.shape, lambda b, h, i: (0, 0))
    return pl.pallas_call(
        functools.partial(_diff_attn_kernel, lam_init=lam_init, tile=tile, heads=hps),
        grid=(batch, ng, seq // tile),
        in_specs=[pl.BlockSpec((1, seq, hps * hl), lambda b, h, i: (b, 0, h)),
                  pl.BlockSpec((1, seq, hps * hl), lambda b, h, i: (b, 0, ng + h)),
                  pl.BlockSpec((1, seq, hps * hl), lambda b, h, i: (b, 0, 2 * ng + h)),
                  pl.BlockSpec((hps, 2, tile, 2 * tile), lambda b, h, i: (h, 0, 0, 0)),
                  vec(lq1), vec(lk1), vec(lq2), vec(lk2), vec(g)],
        out_specs=pl.BlockSpec((1, tile, hps * hl), lambda b, h, i: (b, i, h)),
        out_shape=jax.ShapeDtypeStruct((batch, seq, nh * hl), BF16),
        scratch_shapes=[pltpu.VMEM((hps, seq // tile, hl + ONES_ROWS, tile), BF16),
                        pltpu.VMEM((2, hps, 2 * tile, hl), BF16),
                        pltpu.VMEM((2, hps, tile, 2 * tile), F32),
                        pltpu.VMEM((hps, 1, 2 * tile), F32),
                        pltpu.VMEM((hps, hl + ONES_ROWS, 2 * tile), F32)],
        compiler_params=_cp(("parallel", "parallel", "arbitrary")),
    )(qkv, qkv, qkv, bias, lq1, lk1, lq2, lk2, g)


MLA_STEP_HEADS = 4


def _mla_attn_kernel(q_ref, k_ref, vt_ref, o_ref, vt_scr, s_scr, m_scr, acc_scr, *, tile):
    qi = pl.program_id(2)
    n_q = pl.num_programs(2)
    hl = HEAD_LANES
    vd = MLA_V_DIM
    tq = 2 * tile

    def scores(slot, kj, q_tile):
        rows = pl.ds(pl.multiple_of(kj * tile, tile), tile)
        q_rows = pl.ds(pl.multiple_of(q_tile * tq, tq), tq)
        for h in range(MLA_STEP_HEADS):
            s_scr[slot, h] = _scores_t(k_ref[0, rows, h * hl:(h + 1) * hl], q_ref[0, q_rows, h * hl:(h + 1) * hl])

    @pl.when(qi == 0)
    def _():
        for h in range(MLA_STEP_HEADS):
            for j in range(vt_scr.shape[1]):
                vt_scr[h, j, 0:vd, :] = vt_ref[h * vd:(h + 1) * vd, j * tile:(j + 1) * tile]
                vt_scr[h, j, vd:vd + ONES_ROWS, :] = jnp.ones((ONES_ROWS, tile), BF16)
        scores(0, 0, 0)

    m_scr[...] = jnp.full(m_scr.shape, NEG, F32)
    acc_scr[...] = jnp.zeros(acc_scr.shape, F32)

    def update(slot, kj, key_offset=None):
        for h in range(MLA_STEP_HEADS):
            s = s_scr[slot, h]
            if key_offset is not None:
                key = lax.broadcasted_iota(jnp.int32, s.shape, 0)
                qry = lax.broadcasted_iota(jnp.int32, s.shape, 1)
                s = jnp.where(key + key_offset <= qry, s, NEG)
            _softmax_step_t(s, vt_scr[h, kj], m_scr.at[h], acc_scr.at[h])

    def pair_step(a):
        scores(1, a + 1, qi)
        update(0, a)
        scores(0, a + 2, qi)
        update(1, a + 1)

    def quad_body(j4, carry):
        pair_step(4 * j4)
        pair_step(4 * j4 + 2)
        return carry

    lax.fori_loop(0, qi // 2, quad_body, 0)

    @pl.when(qi % 2 == 1)
    def _():
        pair_step(2 * qi - 2)

    t0 = 2 * qi
    scores(1, t0 + 1, qi)
    update(0, t0, 0)
    update(1, t0 + 1, tile)
    scores(0, 0, jnp.minimum(qi + 1, n_q - 1))
    o = jnp.concatenate([_normalised(acc_scr[h], MLA_V_DIM) for h in range(MLA_STEP_HEADS)], axis=0)
    o_ref[0] = o.T.astype(BF16)


def _mla_attention(q, k, vt, batch, seq):
    tile = ATT_TILE
    tq = 2 * tile
    hl = HEAD_LANES
    nh = MLA_STEP_HEADS
    ng = MLA_HEADS // nh
    v_lanes = nh * MLA_V_DIM
    v_rows = MLA_V_DIM + ONES_ROWS
    return pl.pallas_call(
        functools.partial(_mla_attn_kernel, tile=tile),
        grid=(batch, ng, seq // tq),
        in_specs=[pl.BlockSpec((1, seq, nh * hl), lambda b, h, i: (b, 0, h)),
                  pl.BlockSpec((1, seq, nh * hl), lambda b, h, i: (b, 0, h)),
                  pl.BlockSpec((v_lanes, seq), lambda b, h, i: (h, b))],
        out_specs=pl.BlockSpec((1, tq, v_lanes), lambda b, h, i: (b, i, h)),
        out_shape=jax.ShapeDtypeStruct((batch, seq, MLA_HEADS * MLA_V_DIM), BF16),
        scratch_shapes=[pltpu.VMEM((nh, seq // tile, v_rows, tile), BF16),
                        pltpu.VMEM((2, nh, tile, tq), F32),
                        pltpu.VMEM((nh, 1, tq), F32),
                        pltpu.VMEM((nh, v_rows, tq), F32)],
        compiler_params=_cp(("parallel", "parallel", "arbitrary")),
    )(q, k, vt)


def _mix_ln_kernel(oa_ref, ob_ref, x_ref, wg_ref, bg_ref, wbd_ref, wbm_ref, wo_ref, g_ref, b_ref,
                   xo_ref, xbo_ref, *, alpha):
    d = D_MODEL
    x = x_ref[...]
    xb = x.astype(BF16)
    g_a = _sigmoid(_dot(xb, wg_ref[:, :d]) + bg_ref[:, :d])
    m = g_a * _dot(oa_ref[...], wbd_ref[...])
    g_b = _sigmoid(_dot(xb, wg_ref[:, d:]) + bg_ref[:, d:])
    m = (m + g_b * _dot(ob_ref[...], wbm_ref[...])).astype(BF16)
    y = _layernorm(alpha * x + _dot(m, wo_ref[...]), g_ref[...], b_ref[...])
    xo_ref[...] = y
    xbo_ref[...] = y.astype(BF16)


def _mix_ln(o_a, o_b, x, w_g, b_g, w_bd, w_bm, w_out, g, b, alpha, tm):
    t = x.shape[0]
    row = lambda a: pl.BlockSpec((tm, a.shape[1]), lambda i: (i, 0))
    return pl.pallas_call(
        functools.partial(_mix_ln_kernel, alpha=alpha),
        grid=(t // tm,),
        in_specs=[row(o_a), row(o_b), row(x), _resident(w_g), _resident(b_g), _resident(w_bd),
                  _resident(w_bm), _resident(w_out), _resident(g), _resident(b)],
        out_specs=[row(x), row(o_a)],
        out_shape=[jax.ShapeDtypeStruct((t, D_MODEL), F32), jax.ShapeDtypeStruct((t, D_MODEL), BF16)],
        compiler_params=_cp(("parallel",)),
    )(o_a, o_b, x, w_g, b_g, w_bd, w_bm, w_out, g, b)


def _swiglu_up_kernel(a_ref, w1_ref, w3_ref, o_ref):
    a = a_ref[...]
    u = _dot(a, w1_ref[...])
    o_ref[...] = (u * _sigmoid(u) * _dot(a, w3_ref[...])).astype(BF16)


def _swiglu_up(xb, w1, w3, tm, tn):
    t = xb.shape[0]
    ff = w1.shape[1]
    return pl.pallas_call(
        _swiglu_up_kernel,
        grid=(t // tm, ff // tn),
        in_specs=[pl.BlockSpec((tm, D_MODEL), lambda i, j: (i, 0)),
                  pl.BlockSpec((D_MODEL, tn), lambda i, j: (0, j)),
                  pl.BlockSpec((D_MODEL, tn), lambda i, j: (0, j))],
        out_specs=pl.BlockSpec((tm, tn), lambda i, j: (i, j)),
        out_shape=jax.ShapeDtypeStruct((t, ff), BF16),
        compiler_params=_cp(("parallel", "arbitrary")),
    )(xb, w1, w3)


def _ple_ln(f, x_ref, xb_ref, p_ref, wpg_ref, wpp_ref, g_ref, b_ref, xo_ref, xbo_ref, alpha):
    e = _sigmoid(_dot(xb_ref[...], wpg_ref[...])) * _dot(p_ref[...].astype(BF16), wpp_ref[...])
    y = _layernorm(alpha * x_ref[...] + f + e, g_ref[...], b_ref[...])
    xo_ref[...] = y
    xbo_ref[...] = y.astype(BF16)


def _dense_ln_kernel(h_ref, w2_ref, x_ref, xb_ref, p_ref, wpg_ref, wpp_ref, g_ref, b_ref, xo_ref, xbo_ref, *, alpha):
    _ple_ln(_dot(h_ref[...], w2_ref[...]), x_ref, xb_ref, p_ref, wpg_ref, wpp_ref, g_ref, b_ref, xo_ref, xbo_ref,
            alpha)


def _moe_ln_kernel(y1_ref, y2_ref, gates_ref, x_ref, xb_ref, p_ref, wpg_ref, wpp_ref, g_ref, b_ref, xo_ref, xbo_ref,
                   *, alpha):
    gates = gates_ref[...]
    f = (y1_ref[...].astype(F32) * gates[:, TOP_K:TOP_K + 1]
         + y2_ref[...].astype(F32) * gates[:, TOP_K + 1:TOP_K + 2])
    _ple_ln(f, x_ref, xb_ref, p_ref, wpg_ref, wpp_ref, g_ref, b_ref, xo_ref, xbo_ref, alpha)


def _channel_ln(body, lead, x, xb, p, layer, w_pg, w_pp, g, b, alpha, tm):
    t = x.shape[0]
    row = lambda a: pl.BlockSpec((tm, a.shape[1]), lambda i: (i, 0))
    full = _resident
    tail = [x, xb, p, w_pg, w_pp, g, b]
    return pl.pallas_call(
        functools.partial(body, alpha=alpha),
        grid=(t // tm,),
        in_specs=[row(a) if tiled else full(a) for a, tiled in lead]
        + [row(x), row(xb), pl.BlockSpec((None, tm, p.shape[2]), lambda i: (layer, i, 0)),
           full(w_pg), full(w_pp), full(g), full(b)],
        out_specs=[row(x), row(xb)],
        out_shape=[jax.ShapeDtypeStruct((t, D_MODEL), F32), jax.ShapeDtypeStruct((t, D_MODEL), BF16)],
        compiler_params=_cp(("parallel",)),
    )(*[a for a, _ in lead], *tail)


def _router_kernel(x_ref, whi_ref, wlo_ref, o_ref):
    x = x_ref[...]
    x_hi = x.astype(BF16)
    x_lo = (x - x_hi.astype(F32)).astype(BF16)
    w_hi = whi_ref[...]
    logits = _dot(x_hi, w_hi) + (_dot(x_lo, w_hi) + _dot(x_hi, wlo_ref[...]))
    lane = lax.broadcasted_iota(jnp.int32, logits.shape, 1)
    lg = jnp.where(lane < N_EXPERTS, logits, -jnp.inf)
    v1 = jnp.max(lg, axis=-1, keepdims=True)
    i1 = jnp.min(jnp.where(lg == v1, lane, HEAD_LANES), axis=-1, keepdims=True)
    lg2 = jnp.where(lane == i1, -jnp.inf, lg)
    v2 = jnp.max(lg2, axis=-1, keepdims=True)
    i2 = jnp.min(jnp.where(lg2 == v2, lane, HEAD_LANES), axis=-1, keepdims=True)
    e2 = jnp.exp(v2 - v1)
    g1 = 1.0 / (1.0 + e2)
    g2 = e2 / (1.0 + e2)
    o_ref[...] = jnp.where(lane == 0, i1.astype(F32),
                           jnp.where(lane == 1, i2.astype(F32),
                                     jnp.where(lane == 2, g1, jnp.where(lane == 3, g2, 0.0))))


def _router(x, w_r, tm):
    t = x.shape[0]
    w_hi = w_r.astype(BF16)
    w_lo = (w_r - w_hi.astype(F32)).astype(BF16)
    return pl.pallas_call(
        _router_kernel,
        grid=(t // tm,),
        in_specs=[pl.BlockSpec((tm, D_MODEL), lambda i: (i, 0)), _resident(w_hi), _resident(w_lo)],
        out_specs=pl.BlockSpec((tm, HEAD_LANES), lambda i: (i, 0)),
        out_shape=jax.ShapeDtypeStruct((t, HEAD_LANES), F32),
        compiler_params=_cp(("parallel",)),
    )(x, w_hi, w_lo)


def _expert_changed(be_ref):
    i = pl.program_id(1)
    return jnp.logical_or(i == 0, be_ref[i] != be_ref[jnp.maximum(i - 1, 0)])


def _expert_up_kernel(be_ref, nb_ref, a_ref, w1_ref, w3_ref, *rest):
    o_ref, w1_scr, w3_scr = rest[-3:]

    @pl.when(_expert_changed(be_ref))
    def _():
        w1_scr[...] = w1_ref[0].astype(BF16)
        w3_scr[...] = w3_ref[0].astype(BF16)

    @pl.when(pl.program_id(1) < nb_ref[0])
    def _():
        a = a_ref[...]
        u = _dot(a, w1_scr[...])
        o_ref[...] = (u * _sigmoid(u) * _dot(a, w3_scr[...])).astype(BF16)

    @pl.when(pl.program_id(1) >= nb_ref[0])
    def _():
        o_ref[...] = jnp.zeros(o_ref.shape, BF16)


def _expert_up(a, w1, w3, layer, blk_e, n_used, tm, tn, total_rows, first_block, into):
    rows = a.shape[0]
    ff = w1.shape[3]
    wspec = pl.BlockSpec((None, 1, D_MODEL, tn), lambda j, i, be, nb: (layer, be[i], 0, j),
                         pipeline_mode=pl.Buffered(1))
    chained = [] if into is None else [into]
    grid_spec = pltpu.PrefetchScalarGridSpec(
        num_scalar_prefetch=2,
        grid=(ff // tn, rows // tm),
        in_specs=[pl.BlockSpec((tm, D_MODEL), lambda j, i, be, nb: (i, 0)), wspec, wspec]
        + [pl.BlockSpec(memory_space=pl.ANY) for _ in chained],
        out_specs=pl.BlockSpec((tm, tn), lambda j, i, be, nb: (first_block + i, j)),
        scratch_shapes=[pltpu.VMEM((D_MODEL, tn), BF16), pltpu.VMEM((D_MODEL, tn), BF16)],
    )
    return pl.pallas_call(
        _expert_up_kernel,
        grid_spec=grid_spec,
        out_shape=jax.ShapeDtypeStruct((total_rows, ff), BF16),
        input_output_aliases={5: 0} if chained else {},
        compiler_params=_cp(("arbitrary", "arbitrary")),
    )(blk_e, n_used, a, w1, w3, *chained)


def _expert_down_kernel(be_ref, nb_ref, h_ref, w2_ref, o_ref, w2_scr):
    @pl.when(_expert_changed(be_ref))
    def _():
        w2_scr[...] = w2_ref[0].astype(BF16)

    @pl.when(pl.program_id(1) < nb_ref[0])
    def _():
        o_ref[...] = _dot(h_ref[...], w2_scr[...]).astype(BF16)

    @pl.when(pl.program_id(1) >= nb_ref[0])
    def _():
        o_ref[...] = jnp.zeros(o_ref.shape, BF16)


def _expert_down(h, w2, layer, blk_e, n_used, tm, tn):
    rows, ff = h.shape
    grid_spec = pltpu.PrefetchScalarGridSpec(
        num_scalar_prefetch=2,
        grid=(D_MODEL // tn, rows // tm),
        in_specs=[pl.BlockSpec((tm, ff), lambda j, i, be, nb: (i, 0)),
                  pl.BlockSpec((None, 1, ff, tn), lambda j, i, be, nb: (layer, be[i], 0, j),
                               pipeline_mode=pl.Buffered(1))],
        out_specs=pl.BlockSpec((tm, tn), lambda j, i, be, nb: (i, j)),
        scratch_shapes=[pltpu.VMEM((ff, tn), BF16)],
    )
    return pl.pallas_call(
        _expert_down_kernel,
        grid_spec=grid_spec,
        out_shape=jax.ShapeDtypeStruct((rows, D_MODEL), BF16),
        compiler_params=_cp(("arbitrary", "arbitrary")),
    )(blk_e, n_used, h, w2)


def _moe(x, xb, w_r, w1, w3, w2, layer, tm):
    t = x.shape[0]
    m = t * TOP_K
    gb = GROUP_ROWS
    routed = _router(x, w_r, tm)
    experts = jnp.arange(N_EXPERTS, dtype=jnp.int32)[None, :]
    hot = [(routed[:, k].astype(jnp.int32)[:, None] == experts).astype(jnp.int32) for k in range(TOP_K)]
    both = sum(hot)
    before = jnp.cumsum(both, axis=0) - both
    counts = jnp.sum(both, axis=0)
    padded = ((counts + gb - 1) // gb) * gb
    pend = jnp.cumsum(padded)
    pstart = pend - padded
    dest = [jnp.sum((pstart[None, :] + before) * h, axis=1) for h in hot]
    n_blocks = m // gb + N_EXPERTS
    blk_e = jnp.minimum(jnp.sum((jnp.arange(n_blocks, dtype=jnp.int32)[:, None] * gb >= pend[None, :])
                                .astype(jnp.int32), axis=1), N_EXPERTS - 1)
    n_used = (pend[-1:] // gb).astype(jnp.int32)
    n_rows = n_blocks * gb
    gap = padded - counts
    gap_end = jnp.cumsum(gap)
    i_dummy = jnp.arange(n_rows - m, dtype=jnp.int32)
    in_gap = (i_dummy[:, None] >= (gap_end - gap)[None, :]) & (i_dummy[:, None] < gap_end[None, :])
    dummy_key = jnp.where(i_dummy < gap_end[-1],
                          jnp.sum(jnp.where(in_gap, (pstart + counts - (gap_end - gap))[None, :], 0), axis=1),
                          pend[-1] - gap_end[-1]) + i_dummy
    keys = jnp.concatenate(dest + [dummy_key]).astype(jnp.uint32)
    toks = jnp.concatenate([jnp.arange(t, dtype=jnp.uint32)] * TOP_K + [jnp.zeros((n_rows - m,), jnp.uint32)])
    assert n_rows * t < 2 ** 32
    src_tok = (jnp.sort(keys * t + toks) % t).astype(jnp.int32)
    take = lambda arr, rows: arr.at[rows].get(mode="promise_in_bounds")
    nbc = n_blocks // DISPATCH_CHUNKS
    h = None
    for c in range(DISPATCH_CHUNKS):
        a = take(xb, src_tok[c * nbc * gb:(c + 1) * nbc * gb])
        h = _expert_up(a, w1, w3, layer, blk_e[c * nbc:(c + 1) * nbc], jnp.clip(n_used - c * nbc, 0, nbc),
                       gb, EXPERT_FF // 2, n_rows, c * nbc, h)
    yb = _expert_down(h, w2, layer, blk_e, n_used, gb, D_MODEL)
    return take(yb, dest[0]), take(yb, dest[1]), routed


def _t5_bucket(dist):
    n = jnp.maximum(dist, 0)
    max_exact = REL_BUCKETS // 2
    large = max_exact + (jnp.log(jnp.maximum(n, 1).astype(F32) / max_exact)
                         / math.log(REL_MAX_DIST / max_exact) * (REL_BUCKETS - max_exact)).astype(jnp.int32)
    large = jnp.minimum(large, REL_BUCKETS - 1)
    return jnp.where(n < max_exact, n, large)


def _bias_tiles(table, tile):
    period = 3 * tile
    k = jnp.arange(period, dtype=jnp.int32)
    dist = jnp.where(k < 2 * tile, k, k - period)
    rel = table.astype(F32) - table[REL_BUCKETS - 1].astype(F32)[None, :]
    onehot = _t5_bucket(dist)[:, None] == jnp.arange(REL_BUCKETS, dtype=jnp.int32)[None, :]
    f = jnp.sum(jnp.where(onehot[:, :, None], rel[None], 0.0), axis=1)
    f = jnp.where((dist >= 0)[:, None], f, NEG)
    heads = table.shape[1]
    toep = jnp.tile(f.T, (1, tile))[:, :tile * (period - 1)].reshape(heads, tile, period - 1)
    tiles = jnp.stack([toep[:, :, :tile], toep[:, :, tile:2 * tile]], axis=1)
    return jnp.concatenate([tiles, tiles], axis=-1)


def _rope_tables(positions):
    half = MLA_ROPE_DIM // 2
    b, sq = positions.shape
    inv_freq = ROPE_THETA ** (-jnp.arange(half, dtype=F32) / half)
    per_row = HEAD_LANES // half
    pos = positions.astype(F32).reshape(b * sq // per_row, per_row, 1)
    ang = (pos * inv_freq.reshape(1, 1, half)).reshape(b * sq // per_row, HEAD_LANES)
    c = jnp.cos(ang).reshape(b, sq, half)
    s = jnp.sin(ang).reshape(b, sq, half)
    ones = jnp.ones((b, sq, MLA_NOPE_DIM), F32)
    z_nope = jnp.zeros((b, sq, MLA_NOPE_DIM), F32)
    z_pad = jnp.zeros((b, sq, HEAD_LANES - MLA_NOPE_DIM - MLA_ROPE_DIM), F32)
    cos_t = jnp.concatenate([ones, c, c, z_pad], axis=-1).reshape(b * sq, HEAD_LANES)
    sin_t = jnp.concatenate([z_nope, s, s, z_pad], axis=-1).reshape(b * sq, HEAD_LANES)
    return cos_t, sin_t


def _rot_cols(w):
    half = MLA_ROPE_DIM // 2
    return jnp.concatenate([-w[..., half:], w[..., :half]], axis=-1)


def _mixer_weights(w_in, w_uq, w_ukv):
    d = D_MODEL
    hq = DIFF_HEADS * 2 * DIFF_HEAD_DIM
    o = 0
    w_dq = w_in[:, o:o + hq]; o += hq
    w_dk = w_in[:, o:o + hq]; o += hq
    w_dv = w_in[:, o:o + hq]; o += hq
    w_cq = w_in[:, o:o + MLA_Q_RANK]; o += MLA_Q_RANK
    w_ckv = w_in[:, o:o + MLA_KV_RANK]; o += MLA_KV_RANK
    w_kr = w_in[:, o:o + MLA_ROPE_DIM]; o += MLA_ROPE_DIM
    w_g = w_in[:, o:]
    w_qkv = jnp.concatenate([w_dq, w_dk, w_dv], axis=1).astype(BF16)
    pad = HEAD_LANES - MLA_NOPE_DIM - MLA_ROPE_DIM
    z = lambda n: jnp.zeros((d, n), F32)
    w_c = jnp.concatenate([w_cq, w_ckv, z(MLA_NOPE_DIM), w_kr, z(pad), z(MLA_NOPE_DIM), _rot_cols(w_kr), z(pad)],
                          axis=1).astype(BF16)
    wq = w_uq.reshape(MLA_Q_RANK, MLA_HEADS, MLA_NOPE_DIM + MLA_ROPE_DIM)
    nope, rope = wq[..., :MLA_NOPE_DIM], wq[..., MLA_NOPE_DIM:]
    zq = lambda n: jnp.zeros((MLA_Q_RANK, MLA_HEADS, n), F32)
    plain = jnp.concatenate([nope, rope, zq(pad)], axis=-1).reshape(MLA_Q_RANK, -1)
    rot = jnp.concatenate([zq(MLA_NOPE_DIM), _rot_cols(rope), zq(pad)], axis=-1).reshape(MLA_Q_RANK, -1)
    w_q2 = jnp.concatenate([plain, rot], axis=1).astype(BF16)
    wkv = w_ukv.reshape(MLA_KV_RANK, MLA_HEADS, MLA_NOPE_DIM + MLA_V_DIM)
    zk = jnp.zeros((MLA_KV_RANK, MLA_HEADS, HEAD_LANES - MLA_NOPE_DIM), F32)
    w_k2 = jnp.concatenate([wkv[..., :MLA_NOPE_DIM], zk], axis=-1).reshape(MLA_KV_RANK, -1).astype(BF16)
    w_vt = wkv[..., MLA_NOPE_DIM:].reshape(MLA_KV_RANK, -1).T.astype(BF16)
    return w_qkv, w_c, w_g.astype(BF16), w_q2, w_k2, w_vt


def kernel(x, p, positions, rel_bias_table, w_in, b_gate, lambda_q1, lambda_k1, lambda_q2, lambda_k2,
           diff_subln_g, mla_q_norm_g, w_uq, mla_kv_norm_g, w_ukv, w_branch_diff, w_branch_mla, w_out,
           ln_mix_g, ln_mix_b, dense_w1, dense_w3, dense_w2, router_w, expert_w1, expert_w3, expert_w2,
           w_ple_gate, w_ple_proj, ln_ffn_g, ln_ffn_b):
    batch, seq, d = x.shape
    depth = w_in.shape[0]
    t = batch * seq
    assert d == D_MODEL and seq % (2 * ATT_TILE) == 0 and t % ROW_TILE == 0, (batch, seq, d)
    assert (t * TOP_K // GROUP_ROWS + N_EXPERTS) % DISPATCH_CHUNKS == 0, (batch, seq)
    tm = min(ROW_TILE, t)
    tm_mm = BIG_ROW_TILE if t % BIG_ROW_TILE == 0 else tm
    alpha = (2.0 * depth) ** 0.25
    row = lambda v: v.reshape(1, -1).astype(F32)

    cos_t, sin_t = _rope_tables(positions)
    bias = _bias_tiles(rel_bias_table * LOG2E, ATT_TILE)
    xf = x.reshape(t, d)
    xb = xf
    p_all = p.reshape(depth, t, PLE_DIM)

    for i in range(depth):
        lam_init = 0.8 - 0.6 * math.exp(-0.3 * i)
        w_qkv, w_c, w_g, w_q2, w_k2, w_vt = _mixer_weights(w_in[i], w_uq[i], w_ukv[i])
        qkv = _matmul(xb, w_qkv, BF16, tm_mm, 1024, scaled_tiles=1, scale=DIFF_HEAD_DIM ** -0.5 * LOG2E)
        q_mla, k_mla, vt_mla = _latent(xb, w_c, w_q2, w_k2, w_vt, row(mla_q_norm_g[i]), row(mla_kv_norm_g[i]),
                                       cos_t, sin_t, tm)
        o_a = _diff_attention(qkv.reshape(batch, seq, -1), bias, row(lambda_q1[i]), row(lambda_k1[i]),
                              row(lambda_q2[i]), row(lambda_k2[i]), row(diff_subln_g[i]), lam_init, batch, seq)
        o_b = _mla_attention(q_mla.reshape(batch, seq, -1), k_mla.reshape(batch, seq, -1), vt_mla, batch, seq)
        xf, xb = _mix_ln(o_a.reshape(t, -1), o_b.reshape(t, -1), xf, w_g, row(b_gate[i]),
                         w_branch_diff[i].astype(BF16), w_branch_mla[i].astype(BF16), w_out[i].astype(BF16),
                         row(ln_mix_g[i]), row(ln_mix_b[i]), alpha, tm)
        j = i // 2
        w_pg = w_ple_gate[i].astype(BF16)
        w_pp = w_ple_proj[i].astype(BF16)
        if i % 2 == 0:
            hmid = _swiglu_up(xb, dense_w1[j].astype(BF16), dense_w3[j].astype(BF16), tm_mm, DENSE_FF // 2)
            lead = [(hmid, True), (dense_w2[j].astype(BF16), False)]
            body = _dense_ln_kernel
        else:
            w_r = jnp.concatenate([router_w[j], jnp.zeros((d, HEAD_LANES - N_EXPERTS), F32)], axis=1)
            y1, y2, routed = _moe(xf, xb, w_r, expert_w1, expert_w3, expert_w2, j, tm)
            lead = [(y1, True), (y2, True), (routed, True)]
            body = _moe_ln_kernel
        xf, xb = _channel_ln(body, lead, xf, xb, p_all, i, w_pg, w_pp, row(ln_ffn_g[i]), row(ln_ffn_b[i]), alpha, tm)
    return xf.reshape(batch, seq, d)
```

```python
import functools
import math

import jax
import jax.numpy as jnp
from jax import lax
from jax.experimental import pallas as pl
from jax.experimental.pallas import tpu as pltpu

F32 = jnp.float32
BF16 = jnp.bfloat16

D_MODEL = 1024
PLE_DIM = 256
DIFF_HEADS = 8
DIFF_HEAD_DIM = 64
MLA_HEADS = 8
MLA_Q_RANK = 384
MLA_KV_RANK = 256
MLA_NOPE_DIM = 64
MLA_ROPE_DIM = 32
MLA_V_DIM = 64
ROPE_THETA = 10000.0
REL_BUCKETS = 32
REL_MAX_DIST = 128
DENSE_FF = 2816
N_EXPERTS = 8
TOP_K = 2
EXPERT_FF = 3584
EPS = 1e-5

HEAD_LANES = 128
NEG = -1e30
LOG2E = math.log2(math.e)
ATT_TILE = 256
ATT_HEADS_PER_STEP = 4
ROW_TILE = 512
BIG_ROW_TILE = 1024
GROUP_ROWS = 512
DISPATCH_CHUNKS = 4
VMEM_LIMIT = 52 * 1024 * 1024


def _cp(sem):
    return pltpu.CompilerParams(dimension_semantics=sem, vmem_limit_bytes=VMEM_LIMIT)


def _dot(a, b):
    return jnp.dot(a, b, preferred_element_type=F32)


def _sigmoid(x):
    return 1.0 / (1.0 + jnp.exp(-x))


def _layernorm(r, g, b):
    mu = jnp.mean(r, axis=-1, keepdims=True)
    d = r - mu
    var = jnp.mean(d * d, axis=-1, keepdims=True)
    return d * lax.rsqrt(var + EPS) * g + b


def _rmsnorm(x, g):
    return x * lax.rsqrt(jnp.mean(x * x, axis=-1, keepdims=True) + EPS) * g


def _mm_kernel(a_ref, w_ref, o_ref, *, scaled_tiles, scale):
    acc = _dot(a_ref[...].astype(BF16), w_ref[...])
    if scaled_tiles:
        acc = acc * jnp.where(pl.program_id(1) < scaled_tiles, scale, 1.0)
    o_ref[...] = acc.astype(o_ref.dtype)


def _matmul(a, w, out_dtype, tm, tn, scaled_tiles=0, scale=1.0):
    m, k = a.shape
    n = w.shape[1]
    return pl.pallas_call(
        functools.partial(_mm_kernel, scaled_tiles=scaled_tiles, scale=scale),
        grid=(m // tm, n // tn),
        in_specs=[pl.BlockSpec((tm, k), lambda i, j: (i, 0)),
                  pl.BlockSpec((k, tn), lambda i, j: (0, j))],
        out_specs=pl.BlockSpec((tm, tn), lambda i, j: (i, j)),
        out_shape=jax.ShapeDtypeStruct((m, n), out_dtype),
        compiler_params=_cp(("parallel", "arbitrary")),
    )(a, w)


C_COLS = MLA_Q_RANK + MLA_KV_RANK + 2 * HEAD_LANES


def _resident(a):
    return pl.BlockSpec(a.shape, lambda *_: (0,) * a.ndim, pipeline_mode=pl.Buffered(1))


def _latent_kernel(a_ref, wc_ref, wq_ref, wk_ref, wvt_ref, gq_ref, gkv_ref, cos_ref, sin_ref, q_ref, k_ref, vt_ref, *,
                   scale):
    z = _dot(a_ref[...].astype(BF16), wc_ref[...])
    cq = z[:, :MLA_Q_RANK]
    ckv = z[:, MLA_Q_RANK:MLA_Q_RANK + MLA_KV_RANK]
    kr = z[:, MLA_Q_RANK + MLA_KV_RANK:MLA_Q_RANK + MLA_KV_RANK + HEAD_LANES]
    kr_rot = z[:, MLA_Q_RANK + MLA_KV_RANK + HEAD_LANES:]
    cos, sin = cos_ref[...], sin_ref[...]
    k_rope = kr * cos + kr_rot * sin
    ckv_n = _rmsnorm(ckv, gkv_ref[...]).astype(BF16)
    zk = _dot(ckv_n, wk_ref[...])
    vt_ref[...] = lax.dot_general(wvt_ref[...], ckv_n, (((1,), (1,)), ((), ())),
                                  preferred_element_type=F32).astype(BF16)
    zq = _dot(_rmsnorm(cq, gq_ref[...]).astype(BF16), wq_ref[...])
    c = cos * scale
    s = sin * scale
    hl = HEAD_LANES
    for h in range(MLA_HEADS):
        k_ref[:, h * hl:(h + 1) * hl] = (zk[:, h * hl:(h + 1) * hl] + k_rope).astype(BF16)
        q_ref[:, h * hl:(h + 1) * hl] = (
            zq[:, h * hl:(h + 1) * hl] * c
            + zq[:, (MLA_HEADS + h) * hl:(MLA_HEADS + h + 1) * hl] * s).astype(BF16)


def _latent(xb, w_c, w_q2, w_k2, w_vt, gq, gkv, cos_t, sin_t, tm):
    t = xb.shape[0]
    row = lambda n: pl.BlockSpec((tm, n), lambda i: (i, 0))
    nq = MLA_HEADS * HEAD_LANES
    nv = MLA_HEADS * MLA_V_DIM
    scale = (MLA_NOPE_DIM + MLA_ROPE_DIM) ** -0.5 * LOG2E
    return pl.pallas_call(
        functools.partial(_latent_kernel, scale=scale),
        grid=(t // tm,),
        in_specs=[row(D_MODEL), _resident(w_c), _resident(w_q2), _resident(w_k2), _resident(w_vt), _resident(gq),
                  _resident(gkv), row(HEAD_LANES), row(HEAD_LANES)],
        out_specs=[row(nq), row(nq), pl.BlockSpec((nv, tm), lambda i: (0, i))],
        out_shape=[jax.ShapeDtypeStruct((t, nq), BF16),
                   jax.ShapeDtypeStruct((t, nq), BF16),
                   jax.ShapeDtypeStruct((nv, t), BF16)],
        compiler_params=_cp(("parallel",)),
    )(xb, w_c, w_q2, w_k2, w_vt, gq, gkv, cos_t, sin_t)


ONES_ROWS = 16


def _scores_t(k, q):
    return lax.dot_general(k, q, (((1,), (1,)), ((), ())), preferred_element_type=F32)


def _softmax_step_t(s, vt, m_ref, acc_ref):
    m_old = m_ref[...]
    m_new = jnp.maximum(m_old, jnp.max(s, axis=0, keepdims=True))
    p = jnp.exp2(s - m_new).astype(BF16)
    alpha = jnp.exp2(m_old - m_new)
    acc_ref[...] = alpha * acc_ref[...] + _dot(vt, p)
    m_ref[...] = m_new


def _fill_vt(v_ref, vt_scr, lane0, row0, v_rows, tile):
    for j in range(vt_scr.shape[0]):
        vt = v_ref[0, j * tile:(j + 1) * tile, lane0:lane0 + HEAD_LANES].astype(F32).T
        vt_scr[j, 0:v_rows, :] = vt[row0:row0 + v_rows].astype(BF16)
        vt_scr[j, v_rows:v_rows + ONES_ROWS, :] = jnp.ones((ONES_ROWS, tile), BF16)


def _normalised(acc, v_rows):
    return acc[0:v_rows] * (1.0 / acc[v_rows:v_rows + 1])


def _diff_attn_kernel(q_ref, k_ref, v_ref, bias_ref, lq1_ref, lk1_ref, lq2_ref, lk2_ref, g_ref, o_ref,
                      vt_scr, qcat_scr, s_scr, m_scr, acc_scr, *, lam_init, tile, heads):
    qi = pl.program_id(2)
    n_q = pl.num_programs(2)
    hd = 2 * DIFF_HEAD_DIM
    hl = HEAD_LANES
    cur = qi % 2

    def first_scores(q_tile, slot):
        rows = pl.ds(pl.multiple_of(q_tile * tile, tile), tile)
        for h in range(heads):
            q = q_ref[0, rows, h * hl:(h + 1) * hl]
            lane = lax.broadcasted_iota(jnp.int32, q.shape, 1)
            zero = jnp.zeros_like(q)
            qcat_scr[slot, h] = jnp.concatenate([jnp.where(lane < DIFF_HEAD_DIM, q, zero),
                                                 jnp.where(lane >= DIFF_HEAD_DIM, q, zero)], axis=0)
            s_scr[0, h] = _scores_t(k_ref[0, 0:tile, h * hl:(h + 1) * hl], qcat_scr[slot, h])

    @pl.when(qi == 0)
    def _():
        for h in range(heads):
            _fill_vt(v_ref, vt_scr.at[h], h * hl, 0, hd, tile)
        first_scores(0, 0)

    m_scr[...] = jnp.full(m_scr.shape, NEG, F32)
    acc_scr[...] = jnp.zeros(acc_scr.shape, F32)

    def scores(slot, kj):
        rows = pl.ds(pl.multiple_of(kj * tile, tile), tile)
        for h in range(heads):
            s_scr[slot, h] = _scores_t(k_ref[0, rows, h * hl:(h + 1) * hl], qcat_scr[cur, h])

    def update(slot, kj, bias_idx=None):
        for h in range(heads):
            s = s_scr[slot, h]
            if bias_idx is not None:
                s = s + bias_ref[h, bias_idx]
            _softmax_step_t(s, vt_scr[h, kj], m_scr.at[h], acc_scr.at[h])

    n_far = jnp.maximum(qi - 1, 0)
    quads = n_far // 4

    def pair_step(a):
        scores(1, a + 1)
        update(0, a)
        scores(0, a + 2)
        update(1, a + 1)

    def quad_body(j4, carry):
        pair_step(4 * j4)
        pair_step(4 * j4 + 2)
        return carry

    lax.fori_loop(0, quads, quad_body, 0)
    odd_pair = (n_far - 4 * quads) >= 2

    @pl.when(odd_pair)
    def _():
        pair_step(4 * quads)

    t0 = 4 * quads + jnp.where(odd_pair, 2, 0)

    @pl.when(n_far - t0 == 1)
    def _():
        scores(1, t0 + 1)
        update(0, t0)
        scores(0, t0 + 2)
        update(1, t0 + 1, 1)
        update(0, t0 + 2, 0)

    @pl.when(jnp.logical_and(n_far == t0, qi >= 1))
    def _():
        scores(1, t0 + 1)
        update(0, t0, 1)
        update(1, t0 + 1, 0)

    @pl.when(qi == 0)
    def _():
        update(0, 0, 0)

    first_scores(jnp.minimum(qi + 1, n_q - 1), 1 - cur)

    lam = (jnp.exp(jnp.sum(lq1_ref[...] * lk1_ref[...], axis=-1, keepdims=True))
           - jnp.exp(jnp.sum(lq2_ref[...] * lk2_ref[...], axis=-1, keepdims=True)) + lam_init)
    for h in range(heads):
        acc = acc_scr[h]
        o = _normalised(acc[:, :tile], hd) - lam * _normalised(acc[:, tile:], hd)
        o = o * lax.rsqrt(jnp.mean(o * o, axis=0, keepdims=True) + EPS)
        o_ref[0, :, h * hl:(h + 1) * hl] = (o.T * (g_ref[...] * (1.0 - lam_init))).astype(BF16)


def _diff_attention(qkv, bias, lq1, lk1, lq2, lk2, g, lam_init, batch, seq):
    tile = ATT_TILE
    hl = HEAD_LANES
    nh = DIFF_HEADS
    hps = ATT_HEADS_PER_STEP
    ng = nh // hps
    vec = lambda a: pl.BlockSpec(a.shape, lambda b, h, i: (0, 0))
    return pl.pallas_call(
        functools.partial(_diff_attn_kernel, lam_init=lam_init, tile=tile, heads=hps),
        grid=(batch, ng, seq // tile),
        in_specs=[pl.BlockSpec((1, seq, hps * hl), lambda b, h, i: (b, 0, h)),
                  pl.BlockSpec((1, seq, hps * hl), lambda b, h, i: (b, 0, ng + h)),
                  pl.BlockSpec((1, seq, hps * hl), lambda b, h, i: (b, 0, 2 * ng + h)),
                  pl.BlockSpec((hps, 2, tile, 2 * tile), lambda b, h, i: (h, 0, 0, 0)),
                  vec(lq1), vec(lk1), vec(lq2), vec(lk2), vec(g)],
        out_specs=pl.BlockSpec((1, tile, hps * hl), lambda b, h, i: (b, i, h)),
        out_shape=jax.ShapeDtypeStruct((batch, seq, nh * hl), BF16),
        scratch_shapes=[pltpu.VMEM((hps, seq // tile, hl + ONES_ROWS, tile), BF16),
                        pltpu.VMEM((2, hps, 2 * tile, hl), BF16),
                        pltpu.VMEM((2, hps, tile, 2 * tile), F32),
                        pltpu.VMEM((hps, 1, 2 * tile), F32),
                        pltpu.VMEM((hps, hl + ONES_ROWS, 2 * tile), F32)],
        compiler_params=_cp(("parallel", "parallel", "arbitrary")),
    )(qkv, qkv, qkv, bias, lq1, lk1, lq2, lk2, g)


MLA_STEP_HEADS = 4


def _mla_attn_kernel(q_ref, k_ref, vt_ref, o_ref, vt_scr, s_scr, m_scr, acc_scr, *, tile):
    qi = pl.program_id(2)
    n_q = pl.num_programs(2)
    hl = HEAD_LANES
    vd = MLA_V_DIM
    tq = 2 * tile

    def scores(slot, kj, q_tile):
        rows = pl.ds(pl.multiple_of(kj * tile, tile), tile)
        q_rows = pl.ds(pl.multiple_of(q_tile * tq, tq), tq)
        for h in range(MLA_STEP_HEADS):
            s_scr[slot, h] = _scores_t(k_ref[0, rows, h * hl:(h + 1) * hl], q_ref[0, q_rows, h * hl:(h + 1) * hl])

    @pl.when(qi == 0)
    def _():
        for h in range(MLA_STEP_HEADS):
            for j in range(vt_scr.shape[1]):
                vt_scr[h, j, 0:vd, :] = vt_ref[h * vd:(h + 1) * vd, j * tile:(j + 1) * tile]
                vt_scr[h, j, vd:vd + ONES_ROWS, :] = jnp.ones((ONES_ROWS, tile), BF16)
        scores(0, 0, 0)

    m_scr[...] = jnp.full(m_scr.shape, NEG, F32)
    acc_scr[...] = jnp.zeros(acc_scr.shape, F32)

    def update(slot, kj, key_offset=None):
        for h in range(MLA_STEP_HEADS):
            s = s_scr[slot, h]
            if key_offset is not None:
                key = lax.broadcasted_iota(jnp.int32, s.shape, 0)
                qry = lax.broadcasted_iota(jnp.int32, s.shape, 1)
                s = jnp.where(key + key_offset <= qry, s, NEG)
            _softmax_step_t(s, vt_scr[h, kj], m_scr.at[h], acc_scr.at[h])

    def pair_step(a):
        scores(1, a + 1, qi)
        update(0, a)
        scores(0, a + 2, qi)
        update(1, a + 1)

    def quad_body(j4, carry):
        pair_step(4 * j4)
        pair_step(4 * j4 + 2)
        return carry

    lax.fori_loop(0, qi // 2, quad_body, 0)

    @pl.when(qi % 2 == 1)
    def _():
        pair_step(2 * qi - 2)

    t0 = 2 * qi
    scores(1, t0 + 1, qi)
    update(0, t0, 0)
    update(1, t0 + 1, tile)
    scores(0, 0, jnp.minimum(qi + 1, n_q - 1))
    o = jnp.concatenate([_normalised(acc_scr[h], MLA_V_DIM) for h in range(MLA_STEP_HEADS)], axis=0)
    o_ref[0] = o.T.astype(BF16)


def _mla_attention(q, k, vt, batch, seq):
    tile = ATT_TILE
    tq = 2 * tile
    hl = HEAD_LANES
    nh = MLA_STEP_HEADS
    ng = MLA_HEADS // nh
    v_lanes = nh * MLA_V_DIM
    v_rows = MLA_V_DIM + ONES_ROWS
    return pl.pallas_call(
        functools.partial(_mla_attn_kernel, tile=tile),
        grid=(batch, ng, seq // tq),
        in_specs=[pl.BlockSpec((1, seq, nh * hl), lambda b, h, i: (b, 0, h)),
                  pl.BlockSpec((1, seq, nh * hl), lambda b, h, i: (b, 0, h)),
                  pl.BlockSpec((v_lanes, seq), lambda b, h, i: (h, b))],
        out_specs=pl.BlockSpec((1, tq, v_lanes), lambda b, h, i: (b, i, h)),
        out_shape=jax.ShapeDtypeStruct((batch, seq, MLA_HEADS * MLA_V_DIM), BF16),
        scratch_shapes=[pltpu.VMEM((nh, seq // tile, v_rows, tile), BF16),
                        pltpu.VMEM((2, nh, tile, tq), F32),
                        pltpu.VMEM((nh, 1, tq), F32),
                        pltpu.VMEM((nh, v_rows, tq), F32)],
        compiler_params=_cp(("parallel", "parallel", "arbitrary")),
    )(q, k, vt)


def _mix_ln_kernel(oa_ref, ob_ref, x_ref, wg_ref, bg_ref, wbd_ref, wbm_ref, wo_ref, g_ref, b_ref,
                   xo_ref, xbo_ref, *, alpha):
    d = D_MODEL
    x = x_ref[...]
    xb = x.astype(BF16)
    g_a = _sigmoid(_dot(xb, wg_ref[:, :d]) + bg_ref[:, :d])
    m = g_a * _dot(oa_ref[...], wbd_ref[...])
    g_b = _sigmoid(_dot(xb, wg_ref[:, d:]) + bg_ref[:, d:])
    m = (m + g_b * _dot(ob_ref[...], wbm_ref[...])).astype(BF16)
    y = _layernorm(alpha * x + _dot(m, wo_ref[...]), g_ref[...], b_ref[...])
    xo_ref[...] = y
    xbo_ref[...] = y.astype(BF16)


def _mix_ln(o_a, o_b, x, w_g, b_g, w_bd, w_bm, w_out, g, b, alpha, tm):
    t = x.shape[0]
    row = lambda a: pl.BlockSpec((tm, a.shape[1]), lambda i: (i, 0))
    return pl.pallas_call(
        functools.partial(_mix_ln_kernel, alpha=alpha),
        grid=(t // tm,),
        in_specs=[row(o_a), row(o_b), row(x), _resident(w_g), _resident(b_g), _resident(w_bd),
                  _resident(w_bm), _resident(w_out), _resident(g), _resident(b)],
        out_specs=[row(x), row(o_a)],
        out_shape=[jax.ShapeDtypeStruct((t, D_MODEL), F32), jax.ShapeDtypeStruct((t, D_MODEL), BF16)],
        compiler_params=_cp(("parallel",)),
    )(o_a, o_b, x, w_g, b_g, w_bd, w_bm, w_out, g, b)


def _swiglu_up_kernel(a_ref, w1_ref, w3_ref, o_ref):
    a = a_ref[...]
    u = _dot(a, w1_ref[...])
    o_ref[...] = (u * _sigmoid(u) * _dot(a, w3_ref[...])).astype(BF16)


def _swiglu_up(xb, w1, w3, tm, tn):
    t = xb.shape[0]
    ff = w1.shape[1]
    return pl.pallas_call(
        _swiglu_up_kernel,
        grid=(t // tm, ff // tn),
        in_specs=[pl.BlockSpec((tm, D_MODEL), lambda i, j: (i, 0)),
                  pl.BlockSpec((D_MODEL, tn), lambda i, j: (0, j)),
                  pl.BlockSpec((D_MODEL, tn), lambda i, j: (0, j))],
        out_specs=pl.BlockSpec((tm, tn), lambda i, j: (i, j)),
        out_shape=jax.ShapeDtypeStruct((t, ff), BF16),
        compiler_params=_cp(("parallel", "arbitrary")),
    )(xb, w1, w3)


def _ple_ln(f, x_ref, xb_ref, p_ref, wpg_ref, wpp_ref, g_ref, b_ref, xo_ref, xbo_ref, alpha):
    e = _sigmoid(_dot(xb_ref[...], wpg_ref[...])) * _dot(p_ref[...].astype(BF16), wpp_ref[...])
    y = _layernorm(alpha * x_ref[...] + f + e, g_ref[...], b_ref[...])
    xo_ref[...] = y
    xbo_ref[...] = y.astype(BF16)


def _dense_ln_kernel(h_ref, w2_ref, x_ref, xb_ref, p_ref, wpg_ref, wpp_ref, g_ref, b_ref, xo_ref, xbo_ref, *, alpha):
    _ple_ln(_dot(h_ref[...], w2_ref[...]), x_ref, xb_ref, p_ref, wpg_ref, wpp_ref, g_ref, b_ref, xo_ref, xbo_ref,
            alpha)


def _moe_ln_kernel(y1_ref, y2_ref, gates_ref, x_ref, xb_ref, p_ref, wpg_ref, wpp_ref, g_ref, b_ref, xo_ref, xbo_ref,
                   *, alpha):
    gates = gates_ref[...]
    f = (y1_ref[...].astype(F32) * gates[:, TOP_K:TOP_K + 1]
         + y2_ref[...].astype(F32) * gates[:, TOP_K + 1:TOP_K + 2])
    _ple_ln(f, x_ref, xb_ref, p_ref, wpg_ref, wpp_ref, g_ref, b_ref, xo_ref, xbo_ref, alpha)


def _channel_ln(body, lead, x, xb, p, layer, w_pg, w_pp, g, b, alpha, tm):
    t = x.shape[0]
    row = lambda a: pl.BlockSpec((tm, a.shape[1]), lambda i: (i, 0))
    full = _resident
    tail = [x, xb, p, w_pg, w_pp, g, b]
    return pl.pallas_call(
        functools.partial(body, alpha=alpha),
        grid=(t // tm,),
        in_specs=[row(a) if tiled else full(a) for a, tiled in lead]
        + [row(x), row(xb), pl.BlockSpec((None, tm, p.shape[2]), lambda i: (layer, i, 0)),
           full(w_pg), full(w_pp), full(g), full(b)],
        out_specs=[row(x), row(xb)],
        out_shape=[jax.ShapeDtypeStruct((t, D_MODEL), F32), jax.ShapeDtypeStruct((t, D_MODEL), BF16)],
        compiler_params=_cp(("parallel",)),
    )(*[a for a, _ in lead], *tail)


def _router_kernel(x_ref, whi_ref, wlo_ref, o_ref):
    x = x_ref[...]
    x_hi = x.astype(BF16)
    x_lo = (x - x_hi.astype(F32)).astype(BF16)
    w_hi = whi_ref[...]
    logits = _dot(x_hi, w_hi) + (_dot(x_lo, w_hi) + _dot(x_hi, wlo_ref[...]))
    lane = lax.broadcasted_iota(jnp.int32, logits.shape, 1)
    lg = jnp.where(lane < N_EXPERTS, logits, -jnp.inf)
    v1 = jnp.max(lg, axis=-1, keepdims=True)
    i1 = jnp.min(jnp.where(lg == v1, lane, HEAD_LANES), axis=-1, keepdims=True)
    lg2 = jnp.where(lane == i1, -jnp.inf, lg)
    v2 = jnp.max(lg2, axis=-1, keepdims=True)
    i2 = jnp.min(jnp.where(lg2 == v2, lane, HEAD_LANES), axis=-1, keepdims=True)
    e2 = jnp.exp(v2 - v1)
    g1 = 1.0 / (1.0 + e2)
    g2 = e2 / (1.0 + e2)
    o_ref[...] = jnp.where(lane == 0, i1.astype(F32),
                           jnp.where(lane == 1, i2.astype(F32),
                                     jnp.where(lane == 2, g1, jnp.where(lane == 3, g2, 0.0))))


def _router(x, w_r, tm):
    t = x.shape[0]
    w_hi = w_r.astype(BF16)
    w_lo = (w_r - w_hi.astype(F32)).astype(BF16)
    return pl.pallas_call(
        _router_kernel,
        grid=(t // tm,),
        in_specs=[pl.BlockSpec((tm, D_MODEL), lambda i: (i, 0)), _resident(w_hi), _resident(w_lo)],
        out_specs=pl.BlockSpec((tm, HEAD_LANES), lambda i: (i, 0)),
        out_shape=jax.ShapeDtypeStruct((t, HEAD_LANES), F32),
        compiler_params=_cp(("parallel",)),
    )(x, w_hi, w_lo)


def _expert_changed(be_ref):
    i = pl.program_id(1)
    return jnp.logical_or(i == 0, be_ref[i] != be_ref[jnp.maximum(i - 1, 0)])


def _expert_up_kernel(be_ref, nb_ref, a_ref, w1_ref, w3_ref, *rest):
    o_ref, w1_scr, w3_scr = rest[-3:]

    @pl.when(_expert_changed(be_ref))
    def _():
        w1_scr[...] = w1_ref[0].astype(BF16)
        w3_scr[...] = w3_ref[0].astype(BF16)

    @pl.when(pl.program_id(1) < nb_ref[0])
    def _():
        a = a_ref[...]
        u = _dot(a, w1_scr[...])
        o_ref[...] = (u * _sigmoid(u) * _dot(a, w3_scr[...])).astype(BF16)

    @pl.when(pl.program_id(1) >= nb_ref[0])
    def _():
        o_ref[...] = jnp.zeros(o_ref.shape, BF16)


def _expert_up(a, w1, w3, layer, blk_e, n_used, tm, tn, total_rows, first_block, into):
    rows = a.shape[0]
    ff = w1.shape[3]
    wspec = pl.BlockSpec((None, 1, D_MODEL, tn), lambda j, i, be, nb: (layer, be[i], 0, j))
    chained = [] if into is None else [into]
    grid_spec = pltpu.PrefetchScalarGridSpec(
        num_scalar_prefetch=2,
        grid=(ff // tn, rows // tm),
        in_specs=[pl.BlockSpec((tm, D_MODEL), lambda j, i, be, nb: (i, 0)), wspec, wspec]
        + [pl.BlockSpec(memory_space=pl.ANY) for _ in chained],
        out_specs=pl.BlockSpec((tm, tn), lambda j, i, be, nb: (first_block + i, j)),
        scratch_shapes=[pltpu.VMEM((D_MODEL, tn), BF16), pltpu.VMEM((D_MODEL, tn), BF16)],
    )
    return pl.pallas_call(
        _expert_up_kernel,
        grid_spec=grid_spec,
        out_shape=jax.ShapeDtypeStruct((total_rows, ff), BF16),
        input_output_aliases={5: 0} if chained else {},
        compiler_params=_cp(("arbitrary", "arbitrary")),
    )(blk_e, n_used, a, w1, w3, *chained)


def _expert_down_kernel(be_ref, nb_ref, h_ref, w2_ref, o_ref, w2_scr):
    @pl.when(_expert_changed(be_ref))
    def _():
        w2_scr[...] = w2_ref[0].astype(BF16)

    @pl.when(pl.program_id(1) < nb_ref[0])
    def _():
        o_ref[...] = _dot(h_ref[...], w2_scr[...]).astype(BF16)

    @pl.when(pl.program_id(1) >= nb_ref[0])
    def _():
        o_ref[...] = jnp.zeros(o_ref.shape, BF16)


def _expert_down(h, w2, layer, blk_e, n_used, tm, tn):
    rows, ff = h.shape
    grid_spec = pltpu.PrefetchScalarGridSpec(
        num_scalar_prefetch=2,
        grid=(D_MODEL // tn, rows // tm),
        in_specs=[pl.BlockSpec((tm, ff), lambda j, i, be, nb: (i, 0)),
                  pl.BlockSpec((None, 1, ff, tn), lambda j, i, be, nb: (layer, be[i], 0, j))],
        out_specs=pl.BlockSpec((tm, tn), lambda j, i, be, nb: (i, j)),
        scratch_shapes=[pltpu.VMEM((ff, tn), BF16)],
    )
    return pl.pallas_call(
        _expert_down_kernel,
        grid_spec=grid_spec,
        out_shape=jax.ShapeDtypeStruct((rows, D_MODEL), BF16),
        compiler_params=_cp(("arbitrary", "arbitrary")),
    )(blk_e, n_used, h, w2)


def _moe(x, xb, w_r, w1, w3, w2, layer, tm):
    t = x.shape[0]
    m = t * TOP_K
    gb = GROUP_ROWS
    routed = _router(x, w_r, tm)
    experts = jnp.arange(N_EXPERTS, dtype=jnp.int32)[None, :]
    hot = [(routed[:, k].astype(jnp.int32)[:, None] == experts).astype(jnp.int32) for k in range(TOP_K)]
    both = sum(hot)
    before = jnp.cumsum(both, axis=0) - both
    counts = jnp.sum(both, axis=0)
    padded = ((counts + gb - 1) // gb) * gb
    pend = jnp.cumsum(padded)
    pstart = pend - padded
    dest = [jnp.sum((pstart[None, :] + before) * h, axis=1) for h in hot]
    n_blocks = m // gb + N_EXPERTS
    blk_e = jnp.minimum(jnp.sum((jnp.arange(n_blocks, dtype=jnp.int32)[:, None] * gb >= pend[None, :])
                                .astype(jnp.int32), axis=1), N_EXPERTS - 1)
    n_used = (pend[-1:] // gb).astype(jnp.int32)
    n_rows = n_blocks * gb
    gap = padded - counts
    gap_end = jnp.cumsum(gap)
    i_dummy = jnp.arange(n_rows - m, dtype=jnp.int32)
    in_gap = (i_dummy[:, None] >= (gap_end - gap)[None, :]) & (i_dummy[:, None] < gap_end[None, :])
    dummy_key = jnp.where(i_dummy < gap_end[-1],
                          jnp.sum(jnp.where(in_gap, (pstart + counts - (gap_end - gap))[None, :], 0), axis=1),
                          pend[-1] - gap_end[-1]) + i_dummy
    keys = jnp.concatenate(dest + [dummy_key]).astype(jnp.uint32)
    toks = jnp.concatenate([jnp.arange(t, dtype=jnp.uint32)] * TOP_K + [jnp.zeros((n_rows - m,), jnp.uint32)])
    assert n_rows * t < 2 ** 32
    src_tok = (jnp.sort(keys * t + toks) % t).astype(jnp.int32)
    take = lambda arr, rows: arr.at[rows].get(mode="promise_in_bounds")
    nbc = n_blocks // DISPATCH_CHUNKS
    h = None
    for c in range(DISPATCH_CHUNKS):
        a = take(xb, src_tok[c * nbc * gb:(c + 1) * nbc * gb])
        h = _expert_up(a, w1, w3, layer, blk_e[c * nbc:(c + 1) * nbc], jnp.clip(n_used - c * nbc, 0, nbc),
                       gb, EXPERT_FF // 2, n_rows, c * nbc, h)
    yb = _expert_down(h, w2, layer, blk_e, n_used, gb, D_MODEL)
    return take(yb, dest[0]), take(yb, dest[1]), routed


def _t5_bucket(dist):
    n = jnp.maximum(dist, 0)
    max_exact = REL_BUCKETS // 2
    large = max_exact + (jnp.log(jnp.maximum(n, 1).astype(F32) / max_exact)
                         / math.log(REL_MAX_DIST / max_exact) * (REL_BUCKETS - max_exact)).astype(jnp.int32)
    large = jnp.minimum(large, REL_BUCKETS - 1)
    return jnp.where(n < max_exact, n, large)


def _bias_tiles(table, tile):
    period = 3 * tile
    k = jnp.arange(period, dtype=jnp.int32)
    dist = jnp.where(k < 2 * tile, k, k - period)
    rel = table.astype(F32) - table[REL_BUCKETS - 1].astype(F32)[None, :]
    onehot = _t5_bucket(dist)[:, None] == jnp.arange(REL_BUCKETS, dtype=jnp.int32)[None, :]
    f = jnp.sum(jnp.where(onehot[:, :, None], rel[None], 0.0), axis=1)
    f = jnp.where((dist >= 0)[:, None], f, NEG)
    heads = table.shape[1]
    toep = jnp.tile(f.T, (1, tile))[:, :tile * (period - 1)].reshape(heads, tile, period - 1)
    tiles = jnp.stack([toep[:, :, :tile], toep[:, :, tile:2 * tile]], axis=1)
    return jnp.concatenate([tiles, tiles], axis=-1)


def _rope_tables(positions):
    half = MLA_ROPE_DIM // 2
    b, sq = positions.shape
    inv_freq = ROPE_THETA ** (-jnp.arange(half, dtype=F32) / half)
    per_row = HEAD_LANES // half
    pos = positions.astype(F32).reshape(b * sq // per_row, per_row, 1)
    ang = (pos * inv_freq.reshape(1, 1, half)).reshape(b * sq // per_row, HEAD_LANES)
    c = jnp.cos(ang).reshape(b, sq, half)
    s = jnp.sin(ang).reshape(b, sq, half)
    ones = jnp.ones((b, sq, MLA_NOPE_DIM), F32)
    z_nope = jnp.zeros((b, sq, MLA_NOPE_DIM), F32)
    z_pad = jnp.zeros((b, sq, HEAD_LANES - MLA_NOPE_DIM - MLA_ROPE_DIM), F32)
    cos_t = jnp.concatenate([ones, c, c, z_pad], axis=-1).reshape(b * sq, HEAD_LANES)
    sin_t = jnp.concatenate([z_nope, s, s, z_pad], axis=-1).reshape(b * sq, HEAD_LANES)
    return cos_t, sin_t


def _rot_cols(w):
    half = MLA_ROPE_DIM // 2
    return jnp.concatenate([-w[..., half:], w[..., :half]], axis=-1)


def _mixer_weights(w_in, w_uq, w_ukv):
    d = D_MODEL
    hq = DIFF_HEADS * 2 * DIFF_HEAD_DIM
    o = 0
    w_dq = w_in[:, o:o + hq]; o += hq
    w_dk = w_in[:, o:o + hq]; o += hq
    w_dv = w_in[:, o:o + hq]; o += hq
    w_cq = w_in[:, o:o + MLA_Q_RANK]; o += MLA_Q_RANK
    w_ckv = w_in[:, o:o + MLA_KV_RANK]; o += MLA_KV_RANK
    w_kr = w_in[:, o:o + MLA_ROPE_DIM]; o += MLA_ROPE_DIM
    w_g = w_in[:, o:]
    w_qkv = jnp.concatenate([w_dq, w_dk, w_dv], axis=1).astype(BF16)
    pad = HEAD_LANES - MLA_NOPE_DIM - MLA_ROPE_DIM
    z = lambda n: jnp.zeros((d, n), F32)
    w_c = jnp.concatenate([w_cq, w_ckv, z(MLA_NOPE_DIM), w_kr, z(pad), z(MLA_NOPE_DIM), _rot_cols(w_kr), z(pad)],
                          axis=1).astype(BF16)
    wq = w_uq.reshape(MLA_Q_RANK, MLA_HEADS, MLA_NOPE_DIM + MLA_ROPE_DIM)
    nope, rope = wq[..., :MLA_NOPE_DIM], wq[..., MLA_NOPE_DIM:]
    zq = lambda n: jnp.zeros((MLA_Q_RANK, MLA_HEADS, n), F32)
    plain = jnp.concatenate([nope, rope, zq(pad)], axis=-1).reshape(MLA_Q_RANK, -1)
    rot = jnp.concatenate([zq(MLA_NOPE_DIM), _rot_cols(rope), zq(pad)], axis=-1).reshape(MLA_Q_RANK, -1)
    w_q2 = jnp.concatenate([plain, rot], axis=1).astype(BF16)
    wkv = w_ukv.reshape(MLA_KV_RANK, MLA_HEADS, MLA_NOPE_DIM + MLA_V_DIM)
    zk = jnp.zeros((MLA_KV_RANK, MLA_HEADS, HEAD_LANES - MLA_NOPE_DIM), F32)
    w_k2 = jnp.concatenate([wkv[..., :MLA_NOPE_DIM], zk], axis=-1).reshape(MLA_KV_RANK, -1).astype(BF16)
    w_vt = wkv[..., MLA_NOPE_DIM:].reshape(MLA_KV_RANK, -1).T.astype(BF16)
    return w_qkv, w_c, w_g.astype(BF16), w_q2, w_k2, w_vt


def kernel(x, p, positions, rel_bias_table, w_in, b_gate, lambda_q1, lambda_k1, lambda_q2, lambda_k2,
           diff_subln_g, mla_q_norm_g, w_uq, mla_kv_norm_g, w_ukv, w_branch_diff, w_branch_mla, w_out,
           ln_mix_g, ln_mix_b, dense_w1, dense_w3, dense_w2, router_w, expert_w1, expert_w3, expert_w2,
           w_ple_gate, w_ple_proj, ln_ffn_g, ln_ffn_b):
    batch, seq, d = x.shape
    depth = w_in.shape[0]
    t = batch * seq
    assert d == D_MODEL and seq % (2 * ATT_TILE) == 0 and t % ROW_TILE == 0, (batch, seq, d)
    assert (t * TOP_K // GROUP_ROWS + N_EXPERTS) % DISPATCH_CHUNKS == 0, (batch, seq)
    tm = min(ROW_TILE, t)
    tm_mm = BIG_ROW_TILE if t % BIG_ROW_TILE == 0 else tm
    alpha = (2.0 * depth) ** 0.25
    row = lambda v: v.reshape(1, -1).astype(F32)

    cos_t, sin_t = _rope_tables(positions)
    bias = _bias_tiles(rel_bias_table * LOG2E, ATT_TILE)
    xf = x.reshape(t, d)
    xb = xf
    p_all = p.reshape(depth, t, PLE_DIM)

    for i in range(depth):
        lam_init = 0.8 - 0.6 * math.exp(-0.3 * i)
        w_qkv, w_c, w_g, w_q2, w_k2, w_vt = _mixer_weights(w_in[i], w_uq[i], w_ukv[i])
        qkv = _matmul(xb, w_qkv, BF16, tm_mm, 1024, scaled_tiles=1, scale=DIFF_HEAD_DIM ** -0.5 * LOG2E)
        q_mla, k_mla, vt_mla = _latent(xb, w_c, w_q2, w_k2, w_vt, row(mla_q_norm_g[i]), row(mla_kv_norm_g[i]),
                                       cos_t, sin_t, tm)
        o_a = _diff_attention(qkv.reshape(batch, seq, -1), bias, row(lambda_q1[i]), row(lambda_k1[i]),
                              row(lambda_q2[i]), row(lambda_k2[i]), row(diff_subln_g[i]), lam_init, batch, seq)
        o_b = _mla_attention(q_mla.reshape(batch, seq, -1), k_mla.reshape(batch, seq, -1), vt_mla, batch, seq)
        xf, xb = _mix_ln(o_a.reshape(t, -1), o_b.reshape(t, -1), xf, w_g, row(b_gate[i]),
                         w_branch_diff[i].astype(BF16), w_branch_mla[i].astype(BF16), w_out[i].astype(BF16),
                         row(ln_mix_g[i]), row(ln_mix_b[i]), alpha, tm)
        j = i // 2
        w_pg = w_ple_gate[i].astype(BF16)
        w_pp = w_ple_proj[i].astype(BF16)
        if i % 2 == 0:
            hmid = _swiglu_up(xb, dense_w1[j].astype(BF16), dense_w3[j].astype(BF16), tm_mm, DENSE_FF // 2)
            lead = [(hmid, True), (dense_w2[j].astype(BF16), False)]
            body = _dense_ln_kernel
        else:
            w_r = jnp.concatenate([router_w[j], jnp.zeros((d, HEAD_LANES - N_EXPERTS), F32)], axis=1)
            y1, y2, routed = _moe(xf, xb, w_r, expert_w1, expert_w3, expert_w2, j, tm)
            lead = [(y1, True), (y2, True), (routed, True)]
            body = _moe_ln_kernel
        xf, xb = _channel_ln(body, lead, xf, xb, p_all, i, w_pg, w_pp, row(ln_ffn_g[i]), row(ln_ffn_b[i]), alpha, tm)
    return xf.reshape(batch, seq, d)
```

```python
import functools
import math

import jax
import jax.numpy as jnp
from jax import lax
from jax.experimental import pallas as pl
from jax.experimental.pallas import tpu as pltpu

F32 = jnp.float32
BF16 = jnp.bfloat16

D_MODEL = 1024
PLE_DIM = 256
DIFF_HEADS = 8
DIFF_HEAD_DIM = 64
MLA_HEADS = 8
MLA_Q_RANK = 384
MLA_KV_RANK = 256
MLA_NOPE_DIM = 64
MLA_ROPE_DIM = 32
MLA_V_DIM = 64
ROPE_THETA = 10000.0
REL_BUCKETS = 32
REL_MAX_DIST = 128
DENSE_FF = 2816
N_EXPERTS = 8
TOP_K = 2
EXPERT_FF = 3584
EPS = 1e-5

HEAD_LANES = 128
NEG = -1e30
LOG2E = math.log2(math.e)
ATT_TILE = 256
ATT_HEADS_PER_STEP = 4
ROW_TILE = 512
BIG_ROW_TILE = 1024
GROUP_ROWS = 512
DISPATCH_CHUNKS = 8
VMEM_LIMIT = 52 * 1024 * 1024


def _cp(sem):
    return pltpu.CompilerParams(dimension_semantics=sem, vmem_limit_bytes=VMEM_LIMIT)


def _dot(a, b):
    return jnp.dot(a, b, preferred_element_type=F32)


def _sigmoid(x):
    return 1.0 / (1.0 + jnp.exp(-x))


def _layernorm(r, g, b):
    mu = jnp.mean(r, axis=-1, keepdims=True)
    d = r - mu
    var = jnp.mean(d * d, axis=-1, keepdims=True)
    return d * lax.rsqrt(var + EPS) * g + b


def _rmsnorm(x, g):
    return x * lax.rsqrt(jnp.mean(x * x, axis=-1, keepdims=True) + EPS) * g


def _mm_kernel(a_ref, w_ref, o_ref, *, scaled_tiles, scale):
    acc = _dot(a_ref[...].astype(BF16), w_ref[...])
    if scaled_tiles:
        acc = acc * jnp.where(pl.program_id(1) < scaled_tiles, scale, 1.0)
    o_ref[...] = acc.astype(o_ref.dtype)


def _matmul(a, w, out_dtype, tm, tn, scaled_tiles=0, scale=1.0):
    m, k = a.shape
    n = w.shape[1]
    return pl.pallas_call(
        functools.partial(_mm_kernel, scaled_tiles=scaled_tiles, scale=scale),
        grid=(m // tm, n // tn),
        in_specs=[pl.BlockSpec((tm, k), lambda i, j: (i, 0)),
                  pl.BlockSpec((k, tn), lambda i, j: (0, j))],
        out_specs=pl.BlockSpec((tm, tn), lambda i, j: (i, j)),
        out_shape=jax.ShapeDtypeStruct((m, n), out_dtype),
        compiler_params=_cp(("parallel", "arbitrary")),
    )(a, w)


C_COLS = MLA_Q_RANK + MLA_KV_RANK + 2 * HEAD_LANES


def _resident(a):
    return pl.BlockSpec(a.shape, lambda *_: (0,) * a.ndim, pipeline_mode=pl.Buffered(1))


def _latent_kernel(a_ref, wc_ref, wq_ref, wk_ref, wvt_ref, gq_ref, gkv_ref, cos_ref, sin_ref, q_ref, k_ref, vt_ref, *,
                   scale):
    z = _dot(a_ref[...].astype(BF16), wc_ref[...])
    cq = z[:, :MLA_Q_RANK]
    ckv = z[:, MLA_Q_RANK:MLA_Q_RANK + MLA_KV_RANK]
    kr = z[:, MLA_Q_RANK + MLA_KV_RANK:MLA_Q_RANK + MLA_KV_RANK + HEAD_LANES]
    kr_rot = z[:, MLA_Q_RANK + MLA_KV_RANK + HEAD_LANES:]
    cos, sin = cos_ref[...], sin_ref[...]
    k_rope = kr * cos + kr_rot * sin
    ckv_n = _rmsnorm(ckv, gkv_ref[...]).astype(BF16)
    zk = _dot(ckv_n, wk_ref[...])
    vt_ref[...] = lax.dot_general(wvt_ref[...], ckv_n, (((1,), (1,)), ((), ())),
                                  preferred_element_type=F32).astype(BF16)
    zq = _dot(_rmsnorm(cq, gq_ref[...]).astype(BF16), wq_ref[...])
    c = cos * scale
    s = sin * scale
    hl = HEAD_LANES
    for h in range(MLA_HEADS):
        k_ref[:, h * hl:(h + 1) * hl] = (zk[:, h * hl:(h + 1) * hl] + k_rope).astype(BF16)
        q_ref[:, h * hl:(h + 1) * hl] = (
            zq[:, h * hl:(h + 1) * hl] * c
            + zq[:, (MLA_HEADS + h) * hl:(MLA_HEADS + h + 1) * hl] * s).astype(BF16)


def _latent(xb, w_c, w_q2, w_k2, w_vt, gq, gkv, cos_t, sin_t, tm):
    t = xb.shape[0]
    row = lambda n: pl.BlockSpec((tm, n), lambda i: (i, 0))
    nq = MLA_HEADS * HEAD_LANES
    nv = MLA_HEADS * MLA_V_DIM
    scale = (MLA_NOPE_DIM + MLA_ROPE_DIM) ** -0.5 * LOG2E
    return pl.pallas_call(
        functools.partial(_latent_kernel, scale=scale),
        grid=(t // tm,),
        in_specs=[row(D_MODEL), _resident(w_c), _resident(w_q2), _resident(w_k2), _resident(w_vt), _resident(gq),
                  _resident(gkv), row(HEAD_LANES), row(HEAD_LANES)],
        out_specs=[row(nq), row(nq), pl.BlockSpec((nv, tm), lambda i: (0, i))],
        out_shape=[jax.ShapeDtypeStruct((t, nq), BF16),
                   jax.ShapeDtypeStruct((t, nq), BF16),
                   jax.ShapeDtypeStruct((nv, t), BF16)],
        compiler_params=_cp(("parallel",)),
    )(xb, w_c, w_q2, w_k2, w_vt, gq, gkv, cos_t, sin_t)


ONES_ROWS = 16


def _scores_t(k, q):
    return lax.dot_general(k, q, (((1,), (1,)), ((), ())), preferred_element_type=F32)


def _softmax_step_t(s, vt, m_ref, acc_ref):
    m_old = m_ref[...]
    m_new = jnp.maximum(m_old, jnp.max(s, axis=0, keepdims=True))
    p = jnp.exp2(s - m_new).astype(BF16)
    alpha = jnp.exp2(m_old - m_new)
    acc_ref[...] = alpha * acc_ref[...] + _dot(vt, p)
    m_ref[...] = m_new


def _fill_vt(v_ref, vt_scr, lane0, row0, v_rows, tile):
    for j in range(vt_scr.shape[0]):
        vt = v_ref[0, j * tile:(j + 1) * tile, lane0:lane0 + HEAD_LANES].astype(F32).T
        vt_scr[j, 0:v_rows, :] = vt[row0:row0 + v_rows].astype(BF16)
        vt_scr[j, v_rows:v_rows + ONES_ROWS, :] = jnp.ones((ONES_ROWS, tile), BF16)


def _normalised(acc, v_rows):
    return acc[0:v_rows] * (1.0 / acc[v_rows:v_rows + 1])


def _diff_attn_kernel(q_ref, k_ref, v_ref, bias_ref, lq1_ref, lk1_ref, lq2_ref, lk2_ref, g_ref, o_ref,
                      vt_scr, qcat_scr, s_scr, m_scr, acc_scr, *, lam_init, tile, heads):
    qi = pl.program_id(2)
    n_q = pl.num_programs(2)
    hd = 2 * DIFF_HEAD_DIM
    hl = HEAD_LANES
    cur = qi % 2

    def first_scores(q_tile, slot):
        rows = pl.ds(pl.multiple_of(q_tile * tile, tile), tile)
        for h in range(heads):
            q = q_ref[0, rows, h * hl:(h + 1) * hl]
            lane = lax.broadcasted_iota(jnp.int32, q.shape, 1)
            zero = jnp.zeros_like(q)
            qcat_scr[slot, h] = jnp.concatenate([jnp.where(lane < DIFF_HEAD_DIM, q, zero),
                                                 jnp.where(lane >= DIFF_HEAD_DIM, q, zero)], axis=0)
            s_scr[0, h] = _scores_t(k_ref[0, 0:tile, h * hl:(h + 1) * hl], qcat_scr[slot, h])

    @pl.when(qi == 0)
    def _():
        for h in range(heads):
            _fill_vt(v_ref, vt_scr.at[h], h * hl, 0, hd, tile)
        first_scores(0, 0)

    m_scr[...] = jnp.full(m_scr.shape, NEG, F32)
    acc_scr[...] = jnp.zeros(acc_scr.shape, F32)

    def scores(slot, kj):
        rows = pl.ds(pl.multiple_of(kj * tile, tile), tile)
        for h in range(heads):
            s_scr[slot, h] = _scores_t(k_ref[0, rows, h * hl:(h + 1) * hl], qcat_scr[cur, h])

    def update(slot, kj, bias_idx=None):
        for h in range(heads):
            s = s_scr[slot, h]
            if bias_idx is not None:
                s = s + bias_ref[h, bias_idx]
            _softmax_step_t(s, vt_scr[h, kj], m_scr.at[h], acc_scr.at[h])

    n_far = jnp.maximum(qi - 1, 0)
    quads = n_far // 4

    def pair_step(a):
        scores(1, a + 1)
        update(0, a)
        scores(0, a + 2)
        update(1, a + 1)

    def quad_body(j4, carry):
        pair_step(4 * j4)
        pair_step(4 * j4 + 2)
        return carry

    lax.fori_loop(0, quads, quad_body, 0)
    odd_pair = (n_far - 4 * quads) >= 2

    @pl.when(odd_pair)
    def _():
        pair_step(4 * quads)

    t0 = 4 * quads + jnp.where(odd_pair, 2, 0)

    @pl.when(n_far - t0 == 1)
    def _():
        scores(1, t0 + 1)
        update(0, t0)
        scores(0, t0 + 2)
        update(1, t0 + 1, 1)
        update(0, t0 + 2, 0)

    @pl.when(jnp.logical_and(n_far == t0, qi >= 1))
    def _():
        scores(1, t0 + 1)
        update(0, t0, 1)
        update(1, t0 + 1, 0)

    @pl.when(qi == 0)
    def _():
        update(0, 0, 0)

    first_scores(jnp.minimum(qi + 1, n_q - 1), 1 - cur)

    lam = (jnp.exp(jnp.sum(lq1_ref[...] * lk1_ref[...], axis=-1, keepdims=True))
           - jnp.exp(jnp.sum(lq2_ref[...] * lk2_ref[...], axis=-1, keepdims=True)) + lam_init)
    for h in range(heads):
        acc = acc_scr[h]
        o = _normalised(acc[:, :tile], hd) - lam * _normalised(acc[:, tile:], hd)
        o = o * lax.rsqrt(jnp.mean(o * o, axis=0, keepdims=True) + EPS)
        o_ref[0, :, h * hl:(h + 1) * hl] = (o.T * (g_ref[...] * (1.0 - lam_init))).astype(BF16)


def _diff_attention(qkv, bias, lq1, lk1, lq2, lk2, g, lam_init, batch, seq):
    tile = ATT_TILE
    hl = HEAD_LANES
    nh = DIFF_HEADS
    hps = ATT_HEADS_PER_STEP
    ng = nh // hps
    vec = lambda a: pl.BlockSpec(a.shape, lambda b, h, i: (0, 0))
    return pl.pallas_call(
        functools.partial(_diff_attn_kernel, lam_init=lam_init, tile=tile, heads=hps),
        grid=(batch, ng, seq // tile),
        in_specs=[pl.BlockSpec((1, seq, hps * hl), lambda b, h, i: (b, 0, h)),
                  pl.BlockSpec((1, seq, hps * hl), lambda b, h, i: (b, 0, ng + h)),
                  pl.BlockSpec((1, seq, hps * hl), lambda b, h, i: (b, 0, 2 * ng + h)),
                  pl.BlockSpec((hps, 2, tile, 2 * tile), lambda b, h, i: (h, 0, 0, 0)),
                  vec(lq1), vec(lk1), vec(lq2), vec(lk2), vec(g)],
        out_specs=pl.BlockSpec((1, tile, hps * hl), lambda b, h, i: (b, i, h)),
        out_shape=jax.ShapeDtypeStruct((batch, seq, nh * hl), BF16),
        scratch_shapes=[pltpu.VMEM((hps, seq // tile, hl + ONES_ROWS, tile), BF16),
                        pltpu.VMEM((2, hps, 2 * tile, hl), BF16),
                        pltpu.VMEM((2, hps, tile, 2 * tile), F32),
                        pltpu.VMEM((hps, 1, 2 * tile), F32),
                        pltpu.VMEM((hps, hl + ONES_ROWS, 2 * tile), F32)],
        compiler_params=_cp(("parallel", "parallel", "arbitrary")),
    )(qkv, qkv, qkv, bias, lq1, lk1, lq2, lk2, g)


MLA_STEP_HEADS = 4


def _mla_attn_kernel(q_ref, k_ref, vt_ref, o_ref, vt_scr, s_scr, m_scr, acc_scr, *, tile):
    qi = pl.program_id(2)
    n_q = pl.num_programs(2)
    hl = HEAD_LANES
    vd = MLA_V_DIM
    tq = 2 * tile

    def scores(slot, kj, q_tile):
        rows = pl.ds(pl.multiple_of(kj * tile, tile), tile)
        q_rows = pl.ds(pl.multiple_of(q_tile * tq, tq), tq)
        for h in range(MLA_STEP_HEADS):
            s_scr[slot, h] = _scores_t(k_ref[0, rows, h * hl:(h + 1) * hl], q_ref[0, q_rows, h * hl:(h + 1) * hl])

    @pl.when(qi == 0)
    def _():
        for h in range(MLA_STEP_HEADS):
            for j in range(vt_scr.shape[1]):
                vt_scr[h, j, 0:vd, :] = vt_ref[h * vd:(h + 1) * vd, j * tile:(j + 1) * tile]
                vt_scr[h, j, vd:vd + ONES_ROWS, :] = jnp.ones((ONES_ROWS, tile), BF16)
        scores(0, 0, 0)

    m_scr[...] = jnp.full(m_scr.shape, NEG, F32)
    acc_scr[...] = jnp.zeros(acc_scr.shape, F32)

    def update(slot, kj, key_offset=None):
        for h in range(MLA_STEP_HEADS):
            s = s_scr[slot, h]
            if key_offset is not None:
                key = lax.broadcasted_iota(jnp.int32, s.shape, 0)
                qry = lax.broadcasted_iota(jnp.int32, s.shape, 1)
                s = jnp.where(key + key_offset <= qry, s, NEG)
            _softmax_step_t(s, vt_scr[h, kj], m_scr.at[h], acc_scr.at[h])

    def pair_step(a):
        scores(1, a + 1, qi)
        update(0, a)
        scores(0, a + 2, qi)
        update(1, a + 1)

    def quad_body(j4, carry):
        pair_step(4 * j4)
        pair_step(4 * j4 + 2)
        return carry

    lax.fori_loop(0, qi // 2, quad_body, 0)

    @pl.when(qi % 2 == 1)
    def _():
        pair_step(2 * qi - 2)

    t0 = 2 * qi
    scores(1, t0 + 1, qi)
    update(0, t0, 0)
    update(1, t0 + 1, tile)
    scores(0, 0, jnp.minimum(qi + 1, n_q - 1))
    o = jnp.concatenate([_normalised(acc_scr[h], MLA_V_DIM) for h in range(MLA_STEP_HEADS)], axis=0)
    o_ref[0] = o.T.astype(BF16)


def _mla_attention(q, k, vt, batch, seq):
    tile = ATT_TILE
    tq = 2 * tile
    hl = HEAD_LANES
    nh = MLA_STEP_HEADS
    ng = MLA_HEADS // nh
    v_lanes = nh * MLA_V_DIM
    v_rows = MLA_V_DIM + ONES_ROWS
    return pl.pallas_call(
        functools.partial(_mla_attn_kernel, tile=tile),
        grid=(batch, ng, seq // tq),
        in_specs=[pl.BlockSpec((1, seq, nh * hl), lambda b, h, i: (b, 0, h)),
                  pl.BlockSpec((1, seq, nh * hl), lambda b, h, i: (b, 0, h)),
                  pl.BlockSpec((v_lanes, seq), lambda b, h, i: (h, b))],
        out_specs=pl.BlockSpec((1, tq, v_lanes), lambda b, h, i: (b, i, h)),
        out_shape=jax.ShapeDtypeStruct((batch, seq, MLA_HEADS * MLA_V_DIM), BF16),
        scratch_shapes=[pltpu.VMEM((nh, seq // tile, v_rows, tile), BF16),
                        pltpu.VMEM((2, nh, tile, tq), F32),
                        pltpu.VMEM((nh, 1, tq), F32),
                        pltpu.VMEM((nh, v_rows, tq), F32)],
        compiler_params=_cp(("parallel", "parallel", "arbitrary")),
    )(q, k, vt)


def _mix_ln_kernel(oa_ref, ob_ref, x_ref, wg_ref, bg_ref, wbd_ref, wbm_ref, wo_ref, g_ref, b_ref,
                   xo_ref, xbo_ref, *, alpha):
    d = D_MODEL
    x = x_ref[...]
    xb = x.astype(BF16)
    g_a = _sigmoid(_dot(xb, wg_ref[:, :d]) + bg_ref[:, :d])
    m = g_a * _dot(oa_ref[...], wbd_ref[...])
    g_b = _sigmoid(_dot(xb, wg_ref[:, d:]) + bg_ref[:, d:])
    m = (m + g_b * _dot(ob_ref[...], wbm_ref[...])).astype(BF16)
    y = _layernorm(alpha * x + _dot(m, wo_ref[...]), g_ref[...], b_ref[...])
    xo_ref[...] = y
    xbo_ref[...] = y.astype(BF16)


def _mix_ln(o_a, o_b, x, w_g, b_g, w_bd, w_bm, w_out, g, b, alpha, tm):
    t = x.shape[0]
    row = lambda a: pl.BlockSpec((tm, a.shape[1]), lambda i: (i, 0))
    return pl.pallas_call(
        functools.partial(_mix_ln_kernel, alpha=alpha),
        grid=(t // tm,),
        in_specs=[row(o_a), row(o_b), row(x), _resident(w_g), _resident(b_g), _resident(w_bd),
                  _resident(w_bm), _resident(w_out), _resident(g), _resident(b)],
        out_specs=[row(x), row(o_a)],
        out_shape=[jax.ShapeDtypeStruct((t, D_MODEL), F32), jax.ShapeDtypeStruct((t, D_MODEL), BF16)],
        compiler_params=_cp(("parallel",)),
    )(o_a, o_b, x, w_g, b_g, w_bd, w_bm, w_out, g, b)


def _swiglu_up_kernel(a_ref, w1_ref, w3_ref, o_ref):
    a = a_ref[...]
    u = _dot(a, w1_ref[...])
    o_ref[...] = (u * _sigmoid(u) * _dot(a, w3_ref[...])).astype(BF16)


def _swiglu_up(xb, w1, w3, tm, tn):
    t = xb.shape[0]
    ff = w1.shape[1]
    return pl.pallas_call(
        _swiglu_up_kernel,
        grid=(t // tm, ff // tn),
        in_specs=[pl.BlockSpec((tm, D_MODEL), lambda i, j: (i, 0)),
                  pl.BlockSpec((D_MODEL, tn), lambda i, j: (0, j)),
                  pl.BlockSpec((D_MODEL, tn), lambda i, j: (0, j))],
        out_specs=pl.BlockSpec((tm, tn), lambda i, j: (i, j)),
        out_shape=jax.ShapeDtypeStruct((t, ff), BF16),
        compiler_params=_cp(("parallel", "arbitrary")),
    )(xb, w1, w3)


def _ple_ln(f, x_ref, xb_ref, p_ref, wpg_ref, wpp_ref, g_ref, b_ref, xo_ref, xbo_ref, alpha):
    e = _sigmoid(_dot(xb_ref[...], wpg_ref[...])) * _dot(p_ref[...].astype(BF16), wpp_ref[...])
    y = _layernorm(alpha * x_ref[...] + f + e, g_ref[...], b_ref[...])
    xo_ref[...] = y
    xbo_ref[...] = y.astype(BF16)


def _dense_ln_kernel(h_ref, w2_ref, x_ref, xb_ref, p_ref, wpg_ref, wpp_ref, g_ref, b_ref, xo_ref, xbo_ref, *, alpha):
    _ple_ln(_dot(h_ref[...], w2_ref[...]), x_ref, xb_ref, p_ref, wpg_ref, wpp_ref, g_ref, b_ref, xo_ref, xbo_ref,
            alpha)


def _moe_ln_kernel(y1_ref, y2_ref, gates_ref, x_ref, xb_ref, p_ref, wpg_ref, wpp_ref, g_ref, b_ref, xo_ref, xbo_ref,
                   *, alpha):
    gates = gates_ref[...]
    f = (y1_ref[...].astype(F32) * gates[:, TOP_K:TOP_K + 1]
         + y2_ref[...].astype(F32) * gates[:, TOP_K + 1:TOP_K + 2])
    _ple_ln(f, x_ref, xb_ref, p_ref, wpg_ref, wpp_ref, g_ref, b_ref, xo_ref, xbo_ref, alpha)


def _channel_ln(body, lead, x, xb, p, layer, w_pg, w_pp, g, b, alpha, tm):
    t = x.shape[0]
    row = lambda a: pl.BlockSpec((tm, a.shape[1]), lambda i: (i, 0))
    full = _resident
    tail = [x, xb, p, w_pg, w_pp, g, b]
    return pl.pallas_call(
        functools.partial(body, alpha=alpha),
        grid=(t // tm,),
        in_specs=[row(a) if tiled else full(a) for a, tiled in lead]
        + [row(x), row(xb), pl.BlockSpec((None, tm, p.shape[2]), lambda i: (layer, i, 0)),
           full(w_pg), full(w_pp), full(g), full(b)],
        out_specs=[row(x), row(xb)],
        out_shape=[jax.ShapeDtypeStruct((t, D_MODEL), F32), jax.ShapeDtypeStruct((t, D_MODEL), BF16)],
        compiler_params=_cp(("parallel",)),
    )(*[a for a, _ in lead], *tail)


def _router_kernel(x_ref, whi_ref, wlo_ref, o_ref):
    x = x_ref[...]
    x_hi = x.astype(BF16)
    x_lo = (x - x_hi.astype(F32)).astype(BF16)
    w_hi = whi_ref[...]
    logits = _dot(x_hi, w_hi) + (_dot(x_lo, w_hi) + _dot(x_hi, wlo_ref[...]))
    lane = lax.broadcasted_iota(jnp.int32, logits.shape, 1)
    lg = jnp.where(lane < N_EXPERTS, logits, -jnp.inf)
    v1 = jnp.max(lg, axis=-1, keepdims=True)
    i1 = jnp.min(jnp.where(lg == v1, lane, HEAD_LANES), axis=-1, keepdims=True)
    lg2 = jnp.where(lane == i1, -jnp.inf, lg)
    v2 = jnp.max(lg2, axis=-1, keepdims=True)
    i2 = jnp.min(jnp.where(lg2 == v2, lane, HEAD_LANES), axis=-1, keepdims=True)
    e2 = jnp.exp(v2 - v1)
    g1 = 1.0 / (1.0 + e2)
    g2 = e2 / (1.0 + e2)
    o_ref[...] = jnp.where(lane == 0, i1.astype(F32),
                           jnp.where(lane == 1, i2.astype(F32),
                                     jnp.where(lane == 2, g1, jnp.where(lane == 3, g2, 0.0))))


def _router(x, w_r, tm):
    t = x.shape[0]
    w_hi = w_r.astype(BF16)
    w_lo = (w_r - w_hi.astype(F32)).astype(BF16)
    return pl.pallas_call(
        _router_kernel,
        grid=(t // tm,),
        in_specs=[pl.BlockSpec((tm, D_MODEL), lambda i: (i, 0)), _resident(w_hi), _resident(w_lo)],
        out_specs=pl.BlockSpec((tm, HEAD_LANES), lambda i: (i, 0)),
        out_shape=jax.ShapeDtypeStruct((t, HEAD_LANES), F32),
        compiler_params=_cp(("parallel",)),
    )(x, w_hi, w_lo)


def _expert_changed(be_ref):
    i = pl.program_id(1)
    return jnp.logical_or(i == 0, be_ref[i] != be_ref[jnp.maximum(i - 1, 0)])


def _expert_up_kernel(be_ref, nb_ref, a_ref, w1_ref, w3_ref, *rest):
    o_ref, w1_scr, w3_scr = rest[-3:]

    @pl.when(_expert_changed(be_ref))
    def _():
        w1_scr[...] = w1_ref[0].astype(BF16)
        w3_scr[...] = w3_ref[0].astype(BF16)

    @pl.when(pl.program_id(1) < nb_ref[0])
    def _():
        a = a_ref[...]
        u = _dot(a, w1_scr[...])
        o_ref[...] = (u * _sigmoid(u) * _dot(a, w3_scr[...])).astype(BF16)

    @pl.when(pl.program_id(1) >= nb_ref[0])
    def _():
        o_ref[...] = jnp.zeros(o_ref.shape, BF16)


def _expert_up(a, w1, w3, layer, blk_e, n_used, tm, tn, total_rows, first_block, into):
    rows = a.shape[0]
    ff = w1.shape[3]
    wspec = pl.BlockSpec((None, 1, D_MODEL, tn), lambda j, i, be, nb: (layer, be[i], 0, j))
    chained = [] if into is None else [into]
    grid_spec = pltpu.PrefetchScalarGridSpec(
        num_scalar_prefetch=2,
        grid=(ff // tn, rows // tm),
        in_specs=[pl.BlockSpec((tm, D_MODEL), lambda j, i, be, nb: (i, 0)), wspec, wspec]
        + [pl.BlockSpec(memory_space=pl.ANY) for _ in chained],
        out_specs=pl.BlockSpec((tm, tn), lambda j, i, be, nb: (first_block + i, j)),
        scratch_shapes=[pltpu.VMEM((D_MODEL, tn), BF16), pltpu.VMEM((D_MODEL, tn), BF16)],
    )
    return pl.pallas_call(
        _expert_up_kernel,
        grid_spec=grid_spec,
        out_shape=jax.ShapeDtypeStruct((total_rows, ff), BF16),
        input_output_aliases={5: 0} if chained else {},
        compiler_params=_cp(("arbitrary", "arbitrary")),
    )(blk_e, n_used, a, w1, w3, *chained)


def _expert_down_kernel(be_ref, nb_ref, h_ref, w2_ref, o_ref, w2_scr):
    @pl.when(_expert_changed(be_ref))
    def _():
        w2_scr[...] = w2_ref[0].astype(BF16)

    @pl.when(pl.program_id(1) < nb_ref[0])
    def _():
        o_ref[...] = _dot(h_ref[...], w2_scr[...]).astype(BF16)

    @pl.when(pl.program_id(1) >= nb_ref[0])
    def _():
        o_ref[...] = jnp.zeros(o_ref.shape, BF16)


def _expert_down(h, w2, layer, blk_e, n_used, tm, tn):
    rows, ff = h.shape
    grid_spec = pltpu.PrefetchScalarGridSpec(
        num_scalar_prefetch=2,
        grid=(D_MODEL // tn, rows // tm),
        in_specs=[pl.BlockSpec((tm, ff), lambda j, i, be, nb: (i, 0)),
                  pl.BlockSpec((None, 1, ff, tn), lambda j, i, be, nb: (layer, be[i], 0, j))],
        out_specs=pl.BlockSpec((tm, tn), lambda j, i, be, nb: (i, j)),
        scratch_shapes=[pltpu.VMEM((ff, tn), BF16)],
    )
    return pl.pallas_call(
        _expert_down_kernel,
        grid_spec=grid_spec,
        out_shape=jax.ShapeDtypeStruct((rows, D_MODEL), BF16),
        compiler_params=_cp(("arbitrary", "arbitrary")),
    )(blk_e, n_used, h, w2)


def _moe(x, xb, w_r, w1, w3, w2, layer, tm):
    t = x.shape[0]
    m = t * TOP_K
    gb = GROUP_ROWS
    routed = _router(x, w_r, tm)
    experts = jnp.arange(N_EXPERTS, dtype=jnp.int32)[None, :]
    hot = [(routed[:, k].astype(jnp.int32)[:, None] == experts).astype(jnp.int32) for k in range(TOP_K)]
    both = sum(hot)
    before = jnp.cumsum(both, axis=0) - both
    counts = jnp.sum(both, axis=0)
    padded = ((counts + gb - 1) // gb) * gb
    pend = jnp.cumsum(padded)
    pstart = pend - padded
    dest = [jnp.sum((pstart[None, :] + before) * h, axis=1) for h in hot]
    n_blocks = m // gb + N_EXPERTS
    blk_e = jnp.minimum(jnp.sum((jnp.arange(n_blocks, dtype=jnp.int32)[:, None] * gb >= pend[None, :])
                                .astype(jnp.int32), axis=1), N_EXPERTS - 1)
    n_used = (pend[-1:] // gb).astype(jnp.int32)
    n_rows = n_blocks * gb
    gap = padded - counts
    gap_end = jnp.cumsum(gap)
    i_dummy = jnp.arange(n_rows - m, dtype=jnp.int32)
    in_gap = (i_dummy[:, None] >= (gap_end - gap)[None, :]) & (i_dummy[:, None] < gap_end[None, :])
    dummy_key = jnp.where(i_dummy < gap_end[-1],
                          jnp.sum(jnp.where(in_gap, (pstart + counts - (gap_end - gap))[None, :], 0), axis=1),
                          pend[-1] - gap_end[-1]) + i_dummy
    keys = jnp.concatenate(dest + [dummy_key]).astype(jnp.uint32)
    toks = jnp.concatenate([jnp.arange(t, dtype=jnp.uint32)] * TOP_K + [jnp.zeros((n_rows - m,), jnp.uint32)])
    assert n_rows * t < 2 ** 32
    src_tok = (jnp.sort(keys * t + toks) % t).astype(jnp.int32)
    take = lambda arr, rows: arr.at[rows].get(mode="promise_in_bounds")
    nbc = n_blocks // DISPATCH_CHUNKS
    h = None
    for c in range(DISPATCH_CHUNKS):
        a = take(xb, src_tok[c * nbc * gb:(c + 1) * nbc * gb])
        h = _expert_up(a, w1, w3, layer, blk_e[c * nbc:(c + 1) * nbc], jnp.clip(n_used - c * nbc, 0, nbc),
                       gb, EXPERT_FF // 2, n_rows, c * nbc, h)
    yb = _expert_down(h, w2, layer, blk_e, n_used, gb, D_MODEL)
    return take(yb, dest[0]), take(yb, dest[1]), routed


def _t5_bucket(dist):
    n = jnp.maximum(dist, 0)
    max_exact = REL_BUCKETS // 2
    large = max_exact + (jnp.log(jnp.maximum(n, 1).astype(F32) / max_exact)
                         / math.log(REL_MAX_DIST / max_exact) * (REL_BUCKETS - max_exact)).astype(jnp.int32)
    large = jnp.minimum(large, REL_BUCKETS - 1)
    return jnp.where(n < max_exact, n, large)


def _bias_tiles(table, tile):
    period = 3 * tile
    k = jnp.arange(period, dtype=jnp.int32)
    dist = jnp.where(k < 2 * tile, k, k - period)
    rel = table.astype(F32) - table[REL_BUCKETS - 1].astype(F32)[None, :]
    onehot = _t5_bucket(dist)[:, None] == jnp.arange(REL_BUCKETS, dtype=jnp.int32)[None, :]
    f = jnp.sum(jnp.where(onehot[:, :, None], rel[None], 0.0), axis=1)
    f = jnp.where((dist >= 0)[:, None], f, NEG)
    heads = table.shape[1]
    toep = jnp.tile(f.T, (1, tile))[:, :tile * (period - 1)].reshape(heads, tile, period - 1)
    tiles = jnp.stack([toep[:, :, :tile], toep[:, :, tile:2 * tile]], axis=1)
    return jnp.concatenate([tiles, tiles], axis=-1)


def _rope_tables(positions):
    half = MLA_ROPE_DIM // 2
    b, sq = positions.shape
    inv_freq = ROPE_THETA ** (-jnp.arange(half, dtype=F32) / half)
    per_row = HEAD_LANES // half
    pos = positions.astype(F32).reshape(b * sq // per_row, per_row, 1)
    ang = (pos * inv_freq.reshape(1, 1, half)).reshape(b * sq // per_row, HEAD_LANES)
    c = jnp.cos(ang).reshape(b, sq, half)
    s = jnp.sin(ang).reshape(b, sq, half)
    ones = jnp.ones((b, sq, MLA_NOPE_DIM), F32)
    z_nope = jnp.zeros((b, sq, MLA_NOPE_DIM), F32)
    z_pad = jnp.zeros((b, sq, HEAD_LANES - MLA_NOPE_DIM - MLA_ROPE_DIM), F32)
    cos_t = jnp.concatenate([ones, c, c, z_pad], axis=-1).reshape(b * sq, HEAD_LANES)
    sin_t = jnp.concatenate([z_nope, s, s, z_pad], axis=-1).reshape(b * sq, HEAD_LANES)
    return cos_t, sin_t


def _rot_cols(w):
    half = MLA_ROPE_DIM // 2
    return jnp.concatenate([-w[..., half:], w[..., :half]], axis=-1)


def _mixer_weights(w_in, w_uq, w_ukv):
    d = D_MODEL
    hq = DIFF_HEADS * 2 * DIFF_HEAD_DIM
    o = 0
    w_dq = w_in[:, o:o + hq]; o += hq
    w_dk = w_in[:, o:o + hq]; o += hq
    w_dv = w_in[:, o:o + hq]; o += hq
    w_cq = w_in[:, o:o + MLA_Q_RANK]; o += MLA_Q_RANK
    w_ckv = w_in[:, o:o + MLA_KV_RANK]; o += MLA_KV_RANK
    w_kr = w_in[:, o:o + MLA_ROPE_DIM]; o += MLA_ROPE_DIM
    w_g = w_in[:, o:]
    w_qkv = jnp.concatenate([w_dq, w_dk, w_dv], axis=1).astype(BF16)
    pad = HEAD_LANES - MLA_NOPE_DIM - MLA_ROPE_DIM
    z = lambda n: jnp.zeros((d, n), F32)
    w_c = jnp.concatenate([w_cq, w_ckv, z(MLA_NOPE_DIM), w_kr, z(pad), z(MLA_NOPE_DIM), _rot_cols(w_kr), z(pad)],
                          axis=1).astype(BF16)
    wq = w_uq.reshape(MLA_Q_RANK, MLA_HEADS, MLA_NOPE_DIM + MLA_ROPE_DIM)
    nope, rope = wq[..., :MLA_NOPE_DIM], wq[..., MLA_NOPE_DIM:]
    zq = lambda n: jnp.zeros((MLA_Q_RANK, MLA_HEADS, n), F32)
    plain = jnp.concatenate([nope, rope, zq(pad)], axis=-1).reshape(MLA_Q_RANK, -1)
    rot = jnp.concatenate([zq(MLA_NOPE_DIM), _rot_cols(rope), zq(pad)], axis=-1).reshape(MLA_Q_RANK, -1)
    w_q2 = jnp.concatenate([plain, rot], axis=1).astype(BF16)
    wkv = w_ukv.reshape(MLA_KV_RANK, MLA_HEADS, MLA_NOPE_DIM + MLA_V_DIM)
    zk = jnp.zeros((MLA_KV_RANK, MLA_HEADS, HEAD_LANES - MLA_NOPE_DIM), F32)
    w_k2 = jnp.concatenate([wkv[..., :MLA_NOPE_DIM], zk], axis=-1).reshape(MLA_KV_RANK, -1).astype(BF16)
    w_vt = wkv[..., MLA_NOPE_DIM:].reshape(MLA_KV_RANK, -1).T.astype(BF16)
    return w_qkv, w_c, w_g.astype(BF16), w_q2, w_k2, w_vt


def kernel(x, p, positions, rel_bias_table, w_in, b_gate, lambda_q1, lambda_k1, lambda_q2, lambda_k2,
           diff_subln_g, mla_q_norm_g, w_uq, mla_kv_norm_g, w_ukv, w_branch_diff, w_branch_mla, w_out,
           ln_mix_g, ln_mix_b, dense_w1, dense_w3, dense_w2, router_w, expert_w1, expert_w3, expert_w2,
           w_ple_gate, w_ple_proj, ln_ffn_g, ln_ffn_b):
    batch, seq, d = x.shape
    depth = w_in.shape[0]
    t = batch * seq
    assert d == D_MODEL and seq % (2 * ATT_TILE) == 0 and t % ROW_TILE == 0, (batch, seq, d)
    assert (t * TOP_K // GROUP_ROWS + N_EXPERTS) % DISPATCH_CHUNKS == 0, (batch, seq)
    tm = min(ROW_TILE, t)
    tm_mm = BIG_ROW_TILE if t % BIG_ROW_TILE == 0 else tm
    alpha = (2.0 * depth) ** 0.25
    row = lambda v: v.reshape(1, -1).astype(F32)

    cos_t, sin_t = _rope_tables(positions)
    bias = _bias_tiles(rel_bias_table * LOG2E, ATT_TILE)
    xf = x.reshape(t, d)
    xb = xf
    p_all = p.reshape(depth, t, PLE_DIM)

    for i in range(depth):
        lam_init = 0.8 - 0.6 * math.exp(-0.3 * i)
        w_qkv, w_c, w_g, w_q2, w_k2, w_vt = _mixer_weights(w_in[i], w_uq[i], w_ukv[i])
        qkv = _matmul(xb, w_qkv, BF16, tm_mm, 1024, scaled_tiles=1, scale=DIFF_HEAD_DIM ** -0.5 * LOG2E)
        q_mla, k_mla, vt_mla = _latent(xb, w_c, w_q2, w_k2, w_vt, row(mla_q_norm_g[i]), row(mla_kv_norm_g[i]),
                                       cos_t, sin_t, tm)
        o_a = _diff_attention(qkv.reshape(batch, seq, -1), bias, row(lambda_q1[i]), row(lambda_k1[i]),
                              row(lambda_q2[i]), row(lambda_k2[i]), row(diff_subln_g[i]), lam_init, batch, seq)
        o_b = _mla_attention(q_mla.reshape(batch, seq, -1), k_mla.reshape(batch, seq, -1), vt_mla, batch, seq)
        xf, xb = _mix_ln(o_a.reshape(t, -1), o_b.reshape(t, -1), xf, w_g, row(b_gate[i]),
                         w_branch_diff[i].astype(BF16), w_branch_mla[i].astype(BF16), w_out[i].astype(BF16),
                         row(ln_mix_g[i]), row(ln_mix_b[i]), alpha, tm)
        j = i // 2
        w_pg = w_ple_gate[i].astype(BF16)
        w_pp = w_ple_proj[i].astype(BF16)
        if i % 2 == 0:
            hmid = _swiglu_up(xb, dense_w1[j].astype(BF16), dense_w3[j].astype(BF16), tm_mm, DENSE_FF // 2)
            lead = [(hmid, True), (dense_w2[j].astype(BF16), False)]
            body = _dense_ln_kernel
        else:
            w_r = jnp.concatenate([router_w[j], jnp.zeros((d, HEAD_LANES - N_EXPERTS), F32)], axis=1)
            y1, y2, routed = _moe(xf, xb, w_r, expert_w1, expert_w3, expert_w2, j, tm)
            lead = [(y1, True), (y2, True), (routed, True)]
            body = _moe_ln_kernel
        xf, xb = _channel_ln(body, lead, xf, xb, p_all, i, w_pg, w_pp, row(ln_ffn_g[i]), row(ln_ffn_b[i]), alpha, tm)
    return xf.reshape(batch, seq, d)
```

```python
import functools
import math

import jax
import jax.numpy as jnp
from jax import lax
from jax.experimental import pallas as pl
from jax.experimental.pallas import tpu as pltpu

F32 = jnp.float32
BF16 = jnp.bfloat16

D_MODEL = 1024
PLE_DIM = 256
DIFF_HEADS = 8
DIFF_HEAD_DIM = 64
MLA_HEADS = 8
MLA_Q_RANK = 384
MLA_KV_RANK = 256
MLA_NOPE_DIM = 64
MLA_ROPE_DIM = 32
MLA_V_DIM = 64
ROPE_THETA = 10000.0
REL_BUCKETS = 32
REL_MAX_DIST = 128
DENSE_FF = 2816
N_EXPERTS = 8
TOP_K = 2
EXPERT_FF = 3584
EPS = 1e-5

HEAD_LANES = 128
NEG = -1e30
LOG2E = math.log2(math.e)
ATT_TILE = 256
ATT_HEADS_PER_STEP = 4
ROW_TILE = 512
BIG_ROW_TILE = 1024
GROUP_ROWS = 512
DISPATCH_CHUNKS = 4
VMEM_LIMIT = 52 * 1024 * 1024


def _cp(sem):
    return pltpu.CompilerParams(dimension_semantics=sem, vmem_limit_bytes=VMEM_LIMIT)


def _dot(a, b):
    return jnp.dot(a, b, preferred_element_type=F32)


def _sigmoid(x):
    return 1.0 / (1.0 + jnp.exp(-x))


def _layernorm(r, g, b):
    mu = jnp.mean(r, axis=-1, keepdims=True)
    d = r - mu
    var = jnp.mean(d * d, axis=-1, keepdims=True)
    return d * lax.rsqrt(var + EPS) * g + b


def _rmsnorm(x, g):
    return x * lax.rsqrt(jnp.mean(x * x, axis=-1, keepdims=True) + EPS) * g


def _mm_kernel(a_ref, w_ref, o_ref, *, scaled_tiles, scale):
    acc = _dot(a_ref[...].astype(BF16), w_ref[...])
    if scaled_tiles:
        acc = acc * jnp.where(pl.program_id(1) < scaled_tiles, scale, 1.0)
    o_ref[...] = acc.astype(o_ref.dtype)


def _matmul(a, w, out_dtype, tm, tn, scaled_tiles=0, scale=1.0):
    m, k = a.shape
    n = w.shape[1]
    return pl.pallas_call(
        functools.partial(_mm_kernel, scaled_tiles=scaled_tiles, scale=scale),
        grid=(m // tm, n // tn),
        in_specs=[pl.BlockSpec((tm, k), lambda i, j: (i, 0)),
                  pl.BlockSpec((k, tn), lambda i, j: (0, j))],
        out_specs=pl.BlockSpec((tm, tn), lambda i, j: (i, j)),
        out_shape=jax.ShapeDtypeStruct((m, n), out_dtype),
        compiler_params=_cp(("parallel", "arbitrary")),
    )(a, w)


C_COLS = MLA_Q_RANK + MLA_KV_RANK + 2 * HEAD_LANES


def _resident(a):
    return pl.BlockSpec(a.shape, lambda *_: (0,) * a.ndim, pipeline_mode=pl.Buffered(1))


def _latent_kernel(a_ref, wc_ref, wq_ref, wk_ref, wvt_ref, gq_ref, gkv_ref, cos_ref, sin_ref, q_ref, k_ref, vt_ref, *,
                   scale):
    z = _dot(a_ref[...].astype(BF16), wc_ref[...])
    cq = z[:, :MLA_Q_RANK]
    ckv = z[:, MLA_Q_RANK:MLA_Q_RANK + MLA_KV_RANK]
    kr = z[:, MLA_Q_RANK + MLA_KV_RANK:MLA_Q_RANK + MLA_KV_RANK + HEAD_LANES]
    kr_rot = z[:, MLA_Q_RANK + MLA_KV_RANK + HEAD_LANES:]
    cos, sin = cos_ref[...], sin_ref[...]
    k_rope = kr * cos + kr_rot * sin
    ckv_n = _rmsnorm(ckv, gkv_ref[...]).astype(BF16)
    zk = _dot(ckv_n, wk_ref[...])
    vt_ref[...] = lax.dot_general(wvt_ref[...], ckv_n, (((1,), (1,)), ((), ())),
                                  preferred_element_type=F32).astype(BF16)
    zq = _dot(_rmsnorm(cq, gq_ref[...]).astype(BF16), wq_ref[...])
    c = cos * scale
    s = sin * scale
    hl = HEAD_LANES
    for h in range(MLA_HEADS):
        k_ref[:, h * hl:(h + 1) * hl] = (zk[:, h * hl:(h + 1) * hl] + k_rope).astype(BF16)
        q_ref[:, h * hl:(h + 1) * hl] = (
            zq[:, h * hl:(h + 1) * hl] * c
            + zq[:, (MLA_HEADS + h) * hl:(MLA_HEADS + h + 1) * hl] * s).astype(BF16)


def _latent(xb, w_c, w_q2, w_k2, w_vt, gq, gkv, cos_t, sin_t, tm):
    t = xb.shape[0]
    row = lambda n: pl.BlockSpec((tm, n), lambda i: (i, 0))
    nq = MLA_HEADS * HEAD_LANES
    nv = MLA_HEADS * MLA_V_DIM
    scale = (MLA_NOPE_DIM + MLA_ROPE_DIM) ** -0.5 * LOG2E
    return pl.pallas_call(
        functools.partial(_latent_kernel, scale=scale),
        grid=(t // tm,),
        in_specs=[row(D_MODEL), _resident(w_c), _resident(w_q2), _resident(w_k2), _resident(w_vt), _resident(gq),
                  _resident(gkv), row(HEAD_LANES), row(HEAD_LANES)],
        out_specs=[row(nq), row(nq), pl.BlockSpec((nv, tm), lambda i: (0, i))],
        out_shape=[jax.ShapeDtypeStruct((t, nq), BF16),
                   jax.ShapeDtypeStruct((t, nq), BF16),
                   jax.ShapeDtypeStruct((nv, t), BF16)],
        compiler_params=_cp(("parallel",)),
    )(xb, w_c, w_q2, w_k2, w_vt, gq, gkv, cos_t, sin_t)


ONES_ROWS = 16


def _scores_t(k, q):
    return lax.dot_general(k, q, (((1,), (1,)), ((), ())), preferred_element_type=F32)


def _softmax_step_t(s, vt, m_ref, acc_ref):
    m_old = m_ref[...]
    m_new = jnp.maximum(m_old, jnp.max(s, axis=0, keepdims=True))
    p = jnp.exp2(s - m_new).astype(BF16)
    alpha = jnp.exp2(m_old - m_new)
    acc_ref[...] = alpha * acc_ref[...] + _dot(vt, p)
    m_ref[...] = m_new


def _fill_vt(v_ref, vt_scr, lane0, row0, v_rows, tile):
    for j in range(vt_scr.shape[0]):
        vt = v_ref[0, j * tile:(j + 1) * tile, lane0:lane0 + HEAD_LANES].astype(F32).T
        vt_scr[j, 0:v_rows, :] = vt[row0:row0 + v_rows].astype(BF16)
        vt_scr[j, v_rows:v_rows + ONES_ROWS, :] = jnp.ones((ONES_ROWS, tile), BF16)


def _normalised(acc, v_rows):
    return acc[0:v_rows] * (1.0 / acc[v_rows:v_rows + 1])


def _diff_attn_kernel(q_ref, k_ref, v_ref, bias_ref, lq1_ref, lk1_ref, lq2_ref, lk2_ref, g_ref, o_ref,
                      vt_scr, qcat_scr, s_scr, m_scr, acc_scr, *, lam_init, tile, heads):
    qi = pl.program_id(2)
    n_q = pl.num_programs(2)
    hd = 2 * DIFF_HEAD_DIM
    hl = HEAD_LANES
    cur = qi % 2

    def first_scores(q_tile, slot):
        rows = pl.ds(pl.multiple_of(q_tile * tile, tile), tile)
        for h in range(heads):
            q = q_ref[0, rows, h * hl:(h + 1) * hl]
            lane = lax.broadcasted_iota(jnp.int32, q.shape, 1)
            zero = jnp.zeros_like(q)
            qcat_scr[slot, h] = jnp.concatenate([jnp.where(lane < DIFF_HEAD_DIM, q, zero),
                                                 jnp.where(lane >= DIFF_HEAD_DIM, q, zero)], axis=0)
            s_scr[0, h] = _scores_t(k_ref[0, 0:tile, h * hl:(h + 1) * hl], qcat_scr[slot, h])

    @pl.when(qi == 0)
    def _():
        for h in range(heads):
            _fill_vt(v_ref, vt_scr.at[h], h * hl, 0, hd, tile)
        first_scores(0, 0)

    m_scr[...] = jnp.full(m_scr.shape, NEG, F32)
    acc_scr[...] = jnp.zeros(acc_scr.shape, F32)

    def scores(slot, kj):
        rows = pl.ds(pl.multiple_of(kj * tile, tile), tile)
        for h in range(heads):
            s_scr[slot, h] = _scores_t(k_ref[0, rows, h * hl:(h + 1) * hl], qcat_scr[cur, h])

    def update(slot, kj, bias_idx=None):
        for h in range(heads):
            s = s_scr[slot, h]
            if bias_idx is not None:
                s = s + bias_ref[h, bias_idx]
            _softmax_step_t(s, vt_scr[h, kj], m_scr.at[h], acc_scr.at[h])

    n_far = jnp.maximum(qi - 1, 0)
    quads = n_far // 4

    def pair_step(a):
        scores(1, a + 1)
        update(0, a)
        scores(0, a + 2)
        update(1, a + 1)

    def quad_body(j4, carry):
        pair_step(4 * j4)
        pair_step(4 * j4 + 2)
        return carry

    lax.fori_loop(0, quads, quad_body, 0)
    odd_pair = (n_far - 4 * quads) >= 2

    @pl.when(odd_pair)
    def _():
        pair_step(4 * quads)

    t0 = 4 * quads + jnp.where(odd_pair, 2, 0)

    @pl.when(n_far - t0 == 1)
    def _():
        scores(1, t0 + 1)
        update(0, t0)
        scores(0, t0 + 2)
        update(1, t0 + 1, 1)
        update(0, t0 + 2, 0)

    @pl.when(jnp.logical_and(n_far == t0, qi >= 1))
    def _():
        scores(1, t0 + 1)
        update(0, t0, 1)
        update(1, t0 + 1, 0)

    @pl.when(qi == 0)
    def _():
        update(0, 0, 0)

    first_scores(jnp.minimum(qi + 1, n_q - 1), 1 - cur)

    lam = (jnp.exp(jnp.sum(lq1_ref[...] * lk1_ref[...], axis=-1, keepdims=True))
           - jnp.exp(jnp.sum(lq2_ref[...] * lk2_ref[...], axis=-1, keepdims=True)) + lam_init)
    for h in range(heads):
        acc = acc_scr[h]
        o = _normalised(acc[:, :tile], hd) - lam * _normalised(acc[:, tile:], hd)
        o = o * lax.rsqrt(jnp.mean(o * o, axis=0, keepdims=True) + EPS)
        o_ref[0, :, h * hl:(h + 1) * hl] = (o.T * (g_ref[...] * (1.0 - lam_init))).astype(BF16)


def _diff_attention(qkv, bias, lq1, lk1, lq2, lk2, g, lam_init, batch, seq):
    tile = ATT_TILE
    hl = HEAD_LANES
    nh = DIFF_HEADS
    hps = ATT_HEADS_PER_STEP
    ng = nh // hps
    vec = lambda a: pl.BlockSpec(a.shape, lambda b, h, i: (0, 0))
    return pl.pallas_call(
        functools.partial(_diff_attn_kernel, lam_init=lam_init, tile=tile, heads=hps),
        grid=(batch, ng, seq // tile),
        in_specs=[pl.BlockSpec((1, seq, hps * hl), lambda b, h, i: (b, 0, h)),
                  pl.BlockSpec((1, seq, hps * hl), lambda b, h, i: (b, 0, ng + h)),
                  pl.BlockSpec((1, seq, hps * hl), lambda b, h, i: (b, 0, 2 * ng + h)),
                  pl.BlockSpec((hps, 2, tile, 2 * tile), lambda b, h, i: (h, 0, 0, 0)),
                  vec(lq1), vec(lk1), vec(lq2), vec(lk2), vec(g)],
        out_specs=pl.BlockSpec((1, tile, hps * hl), lambda b, h, i: (b, i, h)),
        out_shape=jax.ShapeDtypeStruct((batch, seq, nh * hl), BF16),
        scratch_shapes=[pltpu.VMEM((hps, seq // tile, hl + ONES_ROWS, tile), BF16),
                        pltpu.VMEM((2, hps, 2 * tile, hl), BF16),
                        pltpu.VMEM((2, hps, tile, 2 * tile), F32),
                        pltpu.VMEM((hps, 1, 2 * tile), F32),
                        pltpu.VMEM((hps, hl + ONES_ROWS, 2 * tile), F32)],
        compiler_params=_cp(("parallel", "parallel", "arbitrary")),
    )(qkv, qkv, qkv, bias, lq1, lk1, lq2, lk2, g)


MLA_STEP_HEADS = 4


def _mla_attn_kernel(q_ref, k_ref, vt_ref, o_ref, vt_scr, s_scr, m_scr, acc_scr, *, tile):
    qi = pl.program_id(2)
    n_q = pl.num_programs(2)
    hl = HEAD_LANES
    vd = MLA_V_DIM
    tq = 2 * tile

    def scores(slot, kj, q_tile):
        rows = pl.ds(pl.multiple_of(kj * tile, tile), tile)
        q_rows = pl.ds(pl.multiple_of(q_tile * tq, tq), tq)
        for h in range(MLA_STEP_HEADS):
            s_scr[slot, h] = _scores_t(k_ref[0, rows, h * hl:(h + 1) * hl], q_ref[0, q_rows, h * hl:(h + 1) * hl])

    @pl.when(qi == 0)
    def _():
        for h in range(MLA_STEP_HEADS):
            for j in range(vt_scr.shape[1]):
                vt_scr[h, j, 0:vd, :] = vt_ref[h * vd:(h + 1) * vd, j * tile:(j + 1) * tile]
                vt_scr[h, j, vd:vd + ONES_ROWS, :] = jnp.ones((ONES_ROWS, tile), BF16)
        scores(0, 0, 0)

    m_scr[...] = jnp.full(m_scr.shape, NEG, F32)
    acc_scr[...] = jnp.zeros(acc_scr.shape, F32)

    def update(slot, kj, key_offset=None):
        for h in range(MLA_STEP_HEADS):
            s = s_scr[slot, h]
            if key_offset is not None:
                key = lax.broadcasted_iota(jnp.int32, s.shape, 0)
                qry = lax.broadcasted_iota(jnp.int32, s.shape, 1)
                s = jnp.where(key + key_offset <= qry, s, NEG)
            _softmax_step_t(s, vt_scr[h, kj], m_scr.at[h], acc_scr.at[h])

    def pair_step(a):
        scores(1, a + 1, qi)
        update(0, a)
        scores(0, a + 2, qi)
        update(1, a + 1)

    def quad_body(j4, carry):
        pair_step(4 * j4)
        pair_step(4 * j4 + 2)
        return carry

    lax.fori_loop(0, qi // 2, quad_body, 0)

    @pl.when(qi % 2 == 1)
    def _():
        pair_step(2 * qi - 2)

    t0 = 2 * qi
    scores(1, t0 + 1, qi)
    update(0, t0, 0)
    update(1, t0 + 1, tile)
    scores(0, 0, jnp.minimum(qi + 1, n_q - 1))
    o = jnp.concatenate([_normalised(acc_scr[h], MLA_V_DIM) for h in range(MLA_STEP_HEADS)], axis=0)
    o_ref[0] = o.T.astype(BF16)


def _mla_attention(q, k, vt, batch, seq):
    tile = ATT_TILE
    tq = 2 * tile
    hl = HEAD_LANES
    nh = MLA_STEP_HEADS
    ng = MLA_HEADS // nh
    v_lanes = nh * MLA_V_DIM
    v_rows = MLA_V_DIM + ONES_ROWS
    return pl.pallas_call(
        functools.partial(_mla_attn_kernel, tile=tile),
        grid=(batch, ng, seq // tq),
        in_specs=[pl.BlockSpec((1, seq, nh * hl), lambda b, h, i: (b, 0, h)),
                  pl.BlockSpec((1, seq, nh * hl), lambda b, h, i: (b, 0, h)),
                  pl.BlockSpec((v_lanes, seq), lambda b, h, i: (h, b))],
        out_specs=pl.BlockSpec((1, tq, v_lanes), lambda b, h, i: (b, i, h)),
        out_shape=jax.ShapeDtypeStruct((batch, seq, MLA_HEADS * MLA_V_DIM), BF16),
        scratch_shapes=[pltpu.VMEM((nh, seq // tile, v_rows, tile), BF16),
                        pltpu.VMEM((2, nh, tile, tq), F32),
                        pltpu.VMEM((nh, 1, tq), F32),
                        pltpu.VMEM((nh, v_rows, tq), F32)],
        compiler_params=_cp(("parallel", "parallel", "arbitrary")),
    )(q, k, vt)


def _mix_ln_kernel(oa_ref, ob_ref, x_ref, wg_ref, bg_ref, wbd_ref, wbm_ref, wo_ref, g_ref, b_ref,
                   xo_ref, xbo_ref, *, alpha):
    d = D_MODEL
    x = x_ref[...]
    xb = x.astype(BF16)
    g_a = _sigmoid(_dot(xb, wg_ref[:, :d]) + bg_ref[:, :d])
    m = g_a * _dot(oa_ref[...], wbd_ref[...])
    g_b = _sigmoid(_dot(xb, wg_ref[:, d:]) + bg_ref[:, d:])
    m = (m + g_b * _dot(ob_ref[...], wbm_ref[...])).astype(BF16)
    y = _layernorm(alpha * x + _dot(m, wo_ref[...]), g_ref[...], b_ref[...])
    xo_ref[...] = y
    xbo_ref[...] = y.astype(BF16)


def _mix_ln(o_a, o_b, x, w_g, b_g, w_bd, w_bm, w_out, g, b, alpha, tm):
    t = x.shape[0]
    row = lambda a: pl.BlockSpec((tm, a.shape[1]), lambda i: (i, 0))
    return pl.pallas_call(
        functools.partial(_mix_ln_kernel, alpha=alpha),
        grid=(t // tm,),
        in_specs=[row(o_a), row(o_b), row(x), _resident(w_g), _resident(b_g), _resident(w_bd),
                  _resident(w_bm), _resident(w_out), _resident(g), _resident(b)],
        out_specs=[row(x), row(o_a)],
        out_shape=[jax.ShapeDtypeStruct((t, D_MODEL), F32), jax.ShapeDtypeStruct((t, D_MODEL), BF16)],
        compiler_params=_cp(("parallel",)),
    )(o_a, o_b, x, w_g, b_g, w_bd, w_bm, w_out, g, b)


def _swiglu_up_kernel(a_ref, w1_ref, w3_ref, o_ref):
    a = a_ref[...]
    u = _dot(a, w1_ref[...])
    o_ref[...] = (u * _sigmoid(u) * _dot(a, w3_ref[...])).astype(BF16)


def _swiglu_up(xb, w1, w3, tm, tn):
    t = xb.shape[0]
    ff = w1.shape[1]
    return pl.pallas_call(
        _swiglu_up_kernel,
        grid=(t // tm, ff // tn),
        in_specs=[pl.BlockSpec((tm, D_MODEL), lambda i, j: (i, 0)),
                  pl.BlockSpec((D_MODEL, tn), lambda i, j: (0, j)),
                  pl.BlockSpec((D_MODEL, tn), lambda i, j: (0, j))],
        out_specs=pl.BlockSpec((tm, tn), lambda i, j: (i, j)),
        out_shape=jax.ShapeDtypeStruct((t, ff), BF16),
        compiler_params=_cp(("parallel", "arbitrary")),
    )(xb, w1, w3)


def _ple_ln(f, x_ref, xb_ref, p_ref, wpg_ref, wpp_ref, g_ref, b_ref, xo_ref, xbo_ref, alpha):
    e = _sigmoid(_dot(xb_ref[...], wpg_ref[...])) * _dot(p_ref[...].astype(BF16), wpp_ref[...])
    y = _layernorm(alpha * x_ref[...] + f + e, g_ref[...], b_ref[...])
    xo_ref[...] = y
    xbo_ref[...] = y.astype(BF16)


def _dense_ln_kernel(h_ref, w2_ref, x_ref, xb_ref, p_ref, wpg_ref, wpp_ref, g_ref, b_ref, xo_ref, xbo_ref, *, alpha):
    _ple_ln(_dot(h_ref[...], w2_ref[...]), x_ref, xb_ref, p_ref, wpg_ref, wpp_ref, g_ref, b_ref, xo_ref, xbo_ref,
            alpha)


def _moe_ln_kernel(y1_ref, y2_ref, gates_ref, x_ref, xb_ref, p_ref, wpg_ref, wpp_ref, g_ref, b_ref, xo_ref, xbo_ref,
                   *, alpha):
    gates = gates_ref[...]
    f = (y1_ref[...].astype(F32) * gates[:, TOP_K:TOP_K + 1]
         + y2_ref[...].astype(F32) * gates[:, TOP_K + 1:TOP_K + 2])
    _ple_ln(f, x_ref, xb_ref, p_ref, wpg_ref, wpp_ref, g_ref, b_ref, xo_ref, xbo_ref, alpha)


def _channel_ln(body, lead, x, xb, p, layer, w_pg, w_pp, g, b, alpha, tm):
    t = x.shape[0]
    row = lambda a: pl.BlockSpec((tm, a.shape[1]), lambda i: (i, 0))
    full = _resident
    tail = [x, xb, p, w_pg, w_pp, g, b]
    return pl.pallas_call(
        functools.partial(body, alpha=alpha),
        grid=(t // tm,),
        in_specs=[row(a) if tiled else full(a) for a, tiled in lead]
        + [row(x), row(xb), pl.BlockSpec((None, tm, p.shape[2]), lambda i: (layer, i, 0)),
           full(w_pg), full(w_pp), full(g), full(b)],
        out_specs=[row(x), row(xb)],
        out_shape=[jax.ShapeDtypeStruct((t, D_MODEL), F32), jax.ShapeDtypeStruct((t, D_MODEL), BF16)],
        compiler_params=_cp(("parallel",)),
    )(*[a for a, _ in lead], *tail)


def _router_kernel(x_ref, whi_ref, wlo_ref, o_ref):
    x = x_ref[...]
    x_hi = x.astype(BF16)
    x_lo = (x - x_hi.astype(F32)).astype(BF16)
    w_hi = whi_ref[...]
    logits = _dot(x_hi, w_hi) + (_dot(x_lo, w_hi) + _dot(x_hi, wlo_ref[...]))
    lane = lax.broadcasted_iota(jnp.int32, logits.shape, 1)
    lg = jnp.where(lane < N_EXPERTS, logits, -jnp.inf)
    v1 = jnp.max(lg, axis=-1, keepdims=True)
    i1 = jnp.min(jnp.where(lg == v1, lane, HEAD_LANES), axis=-1, keepdims=True)
    lg2 = jnp.where(lane == i1, -jnp.inf, lg)
    v2 = jnp.max(lg2, axis=-1, keepdims=True)
    i2 = jnp.min(jnp.where(lg2 == v2, lane, HEAD_LANES), axis=-1, keepdims=True)
    e2 = jnp.exp(v2 - v1)
    g1 = 1.0 / (1.0 + e2)
    g2 = e2 / (1.0 + e2)
    o_ref[...] = jnp.where(lane == 0, i1.astype(F32),
                           jnp.where(lane == 1, i2.astype(F32),
                                     jnp.where(lane == 2, g1, jnp.where(lane == 3, g2, 0.0))))


def _router(x, w_r, tm):
    t = x.shape[0]
    w_hi = w_r.astype(BF16)
    w_lo = (w_r - w_hi.astype(F32)).astype(BF16)
    return pl.pallas_call(
        _router_kernel,
        grid=(t // tm,),
        in_specs=[pl.BlockSpec((tm, D_MODEL), lambda i: (i, 0)), _resident(w_hi), _resident(w_lo)],
        out_specs=pl.BlockSpec((tm, HEAD_LANES), lambda i: (i, 0)),
        out_shape=jax.ShapeDtypeStruct((t, HEAD_LANES), F32),
        compiler_params=_cp(("parallel",)),
    )(x, w_hi, w_lo)


def _expert_changed(be_ref):
    i = pl.program_id(1)
    return jnp.logical_or(i == 0, be_ref[i] != be_ref[jnp.maximum(i - 1, 0)])


def _expert_up_kernel(be_ref, nb_ref, a_ref, w1_ref, w3_ref, *rest):
    o_ref, w1_scr, w3_scr = rest[-3:]

    @pl.when(_expert_changed(be_ref))
    def _():
        w1_scr[...] = w1_ref[0].astype(BF16)
        w3_scr[...] = w3_ref[0].astype(BF16)

    @pl.when(pl.program_id(1) < nb_ref[0])
    def _():
        a = a_ref[...]
        u = _dot(a, w1_scr[...])
        o_ref[...] = (u * _sigmoid(u) * _dot(a, w3_scr[...])).astype(BF16)

    @pl.when(pl.program_id(1) >= nb_ref[0])
    def _():
        o_ref[...] = jnp.zeros(o_ref.shape, BF16)


def _expert_up(a, w1, w3, layer, blk_e, n_used, tm, tn, total_rows, first_block, into):
    rows = a.shape[0]
    ff = w1.shape[3]
    wspec = pl.BlockSpec((None, 1, D_MODEL, tn), lambda j, i, be, nb: (layer, be[i], 0, j))
    chained = [] if into is None else [into]
    grid_spec = pltpu.PrefetchScalarGridSpec(
        num_scalar_prefetch=2,
        grid=(ff // tn, rows // tm),
        in_specs=[pl.BlockSpec((tm, D_MODEL), lambda j, i, be, nb: (i, 0)), wspec, wspec]
        + [pl.BlockSpec(memory_space=pl.ANY) for _ in chained],
        out_specs=pl.BlockSpec((tm, tn), lambda j, i, be, nb: (first_block + i, j)),
        scratch_shapes=[pltpu.VMEM((D_MODEL, tn), BF16), pltpu.VMEM((D_MODEL, tn), BF16)],
    )
    return pl.pallas_call(
        _expert_up_kernel,
        grid_spec=grid_spec,
        out_shape=jax.ShapeDtypeStruct((total_rows, ff), BF16),
        input_output_aliases={5: 0} if chained else {},
        compiler_params=_cp(("arbitrary", "arbitrary")),
    )(blk_e, n_used, a, w1, w3, *chained)


def _expert_down_kernel(be_ref, nb_ref, h_ref, w2_ref, o_ref, w2_scr):
    @pl.when(_expert_changed(be_ref))
    def _():
        w2_scr[...] = w2_ref[0].astype(BF16)

    @pl.when(pl.program_id(1) < nb_ref[0])
    def _():
        o_ref[...] = _dot(h_ref[...], w2_scr[...]).astype(BF16)

    @pl.when(pl.program_id(1) >= nb_ref[0])
    def _():
        o_ref[...] = jnp.zeros(o_ref.shape, BF16)


def _expert_down(h, w2, layer, blk_e, n_used, tm, tn):
    rows, ff = h.shape
    grid_spec = pltpu.PrefetchScalarGridSpec(
        num_scalar_prefetch=2,
        grid=(D_MODEL // tn, rows // tm),
        in_specs=[pl.BlockSpec((tm, ff), lambda j, i, be, nb: (i, 0)),
                  pl.BlockSpec((None, 1, ff, tn), lambda j, i, be, nb: (layer, be[i], 0, j))],
        out_specs=pl.BlockSpec((tm, tn), lambda j, i, be, nb: (i, j)),
        scratch_shapes=[pltpu.VMEM((ff, tn), BF16)],
    )
    return pl.pallas_call(
        _expert_down_kernel,
        grid_spec=grid_spec,
        out_shape=jax.ShapeDtypeStruct((rows, D_MODEL), BF16),
        compiler_params=_cp(("arbitrary", "arbitrary")),
    )(blk_e, n_used, h, w2)


def _moe(x, xb, w_r, w1, w3, w2, layer, tm):
    t = x.shape[0]
    m = t * TOP_K
    gb = GROUP_ROWS
    routed = _router(x, w_r, tm)
    experts = jnp.arange(N_EXPERTS, dtype=jnp.int32)[None, :]
    hot = [(routed[:, k].astype(jnp.int32)[:, None] == experts).astype(jnp.int32) for k in range(TOP_K)]
    both = sum(hot)
    before = jnp.cumsum(both, axis=0) - both
    counts = jnp.sum(both, axis=0)
    padded = ((counts + gb - 1) // gb) * gb
    pend = jnp.cumsum(padded)
    pstart = pend - padded
    dest = [jnp.sum((pstart[None, :] + before) * h, axis=1) for h in hot]
    n_blocks = m // gb + N_EXPERTS
    blk_e = jnp.minimum(jnp.sum((jnp.arange(n_blocks, dtype=jnp.int32)[:, None] * gb >= pend[None, :])
                                .astype(jnp.int32), axis=1), N_EXPERTS - 1)
    n_used = (pend[-1:] // gb).astype(jnp.int32)
    n_rows = n_blocks * gb
    gap = padded - counts
    gap_end = jnp.cumsum(gap)
    i_dummy = jnp.arange(n_rows - m, dtype=jnp.int32)
    in_gap = (i_dummy[:, None] >= (gap_end - gap)[None, :]) & (i_dummy[:, None] < gap_end[None, :])
    dummy_key = jnp.where(i_dummy < gap_end[-1],
                          jnp.sum(jnp.where(in_gap, (pstart + counts - (gap_end - gap))[None, :], 0), axis=1),
                          pend[-1] - gap_end[-1]) + i_dummy
    keys = jnp.concatenate(dest + [dummy_key]).astype(jnp.uint32)
    toks = jnp.concatenate([jnp.arange(t, dtype=jnp.uint32)] * TOP_K + [jnp.zeros((n_rows - m,), jnp.uint32)])
    assert n_rows * t < 2 ** 32
    src_tok = (jnp.sort(keys * t + toks) % t).astype(jnp.int32)
    take = lambda arr, rows: arr.at[rows].get(mode="promise_in_bounds")
    nbc = n_blocks // DISPATCH_CHUNKS
    h = None
    for c in range(DISPATCH_CHUNKS):
        a = take(xb, src_tok[c * nbc * gb:(c + 1) * nbc * gb])
        h = _expert_up(a, w1, w3, layer, blk_e[c * nbc:(c + 1) * nbc], jnp.clip(n_used - c * nbc, 0, nbc),
                       gb, EXPERT_FF // 2, n_rows, c * nbc, h)
    yb = _expert_down(h, w2, layer, blk_e, n_used, gb, D_MODEL)
    return take(yb, dest[0]), take(yb, dest[1]), routed


def _t5_bucket(dist):
    n = jnp.maximum(dist, 0)
    max_exact = REL_BUCKETS // 2
    large = max_exact + (jnp.log(jnp.maximum(n, 1).astype(F32) / max_exact)
                         / math.log(REL_MAX_DIST / max_exact) * (REL_BUCKETS - max_exact)).astype(jnp.int32)
    large = jnp.minimum(large, REL_BUCKETS - 1)
    return jnp.where(n < max_exact, n, large)


def _bias_tiles(table, tile):
    period = 3 * tile
    k = jnp.arange(period, dtype=jnp.int32)
    dist = jnp.where(k < 2 * tile, k, k - period)
    rel = table.astype(F32) - table[REL_BUCKETS - 1].astype(F32)[None, :]
    onehot = _t5_bucket(dist)[:, None] == jnp.arange(REL_BUCKETS, dtype=jnp.int32)[None, :]
    f = jnp.sum(jnp.where(onehot[:, :, None], rel[None], 0.0), axis=1)
    f = jnp.where((dist >= 0)[:, None], f, NEG)
    heads = table.shape[1]
    toep = jnp.tile(f.T, (1, tile))[:, :tile * (period - 1)].reshape(heads, tile, period - 1)
    tiles = jnp.stack([toep[:, :, :tile], toep[:, :, tile:2 * tile]], axis=1)
    return jnp.concatenate([tiles, tiles], axis=-1)


def _rope_tables(positions):
    half = MLA_ROPE_DIM // 2
    b, sq = positions.shape
    inv_freq = ROPE_THETA ** (-jnp.arange(half, dtype=F32) / half)
    per_row = HEAD_LANES // half
    pos = positions.astype(F32).reshape(b * sq // per_row, per_row, 1)
    ang = (pos * inv_freq.reshape(1, 1, half)).reshape(b * sq // per_row, HEAD_LANES)
    c = jnp.cos(ang).reshape(b, sq, half)
    s = jnp.sin(ang).reshape(b, sq, half)
    ones = jnp.ones((b, sq, MLA_NOPE_DIM), F32)
    z_nope = jnp.zeros((b, sq, MLA_NOPE_DIM), F32)
    z_pad = jnp.zeros((b, sq, HEAD_LANES - MLA_NOPE_DIM - MLA_ROPE_DIM), F32)
    cos_t = jnp.concatenate([ones, c, c, z_pad], axis=-1).reshape(b * sq, HEAD_LANES)
    sin_t = jnp.concatenate([z_nope, s, s, z_pad], axis=-1).reshape(b * sq, HEAD_LANES)
    return cos_t, sin_t


def _rot_cols(w):
    half = MLA_ROPE_DIM // 2
    return jnp.concatenate([-w[..., half:], w[..., :half]], axis=-1)


def _mixer_weights(w_in, w_uq, w_ukv):
    d = D_MODEL
    hq = DIFF_HEADS * 2 * DIFF_HEAD_DIM
    o = 0
    w_dq = w_in[:, o:o + hq]; o += hq
    w_dk = w_in[:, o:o + hq]; o += hq
    w_dv = w_in[:, o:o + hq]; o += hq
    w_cq = w_in[:, o:o + MLA_Q_RANK]; o += MLA_Q_RANK
    w_ckv = w_in[:, o:o + MLA_KV_RANK]; o += MLA_KV_RANK
    w_kr = w_in[:, o:o + MLA_ROPE_DIM]; o += MLA_ROPE_DIM
    w_g = w_in[:, o:]
    w_qkv = jnp.concatenate([w_dq, w_dk, w_dv], axis=1).astype(BF16)
    pad = HEAD_LANES - MLA_NOPE_DIM - MLA_ROPE_DIM
    z = lambda n: jnp.zeros((d, n), F32)
    w_c = jnp.concatenate([w_cq, w_ckv, z(MLA_NOPE_DIM), w_kr, z(pad), z(MLA_NOPE_DIM), _rot_cols(w_kr), z(pad)],
                          axis=1).astype(BF16)
    wq = w_uq.reshape(MLA_Q_RANK, MLA_HEADS, MLA_NOPE_DIM + MLA_ROPE_DIM)
    nope, rope = wq[..., :MLA_NOPE_DIM], wq[..., MLA_NOPE_DIM:]
    zq = lambda n: jnp.zeros((MLA_Q_RANK, MLA_HEADS, n), F32)
    plain = jnp.concatenate([nope, rope, zq(pad)], axis=-1).reshape(MLA_Q_RANK, -1)
    rot = jnp.concatenate([zq(MLA_NOPE_DIM), _rot_cols(rope), zq(pad)], axis=-1).reshape(MLA_Q_RANK, -1)
    w_q2 = jnp.concatenate([plain, rot], axis=1).astype(BF16)
    wkv = w_ukv.reshape(MLA_KV_RANK, MLA_HEADS, MLA_NOPE_DIM + MLA_V_DIM)
    zk = jnp.zeros((MLA_KV_RANK, MLA_HEADS, HEAD_LANES - MLA_NOPE_DIM), F32)
    w_k2 = jnp.concatenate([wkv[..., :MLA_NOPE_DIM], zk], axis=-1).reshape(MLA_KV_RANK, -1).astype(BF16)
    w_vt = wkv[..., MLA_NOPE_DIM:].reshape(MLA_KV_RANK, -1).T.astype(BF16)
    return w_qkv, w_c, w_g.astype(BF16), w_q2, w_k2, w_vt


def kernel(x, p, positions, rel_bias_table, w_in, b_gate, lambda_q1, lambda_k1, lambda_q2, lambda_k2,
           diff_subln_g, mla_q_norm_g, w_uq, mla_kv_norm_g, w_ukv, w_branch_diff, w_branch_mla, w_out,
           ln_mix_g, ln_mix_b, dense_w1, dense_w3, dense_w2, router_w, expert_w1, expert_w3, expert_w2,
           w_ple_gate, w_ple_proj, ln_ffn_g, ln_ffn_b):
    batch, seq, d = x.shape
    depth = w_in.shape[0]
    t = batch * seq
    assert d == D_MODEL and seq % (2 * ATT_TILE) == 0 and t % ROW_TILE == 0, (batch, seq, d)
    assert (t * TOP_K // GROUP_ROWS + N_EXPERTS) % DISPATCH_CHUNKS == 0, (batch, seq)
    tm = min(ROW_TILE, t)
    tm_mm = BIG_ROW_TILE if t % BIG_ROW_TILE == 0 else tm
    alpha = (2.0 * depth) ** 0.25
    row = lambda v: v.reshape(1, -1).astype(F32)

    cos_t, sin_t = _rope_tables(positions)
    bias = _bias_tiles(rel_bias_table * LOG2E, ATT_TILE)
    xf = x.reshape(t, d)
    xb = xf
    p_all = p.reshape(depth, t, PLE_DIM)

    for i in range(depth):
        lam_init = 0.8 - 0.6 * math.exp(-0.3 * i)
        w_qkv, w_c, w_g, w_q2, w_k2, w_vt = _mixer_weights(w_in[i], w_uq[i], w_ukv[i])
        qkv = _matmul(xb, w_qkv, BF16, tm_mm, 1024, scaled_tiles=1, scale=DIFF_HEAD_DIM ** -0.5 * LOG2E)
        q_mla, k_mla, vt_mla = _latent(xb, w_c, w_q2, w_k2, w_vt, row(mla_q_norm_g[i]), row(mla_kv_norm_g[i]),
                                       cos_t, sin_t, tm)
        o_a = _diff_attention(qkv.reshape(batch, seq, -1), bias, row(lambda_q1[i]), row(lambda_k1[i]),
                              row(lambda_q2[i]), row(lambda_k2[i]), row(diff_subln_g[i]), lam_init, batch, seq)
        o_b = _mla_attention(q_mla.reshape(batch, seq, -1), k_mla.reshape(batch, seq, -1), vt_mla, batch, seq)
        xf, xb = _mix_ln(o_a.reshape(t, -1), o_b.reshape(t, -1), xf, w_g, row(b_gate[i]),
                         w_branch_diff[i].astype(BF16), w_branch_mla[i].astype(BF16), w_out[i].astype(BF16),
                         row(ln_mix_g[i]), row(ln_mix_b[i]), alpha, tm)
        j = i // 2
        w_pg = w_ple_gate[i].astype(BF16)
        w_pp = w_ple_proj[i].astype(BF16)
        if i % 2 == 0:
            hmid = _swiglu_up(xb, dense_w1[j].astype(BF16), dense_w3[j].astype(BF16), tm_mm, DENSE_FF // 2)
            lead = [(hmid, True), (dense_w2[j].astype(BF16), False)]
            body = _dense_ln_kernel
        else:
            w_r = jnp.concatenate([router_w[j], jnp.zeros((d, HEAD_LANES - N_EXPERTS), F32)], axis=1)
            y1, y2, routed = _moe(xf, xb, w_r, expert_w1, expert_w3, expert_w2, j, tm)
            lead = [(y1, True), (y2, True), (routed, True)]
            body = _moe_ln_kernel
        xf, xb = _channel_ln(body, lead, xf, xb, p_all, i, w_pg, w_pp, row(ln_ffn_g[i]), row(ln_ffn_b[i]), alpha, tm)
    return xf.reshape(batch, seq, d)
```

```python
import functools
import math

import jax
import jax.numpy as jnp
from jax import lax
from jax.experimental import pallas as pl
from jax.experimental.pallas import tpu as pltpu

F32 = jnp.float32
BF16 = jnp.bfloat16

D_MODEL = 1024
PLE_DIM = 256
DIFF_HEADS = 8
DIFF_HEAD_DIM = 64
MLA_HEADS = 8
MLA_Q_RANK = 384
MLA_KV_RANK = 256
MLA_NOPE_DIM = 64
MLA_ROPE_DIM = 32
MLA_V_DIM = 64
ROPE_THETA = 10000.0
REL_BUCKETS = 32
REL_MAX_DIST = 128
DENSE_FF = 2816
N_EXPERTS = 8
TOP_K = 2
EXPERT_FF = 3584
EPS = 1e-5

HEAD_LANES = 128
NEG = -1e30
LOG2E = math.log2(math.e)
ATT_TILE = 256
ATT_HEADS_PER_STEP = 4
ROW_TILE = 512
BIG_ROW_TILE = 1024
GROUP_ROWS = 512
COMBINE_COLS = 2
DISPATCH_CHUNKS = 4
VMEM_LIMIT = 52 * 1024 * 1024


def _cp(sem):
    return pltpu.CompilerParams(dimension_semantics=sem, vmem_limit_bytes=VMEM_LIMIT)


def _dot(a, b):
    return jnp.dot(a, b, preferred_element_type=F32)


def _sigmoid(x):
    return 1.0 / (1.0 + jnp.exp(-x))


def _layernorm(r, g, b):
    mu = jnp.mean(r, axis=-1, keepdims=True)
    d = r - mu
    var = jnp.mean(d * d, axis=-1, keepdims=True)
    return d * lax.rsqrt(var + EPS) * g + b


def _rmsnorm(x, g):
    return x * lax.rsqrt(jnp.mean(x * x, axis=-1, keepdims=True) + EPS) * g


def _mm_kernel(a_ref, w_ref, o_ref, *, scaled_tiles, scale):
    acc = _dot(a_ref[...].astype(BF16), w_ref[...])
    if scaled_tiles:
        acc = acc * jnp.where(pl.program_id(1) < scaled_tiles, scale, 1.0)
    o_ref[...] = acc.astype(o_ref.dtype)


def _matmul(a, w, out_dtype, tm, tn, scaled_tiles=0, scale=1.0):
    m, k = a.shape
    n = w.shape[1]
    return pl.pallas_call(
        functools.partial(_mm_kernel, scaled_tiles=scaled_tiles, scale=scale),
        grid=(m // tm, n // tn),
        in_specs=[pl.BlockSpec((tm, k), lambda i, j: (i, 0)),
                  pl.BlockSpec((k, tn), lambda i, j: (0, j))],
        out_specs=pl.BlockSpec((tm, tn), lambda i, j: (i, j)),
        out_shape=jax.ShapeDtypeStruct((m, n), out_dtype),
        compiler_params=_cp(("parallel", "arbitrary")),
    )(a, w)


C_COLS = MLA_Q_RANK + MLA_KV_RANK + 2 * HEAD_LANES


def _resident(a):
    return pl.BlockSpec(a.shape, lambda *_: (0,) * a.ndim, pipeline_mode=pl.Buffered(1))


def _latent_kernel(a_ref, wc_ref, wq_ref, wk_ref, wvt_ref, gq_ref, gkv_ref, cos_ref, sin_ref, q_ref, k_ref, vt_ref, *,
                   scale):
    z = _dot(a_ref[...].astype(BF16), wc_ref[...])
    cq = z[:, :MLA_Q_RANK]
    ckv = z[:, MLA_Q_RANK:MLA_Q_RANK + MLA_KV_RANK]
    kr = z[:, MLA_Q_RANK + MLA_KV_RANK:MLA_Q_RANK + MLA_KV_RANK + HEAD_LANES]
    kr_rot = z[:, MLA_Q_RANK + MLA_KV_RANK + HEAD_LANES:]
    cos, sin = cos_ref[...], sin_ref[...]
    k_rope = kr * cos + kr_rot * sin
    ckv_n = _rmsnorm(ckv, gkv_ref[...]).astype(BF16)
    zk = _dot(ckv_n, wk_ref[...])
    vt_ref[...] = lax.dot_general(wvt_ref[...], ckv_n, (((1,), (1,)), ((), ())),
                                  preferred_element_type=F32).astype(BF16)
    zq = _dot(_rmsnorm(cq, gq_ref[...]).astype(BF16), wq_ref[...])
    c = cos * scale
    s = sin * scale
    hl = HEAD_LANES
    for h in range(MLA_HEADS):
        k_ref[:, h * hl:(h + 1) * hl] = (zk[:, h * hl:(h + 1) * hl] + k_rope).astype(BF16)
        q_ref[:, h * hl:(h + 1) * hl] = (
            zq[:, h * hl:(h + 1) * hl] * c
            + zq[:, (MLA_HEADS + h) * hl:(MLA_HEADS + h + 1) * hl] * s).astype(BF16)


def _latent(xb, w_c, w_q2, w_k2, w_vt, gq, gkv, cos_t, sin_t, tm):
    t = xb.shape[0]
    row = lambda n: pl.BlockSpec((tm, n), lambda i: (i, 0))
    nq = MLA_HEADS * HEAD_LANES
    nv = MLA_HEADS * MLA_V_DIM
    scale = (MLA_NOPE_DIM + MLA_ROPE_DIM) ** -0.5 * LOG2E
    return pl.pallas_call(
        functools.partial(_latent_kernel, scale=scale),
        grid=(t // tm,),
        in_specs=[row(D_MODEL), _resident(w_c), _resident(w_q2), _resident(w_k2), _resident(w_vt), _resident(gq),
                  _resident(gkv), row(HEAD_LANES), row(HEAD_LANES)],
        out_specs=[row(nq), row(nq), pl.BlockSpec((nv, tm), lambda i: (0, i))],
        out_shape=[jax.ShapeDtypeStruct((t, nq), BF16),
                   jax.ShapeDtypeStruct((t, nq), BF16),
                   jax.ShapeDtypeStruct((nv, t), BF16)],
        compiler_params=_cp(("parallel",)),
    )(xb, w_c, w_q2, w_k2, w_vt, gq, gkv, cos_t, sin_t)


ONES_ROWS = 16


def _scores_t(k, q):
    return lax.dot_general(k, q, (((1,), (1,)), ((), ())), preferred_element_type=F32)


def _softmax_step_t(s, vt, m_ref, acc_ref):
    m_old = m_ref[...]
    m_new = jnp.maximum(m_old, jnp.max(s, axis=0, keepdims=True))
    p = jnp.exp2(s - m_new).astype(BF16)
    alpha = jnp.exp2(m_old - m_new)
    acc_ref[...] = alpha * acc_ref[...] + _dot(vt, p)
    m_ref[...] = m_new


def _fill_vt(v_ref, vt_scr, lane0, row0, v_rows, tile):
    for j in range(vt_scr.shape[0]):
        vt = v_ref[0, j * tile:(j + 1) * tile, lane0:lane0 + HEAD_LANES].astype(F32).T
        vt_scr[j, 0:v_rows, :] = vt[row0:row0 + v_rows].astype(BF16)
        vt_scr[j, v_rows:v_rows + ONES_ROWS, :] = jnp.ones((ONES_ROWS, tile), BF16)


def _normalised(acc, v_rows):
    return acc[0:v_rows] * (1.0 / acc[v_rows:v_rows + 1])


def _diff_attn_kernel(q_ref, k_ref, v_ref, bias_ref, lq1_ref, lk1_ref, lq2_ref, lk2_ref, g_ref, o_ref,
                      vt_scr, qcat_scr, s_scr, m_scr, acc_scr, *, lam_init, tile, heads):
    qi = pl.program_id(2)
    n_q = pl.num_programs(2)
    hd = 2 * DIFF_HEAD_DIM
    hl = HEAD_LANES
    cur = qi % 2

    def first_scores(q_tile, slot):
        rows = pl.ds(pl.multiple_of(q_tile * tile, tile), tile)
        for h in range(heads):
            q = q_ref[0, rows, h * hl:(h + 1) * hl]
            lane = lax.broadcasted_iota(jnp.int32, q.shape, 1)
            zero = jnp.zeros_like(q)
            qcat_scr[slot, h] = jnp.concatenate([jnp.where(lane < DIFF_HEAD_DIM, q, zero),
                                                 jnp.where(lane >= DIFF_HEAD_DIM, q, zero)], axis=0)
            s_scr[0, h] = _scores_t(k_ref[0, 0:tile, h * hl:(h + 1) * hl], qcat_scr[slot, h])

    @pl.when(qi == 0)
    def _():
        for h in range(heads):
            _fill_vt(v_ref, vt_scr.at[h], h * hl, 0, hd, tile)
        first_scores(0, 0)

    m_scr[...] = jnp.full(m_scr.shape, NEG, F32)
    acc_scr[...] = jnp.zeros(acc_scr.shape, F32)

    def scores(slot, kj):
        rows = pl.ds(pl.multiple_of(kj * tile, tile), tile)
        for h in range(heads):
            s_scr[slot, h] = _scores_t(k_ref[0, rows, h * hl:(h + 1) * hl], qcat_scr[cur, h])

    def update(slot, kj, bias_idx=None):
        for h in range(heads):
            s = s_scr[slot, h]
            if bias_idx is not None:
                s = s + bias_ref[h, bias_idx]
            _softmax_step_t(s, vt_scr[h, kj], m_scr.at[h], acc_scr.at[h])

    n_far = jnp.maximum(qi - 1, 0)
    quads = n_far // 4

    def pair_step(a):
        scores(1, a + 1)
        update(0, a)
        scores(0, a + 2)
        update(1, a + 1)

    def quad_body(j4, carry):
        pair_step(4 * j4)
        pair_step(4 * j4 + 2)
        return carry

    lax.fori_loop(0, quads, quad_body, 0)
    odd_pair = (n_far - 4 * quads) >= 2

    @pl.when(odd_pair)
    def _():
        pair_step(4 * quads)

    t0 = 4 * quads + jnp.where(odd_pair, 2, 0)

    @pl.when(n_far - t0 == 1)
    def _():
        scores(1, t0 + 1)
        update(0, t0)
        scores(0, t0 + 2)
        update(1, t0 + 1, 1)
        update(0, t0 + 2, 0)

    @pl.when(jnp.logical_and(n_far == t0, qi >= 1))
    def _():
        scores(1, t0 + 1)
        update(0, t0, 1)
        update(1, t0 + 1, 0)

    @pl.when(qi == 0)
    def _():
        update(0, 0, 0)

    first_scores(jnp.minimum(qi + 1, n_q - 1), 1 - cur)

    lam = (jnp.exp(jnp.sum(lq1_ref[...] * lk1_ref[...], axis=-1, keepdims=True))
           - jnp.exp(jnp.sum(lq2_ref[...] * lk2_ref[...], axis=-1, keepdims=True)) + lam_init)
    for h in range(heads):
        acc = acc_scr[h]
        o = _normalised(acc[:, :tile], hd) - lam * _normalised(acc[:, tile:], hd)
        o = o * lax.rsqrt(jnp.mean(o * o, axis=0, keepdims=True) + EPS)
        o_ref[0, :, h * hl:(h + 1) * hl] = (o.T * (g_ref[...] * (1.0 - lam_init))).astype(BF16)


def _diff_attention(qkv, bias, lq1, lk1, lq2, lk2, g, lam_init, batch, seq):
    tile = ATT_TILE
    hl = HEAD_LANES
    nh = DIFF_HEADS
    hps = ATT_HEADS_PER_STEP
    ng = nh // hps
    vec = lambda a: pl.BlockSpec(a.shape, lambda b, h, i: (0, 0))
    return pl.pallas_call(
        functools.partial(_diff_attn_kernel, lam_init=lam_init, tile=tile, heads=hps),
        grid=(batch, ng, seq // tile),
        in_specs=[pl.BlockSpec((1, seq, hps * hl), lambda b, h, i: (b, 0, h)),
                  pl.BlockSpec((1, seq, hps * hl), lambda b, h, i: (b, 0, ng + h)),
                  pl.BlockSpec((1, seq, hps * hl), lambda b, h, i: (b, 0, 2 * ng + h)),
                  pl.BlockSpec((hps, 2, tile, 2 * tile), lambda b, h, i: (h, 0, 0, 0)),
                  vec(lq1), vec(lk1), vec(lq2), vec(lk2), vec(g)],
        out_specs=pl.BlockSpec((1, tile, hps * hl), lambda b, h, i: (b, i, h)),
        out_shape=jax.ShapeDtypeStruct((batch, seq, nh * hl), BF16),
        scratch_shapes=[pltpu.VMEM((hps, seq // tile, hl + ONES_ROWS, tile), BF16),
                        pltpu.VMEM((2, hps, 2 * tile, hl), BF16),
                        pltpu.VMEM((2, hps, tile, 2 * tile), F32),
                        pltpu.VMEM((hps, 1, 2 * tile), F32),
                        pltpu.VMEM((hps, hl + ONES_ROWS, 2 * tile), F32)],
        compiler_params=_cp(("parallel", "parallel", "arbitrary")),
    )(qkv, qkv, qkv, bias, lq1, lk1, lq2, lk2, g)


MLA_STEP_HEADS = 4


def _mla_attn_kernel(q_ref, k_ref, vt_ref, o_ref, vt_scr, s_scr, m_scr, acc_scr, *, tile):
    qi = pl.program_id(2)
    n_q = pl.num_programs(2)
    hl = HEAD_LANES
    vd = MLA_V_DIM
    tq = 2 * tile

    def scores(slot, kj, q_tile):
        rows = pl.ds(pl.multiple_of(kj * tile, tile), tile)
        q_rows = pl.ds(pl.multiple_of(q_tile * tq, tq), tq)
        for h in range(MLA_STEP_HEADS):
            s_scr[slot, h] = _scores_t(k_ref[0, rows, h * hl:(h + 1) * hl], q_ref[0, q_rows, h * hl:(h + 1) * hl])

    @pl.when(qi == 0)
    def _():
        for h in range(MLA_STEP_HEADS):
            for j in range(vt_scr.shape[1]):
                vt_scr[h, j, 0:vd, :] = vt_ref[h * vd:(h + 1) * vd, j * tile:(j + 1) * tile]
                vt_scr[h, j, vd:vd + ONES_ROWS, :] = jnp.ones((ONES_ROWS, tile), BF16)
        scores(0, 0, 0)

    m_scr[...] = jnp.full(m_scr.shape, NEG, F32)
    acc_scr[...] = jnp.zeros(acc_scr.shape, F32)

    def update(slot, kj, key_offset=None):
        for h in range(MLA_STEP_HEADS):
            s = s_scr[slot, h]
            if key_offset is not None:
                key = lax.broadcasted_iota(jnp.int32, s.shape, 0)
                qry = lax.broadcasted_iota(jnp.int32, s.shape, 1)
                s = jnp.where(key + key_offset <= qry, s, NEG)
            _softmax_step_t(s, vt_scr[h, kj], m_scr.at[h], acc_scr.at[h])

    def pair_step(a):
        scores(1, a + 1, qi)
        update(0, a)
        scores(0, a + 2, qi)
        update(1, a + 1)

    def quad_body(j4, carry):
        pair_step(4 * j4)
        pair_step(4 * j4 + 2)
        return carry

    lax.fori_loop(0, qi // 2, quad_body, 0)

    @pl.when(qi % 2 == 1)
    def _():
        pair_step(2 * qi - 2)

    t0 = 2 * qi
    scores(1, t0 + 1, qi)
    update(0, t0, 0)
    update(1, t0 + 1, tile)
    scores(0, 0, jnp.minimum(qi + 1, n_q - 1))
    o = jnp.concatenate([_normalised(acc_scr[h], MLA_V_DIM) for h in range(MLA_STEP_HEADS)], axis=0)
    o_ref[0] = o.T.astype(BF16)


def _mla_attention(q, k, vt, batch, seq):
    tile = ATT_TILE
    tq = 2 * tile
    hl = HEAD_LANES
    nh = MLA_STEP_HEADS
    ng = MLA_HEADS // nh
    v_lanes = nh * MLA_V_DIM
    v_rows = MLA_V_DIM + ONES_ROWS
    return pl.pallas_call(
        functools.partial(_mla_attn_kernel, tile=tile),
        grid=(batch, ng, seq // tq),
        in_specs=[pl.BlockSpec((1, seq, nh * hl), lambda b, h, i: (b, 0, h)),
                  pl.BlockSpec((1, seq, nh * hl), lambda b, h, i: (b, 0, h)),
                  pl.BlockSpec((v_lanes, seq), lambda b, h, i: (h, b))],
        out_specs=pl.BlockSpec((1, tq, v_lanes), lambda b, h, i: (b, i, h)),
        out_shape=jax.ShapeDtypeStruct((batch, seq, MLA_HEADS * MLA_V_DIM), BF16),
        scratch_shapes=[pltpu.VMEM((nh, seq // tile, v_rows, tile), BF16),
                        pltpu.VMEM((2, nh, tile, tq), F32),
                        pltpu.VMEM((nh, 1, tq), F32),
                        pltpu.VMEM((nh, v_rows, tq), F32)],
        compiler_params=_cp(("parallel", "parallel", "arbitrary")),
    )(q, k, vt)


def _mix_ln_kernel(oa_ref, ob_ref, x_ref, wg_ref, bg_ref, wbd_ref, wbm_ref, wo_ref, g_ref, b_ref,
                   xo_ref, xbo_ref, *, alpha):
    d = D_MODEL
    x = x_ref[...]
    xb = x.astype(BF16)
    g_a = _sigmoid(_dot(xb, wg_ref[:, :d]) + bg_ref[:, :d])
    m = g_a * _dot(oa_ref[...], wbd_ref[...])
    g_b = _sigmoid(_dot(xb, wg_ref[:, d:]) + bg_ref[:, d:])
    m = (m + g_b * _dot(ob_ref[...], wbm_ref[...])).astype(BF16)
    y = _layernorm(alpha * x + _dot(m, wo_ref[...]), g_ref[...], b_ref[...])
    xo_ref[...] = y
    xbo_ref[...] = y.astype(BF16)


def _mix_ln(o_a, o_b, x, w_g, b_g, w_bd, w_bm, w_out, g, b, alpha, tm):
    t = x.shape[0]
    row = lambda a: pl.BlockSpec((tm, a.shape[1]), lambda i: (i, 0))
    return pl.pallas_call(
        functools.partial(_mix_ln_kernel, alpha=alpha),
        grid=(t // tm,),
        in_specs=[row(o_a), row(o_b), row(x), _resident(w_g), _resident(b_g), _resident(w_bd),
                  _resident(w_bm), _resident(w_out), _resident(g), _resident(b)],
        out_specs=[row(x), row(o_a)],
        out_shape=[jax.ShapeDtypeStruct((t, D_MODEL), F32), jax.ShapeDtypeStruct((t, D_MODEL), BF16)],
        compiler_params=_cp(("parallel",)),
    )(o_a, o_b, x, w_g, b_g, w_bd, w_bm, w_out, g, b)


def _swiglu_up_kernel(a_ref, w1_ref, w3_ref, o_ref):
    a = a_ref[...]
    u = _dot(a, w1_ref[...])
    o_ref[...] = (u * _sigmoid(u) * _dot(a, w3_ref[...])).astype(BF16)


def _swiglu_up(xb, w1, w3, tm, tn):
    t = xb.shape[0]
    ff = w1.shape[1]
    return pl.pallas_call(
        _swiglu_up_kernel,
        grid=(t // tm, ff // tn),
        in_specs=[pl.BlockSpec((tm, D_MODEL), lambda i, j: (i, 0)),
                  pl.BlockSpec((D_MODEL, tn), lambda i, j: (0, j)),
                  pl.BlockSpec((D_MODEL, tn), lambda i, j: (0, j))],
        out_specs=pl.BlockSpec((tm, tn), lambda i, j: (i, j)),
        out_shape=jax.ShapeDtypeStruct((t, ff), BF16),
        compiler_params=_cp(("parallel", "arbitrary")),
    )(xb, w1, w3)


def _ple_ln(f, x_ref, xb_ref, p_ref, wpg_ref, wpp_ref, g_ref, b_ref, xo_ref, xbo_ref, alpha):
    e = _sigmoid(_dot(xb_ref[...], wpg_ref[...])) * _dot(p_ref[...].astype(BF16), wpp_ref[...])
    y = _layernorm(alpha * x_ref[...] + f + e, g_ref[...], b_ref[...])
    xo_ref[...] = y
    xbo_ref[...] = y.astype(BF16)


def _dense_ln_kernel(h_ref, w2_ref, x_ref, xb_ref, p_ref, wpg_ref, wpp_ref, g_ref, b_ref, xo_ref, xbo_ref, *, alpha):
    _ple_ln(_dot(h_ref[...], w2_ref[...]), x_ref, xb_ref, p_ref, wpg_ref, wpp_ref, g_ref, b_ref, xo_ref, xbo_ref,
            alpha)


def _moe_ln_kernel(*refs, alpha):
    y_refs, gates_ref, rest = refs[:COMBINE_COLS * TOP_K], refs[COMBINE_COLS * TOP_K], refs[COMBINE_COLS * TOP_K + 1:]
    gates = gates_ref[...]
    f = jnp.concatenate(
        [sum(y_refs[c * TOP_K + k][...].astype(F32) * gates[:, TOP_K + k:TOP_K + k + 1] for k in range(TOP_K))
         for c in range(COMBINE_COLS)], axis=1)
    _ple_ln(f, *rest, alpha)


def _channel_ln(body, lead, x, xb, p, layer, w_pg, w_pp, g, b, alpha, tm):
    t = x.shape[0]
    row = lambda a: pl.BlockSpec((tm, a.shape[1]), lambda i: (i, 0))
    full = _resident
    tail = [x, xb, p, w_pg, w_pp, g, b]
    return pl.pallas_call(
        functools.partial(body, alpha=alpha),
        grid=(t // tm,),
        in_specs=[row(a) if tiled else full(a) for a, tiled in lead]
        + [row(x), row(xb), pl.BlockSpec((None, tm, p.shape[2]), lambda i: (layer, i, 0)),
           full(w_pg), full(w_pp), full(g), full(b)],
        out_specs=[row(x), row(xb)],
        out_shape=[jax.ShapeDtypeStruct((t, D_MODEL), F32), jax.ShapeDtypeStruct((t, D_MODEL), BF16)],
        compiler_params=_cp(("parallel",)),
    )(*[a for a, _ in lead], *tail)


def _router_kernel(x_ref, whi_ref, wlo_ref, o_ref):
    x = x_ref[...]
    x_hi = x.astype(BF16)
    x_lo = (x - x_hi.astype(F32)).astype(BF16)
    w_hi = whi_ref[...]
    logits = _dot(x_hi, w_hi) + (_dot(x_lo, w_hi) + _dot(x_hi, wlo_ref[...]))
    lane = lax.broadcasted_iota(jnp.int32, logits.shape, 1)
    lg = jnp.where(lane < N_EXPERTS, logits, -jnp.inf)
    v1 = jnp.max(lg, axis=-1, keepdims=True)
    i1 = jnp.min(jnp.where(lg == v1, lane, HEAD_LANES), axis=-1, keepdims=True)
    lg2 = jnp.where(lane == i1, -jnp.inf, lg)
    v2 = jnp.max(lg2, axis=-1, keepdims=True)
    i2 = jnp.min(jnp.where(lg2 == v2, lane, HEAD_LANES), axis=-1, keepdims=True)
    e2 = jnp.exp(v2 - v1)
    g1 = 1.0 / (1.0 + e2)
    g2 = e2 / (1.0 + e2)
    o_ref[...] = jnp.where(lane == 0, i1.astype(F32),
                           jnp.where(lane == 1, i2.astype(F32),
                                     jnp.where(lane == 2, g1, jnp.where(lane == 3, g2, 0.0))))


def _router(x, w_r, tm):
    t = x.shape[0]
    w_hi = w_r.astype(BF16)
    w_lo = (w_r - w_hi.astype(F32)).astype(BF16)
    return pl.pallas_call(
        _router_kernel,
        grid=(t // tm,),
        in_specs=[pl.BlockSpec((tm, D_MODEL), lambda i: (i, 0)), _resident(w_hi), _resident(w_lo)],
        out_specs=pl.BlockSpec((tm, HEAD_LANES), lambda i: (i, 0)),
        out_shape=jax.ShapeDtypeStruct((t, HEAD_LANES), F32),
        compiler_params=_cp(("parallel",)),
    )(x, w_hi, w_lo)


def _expert_changed(be_ref):
    i = pl.program_id(1)
    return jnp.logical_or(i == 0, be_ref[i] != be_ref[jnp.maximum(i - 1, 0)])


def _expert_up_kernel(be_ref, nb_ref, a_ref, w1_ref, w3_ref, *rest):
    o_ref, w1_scr, w3_scr = rest[-3:]

    @pl.when(_expert_changed(be_ref))
    def _():
        w1_scr[...] = w1_ref[0].astype(BF16)
        w3_scr[...] = w3_ref[0].astype(BF16)

    @pl.when(pl.program_id(1) < nb_ref[0])
    def _():
        a = a_ref[...]
        u = _dot(a, w1_scr[...])
        o_ref[...] = (u * _sigmoid(u) * _dot(a, w3_scr[...])).astype(BF16)

    @pl.when(pl.program_id(1) >= nb_ref[0])
    def _():
        o_ref[...] = jnp.zeros(o_ref.shape, BF16)


def _expert_up(a, w1, w3, layer, blk_e, n_used, tm, tn, total_rows, first_block, into):
    rows = a.shape[0]
    ff = w1.shape[3]
    wspec = pl.BlockSpec((None, 1, D_MODEL, tn), lambda j, i, be, nb: (layer, be[i], 0, j))
    chained = [] if into is None else [into]
    grid_spec = pltpu.PrefetchScalarGridSpec(
        num_scalar_prefetch=2,
        grid=(ff // tn, rows // tm),
        in_specs=[pl.BlockSpec((tm, D_MODEL), lambda j, i, be, nb: (i, 0)), wspec, wspec]
        + [pl.BlockSpec(memory_space=pl.ANY) for _ in chained],
        out_specs=pl.BlockSpec((tm, tn), lambda j, i, be, nb: (first_block + i, j)),
        scratch_shapes=[pltpu.VMEM((D_MODEL, tn), BF16), pltpu.VMEM((D_MODEL, tn), BF16)],
    )
    return pl.pallas_call(
        _expert_up_kernel,
        grid_spec=grid_spec,
        out_shape=jax.ShapeDtypeStruct((total_rows, ff), BF16),
        input_output_aliases={5: 0} if chained else {},
        compiler_params=_cp(("arbitrary", "arbitrary")),
    )(blk_e, n_used, a, w1, w3, *chained)


def _expert_down_kernel(be_ref, nb_ref, h_ref, w2_ref, o_ref, w2_scr):
    @pl.when(_expert_changed(be_ref))
    def _():
        w2_scr[...] = w2_ref[0].astype(BF16)

    @pl.when(pl.program_id(1) < nb_ref[0])
    def _():
        o_ref[...] = _dot(h_ref[...], w2_scr[...]).astype(BF16)

    @pl.when(pl.program_id(1) >= nb_ref[0])
    def _():
        o_ref[...] = jnp.zeros(o_ref.shape, BF16)


def _expert_down(h, w2, layer, blk_e, n_used, tm, tn, col_block):
    rows, ff = h.shape
    grid_spec = pltpu.PrefetchScalarGridSpec(
        num_scalar_prefetch=2,
        grid=(1, rows // tm),
        in_specs=[pl.BlockSpec((tm, ff), lambda j, i, be, nb: (i, 0)),
                  pl.BlockSpec((None, 1, ff, tn), lambda j, i, be, nb: (layer, be[i], 0, col_block))],
        out_specs=pl.BlockSpec((tm, tn), lambda j, i, be, nb: (i, 0)),
        scratch_shapes=[pltpu.VMEM((ff, tn), BF16)],
    )
    return pl.pallas_call(
        _expert_down_kernel,
        grid_spec=grid_spec,
        out_shape=jax.ShapeDtypeStruct((rows, tn), BF16),
        compiler_params=_cp(("arbitrary", "arbitrary")),
    )(blk_e, n_used, h, w2)


def _moe(x, xb, w_r, w1, w3, w2, layer, tm):
    t = x.shape[0]
    m = t * TOP_K
    gb = GROUP_ROWS
    routed = _router(x, w_r, tm)
    experts = jnp.arange(N_EXPERTS, dtype=jnp.int32)[None, :]
    hot = [(routed[:, k].astype(jnp.int32)[:, None] == experts).astype(jnp.int32) for k in range(TOP_K)]
    both = sum(hot)
    before = jnp.cumsum(both, axis=0) - both
    counts = jnp.sum(both, axis=0)
    padded = ((counts + gb - 1) // gb) * gb
    pend = jnp.cumsum(padded)
    pstart = pend - padded
    dest = [jnp.sum((pstart[None, :] + before) * h, axis=1) for h in hot]
    n_blocks = m // gb + N_EXPERTS
    blk_e = jnp.minimum(jnp.sum((jnp.arange(n_blocks, dtype=jnp.int32)[:, None] * gb >= pend[None, :])
                                .astype(jnp.int32), axis=1), N_EXPERTS - 1)
    n_used = (pend[-1:] // gb).astype(jnp.int32)
    n_rows = n_blocks * gb
    gap = padded - counts
    gap_end = jnp.cumsum(gap)
    i_dummy = jnp.arange(n_rows - m, dtype=jnp.int32)
    in_gap = (i_dummy[:, None] >= (gap_end - gap)[None, :]) & (i_dummy[:, None] < gap_end[None, :])
    dummy_key = jnp.where(i_dummy < gap_end[-1],
                          jnp.sum(jnp.where(in_gap, (pstart + counts - (gap_end - gap))[None, :], 0), axis=1),
                          pend[-1] - gap_end[-1]) + i_dummy
    keys = jnp.concatenate(dest + [dummy_key]).astype(jnp.uint32)
    toks = jnp.concatenate([jnp.arange(t, dtype=jnp.uint32)] * TOP_K + [jnp.zeros((n_rows - m,), jnp.uint32)])
    assert n_rows * t < 2 ** 32
    src_tok = (jnp.sort(keys * t + toks) % t).astype(jnp.int32)
    take = lambda arr, rows: arr.at[rows].get(mode="promise_in_bounds")
    nbc = n_blocks // DISPATCH_CHUNKS
    h = None
    for c in range(DISPATCH_CHUNKS):
        a = take(xb, src_tok[c * nbc * gb:(c + 1) * nbc * gb])
        h = _expert_up(a, w1, w3, layer, blk_e[c * nbc:(c + 1) * nbc], jnp.clip(n_used - c * nbc, 0, nbc),
                       gb, EXPERT_FF // 2, n_rows, c * nbc, h)
    ys = []
    for c in range(COMBINE_COLS):
        yb = _expert_down(h, w2, layer, blk_e, n_used, gb, D_MODEL // COMBINE_COLS, c)
        ys += [take(yb, dest[k]) for k in range(TOP_K)]
    return ys, routed


def _t5_bucket(dist):
    n = jnp.maximum(dist, 0)
    max_exact = REL_BUCKETS // 2
    large = max_exact + (jnp.log(jnp.maximum(n, 1).astype(F32) / max_exact)
                         / math.log(REL_MAX_DIST / max_exact) * (REL_BUCKETS - max_exact)).astype(jnp.int32)
    large = jnp.minimum(large, REL_BUCKETS - 1)
    return jnp.where(n < max_exact, n, large)


def _bias_tiles(table, tile):
    period = 3 * tile
    k = jnp.arange(period, dtype=jnp.int32)
    dist = jnp.where(k < 2 * tile, k, k - period)
    rel = table.astype(F32) - table[REL_BUCKETS - 1].astype(F32)[None, :]
    onehot = _t5_bucket(dist)[:, None] == jnp.arange(REL_BUCKETS, dtype=jnp.int32)[None, :]
    f = jnp.sum(jnp.where(onehot[:, :, None], rel[None], 0.0), axis=1)
    f = jnp.where((dist >= 0)[:, None], f, NEG)
    heads = table.shape[1]
    toep = jnp.tile(f.T, (1, tile))[:, :tile * (period - 1)].reshape(heads, tile, period - 1)
    tiles = jnp.stack([toep[:, :, :tile], toep[:, :, tile:2 * tile]], axis=1)
    return jnp.concatenate([tiles, tiles], axis=-1)


def _rope_tables(positions):
    half = MLA_ROPE_DIM // 2
    b, sq = positions.shape
    inv_freq = ROPE_THETA ** (-jnp.arange(half, dtype=F32) / half)
    per_row = HEAD_LANES // half
    pos = positions.astype(F32).reshape(b * sq // per_row, per_row, 1)
    ang = (pos * inv_freq.reshape(1, 1, half)).reshape(b * sq // per_row, HEAD_LANES)
    c = jnp.cos(ang).reshape(b, sq, half)
    s = jnp.sin(ang).reshape(b, sq, half)
    ones = jnp.ones((b, sq, MLA_NOPE_DIM), F32)
    z_nope = jnp.zeros((b, sq, MLA_NOPE_DIM), F32)
    z_pad = jnp.zeros((b, sq, HEAD_LANES - MLA_NOPE_DIM - MLA_ROPE_DIM), F32)
    cos_t = jnp.concatenate([ones, c, c, z_pad], axis=-1).reshape(b * sq, HEAD_LANES)
    sin_t = jnp.concatenate([z_nope, s, s, z_pad], axis=-1).reshape(b * sq, HEAD_LANES)
    return cos_t, sin_t


def _rot_cols(w):
    half = MLA_ROPE_DIM // 2
    return jnp.concatenate([-w[..., half:], w[..., :half]], axis=-1)


def _mixer_weights(w_in, w_uq, w_ukv):
    d = D_MODEL
    hq = DIFF_HEADS * 2 * DIFF_HEAD_DIM
    o = 0
    w_dq = w_in[:, o:o + hq]; o += hq
    w_dk = w_in[:, o:o + hq]; o += hq
    w_dv = w_in[:, o:o + hq]; o += hq
    w_cq = w_in[:, o:o + MLA_Q_RANK]; o += MLA_Q_RANK
    w_ckv = w_in[:, o:o + MLA_KV_RANK]; o += MLA_KV_RANK
    w_kr = w_in[:, o:o + MLA_ROPE_DIM]; o += MLA_ROPE_DIM
    w_g = w_in[:, o:]
    w_qkv = jnp.concatenate([w_dq, w_dk, w_dv], axis=1).astype(BF16)
    pad = HEAD_LANES - MLA_NOPE_DIM - MLA_ROPE_DIM
    z = lambda n: jnp.zeros((d, n), F32)
    w_c = jnp.concatenate([w_cq, w_ckv, z(MLA_NOPE_DIM), w_kr, z(pad), z(MLA_NOPE_DIM), _rot_cols(w_kr), z(pad)],
                          axis=1).astype(BF16)
    wq = w_uq.reshape(MLA_Q_RANK, MLA_HEADS, MLA_NOPE_DIM + MLA_ROPE_DIM)
    nope, rope = wq[..., :MLA_NOPE_DIM], wq[..., MLA_NOPE_DIM:]
    zq = lambda n: jnp.zeros((MLA_Q_RANK, MLA_HEADS, n), F32)
    plain = jnp.concatenate([nope, rope, zq(pad)], axis=-1).reshape(MLA_Q_RANK, -1)
    rot = jnp.concatenate([zq(MLA_NOPE_DIM), _rot_cols(rope), zq(pad)], axis=-1).reshape(MLA_Q_RANK, -1)
    w_q2 = jnp.concatenate([plain, rot], axis=1).astype(BF16)
    wkv = w_ukv.reshape(MLA_KV_RANK, MLA_HEADS, MLA_NOPE_DIM + MLA_V_DIM)
    zk = jnp.zeros((MLA_KV_RANK, MLA_HEADS, HEAD_LANES - MLA_NOPE_DIM), F32)
    w_k2 = jnp.concatenate([wkv[..., :MLA_NOPE_DIM], zk], axis=-1).reshape(MLA_KV_RANK, -1).astype(BF16)
    w_vt = wkv[..., MLA_NOPE_DIM:].reshape(MLA_KV_RANK, -1).T.astype(BF16)
    return w_qkv, w_c, w_g.astype(BF16), w_q2, w_k2, w_vt


def kernel(x, p, positions, rel_bias_table, w_in, b_gate, lambda_q1, lambda_k1, lambda_q2, lambda_k2,
           diff_subln_g, mla_q_norm_g, w_uq, mla_kv_norm_g, w_ukv, w_branch_diff, w_branch_mla, w_out,
           ln_mix_g, ln_mix_b, dense_w1, dense_w3, dense_w2, router_w, expert_w1, expert_w3, expert_w2,
           w_ple_gate, w_ple_proj, ln_ffn_g, ln_ffn_b):
    batch, seq, d = x.shape
    depth = w_in.shape[0]
    t = batch * seq
    assert d == D_MODEL and seq % (2 * ATT_TILE) == 0 and t % ROW_TILE == 0, (batch, seq, d)
    assert (t * TOP_K // GROUP_ROWS + N_EXPERTS) % DISPATCH_CHUNKS == 0, (batch, seq)
    tm = min(ROW_TILE, t)
    tm_mm = BIG_ROW_TILE if t % BIG_ROW_TILE == 0 else tm
    alpha = (2.0 * depth) ** 0.25
    row = lambda v: v.reshape(1, -1).astype(F32)

    cos_t, sin_t = _rope_tables(positions)
    bias = _bias_tiles(rel_bias_table * LOG2E, ATT_TILE)
    xf = x.reshape(t, d)
    xb = xf
    p_all = p.reshape(depth, t, PLE_DIM)

    for i in range(depth):
        lam_init = 0.8 - 0.6 * math.exp(-0.3 * i)
        w_qkv, w_c, w_g, w_q2, w_k2, w_vt = _mixer_weights(w_in[i], w_uq[i], w_ukv[i])
        qkv = _matmul(xb, w_qkv, BF16, tm_mm, 1024, scaled_tiles=1, scale=DIFF_HEAD_DIM ** -0.5 * LOG2E)
        q_mla, k_mla, vt_mla = _latent(xb, w_c, w_q2, w_k2, w_vt, row(mla_q_norm_g[i]), row(mla_kv_norm_g[i]),
                                       cos_t, sin_t, tm)
        o_a = _diff_attention(qkv.reshape(batch, seq, -1), bias, row(lambda_q1[i]), row(lambda_k1[i]),
                              row(lambda_q2[i]), row(lambda_k2[i]), row(diff_subln_g[i]), lam_init, batch, seq)
        o_b = _mla_attention(q_mla.reshape(batch, seq, -1), k_mla.reshape(batch, seq, -1), vt_mla, batch, seq)
        xf, xb = _mix_ln(o_a.reshape(t, -1), o_b.reshape(t, -1), xf, w_g, row(b_gate[i]),
                         w_branch_diff[i].astype(BF16), w_branch_mla[i].astype(BF16), w_out[i].astype(BF16),
                         row(ln_mix_g[i]), row(ln_mix_b[i]), alpha, tm)
        j = i // 2
        w_pg = w_ple_gate[i].astype(BF16)
        w_pp = w_ple_proj[i].astype(BF16)
        if i % 2 == 0:
            hmid = _swiglu_up(xb, dense_w1[j].astype(BF16), dense_w3[j].astype(BF16), tm_mm, DENSE_FF // 2)
            lead = [(hmid, True), (dense_w2[j].astype(BF16), False)]
            body = _dense_ln_kernel
        else:
            w_r = jnp.concatenate([router_w[j], jnp.zeros((d, HEAD_LANES - N_EXPERTS), F32)], axis=1)
            ys, routed = _moe(xf, xb, w_r, expert_w1, expert_w3, expert_w2, j, tm)
            lead = [(y, True) for y in ys] + [(routed, True)]
            body = _moe_ln_kernel
        xf, xb = _channel_ln(body, lead, xf, xb, p_all, i, w_pg, w_pp, row(ln_ffn_g[i]), row(ln_ffn_b[i]), alpha, tm)
    return xf.reshape(batch, seq, d)
```

```python
import functools
import math

import jax
import jax.numpy as jnp
from jax import lax
from jax.experimental import pallas as pl
from jax.experimental.pallas import tpu as pltpu

F32 = jnp.float32
BF16 = jnp.bfloat16

D_MODEL = 1024
PLE_DIM = 256
DIFF_HEADS = 8
DIFF_HEAD_DIM = 64
MLA_HEADS = 8
MLA_Q_RANK = 384
MLA_KV_RANK = 256
MLA_NOPE_DIM = 64
MLA_ROPE_DIM = 32
MLA_V_DIM = 64
ROPE_THETA = 10000.0
REL_BUCKETS = 32
REL_MAX_DIST = 128
DENSE_FF = 2816
N_EXPERTS = 8
TOP_K = 2
EXPERT_FF = 3584
EPS = 1e-5

HEAD_LANES = 128
NEG = -1e30
LOG2E = math.log2(math.e)
ATT_TILE = 256
ATT_HEADS_PER_STEP = 4
ROW_TILE = 512
BIG_ROW_TILE = 1024
GROUP_ROWS = 512
DISPATCH_CHUNKS = 4
VMEM_LIMIT = 52 * 1024 * 1024


def _cp(sem):
    return pltpu.CompilerParams(dimension_semantics=sem, vmem_limit_bytes=VMEM_LIMIT)


def _dot(a, b):
    return jnp.dot(a, b, preferred_element_type=F32)


def _sigmoid(x):
    return 1.0 / (1.0 + jnp.exp(-x))


def _layernorm(r, g, b):
    mu = jnp.mean(r, axis=-1, keepdims=True)
    d = r - mu
    var = jnp.mean(d * d, axis=-1, keepdims=True)
    return d * lax.rsqrt(var + EPS) * g + b


def _rmsnorm(x, g):
    return x * lax.rsqrt(jnp.mean(x * x, axis=-1, keepdims=True) + EPS) * g


def _mm_kernel(a_ref, w_ref, o_ref, *, scaled_tiles, scale):
    acc = _dot(a_ref[...].astype(BF16), w_ref[...])
    if scaled_tiles:
        acc = acc * jnp.where(pl.program_id(1) < scaled_tiles, scale, 1.0)
    o_ref[...] = acc.astype(o_ref.dtype)


def _matmul(a, w, out_dtype, tm, tn, scaled_tiles=0, scale=1.0):
    m, k = a.shape
    n = w.shape[1]
    return pl.pallas_call(
        functools.partial(_mm_kernel, scaled_tiles=scaled_tiles, scale=scale),
        grid=(m // tm, n // tn),
        in_specs=[pl.BlockSpec((tm, k), lambda i, j: (i, 0)),
                  pl.BlockSpec((k, tn), lambda i, j: (0, j))],
        out_specs=pl.BlockSpec((tm, tn), lambda i, j: (i, j)),
        out_shape=jax.ShapeDtypeStruct((m, n), out_dtype),
        compiler_params=_cp(("parallel", "arbitrary")),
    )(a, w)


C_COLS = MLA_Q_RANK + MLA_KV_RANK + 2 * HEAD_LANES


def _resident(a):
    return pl.BlockSpec(a.shape, lambda *_: (0,) * a.ndim, pipeline_mode=pl.Buffered(1))


def _latent_kernel(a_ref, wc_ref, wq_ref, wk_ref, wvt_ref, gq_ref, gkv_ref, cos_ref, sin_ref, q_ref, k_ref, vt_ref, *,
                   scale):
    z = _dot(a_ref[...].astype(BF16), wc_ref[...])
    cq = z[:, :MLA_Q_RANK]
    ckv = z[:, MLA_Q_RANK:MLA_Q_RANK + MLA_KV_RANK]
    kr = z[:, MLA_Q_RANK + MLA_KV_RANK:MLA_Q_RANK + MLA_KV_RANK + HEAD_LANES]
    kr_rot = z[:, MLA_Q_RANK + MLA_KV_RANK + HEAD_LANES:]
    cos, sin = cos_ref[...], sin_ref[...]
    k_rope = kr * cos + kr_rot * sin
    ckv_n = _rmsnorm(ckv, gkv_ref[...]).astype(BF16)
    zk = _dot(ckv_n, wk_ref[...])
    vt_ref[...] = lax.dot_general(wvt_ref[...], ckv_n, (((1,), (1,)), ((), ())),
                                  preferred_element_type=F32).astype(BF16)
    zq = _dot(_rmsnorm(cq, gq_ref[...]).astype(BF16), wq_ref[...])
    c = cos * scale
    s = sin * scale
    hl = HEAD_LANES
    for h in range(MLA_HEADS):
        k_ref[:, h * hl:(h + 1) * hl] = (zk[:, h * hl:(h + 1) * hl] + k_rope).astype(BF16)
        q_ref[:, h * hl:(h + 1) * hl] = (
            zq[:, h * hl:(h + 1) * hl] * c
            + zq[:, (MLA_HEADS + h) * hl:(MLA_HEADS + h + 1) * hl] * s).astype(BF16)


def _latent(xb, w_c, w_q2, w_k2, w_vt, gq, gkv, cos_t, sin_t, tm):
    t = xb.shape[0]
    row = lambda n: pl.BlockSpec((tm, n), lambda i: (i, 0))
    nq = MLA_HEADS * HEAD_LANES
    nv = MLA_HEADS * MLA_V_DIM
    scale = (MLA_NOPE_DIM + MLA_ROPE_DIM) ** -0.5 * LOG2E
    return pl.pallas_call(
        functools.partial(_latent_kernel, scale=scale),
        grid=(t // tm,),
        in_specs=[row(D_MODEL), _resident(w_c), _resident(w_q2), _resident(w_k2), _resident(w_vt), _resident(gq),
                  _resident(gkv), row(HEAD_LANES), row(HEAD_LANES)],
        out_specs=[row(nq), row(nq), pl.BlockSpec((nv, tm), lambda i: (0, i))],
        out_shape=[jax.ShapeDtypeStruct((t, nq), BF16),
                   jax.ShapeDtypeStruct((t, nq), BF16),
                   jax.ShapeDtypeStruct((nv, t), BF16)],
        compiler_params=_cp(("parallel",)),
    )(xb, w_c, w_q2, w_k2, w_vt, gq, gkv, cos_t, sin_t)


ONES_ROWS = 16


def _scores_t(k, q):
    return lax.dot_general(k, q, (((1,), (1,)), ((), ())), preferred_element_type=F32)


def _softmax_step_t(s, vt, m_ref, acc_ref):
    m_old = m_ref[...]
    m_new = jnp.maximum(m_old, jnp.max(s, axis=0, keepdims=True))
    p = jnp.exp2(s - m_new).astype(BF16)
    alpha = jnp.exp2(m_old - m_new)
    acc_ref[...] = alpha * acc_ref[...] + _dot(vt, p)
    m_ref[...] = m_new


def _fill_vt(v_ref, vt_scr, lane0, row0, v_rows, tile):
    for j in range(vt_scr.shape[0]):
        vt = v_ref[0, j * tile:(j + 1) * tile, lane0:lane0 + HEAD_LANES].astype(F32).T
        vt_scr[j, 0:v_rows, :] = vt[row0:row0 + v_rows].astype(BF16)
        vt_scr[j, v_rows:v_rows + ONES_ROWS, :] = jnp.ones((ONES_ROWS, tile), BF16)


def _normalised(acc, v_rows):
    return acc[0:v_rows] * (1.0 / acc[v_rows:v_rows + 1])


def _diff_attn_kernel(q_ref, k_ref, v_ref, bias_ref, lq1_ref, lk1_ref, lq2_ref, lk2_ref, g_ref, o_ref,
                      vt_scr, qcat_scr, s_scr, m_scr, acc_scr, *, lam_init, tile, heads):
    qi = pl.program_id(2)
    n_q = pl.num_programs(2)
    hd = 2 * DIFF_HEAD_DIM
    hl = HEAD_LANES
    cur = qi % 2

    def first_scores(q_tile, slot):
        rows = pl.ds(pl.multiple_of(q_tile * tile, tile), tile)
        for h in range(heads):
            q = q_ref[0, rows, h * hl:(h + 1) * hl]
            lane = lax.broadcasted_iota(jnp.int32, q.shape, 1)
            zero = jnp.zeros_like(q)
            qcat_scr[slot, h] = jnp.concatenate([jnp.where(lane < DIFF_HEAD_DIM, q, zero),
                                                 jnp.where(lane >= DIFF_HEAD_DIM, q, zero)], axis=0)
            s_scr[0, h] = _scores_t(k_ref[0, 0:tile, h * hl:(h + 1) * hl], qcat_scr[slot, h])

    @pl.when(qi == 0)
    def _():
        for h in range(heads):
            _fill_vt(v_ref, vt_scr.at[h], h * hl, 0, hd, tile)
        first_scores(0, 0)
        acc_scr[...] = jnp.zeros(acc_scr.shape, F32)

    m_scr[...] = jnp.full(m_scr.shape, NEG, F32)

    def scores(slot, kj):
        rows = pl.ds(pl.multiple_of(kj * tile, tile), tile)
        for h in range(heads):
            s_scr[slot, h] = _scores_t(k_ref[0, rows, h * hl:(h + 1) * hl], qcat_scr[cur, h])

    def update(slot, kj, bias_idx=None):
        for h in range(heads):
            s = s_scr[slot, h]
            if bias_idx is not None:
                s = s + bias_ref[h, bias_idx]
            _softmax_step_t(s, vt_scr[h, kj], m_scr.at[h], acc_scr.at[h])

    n_far = jnp.maximum(qi - 1, 0)
    quads = n_far // 4

    def pair_step(a):
        scores(1, a + 1)
        update(0, a)
        scores(0, a + 2)
        update(1, a + 1)

    def quad_body(j4, carry):
        pair_step(4 * j4)
        pair_step(4 * j4 + 2)
        return carry

    lax.fori_loop(0, quads, quad_body, 0)
    odd_pair = (n_far - 4 * quads) >= 2

    @pl.when(odd_pair)
    def _():
        pair_step(4 * quads)

    t0 = 4 * quads + jnp.where(odd_pair, 2, 0)

    @pl.when(n_far - t0 == 1)
    def _():
        scores(1, t0 + 1)
        update(0, t0)
        scores(0, t0 + 2)
        update(1, t0 + 1, 1)
        update(0, t0 + 2, 0)

    @pl.when(jnp.logical_and(n_far == t0, qi >= 1))
    def _():
        scores(1, t0 + 1)
        update(0, t0, 1)
        update(1, t0 + 1, 0)

    @pl.when(qi == 0)
    def _():
        update(0, 0, 0)

    first_scores(jnp.minimum(qi + 1, n_q - 1), 1 - cur)

    lam = (jnp.exp(jnp.sum(lq1_ref[...] * lk1_ref[...], axis=-1, keepdims=True))
           - jnp.exp(jnp.sum(lq2_ref[...] * lk2_ref[...], axis=-1, keepdims=True)) + lam_init)
    for h in range(heads):
        acc = acc_scr[h]
        o = _normalised(acc[:, :tile], hd) - lam * _normalised(acc[:, tile:], hd)
        o = o * lax.rsqrt(jnp.mean(o * o, axis=0, keepdims=True) + EPS)
        o_ref[0, :, h * hl:(h + 1) * hl] = (o.T * (g_ref[...] * (1.0 - lam_init))).astype(BF16)
        acc_scr[h] = jnp.zeros(acc_scr.shape[1:], F32)


def _diff_attention(qkv, bias, lq1, lk1, lq2, lk2, g, lam_init, batch, seq):
    tile = ATT_TILE
    hl = HEAD_LANES
    nh = DIFF_HEADS
    hps = ATT_HEADS_PER_STEP
    ng = nh // hps
    vec = lambda a: pl.BlockSpec(a.shape, lambda b, h, i: (0, 0))
    return pl.pallas_call(
        functools.partial(_diff_attn_kernel, lam_init=lam_init, tile=tile, heads=hps),
        grid=(batch, ng, seq // tile),
        in_specs=[pl.BlockSpec((1, seq, hps * hl), lambda b, h, i: (b, 0, h)),
                  pl.BlockSpec((1, seq, hps * hl), lambda b, h, i: (b, 0, ng + h)),
                  pl.BlockSpec((1, seq, hps * hl), lambda b, h, i: (b, 0, 2 * ng + h)),
                  pl.BlockSpec((hps, 2, tile, 2 * tile), lambda b, h, i: (h, 0, 0, 0)),
                  vec(lq1), vec(lk1), vec(lq2), vec(lk2), vec(g)],
        out_specs=pl.BlockSpec((1, tile, hps * hl), lambda b, h, i: (b, i, h)),
        out_shape=jax.ShapeDtypeStruct((batch, seq, nh * hl), BF16),
        scratch_shapes=[pltpu.VMEM((hps, seq // tile, hl + ONES_ROWS, tile), BF16),
                        pltpu.VMEM((2, hps, 2 * tile, hl), BF16),
                        pltpu.VMEM((2, hps, tile, 2 * tile), F32),
                        pltpu.VMEM((hps, 1, 2 * tile), F32),
                        pltpu.VMEM((hps, hl + ONES_ROWS, 2 * tile), F32)],
        compiler_params=_cp(("parallel", "parallel", "arbitrary")),
    )(qkv, qkv, qkv, bias, lq1, lk1, lq2, lk2, g)


MLA_STEP_HEADS = 4


def _mla_attn_kernel(q_ref, k_ref, vt_ref, o_ref, vt_scr, s_scr, m_scr, acc_scr, *, tile):
    qi = pl.program_id(2)
    n_q = pl.num_programs(2)
    hl = HEAD_LANES
    vd = MLA_V_DIM
    tq = 2 * tile

    def scores(slot, kj, q_tile):
        rows = pl.ds(pl.multiple_of(kj * tile, tile), tile)
        q_rows = pl.ds(pl.multiple_of(q_tile * tq, tq), tq)
        for h in range(MLA_STEP_HEADS):
            s_scr[slot, h] = _scores_t(k_ref[0, rows, h * hl:(h + 1) * hl], q_ref[0, q_rows, h * hl:(h + 1) * hl])

    @pl.when(qi == 0)
    def _():
        for h in range(MLA_STEP_HEADS):
            for j in range(vt_scr.shape[1]):
                vt_scr[h, j, 0:vd, :] = vt_ref[h * vd:(h + 1) * vd, j * tile:(j + 1) * tile]
                vt_scr[h, j, vd:vd + ONES_ROWS, :] = jnp.ones((ONES_ROWS, tile), BF16)
        scores(0, 0, 0)
        acc_scr[...] = jnp.zeros(acc_scr.shape, F32)

    m_scr[...] = jnp.full(m_scr.shape, NEG, F32)

    def update(slot, kj, key_offset=None):
        for h in range(MLA_STEP_HEADS):
            s = s_scr[slot, h]
            if key_offset is not None:
                key = lax.broadcasted_iota(jnp.int32, s.shape, 0)
                qry = lax.broadcasted_iota(jnp.int32, s.shape, 1)
                s = jnp.where(key + key_offset <= qry, s, NEG)
            _softmax_step_t(s, vt_scr[h, kj], m_scr.at[h], acc_scr.at[h])

    def pair_step(a):
        scores(1, a + 1, qi)
        update(0, a)
        scores(0, a + 2, qi)
        update(1, a + 1)

    def quad_body(j4, carry):
        pair_step(4 * j4)
        pair_step(4 * j4 + 2)
        return carry

    lax.fori_loop(0, qi // 2, quad_body, 0)

    @pl.when(qi % 2 == 1)
    def _():
        pair_step(2 * qi - 2)

    t0 = 2 * qi
    scores(1, t0 + 1, qi)
    update(0, t0, 0)
    update(1, t0 + 1, tile)
    scores(0, 0, jnp.minimum(qi + 1, n_q - 1))
    o = jnp.concatenate([_normalised(acc_scr[h], MLA_V_DIM) for h in range(MLA_STEP_HEADS)], axis=0)
    o_ref[0] = o.T.astype(BF16)
    acc_scr[...] = jnp.zeros(acc_scr.shape, F32)


def _mla_attention(q, k, vt, batch, seq):
    tile = ATT_TILE
    tq = 2 * tile
    hl = HEAD_LANES
    nh = MLA_STEP_HEADS
    ng = MLA_HEADS // nh
    v_lanes = nh * MLA_V_DIM
    v_rows = MLA_V_DIM + ONES_ROWS
    return pl.pallas_call(
        functools.partial(_mla_attn_kernel, tile=tile),
        grid=(batch, ng, seq // tq),
        in_specs=[pl.BlockSpec((1, seq, nh * hl), lambda b, h, i: (b, 0, h)),
                  pl.BlockSpec((1, seq, nh * hl), lambda b, h, i: (b, 0, h)),
                  pl.BlockSpec((v_lanes, seq), lambda b, h, i: (h, b))],
        out_specs=pl.BlockSpec((1, tq, v_lanes), lambda b, h, i: (b, i, h)),
        out_shape=jax.ShapeDtypeStruct((batch, seq, MLA_HEADS * MLA_V_DIM), BF16),
        scratch_shapes=[pltpu.VMEM((nh, seq // tile, v_rows, tile), BF16),
                        pltpu.VMEM((2, nh, tile, tq), F32),
                        pltpu.VMEM((nh, 1, tq), F32),
                        pltpu.VMEM((nh, v_rows, tq), F32)],
        compiler_params=_cp(("parallel", "parallel", "arbitrary")),
    )(q, k, vt)


def _mix_ln_kernel(oa_ref, ob_ref, x_ref, wg_ref, bg_ref, wbd_ref, wbm_ref, wo_ref, g_ref, b_ref,
                   xo_ref, xbo_ref, *, alpha):
    d = D_MODEL
    x = x_ref[...]
    xb = x.astype(BF16)
    g_a = _sigmoid(_dot(xb, wg_ref[:, :d]) + bg_ref[:, :d])
    m = g_a * _dot(oa_ref[...], wbd_ref[...])
    g_b = _sigmoid(_dot(xb, wg_ref[:, d:]) + bg_ref[:, d:])
    m = (m + g_b * _dot(ob_ref[...], wbm_ref[...])).astype(BF16)
    y = _layernorm(alpha * x + _dot(m, wo_ref[...]), g_ref[...], b_ref[...])
    xo_ref[...] = y
    xbo_ref[...] = y.astype(BF16)


def _mix_ln(o_a, o_b, x, w_g, b_g, w_bd, w_bm, w_out, g, b, alpha, tm):
    t = x.shape[0]
    row = lambda a: pl.BlockSpec((tm, a.shape[1]), lambda i: (i, 0))
    return pl.pallas_call(
        functools.partial(_mix_ln_kernel, alpha=alpha),
        grid=(t // tm,),
        in_specs=[row(o_a), row(o_b), row(x), _resident(w_g), _resident(b_g), _resident(w_bd),
                  _resident(w_bm), _resident(w_out), _resident(g), _resident(b)],
        out_specs=[row(x), row(o_a)],
        out_shape=[jax.ShapeDtypeStruct((t, D_MODEL), F32), jax.ShapeDtypeStruct((t, D_MODEL), BF16)],
        compiler_params=_cp(("parallel",)),
    )(o_a, o_b, x, w_g, b_g, w_bd, w_bm, w_out, g, b)


def _swiglu_up_kernel(a_ref, w1_ref, w3_ref, o_ref):
    a = a_ref[...]
    u = _dot(a, w1_ref[...])
    o_ref[...] = (u * _sigmoid(u) * _dot(a, w3_ref[...])).astype(BF16)


def _swiglu_up(xb, w1, w3, tm, tn):
    t = xb.shape[0]
    ff = w1.shape[1]
    return pl.pallas_call(
        _swiglu_up_kernel,
        grid=(t // tm, ff // tn),
        in_specs=[pl.BlockSpec((tm, D_MODEL), lambda i, j: (i, 0)),
                  pl.BlockSpec((D_MODEL, tn), lambda i, j: (0, j)),
                  pl.BlockSpec((D_MODEL, tn), lambda i, j: (0, j))],
        out_specs=pl.BlockSpec((tm, tn), lambda i, j: (i, j)),
        out_shape=jax.ShapeDtypeStruct((t, ff), BF16),
        compiler_params=_cp(("parallel", "arbitrary")),
    )(xb, w1, w3)


def _ple_ln(f, x_ref, xb_ref, p_ref, wpg_ref, wpp_ref, g_ref, b_ref, xo_ref, xbo_ref, alpha):
    e = _sigmoid(_dot(xb_ref[...], wpg_ref[...])) * _dot(p_ref[...].astype(BF16), wpp_ref[...])
    y = _layernorm(alpha * x_ref[...] + f + e, g_ref[...], b_ref[...])
    xo_ref[...] = y
    xbo_ref[...] = y.astype(BF16)


def _dense_ln_kernel(h_ref, w2_ref, x_ref, xb_ref, p_ref, wpg_ref, wpp_ref, g_ref, b_ref, xo_ref, xbo_ref, *, alpha):
    _ple_ln(_dot(h_ref[...], w2_ref[...]), x_ref, xb_ref, p_ref, wpg_ref, wpp_ref, g_ref, b_ref, xo_ref, xbo_ref,
            alpha)


def _moe_ln_kernel(y1_ref, y2_ref, gates_ref, x_ref, xb_ref, p_ref, wpg_ref, wpp_ref, g_ref, b_ref, xo_ref, xbo_ref,
                   *, alpha):
    gates = gates_ref[...]
    f = (y1_ref[...].astype(F32) * gates[:, TOP_K:TOP_K + 1]
         + y2_ref[...].astype(F32) * gates[:, TOP_K + 1:TOP_K + 2])
    _ple_ln(f, x_ref, xb_ref, p_ref, wpg_ref, wpp_ref, g_ref, b_ref, xo_ref, xbo_ref, alpha)


def _channel_ln(body, lead, x, xb, p, layer, w_pg, w_pp, g, b, alpha, tm):
    t = x.shape[0]
    row = lambda a: pl.BlockSpec((tm, a.shape[1]), lambda i: (i, 0))
    full = _resident
    tail = [x, xb, p, w_pg, w_pp, g, b]
    return pl.pallas_call(
        functools.partial(body, alpha=alpha),
        grid=(t // tm,),
        in_specs=[row(a) if tiled else full(a) for a, tiled in lead]
        + [row(x), row(xb), pl.BlockSpec((None, tm, p.shape[2]), lambda i: (layer, i, 0)),
           full(w_pg), full(w_pp), full(g), full(b)],
        out_specs=[row(x), row(xb)],
        out_shape=[jax.ShapeDtypeStruct((t, D_MODEL), F32), jax.ShapeDtypeStruct((t, D_MODEL), BF16)],
        compiler_params=_cp(("parallel",)),
    )(*[a for a, _ in lead], *tail)


def _router_kernel(x_ref, whi_ref, wlo_ref, o_ref):
    x = x_ref[...]
    x_hi = x.astype(BF16)
    x_lo = (x - x_hi.astype(F32)).astype(BF16)
    w_hi = whi_ref[...]
    logits = _dot(x_hi, w_hi) + (_dot(x_lo, w_hi) + _dot(x_hi, wlo_ref[...]))
    lane = lax.broadcasted_iota(jnp.int32, logits.shape, 1)
    lg = jnp.where(lane < N_EXPERTS, logits, -jnp.inf)
    v1 = jnp.max(lg, axis=-1, keepdims=True)
    i1 = jnp.min(jnp.where(lg == v1, lane, HEAD_LANES), axis=-1, keepdims=True)
    lg2 = jnp.where(lane == i1, -jnp.inf, lg)
    v2 = jnp.max(lg2, axis=-1, keepdims=True)
    i2 = jnp.min(jnp.where(lg2 == v2, lane, HEAD_LANES), axis=-1, keepdims=True)
    e2 = jnp.exp(v2 - v1)
    g1 = 1.0 / (1.0 + e2)
    g2 = e2 / (1.0 + e2)
    o_ref[...] = jnp.where(lane == 0, i1.astype(F32),
                           jnp.where(lane == 1, i2.astype(F32),
                                     jnp.where(lane == 2, g1, jnp.where(lane == 3, g2, 0.0))))


def _router(x, w_r, tm):
    t = x.shape[0]
    w_hi = w_r.astype(BF16)
    w_lo = (w_r - w_hi.astype(F32)).astype(BF16)
    return pl.pallas_call(
        _router_kernel,
        grid=(t // tm,),
        in_specs=[pl.BlockSpec((tm, D_MODEL), lambda i: (i, 0)), _resident(w_hi), _resident(w_lo)],
        out_specs=pl.BlockSpec((tm, HEAD_LANES), lambda i: (i, 0)),
        out_shape=jax.ShapeDtypeStruct((t, HEAD_LANES), F32),
        compiler_params=_cp(("parallel",)),
    )(x, w_hi, w_lo)


def _expert_changed(be_ref):
    i = pl.program_id(1)
    return jnp.logical_or(i == 0, be_ref[i] != be_ref[jnp.maximum(i - 1, 0)])


def _expert_up_kernel(be_ref, nb_ref, a_ref, w1_ref, w3_ref, *rest):
    o_ref, w1_scr, w3_scr = rest[-3:]

    @pl.when(_expert_changed(be_ref))
    def _():
        w1_scr[...] = w1_ref[0].astype(BF16)
        w3_scr[...] = w3_ref[0].astype(BF16)

    @pl.when(pl.program_id(1) < nb_ref[0])
    def _():
        a = a_ref[...]
        u = _dot(a, w1_scr[...])
        o_ref[...] = (u * _sigmoid(u) * _dot(a, w3_scr[...])).astype(BF16)

    @pl.when(pl.program_id(1) >= nb_ref[0])
    def _():
        o_ref[...] = jnp.zeros(o_ref.shape, BF16)


def _expert_up(a, w1, w3, layer, blk_e, n_used, tm, tn, total_rows, first_block, into):
    rows = a.shape[0]
    ff = w1.shape[3]
    wspec = pl.BlockSpec((None, 1, D_MODEL, tn), lambda j, i, be, nb: (layer, be[i], 0, j))
    chained = [] if into is None else [into]
    grid_spec = pltpu.PrefetchScalarGridSpec(
        num_scalar_prefetch=2,
        grid=(ff // tn, rows // tm),
        in_specs=[pl.BlockSpec((tm, D_MODEL), lambda j, i, be, nb: (i, 0)), wspec, wspec]
        + [pl.BlockSpec(memory_space=pl.ANY) for _ in chained],
        out_specs=pl.BlockSpec((tm, tn), lambda j, i, be, nb: (first_block + i, j)),
        scratch_shapes=[pltpu.VMEM((D_MODEL, tn), BF16), pltpu.VMEM((D_MODEL, tn), BF16)],
    )
    return pl.pallas_call(
        _expert_up_kernel,
        grid_spec=grid_spec,
        out_shape=jax.ShapeDtypeStruct((total_rows, ff), BF16),
        input_output_aliases={5: 0} if chained else {},
        compiler_params=_cp(("arbitrary", "arbitrary")),
    )(blk_e, n_used, a, w1, w3, *chained)


def _expert_down_kernel(be_ref, nb_ref, h_ref, w2_ref, o_ref, w2_scr):
    @pl.when(_expert_changed(be_ref))
    def _():
        w2_scr[...] = w2_ref[0].astype(BF16)

    @pl.when(pl.program_id(1) < nb_ref[0])
    def _():
        o_ref[...] = _dot(h_ref[...], w2_scr[...]).astype(BF16)

    @pl.when(pl.program_id(1) >= nb_ref[0])
    def _():
        o_ref[...] = jnp.zeros(o_ref.shape, BF16)


def _expert_down(h, w2, layer, blk_e, n_used, tm, tn):
    rows, ff = h.shape
    grid_spec = pltpu.PrefetchScalarGridSpec(
        num_scalar_prefetch=2,
        grid=(D_MODEL // tn, rows // tm),
        in_specs=[pl.BlockSpec((tm, ff), lambda j, i, be, nb: (i, 0)),
                  pl.BlockSpec((None, 1, ff, tn), lambda j, i, be, nb: (layer, be[i], 0, j))],
        out_specs=pl.BlockSpec((tm, tn), lambda j, i, be, nb: (i, j)),
        scratch_shapes=[pltpu.VMEM((ff, tn), BF16)],
    )
    return pl.pallas_call(
        _expert_down_kernel,
        grid_spec=grid_spec,
        out_shape=jax.ShapeDtypeStruct((rows, D_MODEL), BF16),
        compiler_params=_cp(("arbitrary", "arbitrary")),
    )(blk_e, n_used, h, w2)


def _moe(x, xb, w_r, w1, w3, w2, layer, tm):
    t = x.shape[0]
    m = t * TOP_K
    gb = GROUP_ROWS
    routed = _router(x, w_r, tm)
    experts = jnp.arange(N_EXPERTS, dtype=jnp.int32)[None, :]
    hot = [(routed[:, k].astype(jnp.int32)[:, None] == experts).astype(jnp.int32) for k in range(TOP_K)]
    both = sum(hot)
    before = jnp.cumsum(both, axis=0) - both
    counts = jnp.sum(both, axis=0)
    padded = ((counts + gb - 1) // gb) * gb
    pend = jnp.cumsum(padded)
    pstart = pend - padded
    dest = [jnp.sum((pstart[None, :] + before) * h, axis=1) for h in hot]
    n_blocks = m // gb + N_EXPERTS
    blk_e = jnp.minimum(jnp.sum((jnp.arange(n_blocks, dtype=jnp.int32)[:, None] * gb >= pend[None, :])
                                .astype(jnp.int32), axis=1), N_EXPERTS - 1)
    n_used = (pend[-1:] // gb).astype(jnp.int32)
    n_rows = n_blocks * gb
    gap = padded - counts
    gap_end = jnp.cumsum(gap)
    i_dummy = jnp.arange(n_rows - m, dtype=jnp.int32)
    in_gap = (i_dummy[:, None] >= (gap_end - gap)[None, :]) & (i_dummy[:, None] < gap_end[None, :])
    dummy_key = jnp.where(i_dummy < gap_end[-1],
                          jnp.sum(jnp.where(in_gap, (pstart + counts - (gap_end - gap))[None, :], 0), axis=1),
                          pend[-1] - gap_end[-1]) + i_dummy
    keys = jnp.concatenate(dest + [dummy_key]).astype(jnp.uint32)
    toks = jnp.concatenate([jnp.arange(t, dtype=jnp.uint32)] * TOP_K + [jnp.zeros((n_rows - m,), jnp.uint32)])
    assert n_rows * t < 2 ** 32
    src_tok = (jnp.sort(keys * t + toks) % t).astype(jnp.int32)
    take = lambda arr, rows: arr.at[rows].get(mode="promise_in_bounds")
    nbc = n_blocks // DISPATCH_CHUNKS
    h = None
    for c in range(DISPATCH_CHUNKS):
        a = take(xb, src_tok[c * nbc * gb:(c + 1) * nbc * gb])
        h = _expert_up(a, w1, w3, layer, blk_e[c * nbc:(c + 1) * nbc], jnp.clip(n_used - c * nbc, 0, nbc),
                       gb, EXPERT_FF // 2, n_rows, c * nbc, h)
    yb = _expert_down(h, w2, layer, blk_e, n_used, gb, D_MODEL)
    return take(yb, dest[0]), take(yb, dest[1]), routed


def _t5_bucket(dist):
    n = jnp.maximum(dist, 0)
    max_exact = REL_BUCKETS // 2
    large = max_exact + (jnp.log(jnp.maximum(n, 1).astype(F32) / max_exact)
                         / math.log(REL_MAX_DIST / max_exact) * (REL_BUCKETS - max_exact)).astype(jnp.int32)
    large = jnp.minimum(large, REL_BUCKETS - 1)
    return jnp.where(n < max_exact, n, large)


def _bias_tiles(table, tile):
    period = 3 * tile
    k = jnp.arange(period, dtype=jnp.int32)
    dist = jnp.where(k < 2 * tile, k, k - period)
    rel = table.astype(F32) - table[REL_BUCKETS - 1].astype(F32)[None, :]
    onehot = _t5_bucket(dist)[:, None] == jnp.arange(REL_BUCKETS, dtype=jnp.int32)[None, :]
    f = jnp.sum(jnp.where(onehot[:, :, None], rel[None], 0.0), axis=1)
    f = jnp.where((dist >= 0)[:, None], f, NEG)
    heads = table.shape[1]
    toep = jnp.tile(f.T, (1, tile))[:, :tile * (period - 1)].reshape(heads, tile, period - 1)
    tiles = jnp.stack([toep[:, :, :tile], toep[:, :, tile:2 * tile]], axis=1)
    return jnp.concatenate([tiles, tiles], axis=-1)


def _rope_tables(positions):
    half = MLA_ROPE_DIM // 2
    b, sq = positions.shape
    inv_freq = ROPE_THETA ** (-jnp.arange(half, dtype=F32) / half)
    per_row = HEAD_LANES // half
    pos = positions.astype(F32).reshape(b * sq // per_row, per_row, 1)
    ang = (pos * inv_freq.reshape(1, 1, half)).reshape(b * sq // per_row, HEAD_LANES)
    c = jnp.cos(ang).reshape(b, sq, half)
    s = jnp.sin(ang).reshape(b, sq, half)
    ones = jnp.ones((b, sq, MLA_NOPE_DIM), F32)
    z_nope = jnp.zeros((b, sq, MLA_NOPE_DIM), F32)
    z_pad = jnp.zeros((b, sq, HEAD_LANES - MLA_NOPE_DIM - MLA_ROPE_DIM), F32)
    cos_t = jnp.concatenate([ones, c, c, z_pad], axis=-1).reshape(b * sq, HEAD_LANES)
    sin_t = jnp.concatenate([z_nope, s, s, z_pad], axis=-1).reshape(b * sq, HEAD_LANES)
    return cos_t, sin_t


def _rot_cols(w):
    half = MLA_ROPE_DIM // 2
    return jnp.concatenate([-w[..., half:], w[..., :half]], axis=-1)


def _mixer_weights(w_in, w_uq, w_ukv):
    d = D_MODEL
    hq = DIFF_HEADS * 2 * DIFF_HEAD_DIM
    o = 0
    w_dq = w_in[:, o:o + hq]; o += hq
    w_dk = w_in[:, o:o + hq]; o += hq
    w_dv = w_in[:, o:o + hq]; o += hq
    w_cq = w_in[:, o:o + MLA_Q_RANK]; o += MLA_Q_RANK
    w_ckv = w_in[:, o:o + MLA_KV_RANK]; o += MLA_KV_RANK
    w_kr = w_in[:, o:o + MLA_ROPE_DIM]; o += MLA_ROPE_DIM
    w_g = w_in[:, o:]
    w_qkv = jnp.concatenate([w_dq, w_dk, w_dv], axis=1).astype(BF16)
    pad = HEAD_LANES - MLA_NOPE_DIM - MLA_ROPE_DIM
    z = lambda n: jnp.zeros((d, n), F32)
    w_c = jnp.concatenate([w_cq, w_ckv, z(MLA_NOPE_DIM), w_kr, z(pad), z(MLA_NOPE_DIM), _rot_cols(w_kr), z(pad)],
                          axis=1).astype(BF16)
    wq = w_uq.reshape(MLA_Q_RANK, MLA_HEADS, MLA_NOPE_DIM + MLA_ROPE_DIM)
    nope, rope = wq[..., :MLA_NOPE_DIM], wq[..., MLA_NOPE_DIM:]
    zq = lambda n: jnp.zeros((MLA_Q_RANK, MLA_HEADS, n), F32)
    plain = jnp.concatenate([nope, rope, zq(pad)], axis=-1).reshape(MLA_Q_RANK, -1)
    rot = jnp.concatenate([zq(MLA_NOPE_DIM), _rot_cols(rope), zq(pad)], axis=-1).reshape(MLA_Q_RANK, -1)
    w_q2 = jnp.concatenate([plain, rot], axis=1).astype(BF16)
    wkv = w_ukv.reshape(MLA_KV_RANK, MLA_HEADS, MLA_NOPE_DIM + MLA_V_DIM)
    zk = jnp.zeros((MLA_KV_RANK, MLA_HEADS, HEAD_LANES - MLA_NOPE_DIM), F32)
    w_k2 = jnp.concatenate([wkv[..., :MLA_NOPE_DIM], zk], axis=-1).reshape(MLA_KV_RANK, -1).astype(BF16)
    w_vt = wkv[..., MLA_NOPE_DIM:].reshape(MLA_KV_RANK, -1).T.astype(BF16)
    return w_qkv, w_c, w_g.astype(BF16), w_q2, w_k2, w_vt


def kernel(x, p, positions, rel_bias_table, w_in, b_gate, lambda_q1, lambda_k1, lambda_q2, lambda_k2,
           diff_subln_g, mla_q_norm_g, w_uq, mla_kv_norm_g, w_ukv, w_branch_diff, w_branch_mla, w_out,
           ln_mix_g, ln_mix_b, dense_w1, dense_w3, dense_w2, router_w, expert_w1, expert_w3, expert_w2,
           w_ple_gate, w_ple_proj, ln_ffn_g, ln_ffn_b):
    batch, seq, d = x.shape
    depth = w_in.shape[0]
    t = batch * seq
    assert d == D_MODEL and seq % (2 * ATT_TILE) == 0 and t % ROW_TILE == 0, (batch, seq, d)
    assert (t * TOP_K // GROUP_ROWS + N_EXPERTS) % DISPATCH_CHUNKS == 0, (batch, seq)
    tm = min(ROW_TILE, t)
    tm_mm = BIG_ROW_TILE if t % BIG_ROW_TILE == 0 else tm
    alpha = (2.0 * depth) ** 0.25
    row = lambda v: v.reshape(1, -1).astype(F32)

    cos_t, sin_t = _rope_tables(positions)
    bias = _bias_tiles(rel_bias_table * LOG2E, ATT_TILE)
    xf = x.reshape(t, d)
    xb = xf
    p_all = p.reshape(depth, t, PLE_DIM)

    for i in range(depth):
        lam_init = 0.8 - 0.6 * math.exp(-0.3 * i)
        w_qkv, w_c, w_g, w_q2, w_k2, w_vt = _mixer_weights(w_in[i], w_uq[i], w_ukv[i])
        qkv = _matmul(xb, w_qkv, BF16, tm_mm, 1024, scaled_tiles=1, scale=DIFF_HEAD_DIM ** -0.5 * LOG2E)
        q_mla, k_mla, vt_mla = _latent(xb, w_c, w_q2, w_k2, w_vt, row(mla_q_norm_g[i]), row(mla_kv_norm_g[i]),
                                       cos_t, sin_t, tm)
        o_a = _diff_attention(qkv.reshape(batch, seq, -1), bias, row(lambda_q1[i]), row(lambda_k1[i]),
                              row(lambda_q2[i]), row(lambda_k2[i]), row(diff_subln_g[i]), lam_init, batch, seq)
        o_b = _mla_attention(q_mla.reshape(batch, seq, -1), k_mla.reshape(batch, seq, -1), vt_mla, batch, seq)
        xf, xb = _mix_ln(o_a.reshape(t, -1), o_b.reshape(t, -1), xf, w_g, row(b_gate[i]),
                         w_branch_diff[i].astype(BF16), w_branch_mla[i].astype(BF16), w_out[i].astype(BF16),
                         row(ln_mix_g[i]), row(ln_mix_b[i]), alpha, tm)
        j = i // 2
        w_pg = w_ple_gate[i].astype(BF16)
        w_pp = w_ple_proj[i].astype(BF16)
        if i % 2 == 0:
            hmid = _swiglu_up(xb, dense_w1[j].astype(BF16), dense_w3[j].astype(BF16), tm_mm, DENSE_FF // 2)
            lead = [(hmid, True), (dense_w2[j].astype(BF16), False)]
            body = _dense_ln_kernel
        else:
            w_r = jnp.concatenate([router_w[j], jnp.zeros((d, HEAD_LANES - N_EXPERTS), F32)], axis=1)
            y1, y2, routed = _moe(xf, xb, w_r, expert_w1, expert_w3, expert_w2, j, tm)
            lead = [(y1, True), (y2, True), (routed, True)]
            body = _moe_ln_kernel
        xf, xb = _channel_ln(body, lead, xf, xb, p_all, i, w_pg, w_pp, row(ln_ffn_g[i]), row(ln_ffn_b[i]), alpha, tm)
    return xf.reshape(batch, seq, d)
```

```python
import functools
import math

import jax
import jax.numpy as jnp
from jax import lax
from jax.experimental import pallas as pl
from jax.experimental.pallas import tpu as pltpu

F32 = jnp.float32
BF16 = jnp.bfloat16

D_MODEL = 1024
PLE_DIM = 256
DIFF_HEADS = 8
DIFF_HEAD_DIM = 64
MLA_HEADS = 8
MLA_Q_RANK = 384
MLA_KV_RANK = 256
MLA_NOPE_DIM = 64
MLA_ROPE_DIM = 32
MLA_V_DIM = 64
ROPE_THETA = 10000.0
REL_BUCKETS = 32
REL_MAX_DIST = 128
DENSE_FF = 2816
N_EXPERTS = 8
TOP_K = 2
EXPERT_FF = 3584
EPS = 1e-5

HEAD_LANES = 128
NEG = -1e30
LOG2E = math.log2(math.e)
ATT_TILE = 256
ATT_HEADS_PER_STEP = 4
ROW_TILE = 512
BIG_ROW_TILE = 1024
GROUP_ROWS = 512
DISPATCH_CHUNKS = 4
VMEM_LIMIT = 52 * 1024 * 1024


def _cp(sem):
    return pltpu.CompilerParams(dimension_semantics=sem, vmem_limit_bytes=VMEM_LIMIT)


def _dot(a, b):
    return jnp.dot(a, b, preferred_element_type=F32)


def _sigmoid(x):
    return 1.0 / (1.0 + jnp.exp(-x))


def _layernorm(r, g, b):
    mu = jnp.mean(r, axis=-1, keepdims=True)
    d = r - mu
    var = jnp.mean(d * d, axis=-1, keepdims=True)
    return d * lax.rsqrt(var + EPS) * g + b


def _rmsnorm(x, g):
    return x * lax.rsqrt(jnp.mean(x * x, axis=-1, keepdims=True) + EPS) * g


def _mm_kernel(a_ref, w_ref, o_ref, *, scaled_tiles, scale):
    acc = _dot(a_ref[...].astype(BF16), w_ref[...])
    if scaled_tiles:
        acc = acc * jnp.where(pl.program_id(1) < scaled_tiles, scale, 1.0)
    o_ref[...] = acc.astype(o_ref.dtype)


def _matmul(a, w, out_dtype, tm, tn, scaled_tiles=0, scale=1.0):
    m, k = a.shape
    n = w.shape[1]
    return pl.pallas_call(
        functools.partial(_mm_kernel, scaled_tiles=scaled_tiles, scale=scale),
        grid=(m // tm, n // tn),
        in_specs=[pl.BlockSpec((tm, k), lambda i, j: (i, 0)),
                  pl.BlockSpec((k, tn), lambda i, j: (0, j))],
        out_specs=pl.BlockSpec((tm, tn), lambda i, j: (i, j)),
        out_shape=jax.ShapeDtypeStruct((m, n), out_dtype),
        compiler_params=_cp(("parallel", "arbitrary")),
    )(a, w)


C_COLS = MLA_Q_RANK + MLA_KV_RANK + 2 * HEAD_LANES


def _resident(a):
    return pl.BlockSpec(a.shape, lambda *_: (0,) * a.ndim, pipeline_mode=pl.Buffered(1))


def _latent_kernel(a_ref, wc_ref, wq_ref, wk_ref, wvt_ref, gq_ref, gkv_ref, cos_ref, sin_ref, q_ref, k_ref, vt_ref, *,
                   scale):
    z = _dot(a_ref[...].astype(BF16), wc_ref[...])
    cq = z[:, :MLA_Q_RANK]
    ckv = z[:, MLA_Q_RANK:MLA_Q_RANK + MLA_KV_RANK]
    kr = z[:, MLA_Q_RANK + MLA_KV_RANK:MLA_Q_RANK + MLA_KV_RANK + HEAD_LANES]
    kr_rot = z[:, MLA_Q_RANK + MLA_KV_RANK + HEAD_LANES:]
    cos, sin = cos_ref[...], sin_ref[...]
    k_rope = kr * cos + kr_rot * sin
    ckv_n = _rmsnorm(ckv, gkv_ref[...]).astype(BF16)
    zk = _dot(ckv_n, wk_ref[...])
    vt_ref[...] = lax.dot_general(wvt_ref[...], ckv_n, (((1,), (1,)), ((), ())),
                                  preferred_element_type=F32).astype(BF16)
    zq = _dot(_rmsnorm(cq, gq_ref[...]).astype(BF16), wq_ref[...])
    c = cos * scale
    s = sin * scale
    hl = HEAD_LANES
    for h in range(MLA_HEADS):
        k_ref[:, h * hl:(h + 1) * hl] = (zk[:, h * hl:(h + 1) * hl] + k_rope).astype(BF16)
        q_ref[:, h * hl:(h + 1) * hl] = (
            zq[:, h * hl:(h + 1) * hl] * c
            + zq[:, (MLA_HEADS + h) * hl:(MLA_HEADS + h + 1) * hl] * s).astype(BF16)


def _latent(xb, w_c, w_q2, w_k2, w_vt, gq, gkv, cos_t, sin_t, tm):
    t = xb.shape[0]
    row = lambda n: pl.BlockSpec((tm, n), lambda i: (i, 0))
    nq = MLA_HEADS * HEAD_LANES
    nv = MLA_HEADS * MLA_V_DIM
    scale = (MLA_NOPE_DIM + MLA_ROPE_DIM) ** -0.5 * LOG2E
    return pl.pallas_call(
        functools.partial(_latent_kernel, scale=scale),
        grid=(t // tm,),
        in_specs=[row(D_MODEL), _resident(w_c), _resident(w_q2), _resident(w_k2), _resident(w_vt), _resident(gq),
                  _resident(gkv), row(HEAD_LANES), row(HEAD_LANES)],
        out_specs=[row(nq), row(nq), pl.BlockSpec((nv, tm), lambda i: (0, i))],
        out_shape=[jax.ShapeDtypeStruct((t, nq), BF16),
                   jax.ShapeDtypeStruct((t, nq), BF16),
                   jax.ShapeDtypeStruct((nv, t), BF16)],
        compiler_params=_cp(("parallel",)),
    )(xb, w_c, w_q2, w_k2, w_vt, gq, gkv, cos_t, sin_t)


ONES_ROWS = 16


def _scores_t(k, q):
    return lax.dot_general(k, q, (((1,), (1,)), ((), ())), preferred_element_type=F32)


def _softmax_step_t(s, vt, m_ref, acc_ref):
    m_old = m_ref[...]
    m_new = jnp.maximum(m_old, jnp.max(s, axis=0, keepdims=True))
    p = jnp.exp2(s - m_new).astype(BF16)
    alpha = jnp.exp2(m_old - m_new)
    acc_ref[...] = alpha * acc_ref[...] + _dot(vt, p)
    m_ref[...] = m_new


def _fill_vt(v_ref, vt_scr, lane0, row0, v_rows, tile):
    for j in range(vt_scr.shape[0]):
        vt = v_ref[0, j * tile:(j + 1) * tile, lane0:lane0 + HEAD_LANES].astype(F32).T
        vt_scr[j, 0:v_rows, :] = vt[row0:row0 + v_rows].astype(BF16)
        vt_scr[j, v_rows:v_rows + ONES_ROWS, :] = jnp.ones((ONES_ROWS, tile), BF16)


def _normalised(acc, v_rows):
    return acc[0:v_rows] * (1.0 / acc[v_rows:v_rows + 1])


def _diff_attn_kernel(q_ref, k_ref, v_ref, bias_ref, lq1_ref, lk1_ref, lq2_ref, lk2_ref, g_ref, o_ref,
                      vt_scr, qcat_scr, s_scr, m_scr, acc_scr, *, lam_init, tile, heads):
    qi = pl.program_id(2)
    n_q = pl.num_programs(2)
    hd = 2 * DIFF_HEAD_DIM
    hl = HEAD_LANES
    cur = qi % 2

    def first_scores(q_tile, slot):
        rows = pl.ds(pl.multiple_of(q_tile * tile, tile), tile)
        for h in range(heads):
            q = q_ref[0, rows, h * hl:(h + 1) * hl]
            lane = lax.broadcasted_iota(jnp.int32, q.shape, 1)
            zero = jnp.zeros_like(q)
            qcat_scr[slot, h] = jnp.concatenate([jnp.where(lane < DIFF_HEAD_DIM, q, zero),
                                                 jnp.where(lane >= DIFF_HEAD_DIM, q, zero)], axis=0)
            s_scr[0, h] = _scores_t(k_ref[0, 0:tile, h * hl:(h + 1) * hl], qcat_scr[slot, h])

    @pl.when(qi == 0)
    def _():
        for h in range(heads):
            _fill_vt(v_ref, vt_scr.at[h], h * hl, 0, hd, tile)
        first_scores(0, 0)

    m_scr[...] = jnp.full(m_scr.shape, NEG, F32)
    acc_scr[...] = jnp.zeros(acc_scr.shape, F32)

    def scores(slot, kj, hs=range(heads)):
        rows = pl.ds(pl.multiple_of(kj * tile, tile), tile)
        for h in hs:
            s_scr[slot, h] = _scores_t(k_ref[0, rows, h * hl:(h + 1) * hl], qcat_scr[cur, h])

    def update(slot, kj, bias_idx=None, hs=range(heads)):
        for h in hs:
            s = s_scr[slot, h]
            if bias_idx is not None:
                s = s + bias_ref[h, bias_idx]
            _softmax_step_t(s, vt_scr[h, kj], m_scr.at[h], acc_scr.at[h])

    n_far = jnp.maximum(qi - 1, 0)
    quads = n_far // 4

    def pair_step(a):
        for h in range(heads):
            scores(1, a + 1, (h,))
            update(0, a, hs=(h,))
        for h in range(heads):
            scores(0, a + 2, (h,))
            update(1, a + 1, hs=(h,))

    def quad_body(j4, carry):
        pair_step(4 * j4)
        pair_step(4 * j4 + 2)
        return carry

    lax.fori_loop(0, quads, quad_body, 0)
    odd_pair = (n_far - 4 * quads) >= 2

    @pl.when(odd_pair)
    def _():
        pair_step(4 * quads)

    t0 = 4 * quads + jnp.where(odd_pair, 2, 0)

    @pl.when(n_far - t0 == 1)
    def _():
        for h in range(heads):
            scores(1, t0 + 1, (h,))
            update(0, t0, hs=(h,))
        for h in range(heads):
            scores(0, t0 + 2, (h,))
            update(1, t0 + 1, 1, (h,))
        update(0, t0 + 2, 0)

    @pl.when(jnp.logical_and(n_far == t0, qi >= 1))
    def _():
        for h in range(heads):
            scores(1, t0 + 1, (h,))
            update(0, t0, 1, (h,))
        update(1, t0 + 1, 0)

    @pl.when(qi == 0)
    def _():
        update(0, 0, 0)

    first_scores(jnp.minimum(qi + 1, n_q - 1), 1 - cur)

    lam = (jnp.exp(jnp.sum(lq1_ref[...] * lk1_ref[...], axis=-1, keepdims=True))
           - jnp.exp(jnp.sum(lq2_ref[...] * lk2_ref[...], axis=-1, keepdims=True)) + lam_init)
    for h in range(heads):
        acc = acc_scr[h]
        o = _normalised(acc[:, :tile], hd) - lam * _normalised(acc[:, tile:], hd)
        o = o * lax.rsqrt(jnp.mean(o * o, axis=0, keepdims=True) + EPS)
        o_ref[0, :, h * hl:(h + 1) * hl] = (o.T * (g_ref[...] * (1.0 - lam_init))).astype(BF16)


def _diff_attention(qkv, bias, lq1, lk1, lq2, lk2, g, lam_init, batch, seq):
    tile = ATT_TILE
    hl = HEAD_LANES
    nh = DIFF_HEADS
    hps = ATT_HEADS_PER_STEP
    ng = nh // hps
    vec = lambda a: pl.BlockSpec(a.shape, lambda b, h, i: (0, 0))
    return pl.pallas_call(
        functools.partial(_diff_attn_kernel, lam_init=lam_init, tile=tile, heads=hps),
        grid=(batch, ng, seq // tile),
        in_specs=[pl.BlockSpec((1, seq, hps * hl), lambda b, h, i: (b, 0, h)),
                  pl.BlockSpec((1, seq, hps * hl), lambda b, h, i: (b, 0, ng + h)),
                  pl.BlockSpec((1, seq, hps * hl), lambda b, h, i: (b, 0, 2 * ng + h)),
                  pl.BlockSpec((hps, 2, tile, 2 * tile), lambda b, h, i: (h, 0, 0, 0)),
                  vec(lq1), vec(lk1), vec(lq2), vec(lk2), vec(g)],
        out_specs=pl.BlockSpec((1, tile, hps * hl), lambda b, h, i: (b, i, h)),
        out_shape=jax.ShapeDtypeStruct((batch, seq, nh * hl), BF16),
        scratch_shapes=[pltpu.VMEM((hps, seq // tile, hl + ONES_ROWS, tile), BF16),
                        pltpu.VMEM((2, hps, 2 * tile, hl), BF16),
                        pltpu.VMEM((2, hps, tile, 2 * tile), F32),
                        pltpu.VMEM((hps, 1, 2 * tile), F32),
                        pltpu.VMEM((hps, hl + ONES_ROWS, 2 * tile), F32)],
        compiler_params=_cp(("parallel", "parallel", "arbitrary")),
    )(qkv, qkv, qkv, bias, lq1, lk1, lq2, lk2, g)


MLA_STEP_HEADS = 4


def _mla_attn_kernel(q_ref, k_ref, vt_ref, o_ref, vt_scr, s_scr, m_scr, acc_scr, *, tile):
    qi = pl.program_id(2)
    n_q = pl.num_programs(2)
    hl = HEAD_LANES
    vd = MLA_V_DIM
    tq = 2 * tile

    all_heads = range(MLA_STEP_HEADS)

    def scores(slot, kj, q_tile, hs=all_heads):
        rows = pl.ds(pl.multiple_of(kj * tile, tile), tile)
        q_rows = pl.ds(pl.multiple_of(q_tile * tq, tq), tq)
        for h in hs:
            s_scr[slot, h] = _scores_t(k_ref[0, rows, h * hl:(h + 1) * hl], q_ref[0, q_rows, h * hl:(h + 1) * hl])

    @pl.when(qi == 0)
    def _():
        for h in range(MLA_STEP_HEADS):
            for j in range(vt_scr.shape[1]):
                vt_scr[h, j, 0:vd, :] = vt_ref[h * vd:(h + 1) * vd, j * tile:(j + 1) * tile]
                vt_scr[h, j, vd:vd + ONES_ROWS, :] = jnp.ones((ONES_ROWS, tile), BF16)
        scores(0, 0, 0)

    m_scr[...] = jnp.full(m_scr.shape, NEG, F32)
    acc_scr[...] = jnp.zeros(acc_scr.shape, F32)

    def update(slot, kj, key_offset=None, hs=all_heads):
        for h in hs:
            s = s_scr[slot, h]
            if key_offset is not None:
                key = lax.broadcasted_iota(jnp.int32, s.shape, 0)
                qry = lax.broadcasted_iota(jnp.int32, s.shape, 1)
                s = jnp.where(key + key_offset <= qry, s, NEG)
            _softmax_step_t(s, vt_scr[h, kj], m_scr.at[h], acc_scr.at[h])

    def pair_step(a):
        for h in all_heads:
            scores(1, a + 1, qi, (h,))
            update(0, a, hs=(h,))
        for h in all_heads:
            scores(0, a + 2, qi, (h,))
            update(1, a + 1, hs=(h,))

    def quad_body(j4, carry):
        pair_step(4 * j4)
        pair_step(4 * j4 + 2)
        return carry

    lax.fori_loop(0, qi // 2, quad_body, 0)

    @pl.when(qi % 2 == 1)
    def _():
        pair_step(2 * qi - 2)

    t0 = 2 * qi
    for h in all_heads:
        scores(1, t0 + 1, qi, (h,))
        update(0, t0, 0, (h,))
    update(1, t0 + 1, tile)
    scores(0, 0, jnp.minimum(qi + 1, n_q - 1))
    o = jnp.concatenate([_normalised(acc_scr[h], MLA_V_DIM) for h in range(MLA_STEP_HEADS)], axis=0)
    o_ref[0] = o.T.astype(BF16)


def _mla_attention(q, k, vt, batch, seq):
    tile = ATT_TILE
    tq = 2 * tile
    hl = HEAD_LANES
    nh = MLA_STEP_HEADS
    ng = MLA_HEADS // nh
    v_lanes = nh * MLA_V_DIM
    v_rows = MLA_V_DIM + ONES_ROWS
    return pl.pallas_call(
        functools.partial(_mla_attn_kernel, tile=tile),
        grid=(batch, ng, seq // tq),
        in_specs=[pl.BlockSpec((1, seq, nh * hl), lambda b, h, i: (b, 0, h)),
                  pl.BlockSpec((1, seq, nh * hl), lambda b, h, i: (b, 0, h)),
                  pl.BlockSpec((v_lanes, seq), lambda b, h, i: (h, b))],
        out_specs=pl.BlockSpec((1, tq, v_lanes), lambda b, h, i: (b, i, h)),
        out_shape=jax.ShapeDtypeStruct((batch, seq, MLA_HEADS * MLA_V_DIM), BF16),
        scratch_shapes=[pltpu.VMEM((nh, seq // tile, v_rows, tile), BF16),
                        pltpu.VMEM((2, nh, tile, tq), F32),
                        pltpu.VMEM((nh, 1, tq), F32),
                        pltpu.VMEM((nh, v_rows, tq), F32)],
        compiler_params=_cp(("parallel", "parallel", "arbitrary")),
    )(q, k, vt)


def _mix_ln_kernel(oa_ref, ob_ref, x_ref, wg_ref, bg_ref, wbd_ref, wbm_ref, wo_ref, g_ref, b_ref,
                   xo_ref, xbo_ref, *, alpha):
    d = D_MODEL
    x = x_ref[...]
    xb = x.astype(BF16)
    g_a = _sigmoid(_dot(xb, wg_ref[:, :d]) + bg_ref[:, :d])
    m = g_a * _dot(oa_ref[...], wbd_ref[...])
    g_b = _sigmoid(_dot(xb, wg_ref[:, d:]) + bg_ref[:, d:])
    m = (m + g_b * _dot(ob_ref[...], wbm_ref[...])).astype(BF16)
    y = _layernorm(alpha * x + _dot(m, wo_ref[...]), g_ref[...], b_ref[...])
    xo_ref[...] = y
    xbo_ref[...] = y.astype(BF16)


def _mix_ln(o_a, o_b, x, w_g, b_g, w_bd, w_bm, w_out, g, b, alpha, tm):
    t = x.shape[0]
    row = lambda a: pl.BlockSpec((tm, a.shape[1]), lambda i: (i, 0))
    return pl.pallas_call(
        functools.partial(_mix_ln_kernel, alpha=alpha),
        grid=(t // tm,),
        in_specs=[row(o_a), row(o_b), row(x), _resident(w_g), _resident(b_g), _resident(w_bd),
                  _resident(w_bm), _resident(w_out), _resident(g), _resident(b)],
        out_specs=[row(x), row(o_a)],
        out_shape=[jax.ShapeDtypeStruct((t, D_MODEL), F32), jax.ShapeDtypeStruct((t, D_MODEL), BF16)],
        compiler_params=_cp(("parallel",)),
    )(o_a, o_b, x, w_g, b_g, w_bd, w_bm, w_out, g, b)


def _swiglu_up_kernel(a_ref, w1_ref, w3_ref, o_ref):
    a = a_ref[...]
    u = _dot(a, w1_ref[...])
    o_ref[...] = (u * _sigmoid(u) * _dot(a, w3_ref[...])).astype(BF16)


def _swiglu_up(xb, w1, w3, tm, tn):
    t = xb.shape[0]
    ff = w1.shape[1]
    return pl.pallas_call(
        _swiglu_up_kernel,
        grid=(t // tm, ff // tn),
        in_specs=[pl.BlockSpec((tm, D_MODEL), lambda i, j: (i, 0)),
                  pl.BlockSpec((D_MODEL, tn), lambda i, j: (0, j)),
                  pl.BlockSpec((D_MODEL, tn), lambda i, j: (0, j))],
        out_specs=pl.BlockSpec((tm, tn), lambda i, j: (i, j)),
        out_shape=jax.ShapeDtypeStruct((t, ff), BF16),
        compiler_params=_cp(("parallel", "arbitrary")),
    )(xb, w1, w3)


def _ple_ln(f, x_ref, xb_ref, p_ref, wpg_ref, wpp_ref, g_ref, b_ref, xo_ref, xbo_ref, alpha):
    e = _sigmoid(_dot(xb_ref[...], wpg_ref[...])) * _dot(p_ref[...].astype(BF16), wpp_ref[...])
    y = _layernorm(alpha * x_ref[...] + f + e, g_ref[...], b_ref[...])
    xo_ref[...] = y
    xbo_ref[...] = y.astype(BF16)


def _dense_ln_kernel(h_ref, w2_ref, x_ref, xb_ref, p_ref, wpg_ref, wpp_ref, g_ref, b_ref, xo_ref, xbo_ref, *, alpha):
    _ple_ln(_dot(h_ref[...], w2_ref[...]), x_ref, xb_ref, p_ref, wpg_ref, wpp_ref, g_ref, b_ref, xo_ref, xbo_ref,
            alpha)


def _moe_ln_kernel(y1_ref, y2_ref, gates_ref, x_ref, xb_ref, p_ref, wpg_ref, wpp_ref, g_ref, b_ref, xo_ref, xbo_ref,
                   *, alpha):
    gates = gates_ref[...]
    f = (y1_ref[...].astype(F32) * gates[:, TOP_K:TOP_K + 1]
         + y2_ref[...].astype(F32) * gates[:, TOP_K + 1:TOP_K + 2])
    _ple_ln(f, x_ref, xb_ref, p_ref, wpg_ref, wpp_ref, g_ref, b_ref, xo_ref, xbo_ref, alpha)


def _channel_ln(body, lead, x, xb, p, layer, w_pg, w_pp, g, b, alpha, tm):
    t = x.shape[0]
    row = lambda a: pl.BlockSpec((tm, a.shape[1]), lambda i: (i, 0))
    full = _resident
    tail = [x, xb, p, w_pg, w_pp, g, b]
    return pl.pallas_call(
        functools.partial(body, alpha=alpha),
        grid=(t // tm,),
        in_specs=[row(a) if tiled else full(a) for a, tiled in lead]
        + [row(x), row(xb), pl.BlockSpec((None, tm, p.shape[2]), lambda i: (layer, i, 0)),
           full(w_pg), full(w_pp), full(g), full(b)],
        out_specs=[row(x), row(xb)],
        out_shape=[jax.ShapeDtypeStruct((t, D_MODEL), F32), jax.ShapeDtypeStruct((t, D_MODEL), BF16)],
        compiler_params=_cp(("parallel",)),
    )(*[a for a, _ in lead], *tail)


def _router_kernel(x_ref, whi_ref, wlo_ref, o_ref):
    x = x_ref[...]
    x_hi = x.astype(BF16)
    x_lo = (x - x_hi.astype(F32)).astype(BF16)
    w_hi = whi_ref[...]
    logits = _dot(x_hi, w_hi) + (_dot(x_lo, w_hi) + _dot(x_hi, wlo_ref[...]))
    lane = lax.broadcasted_iota(jnp.int32, logits.shape, 1)
    lg = jnp.where(lane < N_EXPERTS, logits, -jnp.inf)
    v1 = jnp.max(lg, axis=-1, keepdims=True)
    i1 = jnp.min(jnp.where(lg == v1, lane, HEAD_LANES), axis=-1, keepdims=True)
    lg2 = jnp.where(lane == i1, -jnp.inf, lg)
    v2 = jnp.max(lg2, axis=-1, keepdims=True)
    i2 = jnp.min(jnp.where(lg2 == v2, lane, HEAD_LANES), axis=-1, keepdims=True)
    e2 = jnp.exp(v2 - v1)
    g1 = 1.0 / (1.0 + e2)
    g2 = e2 / (1.0 + e2)
    o_ref[...] = jnp.where(lane == 0, i1.astype(F32),
                           jnp.where(lane == 1, i2.astype(F32),
                                     jnp.where(lane == 2, g1, jnp.where(lane == 3, g2, 0.0))))


def _router(x, w_r, tm):
    t = x.shape[0]
    w_hi = w_r.astype(BF16)
    w_lo = (w_r - w_hi.astype(F32)).astype(BF16)
    return pl.pallas_call(
        _router_kernel,
        grid=(t // tm,),
        in_specs=[pl.BlockSpec((tm, D_MODEL), lambda i: (i, 0)), _resident(w_hi), _resident(w_lo)],
        out_specs=pl.BlockSpec((tm, HEAD_LANES), lambda i: (i, 0)),
        out_shape=jax.ShapeDtypeStruct((t, HEAD_LANES), F32),
        compiler_params=_cp(("parallel",)),
    )(x, w_hi, w_lo)


def _expert_changed(be_ref):
    i = pl.program_id(1)
    return jnp.logical_or(i == 0, be_ref[i] != be_ref[jnp.maximum(i - 1, 0)])


def _expert_up_kernel(be_ref, nb_ref, a_ref, w1_ref, w3_ref, *rest):
    o_ref, w1_scr, w3_scr = rest[-3:]

    @pl.when(_expert_changed(be_ref))
    def _():
        w1_scr[...] = w1_ref[0].astype(BF16)
        w3_scr[...] = w3_ref[0].astype(BF16)

    @pl.when(pl.program_id(1) < nb_ref[0])
    def _():
        a = a_ref[...]
        u = _dot(a, w1_scr[...])
        o_ref[...] = (u * _sigmoid(u) * _dot(a, w3_scr[...])).astype(BF16)

    @pl.when(pl.program_id(1) >= nb_ref[0])
    def _():
        o_ref[...] = jnp.zeros(o_ref.shape, BF16)


def _expert_up(a, w1, w3, layer, blk_e, n_used, tm, tn, total_rows, first_block, into):
    rows = a.shape[0]
    ff = w1.shape[3]
    wspec = pl.BlockSpec((None, 1, D_MODEL, tn), lambda j, i, be, nb: (layer, be[i], 0, j))
    chained = [] if into is None else [into]
    grid_spec = pltpu.PrefetchScalarGridSpec(
        num_scalar_prefetch=2,
        grid=(ff // tn, rows // tm),
        in_specs=[pl.BlockSpec((tm, D_MODEL), lambda j, i, be, nb: (i, 0)), wspec, wspec]
        + [pl.BlockSpec(memory_space=pl.ANY) for _ in chained],
        out_specs=pl.BlockSpec((tm, tn), lambda j, i, be, nb: (first_block + i, j)),
        scratch_shapes=[pltpu.VMEM((D_MODEL, tn), BF16), pltpu.VMEM((D_MODEL, tn), BF16)],
    )
    return pl.pallas_call(
        _expert_up_kernel,
        grid_spec=grid_spec,
        out_shape=jax.ShapeDtypeStruct((total_rows, ff), BF16),
        input_output_aliases={5: 0} if chained else {},
        compiler_params=_cp(("arbitrary", "arbitrary")),
    )(blk_e, n_used, a, w1, w3, *chained)


def _expert_down_kernel(be_ref, nb_ref, h_ref, w2_ref, o_ref, w2_scr):
    @pl.when(_expert_changed(be_ref))
    def _():
        w2_scr[...] = w2_ref[0].astype(BF16)

    @pl.when(pl.program_id(1) < nb_ref[0])
    def _():
        o_ref[...] = _dot(h_ref[...], w2_scr[...]).astype(BF16)

    @pl.when(pl.program_id(1) >= nb_ref[0])
    def _():
        o_ref[...] = jnp.zeros(o_ref.shape, BF16)


def _expert_down(h, w2, layer, blk_e, n_used, tm, tn):
    rows, ff = h.shape
    grid_spec = pltpu.PrefetchScalarGridSpec(
        num_scalar_prefetch=2,
        grid=(D_MODEL // tn, rows // tm),
        in_specs=[pl.BlockSpec((tm, ff), lambda j, i, be, nb: (i, 0)),
                  pl.BlockSpec((None, 1, ff, tn), lambda j, i, be, nb: (layer, be[i], 0, j))],
        out_specs=pl.BlockSpec((tm, tn), lambda j, i, be, nb: (i, j)),
        scratch_shapes=[pltpu.VMEM((ff, tn), BF16)],
    )
    return pl.pallas_call(
        _expert_down_kernel,
        grid_spec=grid_spec,
        out_shape=jax.ShapeDtypeStruct((rows, D_MODEL), BF16),
        compiler_params=_cp(("arbitrary", "arbitrary")),
    )(blk_e, n_used, h, w2)


def _moe(x, xb, w_r, w1, w3, w2, layer, tm):
    t = x.shape[0]
    m = t * TOP_K
    gb = GROUP_ROWS
    routed = _router(x, w_r, tm)
    experts = jnp.arange(N_EXPERTS, dtype=jnp.int32)[None, :]
    hot = [(routed[:, k].astype(jnp.int32)[:, None] == experts).astype(jnp.int32) for k in range(TOP_K)]
    both = sum(hot)
    before = jnp.cumsum(both, axis=0) - both
    counts = jnp.sum(both, axis=0)
    padded = ((counts + gb - 1) // gb) * gb
    pend = jnp.cumsum(padded)
    pstart = pend - padded
    dest = [jnp.sum((pstart[None, :] + before) * h, axis=1) for h in hot]
    n_blocks = m // gb + N_EXPERTS
    blk_e = jnp.minimum(jnp.sum((jnp.arange(n_blocks, dtype=jnp.int32)[:, None] * gb >= pend[None, :])
                                .astype(jnp.int32), axis=1), N_EXPERTS - 1)
    n_used = (pend[-1:] // gb).astype(jnp.int32)
    n_rows = n_blocks * gb
    gap = padded - counts
    gap_end = jnp.cumsum(gap)
    i_dummy = jnp.arange(n_rows - m, dtype=jnp.int32)
    in_gap = (i_dummy[:, None] >= (gap_end - gap)[None, :]) & (i_dummy[:, None] < gap_end[None, :])
    dummy_key = jnp.where(i_dummy < gap_end[-1],
                          jnp.sum(jnp.where(in_gap, (pstart + counts - (gap_end - gap))[None, :], 0), axis=1),
                          pend[-1] - gap_end[-1]) + i_dummy
    keys = jnp.concatenate(dest + [dummy_key]).astype(jnp.uint32)
    toks = jnp.concatenate([jnp.arange(t, dtype=jnp.uint32)] * TOP_K + [jnp.zeros((n_rows - m,), jnp.uint32)])
    assert n_rows * t < 2 ** 32
    src_tok = (jnp.sort(keys * t + toks) % t).astype(jnp.int32)
    take = lambda arr, rows: arr.at[rows].get(mode="promise_in_bounds")
    nbc = n_blocks // DISPATCH_CHUNKS
    h = None
    for c in range(DISPATCH_CHUNKS):
        a = take(xb, src_tok[c * nbc * gb:(c + 1) * nbc * gb])
        h = _expert_up(a, w1, w3, layer, blk_e[c * nbc:(c + 1) * nbc], jnp.clip(n_used - c * nbc, 0, nbc),
                       gb, EXPERT_FF // 2, n_rows, c * nbc, h)
    yb = _expert_down(h, w2, layer, blk_e, n_used, gb, D_MODEL)
    return take(yb, dest[0]), take(yb, dest[1]), routed


def _t5_bucket(dist):
    n = jnp.maximum(dist, 0)
    max_exact = REL_BUCKETS // 2
    large = max_exact + (jnp.log(jnp.maximum(n, 1).astype(F32) / max_exact)
                         / math.log(REL_MAX_DIST / max_exact) * (REL_BUCKETS - max_exact)).astype(jnp.int32)
    large = jnp.minimum(large, REL_BUCKETS - 1)
    return jnp.where(n < max_exact, n, large)


def _bias_tiles(table, tile):
    period = 3 * tile
    k = jnp.arange(period, dtype=jnp.int32)
    dist = jnp.where(k < 2 * tile, k, k - period)
    rel = table.astype(F32) - table[REL_BUCKETS - 1].astype(F32)[None, :]
    onehot = _t5_bucket(dist)[:, None] == jnp.arange(REL_BUCKETS, dtype=jnp.int32)[None, :]
    f = jnp.sum(jnp.where(onehot[:, :, None], rel[None], 0.0), axis=1)
    f = jnp.where((dist >= 0)[:, None], f, NEG)
    heads = table.shape[1]
    toep = jnp.tile(f.T, (1, tile))[:, :tile * (period - 1)].reshape(heads, tile, period - 1)
    tiles = jnp.stack([toep[:, :, :tile], toep[:, :, tile:2 * tile]], axis=1)
    return jnp.concatenate([tiles, tiles], axis=-1)


def _rope_tables(positions):
    half = MLA_ROPE_DIM // 2
    b, sq = positions.shape
    inv_freq = ROPE_THETA ** (-jnp.arange(half, dtype=F32) / half)
    per_row = HEAD_LANES // half
    pos = positions.astype(F32).reshape(b * sq // per_row, per_row, 1)
    ang = (pos * inv_freq.reshape(1, 1, half)).reshape(b * sq // per_row, HEAD_LANES)
    c = jnp.cos(ang).reshape(b, sq, half)
    s = jnp.sin(ang).reshape(b, sq, half)
    ones = jnp.ones((b, sq, MLA_NOPE_DIM), F32)
    z_nope = jnp.zeros((b, sq, MLA_NOPE_DIM), F32)
    z_pad = jnp.zeros((b, sq, HEAD_LANES - MLA_NOPE_DIM - MLA_ROPE_DIM), F32)
    cos_t = jnp.concatenate([ones, c, c, z_pad], axis=-1).reshape(b * sq, HEAD_LANES)
    sin_t = jnp.concatenate([z_nope, s, s, z_pad], axis=-1).reshape(b * sq, HEAD_LANES)
    return cos_t, sin_t


def _rot_cols(w):
    half = MLA_ROPE_DIM // 2
    return jnp.concatenate([-w[..., half:], w[..., :half]], axis=-1)


def _mixer_weights(w_in, w_uq, w_ukv):
    d = D_MODEL
    hq = DIFF_HEADS * 2 * DIFF_HEAD_DIM
    o = 0
    w_dq = w_in[:, o:o + hq]; o += hq
    w_dk = w_in[:, o:o + hq]; o += hq
    w_dv = w_in[:, o:o + hq]; o += hq
    w_cq = w_in[:, o:o + MLA_Q_RANK]; o += MLA_Q_RANK
    w_ckv = w_in[:, o:o + MLA_KV_RANK]; o += MLA_KV_RANK
    w_kr = w_in[:, o:o + MLA_ROPE_DIM]; o += MLA_ROPE_DIM
    w_g = w_in[:, o:]
    w_qkv = jnp.concatenate([w_dq, w_dk, w_dv], axis=1).astype(BF16)
    pad = HEAD_LANES - MLA_NOPE_DIM - MLA_ROPE_DIM
    z = lambda n: jnp.zeros((d, n), F32)
    w_c = jnp.concatenate([w_cq, w_ckv, z(MLA_NOPE_DIM), w_kr, z(pad), z(MLA_NOPE_DIM), _rot_cols(w_kr), z(pad)],
                          axis=1).astype(BF16)
    wq = w_uq.reshape(MLA_Q_RANK, MLA_HEADS, MLA_NOPE_DIM + MLA_ROPE_DIM)
    nope, rope = wq[..., :MLA_NOPE_DIM], wq[..., MLA_NOPE_DIM:]
    zq = lambda n: jnp.zeros((MLA_Q_RANK, MLA_HEADS, n), F32)
    plain = jnp.concatenate([nope, rope, zq(pad)], axis=-1).reshape(MLA_Q_RANK, -1)
    rot = jnp.concatenate([zq(MLA_NOPE_DIM), _rot_cols(rope), zq(pad)], axis=-1).reshape(MLA_Q_RANK, -1)
    w_q2 = jnp.concatenate([plain, rot], axis=1).astype(BF16)
    wkv = w_ukv.reshape(MLA_KV_RANK, MLA_HEADS, MLA_NOPE_DIM + MLA_V_DIM)
    zk = jnp.zeros((MLA_KV_RANK, MLA_HEADS, HEAD_LANES - MLA_NOPE_DIM), F32)
    w_k2 = jnp.concatenate([wkv[..., :MLA_NOPE_DIM], zk], axis=-1).reshape(MLA_KV_RANK, -1).astype(BF16)
    w_vt = wkv[..., MLA_NOPE_DIM:].reshape(MLA_KV_RANK, -1).T.astype(BF16)
    return w_qkv, w_c, w_g.astype(BF16), w_q2, w_k2, w_vt


def kernel(x, p, positions, rel_bias_table, w_in, b_gate, lambda_q1, lambda_k1, lambda_q2, lambda_k2,
           diff_subln_g, mla_q_norm_g, w_uq, mla_kv_norm_g, w_ukv, w_branch_diff, w_branch_mla, w_out,
           ln_mix_g, ln_mix_b, dense_w1, dense_w3, dense_w2, router_w, expert_w1, expert_w3, expert_w2,
           w_ple_gate, w_ple_proj, ln_ffn_g, ln_ffn_b):
    batch, seq, d = x.shape
    depth = w_in.shape[0]
    t = batch * seq
    assert d == D_MODEL and seq % (2 * ATT_TILE) == 0 and t % ROW_TILE == 0, (batch, seq, d)
    assert (t * TOP_K // GROUP_ROWS + N_EXPERTS) % DISPATCH_CHUNKS == 0, (batch, seq)
    tm = min(ROW_TILE, t)
    tm_mm = BIG_ROW_TILE if t % BIG_ROW_TILE == 0 else tm
    alpha = (2.0 * depth) ** 0.25
    row = lambda v: v.reshape(1, -1).astype(F32)

    cos_t, sin_t = _rope_tables(positions)
    bias = _bias_tiles(rel_bias_table * LOG2E, ATT_TILE)
    xf = x.reshape(t, d)
    xb = xf
    p_all = p.reshape(depth, t, PLE_DIM)

    for i in range(depth):
        lam_init = 0.8 - 0.6 * math.exp(-0.3 * i)
        w_qkv, w_c, w_g, w_q2, w_k2, w_vt = _mixer_weights(w_in[i], w_uq[i], w_ukv[i])
        qkv = _matmul(xb, w_qkv, BF16, tm_mm, 1024, scaled_tiles=1, scale=DIFF_HEAD_DIM ** -0.5 * LOG2E)
        q_mla, k_mla, vt_mla = _latent(xb, w_c, w_q2, w_k2, w_vt, row(mla_q_norm_g[i]), row(mla_kv_norm_g[i]),
                                       cos_t, sin_t, tm)
        o_a = _diff_attention(qkv.reshape(batch, seq, -1), bias, row(lambda_q1[i]), row(lambda_k1[i]),
                              row(lambda_q2[i]), row(lambda_k2[i]), row(diff_subln_g[i]), lam_init, batch, seq)
        o_b = _mla_attention(q_mla.reshape(batch, seq, -1), k_mla.reshape(batch, seq, -1), vt_mla, batch, seq)
        xf, xb = _mix_ln(o_a.reshape(t, -1), o_b.reshape(t, -1), xf, w_g, row(b_gate[i]),
                         w_branch_diff[i].astype(BF16), w_branch_mla[i].astype(BF16), w_out[i].astype(BF16),
                         row(ln_mix_g[i]), row(ln_mix_b[i]), alpha, tm)
        j = i // 2
        w_pg = w_ple_gate[i].astype(BF16)
        w_pp = w_ple_proj[i].astype(BF16)
        if i % 2 == 0:
            hmid = _swiglu_up(xb, dense_w1[j].astype(BF16), dense_w3[j].astype(BF16), tm_mm, DENSE_FF // 2)
            lead = [(hmid, True), (dense_w2[j].astype(BF16), False)]
            body = _dense_ln_kernel
        else:
            w_r = jnp.concatenate([router_w[j], jnp.zeros((d, HEAD_LANES - N_EXPERTS), F32)], axis=1)
            y1, y2, routed = _moe(xf, xb, w_r, expert_w1, expert_w3, expert_w2, j, tm)
            lead = [(y1, True), (y2, True), (routed, True)]
            body = _moe_ln_kernel
        xf, xb = _channel_ln(body, lead, xf, xb, p_all, i, w_pg, w_pp, row(ln_ffn_g[i]), row(ln_ffn_b[i]), alpha, tm)
    return xf.reshape(batch, seq, d)
```
